```python
import jax, jax.numpy as jnp
from jax import lax
import numpy as np

D_MODEL = 2048
BATCH = 8
SEQ = 8192
DEPTH = 2

CHUNK = 64
N_MEM = 256
EXPAND = 2
MIX_WIDTH = EXPAND * D_MODEL
W_A = MIX_WIDTH // 2
HEAD_DIM_A = 128
N_HEADS_A = W_A // HEAD_DIM_A
N_PAST_CHUNKS = 8
MAX_REL = 128
W_B = MIX_WIDTH - W_A
CONV_WIDTH = 31
GMLP_CHUNK = 128
N_GROUPS_C = 8
N_HEADS_X = 4
HEAD_DIM_X = D_MODEL // N_HEADS_X
EPS = 1e-6
N_EVEN = (DEPTH + 1) // 2
N_ODD = DEPTH // 2
AB_IN_COLS = 3 * W_A + 2 * W_B + MIX_WIDTH
C_IN_COLS = 3 * MIX_WIDTH

kernel_name = "hybrid_streaming_band_conv_sgu_encoder"


def rmsnorm(x, g):
    xf = x.astype(jnp.float32)
    y = xf * lax.rsqrt(jnp.mean(xf * xf, axis=-1, keepdims=True) + EPS)
    return (y * g.astype(jnp.float32)).astype(x.dtype)


def layernorm(x, g, b):
    xf = x.astype(jnp.float32)
    mu = jnp.mean(xf, axis=-1, keepdims=True)
    var = jnp.mean(jnp.square(xf - mu), axis=-1, keepdims=True)
    y = (xf - mu) * lax.rsqrt(var + EPS)
    return (y * g.astype(jnp.float32) + b.astype(jnp.float32)).astype(x.dtype)


def chunk_band_attention(q, k, v, rel_bias):
    b, s, h, dh = q.shape
    n_chunks = s // CHUNK
    pad = N_PAST_CHUNKS * CHUNK
    band = (N_PAST_CHUNKS + 1) * CHUNK
    k_pad = jnp.pad(k, ((0, 0), (pad, 0), (0, 0), (0, 0)))
    v_pad = jnp.pad(v, ((0, 0), (pad, 0), (0, 0), (0, 0)))
    q_off = np.arange(CHUNK)
    k_off = np.arange(band) - pad
    rel_idx = np.clip(q_off[:, None] - k_off[None, :], -MAX_REL, MAX_REL) + MAX_REL
    bias = jnp.take(rel_bias.astype(jnp.float32), jnp.asarray(rel_idx), axis=1)
    scale = dh ** -0.5
    k_off_j = jnp.asarray(k_off)

    def one_chunk(c):
        start = c * CHUNK
        qc = lax.dynamic_slice_in_dim(q, start, CHUNK, axis=1)
        kb = lax.dynamic_slice_in_dim(k_pad, start, band, axis=1)
        vb = lax.dynamic_slice_in_dim(v_pad, start, band, axis=1)
        sc = jnp.einsum('bqhd,bkhd->bhqk', qc, kb).astype(jnp.float32) * scale + bias[None]
        valid = (start + k_off_j) >= 0
        sc = jnp.where(valid[None, None, None, :], sc, jnp.float32(-1e30))
        p = jax.nn.softmax(sc, axis=-1).astype(vb.dtype)
        return jnp.einsum('bhqk,bkhd->bqhd', p, vb)

    out = lax.map(one_chunk, jnp.arange(n_chunks))
    return jnp.transpose(out, (1, 0, 2, 3, 4)).reshape(b, s, h * dh)


def causal_depthwise_conv(x, w, bias):
    c = x.shape[-1]
    xp = jnp.pad(x, ((0, 0), (w.shape[0] - 1, 0), (0, 0)))
    y = lax.conv_general_dilated(xp, w[:, None, :], window_strides=(1,), padding='VALID',
                                 dimension_numbers=('NWC', 'WIO', 'NWC'),
                                 feature_group_count=c)
    return y + bias


def ab_mixer(hn, w_in, rel_bias, conv_w, conv_b, ln_g, ln_b, w_out):
    b, s, _ = hn.shape
    proj = hn @ w_in
    splits = [W_A, 2 * W_A, 3 * W_A, 3 * W_A + W_B, 3 * W_A + 2 * W_B]
    q, k, v, glu_a, glu_b, gate = jnp.split(proj, splits, axis=-1)
    shp = (b, s, N_HEADS_A, HEAD_DIM_A)
    ya = chunk_band_attention(q.reshape(shp), k.reshape(shp), v.reshape(shp), rel_bias)
    yb = glu_a * jax.nn.sigmoid(glu_b)
    yb = jax.nn.silu(layernorm(causal_depthwise_conv(yb, conv_w, conv_b), ln_g, ln_b))
    y = jnp.concatenate([ya, yb], axis=-1) * jax.nn.silu(gate)
    return y @ w_out


def c_mixer(hn, w_in, ln_g, ln_b, w_s, b_s, w_out):
    b, s, _ = hn.shape
    u, v, gate = jnp.split(hn @ w_in, [MIX_WIDTH, 2 * MIX_WIDTH], axis=-1)
    v = layernorm(v, ln_g, ln_b)
    n_blk = s // GMLP_CHUNK
    vr = v.reshape(b, n_blk, GMLP_CHUNK, N_GROUPS_C, MIX_WIDTH // N_GROUPS_C)
    pos_chunk = np.arange(GMLP_CHUNK) // CHUNK
    mask = jnp.asarray(pos_chunk[:, None] >= pos_chunk[None, :], dtype=w_s.dtype)
    ws = w_s * mask[None]
    sg = jnp.einsum('gij,bnjgc->bnigc', ws, vr) + jnp.transpose(b_s)[None, None, :, :, None]
    y = u * sg.reshape(b, s, MIX_WIDTH) * jax.nn.silu(gate)
    return y @ w_out


def memory_cross_attention(hn, mem_n, wq, wk, wv, wo):
    b, s, _ = hn.shape
    q = (hn @ wq).reshape(b, s, N_HEADS_X, HEAD_DIM_X)
    k = (mem_n @ wk).reshape(b, N_MEM, N_HEADS_X, HEAD_DIM_X)
    v = (mem_n @ wv).reshape(b, N_MEM, N_HEADS_X, HEAD_DIM_X)
    sc = jnp.einsum('bqhd,bkhd->bhqk', q, k).astype(jnp.float32) * (HEAD_DIM_X ** -0.5)
    p = jax.nn.softmax(sc, axis=-1).astype(v.dtype)
    o = jnp.einsum('bhqk,bkhd->bqhd', p, v).reshape(b, s, D_MODEL)
    return o @ wo


def _fwd_setup_inputs(seed: int = 0) -> dict:
    key = jax.random.key(seed)
    ks = iter(jax.random.split(key, 32))
    nrm = lambda shape, scale: jax.random.normal(next(ks), shape, jnp.float32) * scale
    gain = lambda shape: 1.0 + nrm(shape, 0.01)
    d = D_MODEL
    return {
        "x": nrm((BATCH, SEQ, d), 1.0),
        "mem": nrm((BATCH, N_MEM, d), 1.0),
        "norm_mix_g": gain((DEPTH, d)),
        "norm_x_g": gain((DEPTH, d)),
        "norm_mem_g": gain((DEPTH, d)),
        "final_norm_g": gain((d,)),
        "w_in_ab": nrm((N_EVEN, d, AB_IN_COLS), d ** -0.5),
        "rel_bias": nrm((N_EVEN, N_HEADS_A, 2 * MAX_REL + 1), 0.2),
        "conv_w": nrm((N_EVEN, CONV_WIDTH, W_B), CONV_WIDTH ** -0.5),
        "conv_b": nrm((N_EVEN, W_B), 0.01),
        "conv_ln_g": gain((N_EVEN, W_B)),
        "conv_ln_b": nrm((N_EVEN, W_B), 0.01),
        "w_out_ab": nrm((N_EVEN, MIX_WIDTH, d), MIX_WIDTH ** -0.5),
        "w_in_c": nrm((N_ODD, d, C_IN_COLS), d ** -0.5),
        "sgu_ln_g": gain((N_ODD, MIX_WIDTH)),
        "sgu_ln_b": nrm((N_ODD, MIX_WIDTH), 0.01),
        "w_s": nrm((N_ODD, N_GROUPS_C, GMLP_CHUNK, GMLP_CHUNK), GMLP_CHUNK ** -0.5),
        "b_s": gain((N_ODD, N_GROUPS_C, GMLP_CHUNK)),
        "w_out_c": nrm((N_ODD, MIX_WIDTH, d), MIX_WIDTH ** -0.5),
        "w_xq": nrm((DEPTH, d, d), d ** -0.5),
        "w_xk": nrm((DEPTH, d, d), d ** -0.5),
        "w_xv": nrm((DEPTH, d, d), d ** -0.5),
        "w_xo": nrm((DEPTH, d, d), d ** -0.5),
    }


def _fwd_reference(x, mem, norm_mix_g, norm_x_g, norm_mem_g, final_norm_g, w_in_ab, rel_bias,
              conv_w, conv_b, conv_ln_g, conv_ln_b, w_out_ab, w_in_c, sgu_ln_g, sgu_ln_b,
              w_s, b_s, w_out_c, w_xq, w_xk, w_xv, w_xo):
    h = x
    for layer in range(DEPTH):
        i = layer // 2
        hn = rmsnorm(h, norm_mix_g[layer])
        if layer % 2 == 0:
            y = ab_mixer(hn, w_in_ab[i], rel_bias[i], conv_w[i], conv_b[i],
                         conv_ln_g[i], conv_ln_b[i], w_out_ab[i])
        else:
            y = c_mixer(hn, w_in_c[i], sgu_ln_g[i], sgu_ln_b[i], w_s[i], b_s[i], w_out_c[i])
        h = h + y
        h = h + memory_cross_attention(rmsnorm(h, norm_x_g[layer]), rmsnorm(mem, norm_mem_g[layer]),
                                       w_xq[layer], w_xk[layer], w_xv[layer], w_xo[layer])
    return rmsnorm(h, final_norm_g)


import jax as _jax
import jax.numpy as _jnp

TWIN_FORMAT = 'train_step'
FWD_PARAMS = ['x', 'mem', 'norm_mix_g', 'norm_x_g', 'norm_mem_g', 'final_norm_g', 'w_in_ab', 'rel_bias', 'conv_w', 'conv_b', 'conv_ln_g', 'conv_ln_b', 'w_out_ab', 'w_in_c', 'sgu_ln_g', 'sgu_ln_b', 'w_s', 'b_s', 'w_out_c', 'w_xq', 'w_xk', 'w_xv', 'w_xo']
TWIN_WEIGHTS = ['norm_mix_g', 'norm_x_g', 'norm_mem_g', 'final_norm_g', 'w_in_ab', 'rel_bias', 'conv_w', 'conv_b', 'conv_ln_g', 'conv_ln_b', 'w_out_ab', 'w_in_c', 'sgu_ln_g', 'sgu_ln_b', 'w_s', 'b_s', 'w_out_c', 'w_xq', 'w_xk', 'w_xv', 'w_xo']
TWIN_DIFF_INPUT = 'x'
TWIN_INPUTS = ['x', 'mem', 'norm_mix_g', 'norm_x_g', 'norm_mem_g', 'final_norm_g', 'w_in_ab', 'rel_bias', 'conv_w', 'conv_b', 'conv_ln_g', 'conv_ln_b', 'w_out_ab', 'w_in_c', 'sgu_ln_g', 'sgu_ln_b', 'w_s', 'b_s', 'w_out_c', 'w_xq', 'w_xk', 'w_xv', 'w_xo', 'loss_target', 'm_norm_mix_g', 'm_norm_x_g', 'm_norm_mem_g', 'm_final_norm_g', 'm_w_in_ab', 'm_rel_bias', 'm_conv_w', 'm_conv_b', 'm_conv_ln_g', 'm_conv_ln_b', 'm_w_out_ab', 'm_w_in_c', 'm_sgu_ln_g', 'm_sgu_ln_b', 'm_w_s', 'm_b_s', 'm_w_out_c', 'm_w_xq', 'm_w_xk', 'm_w_xv', 'm_w_xo', 'v_norm_mix_g', 'v_norm_x_g', 'v_norm_mem_g', 'v_final_norm_g', 'v_w_in_ab', 'v_rel_bias', 'v_conv_w', 'v_conv_b', 'v_conv_ln_g', 'v_conv_ln_b', 'v_w_out_ab', 'v_w_in_c', 'v_sgu_ln_g', 'v_sgu_ln_b', 'v_w_s', 'v_b_s', 'v_w_out_c', 'v_w_xq', 'v_w_xk', 'v_w_xv', 'v_w_xo']
TWIN_OUTPUTS = ['loss', 'grad_x', 'grad_norm_mix_g', 'grad_norm_x_g', 'grad_norm_mem_g', 'grad_final_norm_g', 'grad_w_in_ab', 'grad_rel_bias', 'grad_conv_w', 'grad_conv_b', 'grad_conv_ln_g', 'grad_conv_ln_b', 'grad_w_out_ab', 'grad_w_in_c', 'grad_sgu_ln_g', 'grad_sgu_ln_b', 'grad_w_s', 'grad_b_s', 'grad_w_out_c', 'grad_w_xq', 'grad_w_xk', 'grad_w_xv', 'grad_w_xo', 'delta_norm_mix_g', 'delta_norm_x_g', 'delta_norm_mem_g', 'delta_final_norm_g', 'delta_w_in_ab', 'delta_rel_bias', 'delta_conv_w', 'delta_conv_b', 'delta_conv_ln_g', 'delta_conv_ln_b', 'delta_w_out_ab', 'delta_w_in_c', 'delta_sgu_ln_g', 'delta_sgu_ln_b', 'delta_w_s', 'delta_b_s', 'delta_w_out_c', 'delta_w_xq', 'delta_w_xk', 'delta_w_xv', 'delta_w_xo', 'new_m_norm_mix_g', 'new_m_norm_x_g', 'new_m_norm_mem_g', 'new_m_final_norm_g', 'new_m_w_in_ab', 'new_m_rel_bias', 'new_m_conv_w', 'new_m_conv_b', 'new_m_conv_ln_g', 'new_m_conv_ln_b', 'new_m_w_out_ab', 'new_m_w_in_c', 'new_m_sgu_ln_g', 'new_m_sgu_ln_b', 'new_m_w_s', 'new_m_b_s', 'new_m_w_out_c', 'new_m_w_xq', 'new_m_w_xk', 'new_m_w_xv', 'new_m_w_xo', 'new_v_norm_mix_g', 'new_v_norm_x_g', 'new_v_norm_mem_g', 'new_v_final_norm_g', 'new_v_w_in_ab', 'new_v_rel_bias', 'new_v_conv_w', 'new_v_conv_b', 'new_v_conv_ln_g', 'new_v_conv_ln_b', 'new_v_w_out_ab', 'new_v_w_in_c', 'new_v_sgu_ln_g', 'new_v_sgu_ln_b', 'new_v_w_s', 'new_v_b_s', 'new_v_w_out_c', 'new_v_w_xq', 'new_v_w_xk', 'new_v_w_xv', 'new_v_w_xo']
TWIN_LEAF_KINDS = {'loss': 'loss', 'grad_x': 'grad_x', 'grad_norm_mix_g': 'grad_w', 'grad_norm_x_g': 'grad_w', 'grad_norm_mem_g': 'grad_w', 'grad_final_norm_g': 'grad_w', 'grad_w_in_ab': 'grad_w', 'grad_rel_bias': 'grad_w', 'grad_conv_w': 'grad_w', 'grad_conv_b': 'grad_w', 'grad_conv_ln_g': 'grad_w', 'grad_conv_ln_b': 'grad_w', 'grad_w_out_ab': 'grad_w', 'grad_w_in_c': 'grad_w', 'grad_sgu_ln_g': 'grad_w', 'grad_sgu_ln_b': 'grad_w', 'grad_w_s': 'grad_w', 'grad_b_s': 'grad_w', 'grad_w_out_c': 'grad_w', 'grad_w_xq': 'grad_w', 'grad_w_xk': 'grad_w', 'grad_w_xv': 'grad_w', 'grad_w_xo': 'grad_w', 'delta_norm_mix_g': 'delta_w', 'delta_norm_x_g': 'delta_w', 'delta_norm_mem_g': 'delta_w', 'delta_final_norm_g': 'delta_w', 'delta_w_in_ab': 'delta_w', 'delta_rel_bias': 'delta_w', 'delta_conv_w': 'delta_w', 'delta_conv_b': 'delta_w', 'delta_conv_ln_g': 'delta_w', 'delta_conv_ln_b': 'delta_w', 'delta_w_out_ab': 'delta_w', 'delta_w_in_c': 'delta_w', 'delta_sgu_ln_g': 'delta_w', 'delta_sgu_ln_b': 'delta_w', 'delta_w_s': 'delta_w', 'delta_b_s': 'delta_w', 'delta_w_out_c': 'delta_w', 'delta_w_xq': 'delta_w', 'delta_w_xk': 'delta_w', 'delta_w_xv': 'delta_w', 'delta_w_xo': 'delta_w', 'new_m_norm_mix_g': 'new_m', 'new_m_norm_x_g': 'new_m', 'new_m_norm_mem_g': 'new_m', 'new_m_final_norm_g': 'new_m', 'new_m_w_in_ab': 'new_m', 'new_m_rel_bias': 'new_m', 'new_m_conv_w': 'new_m', 'new_m_conv_b': 'new_m', 'new_m_conv_ln_g': 'new_m', 'new_m_conv_ln_b': 'new_m', 'new_m_w_out_ab': 'new_m', 'new_m_w_in_c': 'new_m', 'new_m_sgu_ln_g': 'new_m', 'new_m_sgu_ln_b': 'new_m', 'new_m_w_s': 'new_m', 'new_m_b_s': 'new_m', 'new_m_w_out_c': 'new_m', 'new_m_w_xq': 'new_m', 'new_m_w_xk': 'new_m', 'new_m_w_xv': 'new_m', 'new_m_w_xo': 'new_m', 'new_v_norm_mix_g': 'new_v', 'new_v_norm_x_g': 'new_v', 'new_v_norm_mem_g': 'new_v', 'new_v_final_norm_g': 'new_v', 'new_v_w_in_ab': 'new_v', 'new_v_rel_bias': 'new_v', 'new_v_conv_w': 'new_v', 'new_v_conv_b': 'new_v', 'new_v_conv_ln_g': 'new_v', 'new_v_conv_ln_b': 'new_v', 'new_v_w_out_ab': 'new_v', 'new_v_w_in_c': 'new_v', 'new_v_sgu_ln_g': 'new_v', 'new_v_sgu_ln_b': 'new_v', 'new_v_w_s': 'new_v', 'new_v_b_s': 'new_v', 'new_v_w_out_c': 'new_v', 'new_v_w_xq': 'new_v', 'new_v_w_xk': 'new_v', 'new_v_w_xv': 'new_v', 'new_v_w_xo': 'new_v'}


def _forward(args):
    return _fwd_reference(*[args[k] for k in FWD_PARAMS])


def _output_shape():
    def fwd():
        inp = _fwd_setup_inputs(0)
        return _fwd_reference(*[inp[k] for k in FWD_PARAMS])
    out = _jax.eval_shape(fwd)
    return out.shape, out.dtype

N_MICROBATCH = 1
ADAM_LR = 0.001
ADAM_B1 = 0.9
ADAM_B2 = 0.999
ADAM_EPS = 1e-08
ADAM_WD = 0.01
ADAM_STEP = 10
PER_EXAMPLE_BATCH_AXIS = {'x': 0, 'mem': 0, 'loss_target': 0}
SHARED_INPUTS = []
_WEIGHT_DTYPES = {'norm_mix_g': _jnp.float32, 'norm_x_g': _jnp.float32, 'norm_mem_g': _jnp.float32, 'final_norm_g': _jnp.float32, 'w_in_ab': _jnp.float32, 'rel_bias': _jnp.float32, 'conv_w': _jnp.float32, 'conv_b': _jnp.float32, 'conv_ln_g': _jnp.float32, 'conv_ln_b': _jnp.float32, 'w_out_ab': _jnp.float32, 'w_in_c': _jnp.float32, 'sgu_ln_g': _jnp.float32, 'sgu_ln_b': _jnp.float32, 'w_s': _jnp.float32, 'b_s': _jnp.float32, 'w_out_c': _jnp.float32, 'w_xq': _jnp.float32, 'w_xk': _jnp.float32, 'w_xv': _jnp.float32, 'w_xo': _jnp.float32}
MOMENT_SCALE = {'norm_mix_g': 9.474001e-02, 'norm_x_g': 1.285731e-02, 'norm_mem_g': 1.869901e-02, 'final_norm_g': 3.196988e+01, 'w_in_ab': 2.291476e-02, 'rel_bias': 6.175574e-03, 'conv_w': 3.933932e-02, 'conv_b': 8.219835e-02, 'conv_ln_g': 4.881079e-02, 'conv_ln_b': 4.273737e-02, 'w_out_ab': 3.910203e-02, 'w_in_c': 4.877277e-02, 'sgu_ln_g': 3.487229e-02, 'sgu_ln_b': 3.494556e-02, 'w_s': 7.051922e-02, 'b_s': 8.174977e-02, 'w_out_c': 7.567465e-02, 'w_xq': 1.282404e-02, 'w_xk': 1.283166e-02, 'w_xv': 1.303985e-02, 'w_xo': 1.305811e-02}


def _to_microbatches(a, axis):
    t = _jnp.moveaxis(a, axis, 0)
    t = t.reshape((N_MICROBATCH, t.shape[0] // N_MICROBATCH) + t.shape[1:])
    return _jnp.moveaxis(t, 1, axis + 1)


def setup_inputs(seed: int = 0) -> dict:
    inp = _fwd_setup_inputs(seed)
    key = _jax.random.fold_in(_jax.random.key(seed), 7919)
    shape, _ = _output_shape()
    out = dict(inp)
    out["loss_target"] = _jax.random.normal(_jax.random.fold_in(key, 0), shape, _jnp.float32)
    for i, name in enumerate(TWIN_WEIGHTS):
        w = inp[name].astype(_jnp.float32)
        if MOMENT_SCALE is None:
            s = _jnp.sqrt(_jnp.mean(_jnp.square(w)) + 1e-30)
        else:
            s = MOMENT_SCALE[name]
        km, kv = _jax.random.split(_jax.random.fold_in(key, i + 1))
        out[name] = w
        out["m_" + name] = s * _jax.random.normal(km, w.shape, _jnp.float32)
        out["v_" + name] = (s * s) * _jax.random.uniform(kv, w.shape, _jnp.float32, 0.5, 1.5)
    if N_MICROBATCH > 1:
        for name, axis in PER_EXAMPLE_BATCH_AXIS.items():
            out[name] = _to_microbatches(out[name], axis)
    return {'x': out['x'], 'mem': out['mem'], 'norm_mix_g': out['norm_mix_g'], 'norm_x_g': out['norm_x_g'], 'norm_mem_g': out['norm_mem_g'], 'final_norm_g': out['final_norm_g'], 'w_in_ab': out['w_in_ab'], 'rel_bias': out['rel_bias'], 'conv_w': out['conv_w'], 'conv_b': out['conv_b'], 'conv_ln_g': out['conv_ln_g'], 'conv_ln_b': out['conv_ln_b'], 'w_out_ab': out['w_out_ab'], 'w_in_c': out['w_in_c'], 'sgu_ln_g': out['sgu_ln_g'], 'sgu_ln_b': out['sgu_ln_b'], 'w_s': out['w_s'], 'b_s': out['b_s'], 'w_out_c': out['w_out_c'], 'w_xq': out['w_xq'], 'w_xk': out['w_xk'], 'w_xv': out['w_xv'], 'w_xo': out['w_xo'], 'loss_target': out['loss_target'], 'm_norm_mix_g': out['m_norm_mix_g'], 'm_norm_x_g': out['m_norm_x_g'], 'm_norm_mem_g': out['m_norm_mem_g'], 'm_final_norm_g': out['m_final_norm_g'], 'm_w_in_ab': out['m_w_in_ab'], 'm_rel_bias': out['m_rel_bias'], 'm_conv_w': out['m_conv_w'], 'm_conv_b': out['m_conv_b'], 'm_conv_ln_g': out['m_conv_ln_g'], 'm_conv_ln_b': out['m_conv_ln_b'], 'm_w_out_ab': out['m_w_out_ab'], 'm_w_in_c': out['m_w_in_c'], 'm_sgu_ln_g': out['m_sgu_ln_g'], 'm_sgu_ln_b': out['m_sgu_ln_b'], 'm_w_s': out['m_w_s'], 'm_b_s': out['m_b_s'], 'm_w_out_c': out['m_w_out_c'], 'm_w_xq': out['m_w_xq'], 'm_w_xk': out['m_w_xk'], 'm_w_xv': out['m_w_xv'], 'm_w_xo': out['m_w_xo'], 'v_norm_mix_g': out['v_norm_mix_g'], 'v_norm_x_g': out['v_norm_x_g'], 'v_norm_mem_g': out['v_norm_mem_g'], 'v_final_norm_g': out['v_final_norm_g'], 'v_w_in_ab': out['v_w_in_ab'], 'v_rel_bias': out['v_rel_bias'], 'v_conv_w': out['v_conv_w'], 'v_conv_b': out['v_conv_b'], 'v_conv_ln_g': out['v_conv_ln_g'], 'v_conv_ln_b': out['v_conv_ln_b'], 'v_w_out_ab': out['v_w_out_ab'], 'v_w_in_c': out['v_w_in_c'], 'v_sgu_ln_g': out['v_sgu_ln_g'], 'v_sgu_ln_b': out['v_sgu_ln_b'], 'v_w_s': out['v_w_s'], 'v_b_s': out['v_b_s'], 'v_w_out_c': out['v_w_out_c'], 'v_w_xq': out['v_w_xq'], 'v_w_xk': out['v_w_xk'], 'v_w_xv': out['v_w_xv'], 'v_w_xo': out['v_w_xo']}


def _loss(weights, diff, rest, loss_target):
    with _jax.named_scope("forward"):
        args = {**rest, TWIN_DIFF_INPUT: diff, **{k: w.astype(_WEIGHT_DTYPES[k]) for k, w in weights.items()}}
        y = _forward(args)
    with _jax.named_scope("loss_head"):
        err = _jnp.square(y.astype(_jnp.float32) - loss_target)
        return 0.5 * _jnp.sum(_jnp.mean(err, axis=-1)) if err.ndim else 0.5 * err


def _adamw(w, g, m, v):
    m = ADAM_B1 * m + (1.0 - ADAM_B1) * g
    v = ADAM_B2 * v + (1.0 - ADAM_B2) * _jnp.square(g)
    m_hat = m / (1.0 - ADAM_B1 ** ADAM_STEP)
    v_hat = v / (1.0 - ADAM_B2 ** ADAM_STEP)
    delta = -ADAM_LR * (m_hat / (_jnp.sqrt(v_hat) + ADAM_EPS) + ADAM_WD * w)
    return delta, m, v


def reference(x, mem, norm_mix_g, norm_x_g, norm_mem_g, final_norm_g, w_in_ab, rel_bias, conv_w, conv_b, conv_ln_g, conv_ln_b, w_out_ab, w_in_c, sgu_ln_g, sgu_ln_b, w_s, b_s, w_out_c, w_xq, w_xk, w_xv, w_xo, loss_target, m_norm_mix_g, m_norm_x_g, m_norm_mem_g, m_final_norm_g, m_w_in_ab, m_rel_bias, m_conv_w, m_conv_b, m_conv_ln_g, m_conv_ln_b, m_w_out_ab, m_w_in_c, m_sgu_ln_g, m_sgu_ln_b, m_w_s, m_b_s, m_w_out_c, m_w_xq, m_w_xk, m_w_xv, m_w_xo, v_norm_mix_g, v_norm_x_g, v_norm_mem_g, v_final_norm_g, v_w_in_ab, v_rel_bias, v_conv_w, v_conv_b, v_conv_ln_g, v_conv_ln_b, v_w_out_ab, v_w_in_c, v_sgu_ln_g, v_sgu_ln_b, v_w_s, v_b_s, v_w_out_c, v_w_xq, v_w_xk, v_w_xv, v_w_xo):
    given = dict(x=x, mem=mem, norm_mix_g=norm_mix_g, norm_x_g=norm_x_g, norm_mem_g=norm_mem_g, final_norm_g=final_norm_g, w_in_ab=w_in_ab, rel_bias=rel_bias, conv_w=conv_w, conv_b=conv_b, conv_ln_g=conv_ln_g, conv_ln_b=conv_ln_b, w_out_ab=w_out_ab, w_in_c=w_in_c, sgu_ln_g=sgu_ln_g, sgu_ln_b=sgu_ln_b, w_s=w_s, b_s=b_s, w_out_c=w_out_c, w_xq=w_xq, w_xk=w_xk, w_xv=w_xv, w_xo=w_xo, loss_target=loss_target, m_norm_mix_g=m_norm_mix_g, m_norm_x_g=m_norm_x_g, m_norm_mem_g=m_norm_mem_g, m_final_norm_g=m_final_norm_g, m_w_in_ab=m_w_in_ab, m_rel_bias=m_rel_bias, m_conv_w=m_conv_w, m_conv_b=m_conv_b, m_conv_ln_g=m_conv_ln_g, m_conv_ln_b=m_conv_ln_b, m_w_out_ab=m_w_out_ab, m_w_in_c=m_w_in_c, m_sgu_ln_g=m_sgu_ln_g, m_sgu_ln_b=m_sgu_ln_b, m_w_s=m_w_s, m_b_s=m_b_s, m_w_out_c=m_w_out_c, m_w_xq=m_w_xq, m_w_xk=m_w_xk, m_w_xv=m_w_xv, m_w_xo=m_w_xo, v_norm_mix_g=v_norm_mix_g, v_norm_x_g=v_norm_x_g, v_norm_mem_g=v_norm_mem_g, v_final_norm_g=v_final_norm_g, v_w_in_ab=v_w_in_ab, v_rel_bias=v_rel_bias, v_conv_w=v_conv_w, v_conv_b=v_conv_b, v_conv_ln_g=v_conv_ln_g, v_conv_ln_b=v_conv_ln_b, v_w_out_ab=v_w_out_ab, v_w_in_c=v_w_in_c, v_sgu_ln_g=v_sgu_ln_g, v_sgu_ln_b=v_sgu_ln_b, v_w_s=v_w_s, v_b_s=v_b_s, v_w_out_c=v_w_out_c, v_w_xq=v_w_xq, v_w_xk=v_w_xk, v_w_xv=v_w_xv, v_w_xo=v_w_xo)
    weights = {n: given[n] for n in TWIN_WEIGHTS}
    shared = {n: given[n] for n in SHARED_INPUTS}
    per_example = {n: given[n] for n in ['x', 'mem']}
    grad_fn = _jax.value_and_grad(_loss, argnums=(0, 1))

    def one_microbatch(ex, loss_target):
        ex = dict(ex)
        diff = ex.pop(TWIN_DIFF_INPUT)
        return grad_fn(weights, diff, {**shared, **ex}, loss_target)

    if N_MICROBATCH == 1:
        loss, (grad_w, grad_x) = one_microbatch(per_example, given["loss_target"])
    else:
        def body(carry, xs):
            loss_sum, grad_sum = carry
            l_k, (gw_k, gx_k) = one_microbatch(xs[0], xs[1])
            with _jax.named_scope("update"):
                return (loss_sum + l_k, _jax.tree.map(_jnp.add, grad_sum, gw_k)), gx_k

        init = (_jnp.zeros((), _jnp.float32), _jax.tree.map(_jnp.zeros_like, weights))
        (loss, grad_w), grad_x = _jax.lax.scan(body, init, (per_example, given["loss_target"]))
    with _jax.named_scope("update"):
        delta_w, new_m, new_v = {}, {}, {}
        for n in TWIN_WEIGHTS:
            delta_w[n], new_m[n], new_v[n] = _adamw(weights[n], grad_w[n], given["m_" + n], given["v_" + n])
    return (loss, grad_x, *[grad_w[n] for n in TWIN_WEIGHTS], *[delta_w[n] for n in TWIN_WEIGHTS],
            *[new_m[n] for n in TWIN_WEIGHTS], *[new_v[n] for n in TWIN_WEIGHTS])
```

```python
import functools

import jax
import jax.numpy as jnp
from jax import lax
from jax.experimental import pallas as pl
from jax.experimental.pallas import tpu as pltpu

F32 = jnp.float32
BF16 = jnp.bfloat16
I32 = jnp.int32

N_DEV = 8
CHUNK = 64
N_PAST = 8
MAX_REL = 128
HEAD_A = 128
CONV_K = 31
GMLP_CHUNK = 128
N_GROUPS = 8
N_HEADS_X = 4
EPS = 1e-6
NEG = -1e30

ADAM_LR, ADAM_B1, ADAM_B2, ADAM_EPS, ADAM_WD, ADAM_STEP = 0.001, 0.9, 0.999, 1e-08, 0.01, 10

LANE = 128
SUBLANE = 8
VMEM_LIMIT = 56 * 1024 * 1024
QB = 4 * CHUNK
KW = QB + N_PAST * CHUNK
ROLL_W = 1024
REL_PAD = 384
HALO = 32
MESH_AXES = ("x", "y", "c")

NN = (((1,), (0,)), ((), ()))
NT = (((1,), (1,)), ((), ()))
TN = (((0,), (0,)), ((), ()))


def _cp(*sem):
    return pltpu.CompilerParams(dimension_semantics=sem, vmem_limit_bytes=VMEM_LIMIT)


def _tile(dim, pref):
    if dim <= pref:
        return dim
    t = (pref // LANE) * LANE
    while dim % t:
        t -= LANE
    return t


def _sigmoid(x):
    return 1.0 / (1.0 + jnp.exp(-x))


def _dsilu(x, s):
    return s * (1.0 + x * (1.0 - s))


def _dot(a, b, dims):
    return lax.dot_general(a, b, dims, preferred_element_type=F32)


def _matmul(name, a, b, *, dims, grid, a_spec, b_spec, out_sds, out_spec, acc_shape, add=None, add_spec=None):
    nk = grid[2]
    has_add = add is not None

    def body(*refs):
        a_ref, b_ref = refs[0], refs[1]
        o_ref = refs[3] if has_add else refs[2]
        part = _dot(a_ref[...].astype(BF16), b_ref[...].astype(BF16), dims)

        def finish(r):
            if has_add:
                r = r + refs[2][...]
            o_ref[...] = r.astype(o_ref.dtype)

        if nk == 1:
            finish(part)
        else:
            acc_ref = refs[-1]
            k = pl.program_id(2)

            @pl.when(k == 0)
            def _():
                acc_ref[...] = part

            @pl.when(k > 0)
            def _():
                acc_ref[...] += part

            @pl.when(k == nk - 1)
            def _():
                finish(acc_ref[...])

    in_specs = [a_spec, b_spec] + ([add_spec] if has_add else [])
    args = (a, b) + ((add,) if has_add else ())
    return pl.pallas_call(
        body, name=name, grid=grid, in_specs=in_specs, out_specs=out_spec, out_shape=out_sds,
        scratch_shapes=[pltpu.VMEM(acc_shape, F32)] if nk > 1 else [],
        compiler_params=_cp("parallel", "parallel", "arbitrary"),
    )(*args)


def _blk(per):
    return per if per <= 1024 else per // 2


def _mm_nn(name, a, b, out_dtype, *, add=None, tm=1024, tn=1024, tk=2048):
    M, K = a.shape
    tm, tk = _tile(M, tm), _tile(K, tk)
    if b.ndim == 3:
        per = b.shape[2]
        tn = _blk(per)
        q = per // tn
        N = N_DEV * per
        b_spec = pl.BlockSpec((None, tk, tn), lambda m, n, k: (n // q, k, n % q))
    else:
        N = b.shape[1]
        tn = _tile(N, tn)
        b_spec = pl.BlockSpec((tk, tn), lambda m, n, k: (k, n))
    return _matmul(
        name, a, b, dims=NN, grid=(M // tm, N // tn, K // tk),
        a_spec=pl.BlockSpec((tm, tk), lambda m, n, k: (m, k)), b_spec=b_spec,
        out_sds=jax.ShapeDtypeStruct((M, N), out_dtype), out_spec=pl.BlockSpec((tm, tn), lambda m, n, k: (m, n)),
        acc_shape=(tm, tn), add=add, add_spec=pl.BlockSpec((tm, tn), lambda m, n, k: (m, n)))


def _mm_nt(name, a, b, out_dtype, *, add=None, tm=1024, tn=1024, tk=2048):
    M, K = a.shape
    tm = _tile(M, tm)
    if b.ndim == 3:
        N, per = b.shape[1], b.shape[2]
        tk = _blk(per)
        q = per // tk
        tn = _tile(N, tn)
        b_spec = pl.BlockSpec((None, tn, tk), lambda m, n, k: (k // q, n, k % q))
    else:
        N = b.shape[0]
        tn, tk = _tile(N, tn), _tile(K, tk)
        b_spec = pl.BlockSpec((tn, tk), lambda m, n, k: (n, k))
    return _matmul(
        name, a, b, dims=NT, grid=(M // tm, N // tn, K // tk),
        a_spec=pl.BlockSpec((tm, tk), lambda m, n, k: (m, k)), b_spec=b_spec,
        out_sds=jax.ShapeDtypeStruct((M, N), out_dtype), out_spec=pl.BlockSpec((tm, tn), lambda m, n, k: (m, n)),
        acc_shape=(tm, tn), add=add, add_spec=pl.BlockSpec((tm, tn), lambda m, n, k: (m, n)))


def _mm_tn(name, a, b, out_dtype, *, per=None, tm=1024, tn=1024, tk=2048):
    K, M = a.shape
    N = b.shape[1]
    tm, tk = _tile(M, tm), _tile(K, tk)
    if per is not None:
        tn = _blk(per)
        q = per // tn
        out_sds = jax.ShapeDtypeStruct((N_DEV, M, per), out_dtype)
        out_spec = pl.BlockSpec((None, tm, tn), lambda m, n, k: (n // q, m, n % q))
    else:
        tn = _tile(N, tn)
        out_sds = jax.ShapeDtypeStruct((M, N), out_dtype)
        out_spec = pl.BlockSpec((tm, tn), lambda m, n, k: (m, n))
    return _matmul(
        name, a, b, dims=TN, grid=(M // tm, N // tn, K // tk),
        a_spec=pl.BlockSpec((tk, tm), lambda m, n, k: (k, m)), b_spec=pl.BlockSpec((tk, tn), lambda m, n, k: (k, n)),
        out_sds=out_sds, out_spec=out_spec, acc_shape=(tm, tn))


def _rms_fwd(name, h, g):
    T, D = h.shape
    tb = _tile(T, 512)

    def body(h_ref, g_ref, o_ref):
        x = h_ref[...]
        r = lax.rsqrt(jnp.mean(x * x, axis=-1, keepdims=True) + EPS)
        o_ref[...] = (x * r * g_ref[...]).astype(o_ref.dtype)

    return pl.pallas_call(
        body, name=name, grid=(T // tb,),
        in_specs=[pl.BlockSpec((tb, D), lambda i: (i, 0)), pl.BlockSpec((1, D), lambda i: (0, 0))],
        out_specs=pl.BlockSpec((tb, D), lambda i: (i, 0)), out_shape=jax.ShapeDtypeStruct((T, D), BF16),
        compiler_params=_cp("parallel"))(h, g)


def _rms_bwd(name, dhn, h, g, dres):
    T, D = h.shape
    tb = _tile(T, 256)

    def body(dhn_ref, h_ref, g_ref, dres_ref, dh_ref, dg_ref):
        i = pl.program_id(0)
        x = h_ref[...]
        r = lax.rsqrt(jnp.mean(x * x, axis=-1, keepdims=True) + EPS)
        y = x * r
        d = dhn_ref[...]
        dy = d * g_ref[...]
        dh_ref[...] = dres_ref[...] + r * (dy - y * jnp.mean(dy * y, axis=-1, keepdims=True))
        part = jnp.sum(d * y, axis=0, keepdims=True)

        @pl.when(i == 0)
        def _():
            dg_ref[...] = part

        @pl.when(i > 0)
        def _():
            dg_ref[...] += part

    row = pl.BlockSpec((tb, D), lambda i: (i, 0))
    vec = pl.BlockSpec((1, D), lambda i: (0, 0))
    return pl.pallas_call(
        body, name=name, grid=(T // tb,), in_specs=[row, row, vec, row], out_specs=[row, vec],
        out_shape=[jax.ShapeDtypeStruct((T, D), F32), jax.ShapeDtypeStruct((1, D), F32)],
        compiler_params=_cp("arbitrary"))(dhn, h, g, dres)


def _loss_head(h, tgt, g):
    T, D = h.shape
    tb = _tile(T, 256)

    def body(h_ref, t_ref, g_ref, loss_ref, dh_ref, dg_ref):
        i = pl.program_id(0)
        x = h_ref[...]
        gg = g_ref[...]
        r = lax.rsqrt(jnp.mean(x * x, axis=-1, keepdims=True) + EPS)
        y0 = x * r
        err = y0 * gg - t_ref[...]
        tot = 0.5 * jnp.sum(jnp.mean(err * err, axis=-1, keepdims=True), axis=0, keepdims=True)
        dy = err * (1.0 / D)
        dyg = dy * gg
        dh_ref[...] = r * (dyg - y0 * jnp.mean(dyg * y0, axis=-1, keepdims=True))
        part = jnp.sum(dy * y0, axis=0, keepdims=True)
        tot = jnp.broadcast_to(tot, loss_ref.shape)

        @pl.when(i == 0)
        def _():
            dg_ref[...] = part
            loss_ref[...] = tot

        @pl.when(i > 0)
        def _():
            dg_ref[...] += part
            loss_ref[...] += tot

    row = pl.BlockSpec((tb, D), lambda i: (i, 0))
    vec = pl.BlockSpec((1, D), lambda i: (0, 0))
    return pl.pallas_call(
        body, name="loss_head", grid=(T // tb,), in_specs=[row, row, vec],
        out_specs=[pl.BlockSpec((SUBLANE, LANE), lambda i: (0, 0)), row, vec],
        out_shape=[jax.ShapeDtypeStruct((SUBLANE, LANE), F32), jax.ShapeDtypeStruct((T, D), F32),
                   jax.ShapeDtypeStruct((1, D), F32)],
        compiler_params=_cp("arbitrary"))(h, tgt, g)


def _rel_onehot(pos_axis, shape):
    pos = lax.broadcasted_iota(I32, shape, pos_axis)
    r = lax.broadcasted_iota(I32, shape, 1 - pos_axis)
    d = jnp.where(pos < KW, N_PAST * CHUNK - pos, N_PAST * CHUNK + ROLL_W - pos)
    return (jnp.clip(d, -MAX_REL, MAX_REL) + MAX_REL == r).astype(F32)


def _roll_rows(x, left):
    row = lax.broadcasted_iota(I32, x.shape, 0)
    for b in range(QB.bit_length() - 1):
        shift = (ROLL_W - (1 << b)) if left else (1 << b)
        x = jnp.where(((row >> b) & 1) == 1, pltpu.roll(x, shift, 1), x)
    return x


def _bias_tile(rel_bias):
    H = rel_bias.shape[0]
    rb = jnp.pad(rel_bias, ((0, 0), (0, REL_PAD - rel_bias.shape[1]))).reshape(H, 1, REL_PAD)

    def body(rb_ref, o_ref):
        row = jnp.broadcast_to(rb_ref[...], (SUBLANE, REL_PAD))
        base = jnp.dot(row, _rel_onehot(1, (REL_PAD, ROLL_W)), precision=lax.Precision.HIGHEST,
                       preferred_element_type=F32)[0:1]
        tile = _roll_rows(jnp.broadcast_to(base, (QB, ROLL_W)), left=False)[:, :KW]
        qc = lax.broadcasted_iota(I32, (QB, KW), 0) // CHUNK
        kc = lax.broadcasted_iota(I32, (QB, KW), 1) // CHUNK - N_PAST
        o_ref[...] = jnp.where((kc >= qc - N_PAST) & (kc <= qc), tile, NEG)

    return pl.pallas_call(
        body, name="bias_tile", grid=(H,),
        in_specs=[pl.BlockSpec((None, 1, REL_PAD), lambda h: (h, 0, 0))],
        out_specs=pl.BlockSpec((None, QB, KW), lambda h: (h, 0, 0)),
        out_shape=jax.ShapeDtypeStruct((H, QB, KW), F32), compiler_params=_cp("parallel"))(rb)


def _bias_tile_grad(dtile, n_rel):
    H = dtile.shape[0]

    def body(dt_ref, o_ref):
        x = jnp.concatenate([dt_ref[...], jnp.zeros((QB, ROLL_W - KW), F32)], axis=1)
        cs = jnp.sum(_roll_rows(x, left=True), axis=0, keepdims=True)
        o_ref[...] = jnp.dot(jnp.broadcast_to(cs, (SUBLANE, ROLL_W)), _rel_onehot(0, (ROLL_W, REL_PAD)),
                             precision=lax.Precision.HIGHEST, preferred_element_type=F32)[0:1]

    out = pl.pallas_call(
        body, name="bias_tile_grad", grid=(H,),
        in_specs=[pl.BlockSpec((None, QB, KW), lambda h: (h, 0, 0))],
        out_specs=pl.BlockSpec((None, 1, REL_PAD), lambda h: (h, 0, 0)),
        out_shape=jax.ShapeDtypeStruct((H, 1, REL_PAD), F32), compiler_params=_cp("parallel"))(dtile)
    return out.reshape(H, REL_PAD)[:, :n_rel]


def _band_specs(H):
    spec = lambda f: pl.BlockSpec((QB, HEAD_A), f)
    q = spec(lambda h, i: (i, h))
    ks = [spec(lambda h, i, r=r: (jnp.maximum(i - 2 + r, 0), H + h)) for r in range(3)]
    vs = [spec(lambda h, i, r=r: (jnp.maximum(i - 2 + r, 0), 2 * H + h)) for r in range(3)]
    gate = spec(lambda h, i: (i, 5 * H + h))
    bias = pl.BlockSpec((None, QB, KW), lambda h, i: (h, 0, 0))
    return [q] + ks + vs + [gate, bias]


def _band_probs(i, q_ref, k_refs, v_refs, bias_ref):
    q = q_ref[...].astype(BF16)
    k = jnp.concatenate([r[...] for r in k_refs], axis=0).astype(BF16)
    v = jnp.concatenate([r[...] for r in v_refs], axis=0).astype(BF16)
    s = _dot(q, k, NT) * (HEAD_A ** -0.5) + bias_ref[...]
    kpos = (i - 2) * QB + lax.broadcasted_iota(I32, (1, KW), 1)
    s = jnp.where(kpos >= 0, s, NEG)
    e = jnp.exp(s - jnp.max(s, axis=-1, keepdims=True))
    p = e / jnp.sum(e, axis=-1, keepdims=True)
    return p, q, k, v


def _band_attn_fwd(proj, bias_tile):
    T = proj.shape[0]
    H = bias_tile.shape[0]

    def body(q_ref, k0, k1, k2, v0, v1, v2, gate_ref, bias_ref, ya_ref, y_ref):
        p, _, _, v = _band_probs(pl.program_id(1), q_ref, (k0, k1, k2), (v0, v1, v2), bias_ref)
        o = _dot(p.astype(BF16), v, NN)
        g = gate_ref[...]
        ya_ref[...] = o
        y_ref[...] = (o * (g * _sigmoid(g))).astype(y_ref.dtype)

    out = pl.BlockSpec((QB, HEAD_A), lambda h, i: (i, h))
    return pl.pallas_call(
        body, name="band_attn_fwd", grid=(H, T // QB), in_specs=_band_specs(H), out_specs=[out, out],
        out_shape=[jax.ShapeDtypeStruct((T, H * HEAD_A), F32), jax.ShapeDtypeStruct((T, 2 * H * HEAD_A), BF16)],
        compiler_params=_cp("parallel", "parallel"))(*([proj] * 8), bias_tile)


def _band_attn_bwd(proj, bias_tile, dy, ya):
    T = proj.shape[0]
    H = bias_tile.shape[0]
    n_i = T // QB

    def body(q_ref, k0, k1, k2, v0, v1, v2, gate_ref, bias_ref, dy_ref, ya_ref,
             dq_ref, dk_ref, dv_ref, dgate_ref, dbias_ref, dk_acc, dv_acc):
        i = pl.program_id(1)

        @pl.when(i == 0)
        def _():
            dk_acc[...] = jnp.zeros_like(dk_acc)
            dv_acc[...] = jnp.zeros_like(dv_acc)
            dbias_ref[...] = jnp.zeros_like(dbias_ref)

        p, q, k, v = _band_probs(i, q_ref, (k0, k1, k2), (v0, v1, v2), bias_ref)
        g = gate_ref[...]
        sg = _sigmoid(g)
        dyv = dy_ref[...]
        dgate_ref[...] = (dyv * ya_ref[...] * _dsilu(g, sg)).astype(dgate_ref.dtype)
        do = (dyv * (g * sg)).astype(BF16)
        dp = _dot(do, v, NT)
        ds = p * (dp - jnp.sum(dp * p, axis=-1, keepdims=True))
        dbias_ref[...] += ds
        dsb = (ds * (HEAD_A ** -0.5)).astype(BF16)
        dq_ref[...] = _dot(dsb, k, NN).astype(dq_ref.dtype)
        dkc = _dot(dsb, q, TN)
        dvc = _dot(p.astype(BF16), do, TN)
        for r in range(3):
            blk = i - 2 + r

            @pl.when(blk >= 0)
            def _(r=r, blk=blk):
                rows = pl.ds(pl.multiple_of(blk * QB, QB), QB)
                dk_acc[rows, :] += dkc[r * QB:(r + 1) * QB]
                dv_acc[rows, :] += dvc[r * QB:(r + 1) * QB]

        @pl.when(i == n_i - 1)
        def _():
            dk_ref[...] = dk_acc[...].astype(dk_ref.dtype)
            dv_ref[...] = dv_acc[...].astype(dv_ref.dtype)

    blk = pl.BlockSpec((QB, HEAD_A), lambda h, i: (i, h))
    col = pl.BlockSpec((T, HEAD_A), lambda h, i: (0, h))
    sds = jax.ShapeDtypeStruct((T, H * HEAD_A), BF16)
    return pl.pallas_call(
        body, name="band_attn_bwd", grid=(H, n_i),
        in_specs=_band_specs(H) + [blk, blk],
        out_specs=[blk, col, col, blk, pl.BlockSpec((None, QB, KW), lambda h, i: (h, 0, 0))],
        out_shape=[sds, sds, sds, sds, jax.ShapeDtypeStruct((H, QB, KW), F32)],
        scratch_shapes=[pltpu.VMEM((T, HEAD_A), F32), pltpu.VMEM((T, HEAD_A), F32)],
        compiler_params=_cp("parallel", "arbitrary"))(*([proj] * 8), bias_tile, dy, ya)


def _conv_in_specs(tb, C):
    per = tb // HALO
    return [pl.BlockSpec((tb, C), lambda i: (i, 3)), pl.BlockSpec((tb, C), lambda i: (i, 4)),
            pl.BlockSpec((HALO, C), lambda i: (jnp.maximum(i * per - 1, 0), 3)),
            pl.BlockSpec((HALO, C), lambda i: (jnp.maximum(i * per - 1, 0), 4))]


def _glu_with_halo(i, ga_ref, gb_ref, ha_ref, hb_ref, scr):
    halo = ha_ref[...] * _sigmoid(hb_ref[...])
    scr[0:HALO, :] = jnp.where(i > 0, halo, 0.0)
    scr[HALO:, :] = ga_ref[...] * _sigmoid(gb_ref[...])


def _conv_taps(scr, w_ref, tb, first):
    z = None
    for j in range(CONV_K):
        t = w_ref[j:j + 1, :] * scr[pl.ds(first(j), tb), :]
        z = t if z is None else z + t
    return z


def _layernorm_stats(z):
    mu = jnp.mean(z, axis=-1, keepdims=True)
    zc = z - mu
    rstd = lax.rsqrt(jnp.mean(zc * zc, axis=-1, keepdims=True) + EPS)
    return zc * rstd, rstd


def _conv_fwd(proj, y, conv_w, conv_b, ln_g, ln_b):
    T = proj.shape[0]
    C = conv_w.shape[1]
    tb = _tile(T, 256)

    def body(ga_ref, gb_ref, ha_ref, hb_ref, gate_ref, w_ref, cb_ref, g_ref, b_ref, y_in, y_ref, scr):
        _glu_with_halo(pl.program_id(0), ga_ref, gb_ref, ha_ref, hb_ref, scr)
        z = _conv_taps(scr, w_ref, tb, lambda j: HALO - (CONV_K - 1) + j) + cb_ref[...]
        xhat, _ = _layernorm_stats(z)
        ln = xhat * g_ref[...] + b_ref[...]
        gate = gate_ref[...]
        y_ref[...] = (ln * _sigmoid(ln) * (gate * _sigmoid(gate))).astype(y_ref.dtype)

    vec = pl.BlockSpec((1, C), lambda i: (0, 0))
    return pl.pallas_call(
        body, name="conv_fwd", grid=(T // tb,),
        in_specs=_conv_in_specs(tb, C) + [pl.BlockSpec((tb, C), lambda i: (i, 6)),
                                          pl.BlockSpec((HALO, C), lambda i: (0, 0)), vec, vec, vec,
                                          pl.BlockSpec(memory_space=pl.ANY)],
        out_specs=pl.BlockSpec((tb, C), lambda i: (i, 1)), out_shape=jax.ShapeDtypeStruct(y.shape, y.dtype),
        scratch_shapes=[pltpu.VMEM((HALO + tb, C), F32)], input_output_aliases={9: 0},
        compiler_params=_cp("parallel"))(proj, proj, proj, proj, proj, conv_w, conv_b, ln_g, ln_b, y)


def _conv_bwd_ln(proj, dy, conv_w, conv_b, ln_g, ln_b):
    T = proj.shape[0]
    C = conv_w.shape[1]
    tb = _tile(T, 256)

    def body(ga_ref, gb_ref, ha_ref, hb_ref, gate_ref, dy_ref, w_ref, cb_ref, g_ref, b_ref,
             dz_ref, dgate_ref, dg_ref, db_ref, dcb_ref, scr):
        i = pl.program_id(0)
        _glu_with_halo(i, ga_ref, gb_ref, ha_ref, hb_ref, scr)
        z = _conv_taps(scr, w_ref, tb, lambda j: HALO - (CONV_K - 1) + j) + cb_ref[...]
        xhat, rstd = _layernorm_stats(z)
        ln = xhat * g_ref[...] + b_ref[...]
        sl = _sigmoid(ln)
        gate = gate_ref[...]
        sg = _sigmoid(gate)
        dyv = dy_ref[...]
        dgate_ref[...] = (dyv * (ln * sl) * _dsilu(gate, sg)).astype(dgate_ref.dtype)
        dln = dyv * (gate * sg) * _dsilu(ln, sl)
        dxh = dln * g_ref[...]
        dz = rstd * (dxh - jnp.mean(dxh, axis=-1, keepdims=True) - xhat * jnp.mean(dxh * xhat, axis=-1, keepdims=True))
        dz_ref[...] = dz
        parts = (jnp.sum(dln * xhat, axis=0, keepdims=True), jnp.sum(dln, axis=0, keepdims=True),
                 jnp.sum(dz, axis=0, keepdims=True))

        @pl.when(i == 0)
        def _():
            for ref, part in zip((dg_ref, db_ref, dcb_ref), parts):
                ref[...] = part

        @pl.when(i > 0)
        def _():
            for ref, part in zip((dg_ref, db_ref, dcb_ref), parts):
                ref[...] += part

    vec = pl.BlockSpec((1, C), lambda i: (0, 0))
    row = pl.BlockSpec((tb, C), lambda i: (i, 0))
    vsd = jax.ShapeDtypeStruct((1, C), F32)
    return pl.pallas_call(
        body, name="conv_bwd_ln", grid=(T // tb,),
        in_specs=_conv_in_specs(tb, C) + [pl.BlockSpec((tb, C), lambda i: (i, 6)), pl.BlockSpec((tb, C), lambda i: (i, 1)),
                                          pl.BlockSpec((HALO, C), lambda i: (0, 0)), vec, vec, vec],
        out_specs=[row, row, vec, vec, vec],
        out_shape=[jax.ShapeDtypeStruct((T, C), F32), jax.ShapeDtypeStruct((T, C), BF16), vsd, vsd, vsd],
        scratch_shapes=[pltpu.VMEM((HALO + tb, C), F32)],
        compiler_params=_cp("arbitrary"))(proj, proj, proj, proj, proj, dy, conv_w, conv_b, ln_g, ln_b)


def _conv_bwd_taps(proj, dz, conv_w):
    T = proj.shape[0]
    C = conv_w.shape[1]
    tb = _tile(T, 256)
    per = tb // HALO
    n_i = T // tb

    def body(ga_ref, gb_ref, ha_ref, hb_ref, dz_ref, dzn_ref, w_ref, da_ref, db_ref, dw_ref, scr, dscr):
        i = pl.program_id(0)
        _glu_with_halo(i, ga_ref, gb_ref, ha_ref, hb_ref, scr)
        dz = dz_ref[...]
        dscr[0:tb, :] = dz
        dscr[tb:, :] = jnp.where(i < n_i - 1, dzn_ref[...], 0.0)
        dglu = _conv_taps(dscr, w_ref, tb, lambda j: CONV_K - 1 - j)
        ga = ga_ref[...]
        sb = _sigmoid(gb_ref[...])
        da_ref[...] = (dglu * sb).astype(da_ref.dtype)
        db_ref[...] = (dglu * ga * sb * (1.0 - sb)).astype(db_ref.dtype)

        @pl.when(i == 0)
        def _():
            dw_ref[...] = jnp.zeros_like(dw_ref)

        for j in range(CONV_K):
            dw_ref[j:j + 1, :] += jnp.sum(dz * scr[pl.ds(HALO - (CONV_K - 1) + j, tb), :], axis=0, keepdims=True)

    row = pl.BlockSpec((tb, C), lambda i: (i, 0))
    wspec = pl.BlockSpec((HALO, C), lambda i: (0, 0))
    return pl.pallas_call(
        body, name="conv_bwd_taps", grid=(n_i,),
        in_specs=_conv_in_specs(tb, C) + [row, pl.BlockSpec((HALO, C), lambda i: (jnp.minimum((i + 1) * per, T // HALO - 1), 0)),
                                          wspec],
        out_specs=[row, row, wspec],
        out_shape=[jax.ShapeDtypeStruct((T, C), BF16), jax.ShapeDtypeStruct((T, C), BF16),
                   jax.ShapeDtypeStruct((HALO, C), F32)],
        scratch_shapes=[pltpu.VMEM((HALO + tb, C), F32), pltpu.VMEM((tb + HALO, C), F32)],
        compiler_params=_cp("arbitrary"))(proj, proj, proj, proj, dz, dz, conv_w)


def _sgu_mask():
    r = lax.broadcasted_iota(I32, (GMLP_CHUNK, GMLP_CHUNK), 0) // CHUNK
    c = lax.broadcasted_iota(I32, (GMLP_CHUNK, GMLP_CHUNK), 1) // CHUNK
    return r >= c


def _sgu_fwd(proj, ln_g, ln_b, w_s, b_s_t):
    T = proj.shape[0]
    W = ln_g.shape[1]
    G = w_s.shape[0]
    cg = W // G
    tb = GMLP_CHUNK

    def body(u_ref, v_ref, gate_ref, g_ref, b_ref, ws_ref, bs_ref, y_ref):
        xhat, _ = _layernorm_stats(v_ref[...])
        vln = (xhat * g_ref[...] + b_ref[...]).astype(BF16)
        mask = _sgu_mask()
        for gi in range(G):
            cols = slice(gi * cg, (gi + 1) * cg)
            ws = jnp.where(mask, ws_ref[gi], 0.0).astype(BF16)
            sg = _dot(ws, vln[:, cols], NN) + bs_ref[:, gi:gi + 1]
            gate = gate_ref[:, cols]
            y_ref[:, cols] = (u_ref[:, cols] * sg * (gate * _sigmoid(gate))).astype(y_ref.dtype)

    vec = pl.BlockSpec((1, W), lambda i: (0, 0))
    return pl.pallas_call(
        body, name="sgu_fwd", grid=(T // tb,),
        in_specs=[pl.BlockSpec((tb, W), lambda i: (i, 0)), pl.BlockSpec((tb, W), lambda i: (i, 1)),
                  pl.BlockSpec((tb, W), lambda i: (i, 2)), vec, vec,
                  pl.BlockSpec((G, GMLP_CHUNK, GMLP_CHUNK), lambda i: (0, 0, 0)),
                  pl.BlockSpec((GMLP_CHUNK, G), lambda i: (0, 0))],
        out_specs=pl.BlockSpec((tb, W), lambda i: (i, 0)), out_shape=jax.ShapeDtypeStruct((T, W), BF16),
        compiler_params=_cp("parallel"))(proj, proj, proj, ln_g, ln_b, w_s, b_s_t)


def _sgu_bwd(proj, dy, ln_g, ln_b, w_s, b_s_t):
    T = proj.shape[0]
    W = ln_g.shape[1]
    G = w_s.shape[0]
    cg = W // G
    tb = GMLP_CHUNK

    def body(u_ref, v_ref, gate_ref, dy_ref, g_ref, b_ref, ws_ref, bs_ref,
             dp_ref, dws_ref, dbs_ref, dg_ref, db_ref, dvln_scr):
        i = pl.program_id(0)

        @pl.when(i == 0)
        def _():
            dws_ref[...] = jnp.zeros_like(dws_ref)
            dbs_ref[...] = jnp.zeros_like(dbs_ref)
            dg_ref[...] = jnp.zeros_like(dg_ref)
            db_ref[...] = jnp.zeros_like(db_ref)

        xhat, rstd = _layernorm_stats(v_ref[...])
        vln = (xhat * g_ref[...] + b_ref[...]).astype(BF16)
        mask = _sgu_mask()
        for gi in range(G):
            cols = slice(gi * cg, (gi + 1) * cg)
            ws = jnp.where(mask, ws_ref[gi], 0.0).astype(BF16)
            vg = vln[:, cols]
            sg = _dot(ws, vg, NN) + bs_ref[:, gi:gi + 1]
            gate = gate_ref[:, cols]
            s = _sigmoid(gate)
            u = u_ref[:, cols]
            dyv = dy_ref[:, cols]
            dyu = dyv * u
            dp_ref[:, cols] = (dyv * sg * (gate * s)).astype(dp_ref.dtype)
            dp_ref[:, 2 * W + gi * cg:2 * W + (gi + 1) * cg] = (dyu * sg * _dsilu(gate, s)).astype(dp_ref.dtype)
            dsg = dyu * (gate * s)
            dsgb = dsg.astype(BF16)
            dvln_scr[:, cols] = _dot(ws, dsgb, TN)
            dws_ref[gi] += _dot(dsgb, vg, NT)
            dbs_ref[:, gi:gi + 1] += jnp.sum(dsg, axis=-1, keepdims=True)
        dvln = dvln_scr[...]
        dg_ref[...] += jnp.sum(dvln * xhat, axis=0, keepdims=True)
        db_ref[...] += jnp.sum(dvln, axis=0, keepdims=True)
        dxh = dvln * g_ref[...]
        dv = rstd * (dxh - jnp.mean(dxh, axis=-1, keepdims=True) - xhat * jnp.mean(dxh * xhat, axis=-1, keepdims=True))
        dp_ref[:, W:2 * W] = dv.astype(dp_ref.dtype)

    vec = pl.BlockSpec((1, W), lambda i: (0, 0))
    wsp = pl.BlockSpec((G, GMLP_CHUNK, GMLP_CHUNK), lambda i: (0, 0, 0))
    bsp = pl.BlockSpec((GMLP_CHUNK, G), lambda i: (0, 0))
    return pl.pallas_call(
        body, name="sgu_bwd", grid=(T // tb,),
        in_specs=[pl.BlockSpec((tb, W), lambda i: (i, 0)), pl.BlockSpec((tb, W), lambda i: (i, 1)),
                  pl.BlockSpec((tb, W), lambda i: (i, 2)), pl.BlockSpec((tb, W), lambda i: (i, 0)), vec, vec, wsp, bsp],
        out_specs=[pl.BlockSpec((tb, 3 * W), lambda i: (i, 0)), wsp, bsp, vec, vec],
        out_shape=[jax.ShapeDtypeStruct((T, 3 * W), BF16), jax.ShapeDtypeStruct((G, GMLP_CHUNK, GMLP_CHUNK), F32),
                   jax.ShapeDtypeStruct((GMLP_CHUNK, G), F32), jax.ShapeDtypeStruct((1, W), F32),
                   jax.ShapeDtypeStruct((1, W), F32)],
        scratch_shapes=[pltpu.VMEM((tb, W), F32)],
        compiler_params=_cp("arbitrary"))(proj, proj, proj, dy, ln_g, ln_b, w_s, b_s_t)


def _xattn_probs(q, k, hd):
    s = _dot(q, k, NT) * (hd ** -0.5)
    e = jnp.exp(s - jnp.max(s, axis=-1, keepdims=True))
    return e / jnp.sum(e, axis=-1, keepdims=True)


def _xattn_fwd(name, q, k, v):
    T, D = q.shape
    M = k.shape[0]
    hd = D // N_HEADS_X
    tb = _tile(T, 512)

    def body(q_ref, k_ref, v_ref, o_ref):
        for h in range(N_HEADS_X):
            cols = slice(h * hd, (h + 1) * hd)
            p = _xattn_probs(q_ref[:, cols], k_ref[:, cols], hd)
            o_ref[:, cols] = _dot(p.astype(BF16), v_ref[:, cols], NN).astype(o_ref.dtype)

    row = pl.BlockSpec((tb, D), lambda i: (i, 0))
    kv = pl.BlockSpec((M, D), lambda i: (0, 0))
    return pl.pallas_call(
        body, name=name, grid=(T // tb,), in_specs=[row, kv, kv], out_specs=row,
        out_shape=jax.ShapeDtypeStruct((T, D), BF16), compiler_params=_cp("parallel"))(q, k, v)


def _xattn_bwd(name, q, k, v, do):
    T, D = q.shape
    M = k.shape[0]
    hd = D // N_HEADS_X
    tb = _tile(T, 512)

    def body(q_ref, k_ref, v_ref, do_ref, dq_ref, dk_ref, dv_ref):
        @pl.when(pl.program_id(0) == 0)
        def _():
            dk_ref[...] = jnp.zeros_like(dk_ref)
            dv_ref[...] = jnp.zeros_like(dv_ref)

        for h in range(N_HEADS_X):
            cols = slice(h * hd, (h + 1) * hd)
            qh, kh, doh = q_ref[:, cols], k_ref[:, cols], do_ref[:, cols]
            p = _xattn_probs(qh, kh, hd)
            dp = _dot(doh, v_ref[:, cols], NT)
            ds = p * (dp - jnp.sum(dp * p, axis=-1, keepdims=True))
            dsb = (ds * (hd ** -0.5)).astype(BF16)
            dq_ref[:, cols] = _dot(dsb, kh, NN).astype(dq_ref.dtype)
            dk_ref[:, cols] += _dot(dsb, qh, TN)
            dv_ref[:, cols] += _dot(p.astype(BF16), doh, TN)

    row = pl.BlockSpec((tb, D), lambda i: (i, 0))
    kv = pl.BlockSpec((M, D), lambda i: (0, 0))
    return pl.pallas_call(
        body, name=name, grid=(T // tb,), in_specs=[row, kv, kv, row], out_specs=[row, kv, kv],
        out_shape=[jax.ShapeDtypeStruct((T, D), BF16), jax.ShapeDtypeStruct((M, D), F32),
                   jax.ShapeDtypeStruct((M, D), F32)],
        compiler_params=_cp("arbitrary"))(q, k, v, do)


def _mesh_pos():
    return lax.axis_index("x"), lax.axis_index("y"), lax.axis_index("c")


def _lin(x, y, c):
    return 4 * x + 2 * y + c


def _remote(src, dst, send_sem, recv_sem, to):
    return pltpu.make_async_remote_copy(src_ref=src, dst_ref=dst, send_sem=send_sem, recv_sem=recv_sem,
                                        device_id=to, device_id_type=pl.DeviceIdType.MESH)


def _all_gather(arrs):
    n = len(arrs)

    def body(*refs):
        ins, outs = refs[:n], refs[n:2 * n]
        send, recv, local = refs[2 * n:]
        x, y, c = _mesh_pos()
        me, sib = (x, y, c), (x, y, 1 - c)
        chips = [(1 - x, y), (x, 1 - y), (1 - x, 1 - y)]

        def copy(a, k, block, to, src=None):
            rows = outs[a].at[_lin(*block)]
            return _remote(rows if src is None else src, rows, send.at[a, k], recv.at[a, k], to)

        mine = [pltpu.make_async_copy(ins[a], outs[a].at[_lin(*me)], local.at[a]) for a in range(n)]
        for cp in mine:
            cp.start()
        first = []
        for a in range(n):
            first.append(copy(a, 0, me, sib, src=ins[a]))
            first += [copy(a, 1 + j, me, (*chip, c), src=ins[a]) for j, chip in enumerate(chips)]
        for cp in first:
            cp.start()
        passed = []
        for j, chip in enumerate(chips):
            for a in range(n):
                copy(a, 1 + j, (*chip, c), me).wait_recv()
                fwd = copy(a, 4 + j, (*chip, c), sib)
                fwd.start()
                passed.append(fwd)
        for a in range(n):
            copy(a, 0, sib, me).wait_recv()
        for j, chip in enumerate(chips):
            for a in range(n):
                copy(a, 4 + j, (*chip, 1 - c), me).wait_recv()
        for cp in first + passed:
            cp.wait_send()
        for cp in mine:
            cp.wait()

    hbm = pl.BlockSpec(memory_space=pl.ANY)
    return pl.pallas_call(
        body, name="all_gather", in_specs=[hbm] * n, out_specs=[hbm] * n,
        out_shape=[jax.ShapeDtypeStruct((N_DEV,) + a.shape, a.dtype) for a in arrs],
        scratch_shapes=[pltpu.SemaphoreType.DMA((n, 7)), pltpu.SemaphoreType.DMA((n, 7)), pltpu.SemaphoreType.DMA((n,))],
    )(*arrs)


def _exchange(name, arrs, bcast):
    n = len(arrs)

    def body(*refs):
        ins, outs = refs[:n], refs[n:2 * n]
        send, recv, local = refs[2 * n:]
        x, y, c = _mesh_pos()
        me = _lin(x, y, c)

        def peer(k):
            return (1 - x if k & 4 else x, 1 - y if k & 2 else y, 1 - c if k & 1 else c)

        def src(a, d):
            return ins[a] if bcast[a] else ins[a].at[d]

        mine = [pltpu.make_async_copy(src(a, me), outs[a].at[me], local.at[a]) for a in range(n)]
        for cp in mine:
            cp.start()
        sends = [_remote(src(a, _lin(*peer(k))), outs[a].at[me], send.at[a, k - 1], recv.at[a, k - 1], peer(k))
                 for k in range(1, N_DEV) for a in range(n)]
        for cp in sends:
            cp.start()
        for k in range(1, N_DEV):
            for a in range(n):
                got = outs[a].at[_lin(*peer(k))]
                _remote(got, got, send.at[a, k - 1], recv.at[a, k - 1], peer(k)).wait_recv()
        for cp in sends:
            cp.wait_send()
        for cp in mine:
            cp.wait()

    hbm = pl.BlockSpec(memory_space=pl.ANY)
    return pl.pallas_call(
        body, name=name, in_specs=[hbm] * n, out_specs=[hbm] * n,
        out_shape=[jax.ShapeDtypeStruct(((N_DEV,) + a.shape) if b else a.shape, a.dtype) for a, b in zip(arrs, bcast)],
        scratch_shapes=[pltpu.SemaphoreType.DMA((n, 7)), pltpu.SemaphoreType.DMA((n, 7)), pltpu.SemaphoreType.DMA((n,))],
    )(*arrs)


def _adamw(name, contrib, w, m, v):
    R, C = w.shape
    tr = min(R, 128)
    while R % tr:
        tr -= SUBLANE

    def body(c_ref, w_ref, m_ref, v_ref, g_ref, d_ref, nm_ref, nv_ref):
        g = c_ref[0].astype(F32)
        for s in range(1, N_DEV):
            g = g + c_ref[s].astype(F32)
        nm = ADAM_B1 * m_ref[...] + (1.0 - ADAM_B1) * g
        nv = ADAM_B2 * v_ref[...] + (1.0 - ADAM_B2) * (g * g)
        m_hat = nm / (1.0 - ADAM_B1 ** ADAM_STEP)
        v_hat = nv / (1.0 - ADAM_B2 ** ADAM_STEP)
        g_ref[...] = g
        d_ref[...] = -ADAM_LR * (m_hat / (jnp.sqrt(v_hat) + ADAM_EPS) + ADAM_WD * w_ref[...])
        nm_ref[...] = nm
        nv_ref[...] = nv

    row = pl.BlockSpec((tr, C), lambda i: (i, 0))
    sds = jax.ShapeDtypeStruct((R, C), F32)
    return pl.pallas_call(
        body, name=name, grid=(R // tr,),
        in_specs=[pl.BlockSpec((N_DEV, tr, C), lambda i: (0, i, 0)), row, row, row], out_specs=[row] * 4,
        out_shape=[sds] * 4, compiler_params=_cp("parallel"))(contrib, w, m, v)


def _pack(arrs):
    unit = SUBLANE * LANE
    flat = [jnp.pad(a.reshape(-1), (0, -a.size % unit)) for a in arrs]
    return jnp.concatenate(flat).reshape(-1, LANE)


def _unpack(buf, shapes):
    unit = SUBLANE * LANE
    flat = buf.reshape(-1)
    out, off = [], 0
    for s in shapes:
        size = 1
        for d in s:
            size *= d
        out.append(flat[off:off + size].reshape(s))
        off += size + (-size % unit)
    return out


def _cross_attention_fwd(l, h, mem, g_x, g_mem, wq, wk, wv, wo):
    hx = _rms_fwd(f"rms_x{l}", h, g_x)
    memn = _rms_fwd(f"rms_mem{l}", mem, g_mem)
    q = _mm_nn(f"xq{l}", hx, wq, BF16)
    k = _mm_nn(f"xk{l}", memn, wk, BF16)
    v = _mm_nn(f"xv{l}", memn, wv, BF16)
    o = _xattn_fwd(f"xattn_fwd{l}", q, k, v)
    h_out = _mm_nn(f"xo{l}", o, wo, F32, add=h)
    return h_out, (hx, memn, q, k, v, o)


def _cross_attention_bwd(l, dh, h, mem, g_x, g_mem, wq, wk, wv, wo, saved):
    hx, memn, q, k, v, o = saved
    do = _mm_nt(f"xo_dx{l}", dh, wo, BF16)
    dwo = _mm_tn(f"xo_dw{l}", o, dh, BF16)
    dq, dk, dv = _xattn_bwd(f"xattn_bwd{l}", q, k, v, do)
    dwq = _mm_tn(f"xq_dw{l}", hx, dq, BF16)
    dwk = _mm_tn(f"xk_dw{l}", memn, dk, BF16)
    dwv = _mm_tn(f"xv_dw{l}", memn, dv, BF16)
    dmemn = _mm_nt(f"xk_dx{l}", dk, wk, F32)
    dmemn = _mm_nt(f"xv_dx{l}", dv, wv, F32, add=dmemn)
    _, dg_mem = _rms_bwd(f"rms_mem_bwd{l}", dmemn, mem, g_mem, jnp.zeros_like(mem))
    dhx = _mm_nt(f"xq_dx{l}", dq, wq, F32)
    dh_in, dg_x = _rms_bwd(f"rms_x_bwd{l}", dhx, h, g_x, dh)
    return dh_in, (dg_x, dg_mem, dwq, dwk, dwv, dwo)


def kernel(x, mem, norm_mix_g, norm_x_g, norm_mem_g, final_norm_g, w_in_ab, rel_bias, conv_w, conv_b, conv_ln_g, conv_ln_b, w_out_ab, w_in_c, sgu_ln_g, sgu_ln_b, w_s, b_s, w_out_c, w_xq, w_xk, w_xv, w_xo, loss_target, m_norm_mix_g, m_norm_x_g, m_norm_mem_g, m_final_norm_g, m_w_in_ab, m_rel_bias, m_conv_w, m_conv_b, m_conv_ln_g, m_conv_ln_b, m_w_out_ab, m_w_in_c, m_sgu_ln_g, m_sgu_ln_b, m_w_s, m_b_s, m_w_out_c, m_w_xq, m_w_xk, m_w_xv, m_w_xo, v_norm_mix_g, v_norm_x_g, v_norm_mem_g, v_final_norm_g, v_w_in_ab, v_rel_bias, v_conv_w, v_conv_b, v_conv_ln_g, v_conv_ln_b, v_w_out_ab, v_w_in_c, v_sgu_ln_g, v_sgu_ln_b, v_w_s, v_b_s, v_w_out_c, v_w_xq, v_w_xk, v_w_xv, v_w_xo):
    names = ["norm_mix_g", "norm_x_g", "norm_mem_g", "final_norm_g", "w_in_ab", "rel_bias", "conv_w", "conv_b",
             "conv_ln_g", "conv_ln_b", "w_out_ab", "w_in_c", "sgu_ln_g", "sgu_ln_b", "w_s", "b_s", "w_out_c",
             "w_xq", "w_xk", "w_xv", "w_xo"]
    W = dict(zip(names, (norm_mix_g, norm_x_g, norm_mem_g, final_norm_g, w_in_ab, rel_bias, conv_w, conv_b, conv_ln_g,
                         conv_ln_b, w_out_ab, w_in_c, sgu_ln_g, sgu_ln_b, w_s, b_s, w_out_c, w_xq, w_xk, w_xv, w_xo)))
    M1 = dict(zip(names, (m_norm_mix_g, m_norm_x_g, m_norm_mem_g, m_final_norm_g, m_w_in_ab, m_rel_bias, m_conv_w, m_conv_b,
                          m_conv_ln_g, m_conv_ln_b, m_w_out_ab, m_w_in_c, m_sgu_ln_g, m_sgu_ln_b, m_w_s, m_b_s, m_w_out_c,
                          m_w_xq, m_w_xk, m_w_xv, m_w_xo)))
    M2 = dict(zip(names, (v_norm_mix_g, v_norm_x_g, v_norm_mem_g, v_final_norm_g, v_w_in_ab, v_rel_bias, v_conv_w, v_conv_b,
                          v_conv_ln_g, v_conv_ln_b, v_w_out_ab, v_w_in_c, v_sgu_ln_g, v_sgu_ln_b, v_w_s, v_b_s, v_w_out_c,
                          v_w_xq, v_w_xk, v_w_xv, v_w_xo)))

    h0, memv, tgt = x[0], mem[0], loss_target[0]
    T, D = h0.shape
    H = rel_bias.shape[1]
    n_rel = rel_bias.shape[2]
    C = conv_b.shape[1]
    WC = sgu_ln_g.shape[1] * N_DEV
    n_layers = norm_mix_g.shape[0]

    xnames = ["w_xq", "w_xk", "w_xv", "w_xo"]
    big = [w_in_ab[0], w_out_ab[0], w_in_c[0], w_out_c[0]] + [W[n][l] for l in range(n_layers) for n in xnames]
    small = _pack([conv_w[0], sgu_ln_g[0], sgu_ln_b[0]])
    gathered = _all_gather([a.astype(BF16) for a in big] + [small])
    win_ab, wout_ab, win_c, wout_c = gathered[0], gathered[1], gathered[2], gathered[3]
    wout_ab = wout_ab.reshape(-1, D)
    wout_c = wout_c.reshape(-1, D)
    wx = [[gathered[4 + 4 * l + j].reshape(D, D) for j in range(4)] for l in range(n_layers)]
    per_cw = conv_w.shape[2]
    per_ln = sgu_ln_g.shape[1]
    cw_s, lg_s, lb_s = zip(*[_unpack(gathered[-1][d], [(CONV_K, per_cw), (1, per_ln), (1, per_ln)]) for d in range(N_DEV)])
    conv_w_full = jnp.pad(jnp.concatenate(cw_s, axis=1), ((0, HALO - CONV_K), (0, 0)))
    sgu_g_full = jnp.concatenate(lg_s, axis=1)
    sgu_b_full = jnp.concatenate(lb_s, axis=1)
    b_s_t = b_s[0].T

    hn0 = _rms_fwd("rms_mix0", h0, norm_mix_g[0:1])
    proj_ab = _mm_nn("in_ab", hn0, win_ab, F32)
    btile = _bias_tile(rel_bias[0])
    ya, y_ab = _band_attn_fwd(proj_ab, btile)
    y_ab = _conv_fwd(proj_ab, y_ab, conv_w_full, conv_b, conv_ln_g, conv_ln_b)
    h1 = _mm_nn("out_ab", y_ab, wout_ab, F32, add=h0)
    h2, xs0 = _cross_attention_fwd(0, h1, memv, norm_x_g[0:1], norm_mem_g[0:1], *wx[0])
    hn1 = _rms_fwd("rms_mix1", h2, norm_mix_g[1:2])
    proj_c = _mm_nn("in_c", hn1, win_c, F32)
    y_c = _sgu_fwd(proj_c, sgu_g_full, sgu_b_full, w_s[0], b_s_t)
    h3 = _mm_nn("out_c", y_c, wout_c, F32, add=h2)
    h4, xs1 = _cross_attention_fwd(1, h3, memv, norm_x_g[1:2], norm_mem_g[1:2], *wx[1])
    loss_acc, dh4, dg_final = _loss_head(h4, tgt, final_norm_g.reshape(1, D))

    dh3, gx1 = _cross_attention_bwd(1, dh4, h3, memv, norm_x_g[1:2], norm_mem_g[1:2], *wx[1], xs1)
    dy_c = _mm_nt("out_c_dx", dh3, wout_c, F32)
    dwout_c = _mm_tn("out_c_dw", y_c, dh3, BF16)
    dproj_c, dws, dbs_t, dsgu_g, dsgu_b = _sgu_bwd(proj_c, dy_c, sgu_g_full, sgu_b_full, w_s[0], b_s_t)
    dwin_c = _mm_tn("in_c_dw", hn1, dproj_c, BF16, per=win_c.shape[2])
    dhn1 = _mm_nt("in_c_dx", dproj_c, win_c, F32)
    dh2, dg_mix1 = _rms_bwd("rms_mix1_bwd", dhn1, h2, norm_mix_g[1:2], dh3)
    dh1, gx0 = _cross_attention_bwd(0, dh2, h1, memv, norm_x_g[0:1], norm_mem_g[0:1], *wx[0], xs0)
    dy_ab = _mm_nt("out_ab_dx", dh1, wout_ab, F32)
    dwout_ab = _mm_tn("out_ab_dw", y_ab, dh1, BF16)
    dq, dk, dv, dgate_a, dbtile = _band_attn_bwd(proj_ab, btile, dy_ab, ya)
    drel = _bias_tile_grad(dbtile, n_rel)
    dz, dgate_b, dcln_g, dcln_b, dconv_b = _conv_bwd_ln(proj_ab, dy_ab, conv_w_full, conv_b, conv_ln_g, conv_ln_b)
    dglu_a, dglu_b, dconv_w = _conv_bwd_taps(proj_ab, dz, conv_w_full)
    dproj_ab = jnp.concatenate([dq, dk, dv, dglu_a, dglu_b, dgate_a, dgate_b], axis=1)
    dwin_ab = _mm_tn("in_ab_dw", hn0, dproj_ab, BF16, per=win_ab.shape[2])
    dhn0 = _mm_nt("in_ab_dx", dproj_ab, win_ab, F32)
    dx, dg_mix0 = _rms_bwd("rms_mix0_bwd", dhn0, h0, norm_mix_g[0:1], dh1)

    gxs = [gx0, gx1]
    big_grads = [dwin_ab, dwout_ab.reshape(N_DEV, -1, D), dwin_c, dwout_c.reshape(N_DEV, -1, D)]
    big_grads += [gxs[l][2 + j].reshape(N_DEV, -1, D) for l in range(n_layers) for j in range(4)]
    sm = [_pack([dconv_w[:CONV_K, d * per_cw:(d + 1) * per_cw], dsgu_g[:, d * per_ln:(d + 1) * per_ln],
                 dsgu_b[:, d * per_ln:(d + 1) * per_ln]]) for d in range(N_DEV)]
    small_grads = jnp.stack(sm)
    mask = (jnp.arange(GMLP_CHUNK)[:, None] // CHUNK >= jnp.arange(GMLP_CHUNK)[None, :] // CHUNK).astype(F32)
    rep_names = ["norm_mix_g", "norm_x_g", "norm_mem_g", "final_norm_g", "rel_bias", "conv_b", "conv_ln_g", "conv_ln_b",
                 "w_s", "b_s"]
    rep_grads = {
        "norm_mix_g": jnp.concatenate([dg_mix0, dg_mix1], axis=0),
        "norm_x_g": jnp.concatenate([gx0[0], gx1[0]], axis=0),
        "norm_mem_g": jnp.concatenate([gx0[1], gx1[1]], axis=0),
        "final_norm_g": dg_final.reshape(D),
        "rel_bias": drel[None], "conv_b": dconv_b, "conv_ln_g": dcln_g, "conv_ln_b": dcln_b,
        "w_s": (dws * mask[None])[None], "b_s": dbs_t.T[None],
    }
    rep_packed = _pack([rep_grads[n] for n in rep_names])
    recv = _exchange("grad_exchange", big_grads + [small_grads, rep_packed], [False] * (len(big_grads) + 1) + [True])

    out = {}

    def put(name, res, shape):
        for kind, r in zip(("grad", "delta", "new_m", "new_v"), res):
            out[(kind, name)] = r.reshape(shape)

    for j, n in enumerate(["w_in_ab", "w_out_ab", "w_in_c", "w_out_c"]):
        shp = W[n].shape
        put(n, _adamw(f"adamw_{n}", recv[j], W[n][0], M1[n][0], M2[n][0]), shp)
    for j, n in enumerate(xnames):
        res = [_adamw(f"adamw_{n}{l}", recv[4 + 4 * l + j], W[n][l], M1[n][l], M2[n][l]) for l in range(n_layers)]
        put(n, [jnp.stack(r) for r in zip(*res)], W[n].shape)
    sm_names = ["conv_w", "sgu_ln_g", "sgu_ln_b"]
    res = _adamw("adamw_small", recv[-2], *[_pack([D_[n][0] for n in sm_names]) for D_ in (W, M1, M2)])
    for kind, r in zip(("grad", "delta", "new_m", "new_v"), res):
        for n, piece in zip(sm_names, _unpack(r, [W[n].shape for n in sm_names])):
            out[(kind, n)] = piece
    res = _adamw("adamw_replicated", recv[-1], *[_pack([D_[n] for n in rep_names]) for D_ in (W, M1, M2)])
    for kind, r in zip(("grad", "delta", "new_m", "new_v"), res):
        for n, piece in zip(rep_names, _unpack(r, [W[n].shape for n in rep_names])):
            out[(kind, n)] = piece

    loss = lax.psum(loss_acc[0, 0], MESH_AXES)
    return (loss, dx[None]) + tuple(out[(kind, n)] for kind in ("grad", "delta", "new_m", "new_v") for n in names)
```

```python
import jax
import jax.numpy as jnp
from jax import lax
from jax.experimental import pallas as pl
from jax.experimental.pallas import tpu as pltpu

F32 = jnp.float32
BF16 = jnp.bfloat16
I32 = jnp.int32

N_DEV = 8
CHUNK = 64
N_PAST = 8
MAX_REL = 128
HEAD_A = 128
CONV_K = 31
GMLP_CHUNK = 128
N_HEADS_X = 4
EPS = 1e-6
NEG = -1e30

ADAM_LR, ADAM_B1, ADAM_B2, ADAM_EPS, ADAM_WD, ADAM_STEP = 0.001, 0.9, 0.999, 1e-08, 0.01, 10

LANE = 128
SUBLANE = 8
VMEM_LIMIT = 56 * 1024 * 1024
QB = 4 * CHUNK
KW = QB + N_PAST * CHUNK
ROLL_W = 1024
REL_PAD = 384
HALO = 32
MESH_AXES = ("x", "y", "c")

NN = (((1,), (0,)), ((), ()))
NT = (((1,), (1,)), ((), ()))
TN = (((0,), (0,)), ((), ()))


def _cp(*sem):
    return pltpu.CompilerParams(dimension_semantics=sem, vmem_limit_bytes=VMEM_LIMIT)


def _tile(dim, pref):
    if dim <= pref:
        return dim
    t = (pref // LANE) * LANE
    while dim % t:
        t -= LANE
    return t


def _sigmoid(x):
    return 1.0 / (1.0 + jnp.exp(-x))


def _dsilu(x, s):
    return s * (1.0 + x * (1.0 - s))


def _dot(a, b, dims):
    return lax.dot_general(a, b, dims, preferred_element_type=F32)


def _mesh_pos():
    return lax.axis_index("x"), lax.axis_index("y"), lax.axis_index("c")


def _lin(x, y, c):
    return 4 * x + 2 * y + c


def _remote(src, dst, send_sem, recv_sem, to):
    return pltpu.make_async_remote_copy(src_ref=src, dst_ref=dst, send_sem=send_sem, recv_sem=recv_sem,
                                        device_id=to, device_id_type=pl.DeviceIdType.MESH)


class _Carry:
    def __init__(self, kind, arrs, bcast=None):
        n = len(arrs)
        self.kind, self.arrs, self.n = kind, list(arrs), n
        self.bcast = [kind == "gather"] * n if bcast is None else list(bcast)
        self.out_shape = [jax.ShapeDtypeStruct(((N_DEV,) + a.shape) if b else a.shape, a.dtype)
                          for a, b in zip(arrs, self.bcast)]
        self.scratch = [pltpu.SemaphoreType.DMA((n, 7)), pltpu.SemaphoreType.DMA((n, 7)), pltpu.SemaphoreType.DMA((n,))]
        self.result = None

    def _src(self, ins, a, d):
        return ins[a] if self.bcast[a] else ins[a].at[d]

    def _local(self, ins, outs, sems):
        me = _lin(*_mesh_pos())
        return [pltpu.make_async_copy(self._src(ins, a, me), outs[a].at[me], sems[2].at[a]) for a in range(self.n)]

    @staticmethod
    def _chips():
        x, y, _ = _mesh_pos()
        return [(1 - x, y), (x, 1 - y), (1 - x, 1 - y)]

    def _g_copy(self, ins, outs, sems, a, k, block, to, own=False):
        rows = outs[a].at[_lin(*block)]
        return _remote(ins[a] if own else rows, rows, sems[0].at[a, k], sems[1].at[a, k], to)

    def _g_first(self, ins, outs, sems):
        x, y, c = _mesh_pos()
        cps = []
        for a in range(self.n):
            cps.append(self._g_copy(ins, outs, sems, a, 0, (x, y, c), (x, y, 1 - c), own=True))
            cps += [self._g_copy(ins, outs, sems, a, 1 + j, (x, y, c), (*chip, c), own=True)
                    for j, chip in enumerate(self._chips())]
        return cps

    def _g_passed(self, ins, outs, sems):
        x, y, c = _mesh_pos()
        return [self._g_copy(ins, outs, sems, a, 4 + j, (*chip, c), (x, y, 1 - c))
                for j, chip in enumerate(self._chips()) for a in range(self.n)]

    @staticmethod
    def _peer(k):
        x, y, c = _mesh_pos()
        return (1 - x if k & 4 else x, 1 - y if k & 2 else y, 1 - c if k & 1 else c)

    def _x_sends(self, ins, outs, sems):
        me = _lin(*_mesh_pos())
        return [_remote(self._src(ins, a, _lin(*self._peer(k))), outs[a].at[me], sems[0].at[a, k - 1],
                        sems[1].at[a, k - 1], self._peer(k)) for k in range(1, N_DEV) for a in range(self.n)]

    def start(self, ins, outs, sems):
        for cp in self._local(ins, outs, sems):
            cp.start()
        for cp in (self._g_first if self.kind == "gather" else self._x_sends)(ins, outs, sems):
            cp.start()

    def mid(self, ins, outs, sems):
        if self.kind != "gather":
            return
        x, y, c = _mesh_pos()
        passed = self._g_passed(ins, outs, sems)
        for j, chip in enumerate(self._chips()):
            for a in range(self.n):
                self._g_copy(ins, outs, sems, a, 1 + j, (*chip, c), (x, y, c)).wait_recv()
                passed[j * self.n + a].start()

    def finish(self, ins, outs, sems):
        x, y, c = _mesh_pos()
        if self.kind == "gather":
            for a in range(self.n):
                self._g_copy(ins, outs, sems, a, 0, (x, y, 1 - c), (x, y, c)).wait_recv()
            for j, chip in enumerate(self._chips()):
                for a in range(self.n):
                    self._g_copy(ins, outs, sems, a, 4 + j, (*chip, 1 - c), (x, y, c)).wait_recv()
            sent = self._g_first(ins, outs, sems) + self._g_passed(ins, outs, sems)
        else:
            for k in range(1, N_DEV):
                for a in range(self.n):
                    got = outs[a].at[_lin(*self._peer(k))]
                    _remote(got, got, sems[0].at[a, k - 1], sems[1].at[a, k - 1], self._peer(k)).wait_recv()
            sent = self._x_sends(ins, outs, sems)
        for cp in sent:
            cp.wait_send()
        for cp in self._local(ins, outs, sems):
            cp.wait()


def _comm_call(name, carry):
    n = carry.n

    def body(*refs):
        ins, outs, sems = refs[:n], refs[n:2 * n], refs[2 * n:]
        carry.start(ins, outs, sems)
        carry.mid(ins, outs, sems)
        carry.finish(ins, outs, sems)

    hbm = pl.BlockSpec(memory_space=pl.ANY)
    return pl.pallas_call(body, name=name, in_specs=[hbm] * n, out_specs=[hbm] * n, out_shape=carry.out_shape,
                          scratch_shapes=carry.scratch)(*carry.arrs)


def _call(name, body, *, grid, in_specs, out_specs, out_shape, args, sem, scratch=(), aliases=None, carry=None):
    aliases = aliases or {}
    if carry is None:
        return pl.pallas_call(body, name=name, grid=grid, in_specs=in_specs, out_specs=out_specs, out_shape=out_shape,
                              scratch_shapes=list(scratch), input_output_aliases=aliases, compiler_params=_cp(*sem))(*args)
    ni, no, ns, nc = len(in_specs), len(out_specs), len(scratch), carry.n
    total = 1
    for g in grid:
        total *= g

    def full(*refs):
        ins, cins = refs[:ni], refs[ni:ni + nc]
        outs, couts = refs[ni + nc:ni + nc + no], refs[ni + nc + no:ni + 2 * nc + no]
        scr, sems = refs[ni + 2 * nc + no:ni + 2 * nc + no + ns], refs[ni + 2 * nc + no + ns:]
        step = pl.program_id(0)
        for d in range(1, len(grid)):
            step = step * grid[d] + pl.program_id(d)

        @pl.when(step == 0)
        def _():
            carry.start(cins, couts, sems)

        body(*ins, *outs, *scr)

        @pl.when(step == total // 2)
        def _():
            carry.mid(cins, couts, sems)

        @pl.when(step == total - 1)
        def _():
            carry.finish(cins, couts, sems)

    hbm = pl.BlockSpec(memory_space=pl.ANY)
    res = pl.pallas_call(
        full, name=name, grid=grid, in_specs=list(in_specs) + [hbm] * nc, out_specs=list(out_specs) + [hbm] * nc,
        out_shape=list(out_shape) + carry.out_shape, scratch_shapes=list(scratch) + carry.scratch,
        input_output_aliases=aliases, compiler_params=_cp(*["arbitrary"] * len(grid)))(*args, *carry.arrs)
    carry.result = list(res[no:])
    return list(res[:no])


def _matmul(name, a, b, *, dims, grid, a_spec, b_spec, out_sds, out_spec, acc_shape, add=None, add_spec=None, carry=None):
    nk = grid[2]
    has_add = add is not None

    def body(*refs):
        a_ref, b_ref = refs[0], refs[1]
        o_ref = refs[3] if has_add else refs[2]
        part = _dot(a_ref[...].astype(BF16), b_ref[...].astype(BF16), dims)

        def finish(r):
            if has_add:
                r = r + refs[2][...]
            o_ref[...] = r.astype(o_ref.dtype)

        if nk == 1:
            finish(part)
        else:
            acc_ref = refs[-1]
            k = pl.program_id(2)

            @pl.when(k == 0)
            def _():
                acc_ref[...] = part

            @pl.when(k > 0)
            def _():
                acc_ref[...] += part

            @pl.when(k == nk - 1)
            def _():
                finish(acc_ref[...])

    in_specs = [a_spec, b_spec] + ([add_spec] if has_add else [])
    args = (a, b) + ((add,) if has_add else ())
    return _call(name, body, grid=grid, in_specs=in_specs, out_specs=[out_spec], out_shape=[out_sds], args=args,
                 sem=("parallel", "parallel", "arbitrary"), scratch=[pltpu.VMEM(acc_shape, F32)] if nk > 1 else [],
                 carry=carry)[0]


def _blk(per):
    return per if per <= 1024 else per // 2


def _mm_nn(name, a, b, out_dtype, *, add=None, tm=1024, tn=1024, tk=2048, carry=None):
    M, K = a.shape
    tm, tk = _tile(M, tm), _tile(K, tk)
    if b.ndim == 3:
        per = b.shape[2]
        tn = _blk(per)
        q = per // tn
        N = N_DEV * per
        b_spec = pl.BlockSpec((None, tk, tn), lambda m, n, k: (n // q, k, n % q))
    else:
        N = b.shape[1]
        tn = _tile(N, tn)
        b_spec = pl.BlockSpec((tk, tn), lambda m, n, k: (k, n))
    return _matmul(
        name, a, b, dims=NN, grid=(M // tm, N // tn, K // tk),
        a_spec=pl.BlockSpec((tm, tk), lambda m, n, k: (m, k)), b_spec=b_spec,
        out_sds=jax.ShapeDtypeStruct((M, N), out_dtype), out_spec=pl.BlockSpec((tm, tn), lambda m, n, k: (m, n)),
        acc_shape=(tm, tn), add=add, add_spec=pl.BlockSpec((tm, tn), lambda m, n, k: (m, n)), carry=carry)


def _mm_nt(name, a, b, out_dtype, *, add=None, tm=1024, tn=1024, tk=2048, carry=None):
    M, K = a.shape
    tm = _tile(M, tm)
    if b.ndim == 3:
        N, tk = b.shape[1], b.shape[2]
        tn = _tile(N, tn)
        b_spec = pl.BlockSpec((None, tn, tk), lambda m, n, k: (k, n, 0))
    else:
        N = b.shape[0]
        tn, tk = _tile(N, tn), _tile(K, tk)
        b_spec = pl.BlockSpec((tn, tk), lambda m, n, k: (n, k))
    return _matmul(
        name, a, b, dims=NT, grid=(M // tm, N // tn, K // tk),
        a_spec=pl.BlockSpec((tm, tk), lambda m, n, k: (m, k)), b_spec=b_spec,
        out_sds=jax.ShapeDtypeStruct((M, N), out_dtype), out_spec=pl.BlockSpec((tm, tn), lambda m, n, k: (m, n)),
        acc_shape=(tm, tn), add=add, add_spec=pl.BlockSpec((tm, tn), lambda m, n, k: (m, n)), carry=carry)


def _mm_tn(name, a, b, out_dtype, *, per=None, tm=1024, tn=1024, tk=2048, carry=None):
    K, M = a.shape
    N = b.shape[1]
    tm, tk = _tile(M, tm), _tile(K, tk)
    if per is not None:
        tn = _blk(per)
        q = per // tn
        out_sds = jax.ShapeDtypeStruct((N_DEV, M, per), out_dtype)
        out_spec = pl.BlockSpec((None, tm, tn), lambda m, n, k: (n // q, m, n % q))
    else:
        tn = _tile(N, tn)
        out_sds = jax.ShapeDtypeStruct((M, N), out_dtype)
        out_spec = pl.BlockSpec((tm, tn), lambda m, n, k: (m, n))
    return _matmul(
        name, a, b, dims=TN, grid=(M // tm, N // tn, K // tk),
        a_spec=pl.BlockSpec((tk, tm), lambda m, n, k: (k, m)), b_spec=pl.BlockSpec((tk, tn), lambda m, n, k: (k, n)),
        out_sds=out_sds, out_spec=out_spec, acc_shape=(tm, tn), carry=carry)


def _rms_fwd(name, h, g):
    T, D = h.shape
    tb = _tile(T, 512)

    def body(h_ref, g_ref, o_ref):
        x = h_ref[...]
        r = lax.rsqrt(jnp.mean(x * x, axis=-1, keepdims=True) + EPS)
        o_ref[...] = (x * r * g_ref[...]).astype(o_ref.dtype)

    return pl.pallas_call(
        body, name=name, grid=(T // tb,),
        in_specs=[pl.BlockSpec((tb, D), lambda i: (i, 0)), pl.BlockSpec((1, D), lambda i: (0, 0))],
        out_specs=pl.BlockSpec((tb, D), lambda i: (i, 0)), out_shape=jax.ShapeDtypeStruct((T, D), BF16),
        compiler_params=_cp("parallel"))(h, g)


def _rms_bwd(name, dhn, h, g, dres):
    T, D = h.shape
    tb = _tile(T, 256)

    def body(dhn_ref, h_ref, g_ref, dres_ref, dh_ref, dg_ref):
        i = pl.program_id(0)
        x = h_ref[...]
        r = lax.rsqrt(jnp.mean(x * x, axis=-1, keepdims=True) + EPS)
        y = x * r
        d = dhn_ref[...]
        dy = d * g_ref[...]
        dh_ref[...] = dres_ref[...] + r * (dy - y * jnp.mean(dy * y, axis=-1, keepdims=True))
        part = jnp.sum(d * y, axis=0, keepdims=True)

        @pl.when(i == 0)
        def _():
            dg_ref[...] = part

        @pl.when(i > 0)
        def _():
            dg_ref[...] += part

    row = pl.BlockSpec((tb, D), lambda i: (i, 0))
    vec = pl.BlockSpec((1, D), lambda i: (0, 0))
    return pl.pallas_call(
        body, name=name, grid=(T // tb,), in_specs=[row, row, vec, row], out_specs=[row, vec],
        out_shape=[jax.ShapeDtypeStruct((T, D), F32), jax.ShapeDtypeStruct((1, D), F32)],
        compiler_params=_cp("arbitrary"))(dhn, h, g, dres)


def _loss_head(h, tgt, g):
    T, D = h.shape
    tb = _tile(T, 256)

    def body(h_ref, t_ref, g_ref, loss_ref, dh_ref, dg_ref):
        i = pl.program_id(0)
        x = h_ref[...]
        gg = g_ref[...]
        r = lax.rsqrt(jnp.mean(x * x, axis=-1, keepdims=True) + EPS)
        y0 = x * r
        err = y0 * gg - t_ref[...]
        tot = 0.5 * jnp.sum(jnp.mean(err * err, axis=-1, keepdims=True), axis=0, keepdims=True)
        dy = err * (1.0 / D)
        dyg = dy * gg
        dh_ref[...] = r * (dyg - y0 * jnp.mean(dyg * y0, axis=-1, keepdims=True))
        part = jnp.sum(dy * y0, axis=0, keepdims=True)
        tot = jnp.broadcast_to(tot, loss_ref.shape)

        @pl.when(i == 0)
        def _():
            dg_ref[...] = part
            loss_ref[...] = tot

        @pl.when(i > 0)
        def _():
            dg_ref[...] += part
            loss_ref[...] += tot

    row = pl.BlockSpec((tb, D), lambda i: (i, 0))
    vec = pl.BlockSpec((1, D), lambda i: (0, 0))
    return pl.pallas_call(
        body, name="loss_head", grid=(T // tb,), in_specs=[row, row, vec],
        out_specs=[pl.BlockSpec((SUBLANE, LANE), lambda i: (0, 0)), row, vec],
        out_shape=[jax.ShapeDtypeStruct((SUBLANE, LANE), F32), jax.ShapeDtypeStruct((T, D), F32),
                   jax.ShapeDtypeStruct((1, D), F32)],
        compiler_params=_cp("arbitrary"))(h, tgt, g)


def _rel_onehot(pos_axis, shape):
    pos = lax.broadcasted_iota(I32, shape, pos_axis)
    r = lax.broadcasted_iota(I32, shape, 1 - pos_axis)
    d = jnp.where(pos < KW, N_PAST * CHUNK - pos, N_PAST * CHUNK + ROLL_W - pos)
    return (jnp.clip(d, -MAX_REL, MAX_REL) + MAX_REL == r).astype(F32)


def _roll_rows(x, left):
    row = lax.broadcasted_iota(I32, x.shape, 0)
    for b in range(QB.bit_length() - 1):
        shift = (ROLL_W - (1 << b)) if left else (1 << b)
        x = jnp.where(((row >> b) & 1) == 1, pltpu.roll(x, shift, 1), x)
    return x


def _bias_tile(rel_bias):
    H = rel_bias.shape[0]
    rb = jnp.pad(rel_bias, ((0, 0), (0, REL_PAD - rel_bias.shape[1]))).reshape(H, 1, REL_PAD)

    def body(rb_ref, o_ref):
        row = jnp.broadcast_to(rb_ref[...], (SUBLANE, REL_PAD))
        base = jnp.dot(row, _rel_onehot(1, (REL_PAD, ROLL_W)), precision=lax.Precision.HIGHEST,
                       preferred_element_type=F32)[0:1]
        tile = _roll_rows(jnp.broadcast_to(base, (QB, ROLL_W)), left=False)[:, :KW]
        qc = lax.broadcasted_iota(I32, (QB, KW), 0) // CHUNK
        kc = lax.broadcasted_iota(I32, (QB, KW), 1) // CHUNK - N_PAST
        o_ref[...] = jnp.where((kc >= qc - N_PAST) & (kc <= qc), tile, NEG)

    return pl.pallas_call(
        body, name="bias_tile", grid=(H,),
        in_specs=[pl.BlockSpec((None, 1, REL_PAD), lambda h: (h, 0, 0))],
        out_specs=pl.BlockSpec((None, QB, KW), lambda h: (h, 0, 0)),
        out_shape=jax.ShapeDtypeStruct((H, QB, KW), F32), compiler_params=_cp("parallel"))(rb)


def _bias_tile_grad(dtile, n_rel):
    H = dtile.shape[0]

    def body(dt_ref, o_ref):
        x = jnp.concatenate([dt_ref[...], jnp.zeros((QB, ROLL_W - KW), F32)], axis=1)
        cs = jnp.sum(_roll_rows(x, left=True), axis=0, keepdims=True)
        o_ref[...] = jnp.dot(jnp.broadcast_to(cs, (SUBLANE, ROLL_W)), _rel_onehot(0, (ROLL_W, REL_PAD)),
                             precision=lax.Precision.HIGHEST, preferred_element_type=F32)[0:1]

    out = pl.pallas_call(
        body, name="bias_tile_grad", grid=(H,),
        in_specs=[pl.BlockSpec((None, QB, KW), lambda h: (h, 0, 0))],
        out_specs=pl.BlockSpec((None, 1, REL_PAD), lambda h: (h, 0, 0)),
        out_shape=jax.ShapeDtypeStruct((H, 1, REL_PAD), F32), compiler_params=_cp("parallel"))(dtile)
    return out.reshape(H, REL_PAD)[:, :n_rel]


def _band_specs(H):
    spec = lambda f: pl.BlockSpec((QB, HEAD_A), f)
    q = spec(lambda h, i: (i, h))
    ks = [spec(lambda h, i, r=r: (jnp.maximum(i - 2 + r, 0), H + h)) for r in range(3)]
    vs = [spec(lambda h, i, r=r: (jnp.maximum(i - 2 + r, 0), 2 * H + h)) for r in range(3)]
    gate = spec(lambda h, i: (i, 5 * H + h))
    bias = pl.BlockSpec((None, QB, KW), lambda h, i: (h, 0, 0))
    return [q] + ks + vs + [gate, bias]


def _band_probs(i, q_ref, k_refs, v_refs, bias_ref):
    q = q_ref[...].astype(BF16)
    k = jnp.concatenate([r[...] for r in k_refs], axis=0).astype(BF16)
    v = jnp.concatenate([r[...] for r in v_refs], axis=0).astype(BF16)
    s = _dot(q, k, NT) * (HEAD_A ** -0.5) + bias_ref[...]
    kpos = (i - 2) * QB + lax.broadcasted_iota(I32, (1, KW), 1)
    s = jnp.where(kpos >= 0, s, NEG)
    e = jnp.exp(s - jnp.max(s, axis=-1, keepdims=True))
    p = e / jnp.sum(e, axis=-1, keepdims=True)
    return p, q, k, v


def _band_attn_fwd(proj, bias_tile, carry=None):
    T = proj.shape[0]
    H = bias_tile.shape[0]

    def body(q_ref, k0, k1, k2, v0, v1, v2, gate_ref, bias_ref, ya_ref, y_ref):
        p, _, _, v = _band_probs(pl.program_id(1), q_ref, (k0, k1, k2), (v0, v1, v2), bias_ref)
        o = _dot(p.astype(BF16), v, NN)
        g = gate_ref[...]
        ya_ref[...] = o
        y_ref[...] = (o * (g * _sigmoid(g))).astype(y_ref.dtype)

    out = pl.BlockSpec((QB, HEAD_A), lambda h, i: (i, h))
    return _call(
        "band_attn_fwd", body, grid=(H, T // QB), in_specs=_band_specs(H), out_specs=[out, out],
        out_shape=[jax.ShapeDtypeStruct((T, H * HEAD_A), F32), jax.ShapeDtypeStruct((T, 2 * H * HEAD_A), BF16)],
        args=[proj] * 8 + [bias_tile], sem=("parallel", "parallel"), carry=carry)


def _band_attn_bwd(proj, bias_tile, dy, ya, carry=None):
    T = proj.shape[0]
    H = bias_tile.shape[0]
    n_i = T // QB

    def body(q_ref, k0, k1, k2, v0, v1, v2, gate_ref, bias_ref, dy_ref, ya_ref,
             dq_ref, dk_ref, dv_ref, dgate_ref, dbias_ref, dk_acc, dv_acc):
        i = pl.program_id(1)

        @pl.when(i == 0)
        def _():
            dk_acc[...] = jnp.zeros_like(dk_acc)
            dv_acc[...] = jnp.zeros_like(dv_acc)
            dbias_ref[...] = jnp.zeros_like(dbias_ref)

        p, q, k, v = _band_probs(i, q_ref, (k0, k1, k2), (v0, v1, v2), bias_ref)
        g = gate_ref[...]
        sg = _sigmoid(g)
        dyv = dy_ref[...]
        dgate_ref[...] = (dyv * ya_ref[...] * _dsilu(g, sg)).astype(dgate_ref.dtype)
        do = (dyv * (g * sg)).astype(BF16)
        dp = _dot(do, v, NT)
        ds = p * (dp - jnp.sum(dp * p, axis=-1, keepdims=True))
        dbias_ref[...] += ds
        dsb = (ds * (HEAD_A ** -0.5)).astype(BF16)
        dq_ref[...] = _dot(dsb, k, NN).astype(dq_ref.dtype)
        dkc = _dot(dsb, q, TN)
        dvc = _dot(p.astype(BF16), do, TN)
        for r in range(3):
            blk = i - 2 + r

            @pl.when(blk >= 0)
            def _(r=r, blk=blk):
                rows = pl.ds(pl.multiple_of(blk * QB, QB), QB)
                dk_acc[rows, :] += dkc[r * QB:(r + 1) * QB]
                dv_acc[rows, :] += dvc[r * QB:(r + 1) * QB]

        @pl.when(i == n_i - 1)
        def _():
            dk_ref[...] = dk_acc[...].astype(dk_ref.dtype)
            dv_ref[...] = dv_acc[...].astype(dv_ref.dtype)

    blk = pl.BlockSpec((QB, HEAD_A), lambda h, i: (i, h))
    col = pl.BlockSpec((T, HEAD_A), lambda h, i: (0, h))
    sds = jax.ShapeDtypeStruct((T, H * HEAD_A), BF16)
    return _call(
        "band_attn_bwd", body, grid=(H, n_i), in_specs=_band_specs(H) + [blk, blk],
        out_specs=[blk, col, col, blk, pl.BlockSpec((None, QB, KW), lambda h, i: (h, 0, 0))],
        out_shape=[sds, sds, sds, sds, jax.ShapeDtypeStruct((H, QB, KW), F32)],
        args=[proj] * 8 + [bias_tile, dy, ya], sem=("parallel", "arbitrary"),
        scratch=[pltpu.VMEM((T, HEAD_A), F32), pltpu.VMEM((T, HEAD_A), F32)], carry=carry)


def _conv_in_specs(tb, C):
    per = tb // HALO
    return [pl.BlockSpec((tb, C), lambda i: (i, 3)), pl.BlockSpec((tb, C), lambda i: (i, 4)),
            pl.BlockSpec((HALO, C), lambda i: (jnp.maximum(i * per - 1, 0), 3)),
            pl.BlockSpec((HALO, C), lambda i: (jnp.maximum(i * per - 1, 0), 4))]


def _glu_with_halo(i, ga_ref, gb_ref, ha_ref, hb_ref, scr):
    halo = ha_ref[...] * _sigmoid(hb_ref[...])
    scr[0:HALO, :] = jnp.where(i > 0, halo, 0.0)
    scr[HALO:, :] = ga_ref[...] * _sigmoid(gb_ref[...])


def _conv_taps(scr, w_ref, tb, first):
    z = None
    for j in range(CONV_K):
        t = w_ref[j:j + 1, :] * scr[pl.ds(first(j), tb), :]
        z = t if z is None else z + t
    return z


def _layernorm_stats(z):
    mu = jnp.mean(z, axis=-1, keepdims=True)
    zc = z - mu
    rstd = lax.rsqrt(jnp.mean(zc * zc, axis=-1, keepdims=True) + EPS)
    return zc * rstd, rstd


def _conv_fwd(proj, y, conv_w, conv_b, ln_g, ln_b):
    T = proj.shape[0]
    C = conv_w.shape[1]
    tb = _tile(T, 256)

    def body(ga_ref, gb_ref, ha_ref, hb_ref, gate_ref, w_ref, cb_ref, g_ref, b_ref, y_in, y_ref, z_ref, scr):
        _glu_with_halo(pl.program_id(0), ga_ref, gb_ref, ha_ref, hb_ref, scr)
        z = _conv_taps(scr, w_ref, tb, lambda j: HALO - (CONV_K - 1) + j) + cb_ref[...]
        z_ref[...] = z
        xhat, _ = _layernorm_stats(z)
        ln = xhat * g_ref[...] + b_ref[...]
        gate = gate_ref[...]
        y_ref[...] = (ln * _sigmoid(ln) * (gate * _sigmoid(gate))).astype(y_ref.dtype)

    vec = pl.BlockSpec((1, C), lambda i: (0, 0))
    return pl.pallas_call(
        body, name="conv_fwd", grid=(T // tb,),
        in_specs=_conv_in_specs(tb, C) + [pl.BlockSpec((tb, C), lambda i: (i, 6)),
                                          pl.BlockSpec((HALO, C), lambda i: (0, 0)), vec, vec, vec,
                                          pl.BlockSpec(memory_space=pl.ANY)],
        out_specs=[pl.BlockSpec((tb, C), lambda i: (i, 1)), pl.BlockSpec((tb, C), lambda i: (i, 0))],
        out_shape=[jax.ShapeDtypeStruct(y.shape, y.dtype), jax.ShapeDtypeStruct((T, C), F32)],
        scratch_shapes=[pltpu.VMEM((HALO + tb, C), F32)], input_output_aliases={9: 0},
        compiler_params=_cp("parallel"))(proj, proj, proj, proj, proj, conv_w, conv_b, ln_g, ln_b, y)


def _conv_bwd_ln(proj, z, dy, ln_g, ln_b):
    T, C = z.shape
    tb = _tile(T, 256)

    def body(z_ref, gate_ref, dy_ref, g_ref, b_ref, dz_ref, dgate_ref, dg_ref, db_ref, dcb_ref):
        i = pl.program_id(0)
        xhat, rstd = _layernorm_stats(z_ref[...])
        ln = xhat * g_ref[...] + b_ref[...]
        sl = _sigmoid(ln)
        gate = gate_ref[...]
        sg = _sigmoid(gate)
        dyv = dy_ref[...]
        dgate_ref[...] = (dyv * (ln * sl) * _dsilu(gate, sg)).astype(dgate_ref.dtype)
        dln = dyv * (gate * sg) * _dsilu(ln, sl)
        dxh = dln * g_ref[...]
        dz = rstd * (dxh - jnp.mean(dxh, axis=-1, keepdims=True) - xhat * jnp.mean(dxh * xhat, axis=-1, keepdims=True))
        dz_ref[...] = dz
        parts = (jnp.sum(dln * xhat, axis=0, keepdims=True), jnp.sum(dln, axis=0, keepdims=True),
                 jnp.sum(dz, axis=0, keepdims=True))

        @pl.when(i == 0)
        def _():
            for ref, part in zip((dg_ref, db_ref, dcb_ref), parts):
                ref[...] = part

        @pl.when(i > 0)
        def _():
            for ref, part in zip((dg_ref, db_ref, dcb_ref), parts):
                ref[...] += part

    vec = pl.BlockSpec((1, C), lambda i: (0, 0))
    row = pl.BlockSpec((tb, C), lambda i: (i, 0))
    vsd = jax.ShapeDtypeStruct((1, C), F32)
    return pl.pallas_call(
        body, name="conv_bwd_ln", grid=(T // tb,),
        in_specs=[row, pl.BlockSpec((tb, C), lambda i: (i, 6)), pl.BlockSpec((tb, C), lambda i: (i, 1)), vec, vec],
        out_specs=[row, row, vec, vec, vec],
        out_shape=[jax.ShapeDtypeStruct((T, C), F32), jax.ShapeDtypeStruct((T, C), BF16), vsd, vsd, vsd],
        compiler_params=_cp("arbitrary"))(z, proj, dy, ln_g, ln_b)


def _conv_bwd_taps(proj, dz, conv_w):
    T = proj.shape[0]
    C = conv_w.shape[1]
    tb = _tile(T, 256)
    per = tb // HALO
    n_i = T // tb

    def body(ga_ref, gb_ref, ha_ref, hb_ref, dz_ref, dzn_ref, w_ref, da_ref, db_ref, dw_ref, scr, dscr):
        i = pl.program_id(0)
        _glu_with_halo(i, ga_ref, gb_ref, ha_ref, hb_ref, scr)
        dz = dz_ref[...]
        dscr[0:tb, :] = dz
        dscr[tb:, :] = jnp.where(i < n_i - 1, dzn_ref[...], 0.0)
        dglu = _conv_taps(dscr, w_ref, tb, lambda j: CONV_K - 1 - j)
        ga = ga_ref[...]
        sb = _sigmoid(gb_ref[...])
        da_ref[...] = (dglu * sb).astype(da_ref.dtype)
        db_ref[...] = (dglu * ga * sb * (1.0 - sb)).astype(db_ref.dtype)

        @pl.when(i == 0)
        def _():
            dw_ref[...] = jnp.zeros_like(dw_ref)

        for j in range(CONV_K):
            dw_ref[j:j + 1, :] += jnp.sum(dz * scr[pl.ds(HALO - (CONV_K - 1) + j, tb), :], axis=0, keepdims=True)

    row = pl.BlockSpec((tb, C), lambda i: (i, 0))
    wspec = pl.BlockSpec((HALO, C), lambda i: (0, 0))
    return pl.pallas_call(
        body, name="conv_bwd_taps", grid=(n_i,),
        in_specs=_conv_in_specs(tb, C) + [row, pl.BlockSpec((HALO, C), lambda i: (jnp.minimum((i + 1) * per, T // HALO - 1), 0)),
                                          wspec],
        out_specs=[row, row, wspec],
        out_shape=[jax.ShapeDtypeStruct((T, C), BF16), jax.ShapeDtypeStruct((T, C), BF16),
                   jax.ShapeDtypeStruct((HALO, C), F32)],
        scratch_shapes=[pltpu.VMEM((HALO + tb, C), F32), pltpu.VMEM((tb + HALO, C), F32)],
        compiler_params=_cp("arbitrary"))(proj, proj, proj, proj, dz, dz, conv_w)


def _sgu_mask():
    r = lax.broadcasted_iota(I32, (GMLP_CHUNK, GMLP_CHUNK), 0) // CHUNK
    c = lax.broadcasted_iota(I32, (GMLP_CHUNK, GMLP_CHUNK), 1) // CHUNK
    return r >= c


def _sgu_fwd(proj, ln_g, ln_b, w_s, b_s_t):
    T = proj.shape[0]
    W = ln_g.shape[1]
    G = w_s.shape[0]
    cg = W // G
    tb = GMLP_CHUNK

    def body(u_ref, v_ref, gate_ref, g_ref, b_ref, ws_ref, bs_ref, y_ref):
        xhat, _ = _layernorm_stats(v_ref[...])
        vln = (xhat * g_ref[...] + b_ref[...]).astype(BF16)
        mask = _sgu_mask()
        for gi in range(G):
            cols = slice(gi * cg, (gi + 1) * cg)
            ws = jnp.where(mask, ws_ref[gi], 0.0).astype(BF16)
            sg = _dot(ws, vln[:, cols], NN) + bs_ref[:, gi:gi + 1]
            gate = gate_ref[:, cols]
            y_ref[:, cols] = (u_ref[:, cols] * sg * (gate * _sigmoid(gate))).astype(y_ref.dtype)

    vec = pl.BlockSpec((1, W), lambda i: (0, 0))
    return pl.pallas_call(
        body, name="sgu_fwd", grid=(T // tb,),
        in_specs=[pl.BlockSpec((tb, W), lambda i: (i, 0)), pl.BlockSpec((tb, W), lambda i: (i, 1)),
                  pl.BlockSpec((tb, W), lambda i: (i, 2)), vec, vec,
                  pl.BlockSpec((G, GMLP_CHUNK, GMLP_CHUNK), lambda i: (0, 0, 0)),
                  pl.BlockSpec((GMLP_CHUNK, G), lambda i: (0, 0))],
        out_specs=pl.BlockSpec((tb, W), lambda i: (i, 0)), out_shape=jax.ShapeDtypeStruct((T, W), BF16),
        compiler_params=_cp("parallel"))(proj, proj, proj, ln_g, ln_b, w_s, b_s_t)


def _sgu_bwd(proj, dy, ln_g, ln_b, w_s, b_s_t):
    T = proj.shape[0]
    W = ln_g.shape[1]
    G = w_s.shape[0]
    cg = W // G
    tb = GMLP_CHUNK

    def body(u_ref, v_ref, gate_ref, dy_ref, g_ref, b_ref, ws_ref, bs_ref,
             dp_ref, dws_ref, dbs_ref, dg_ref, db_ref, dvln_scr):
        i = pl.program_id(0)

        @pl.when(i == 0)
        def _():
            dws_ref[...] = jnp.zeros_like(dws_ref)
            dbs_ref[...] = jnp.zeros_like(dbs_ref)
            dg_ref[...] = jnp.zeros_like(dg_ref)
            db_ref[...] = jnp.zeros_like(db_ref)

        xhat, rstd = _layernorm_stats(v_ref[...])
        vln = (xhat * g_ref[...] + b_ref[...]).astype(BF16)
        mask = _sgu_mask()
        for gi in range(G):
            cols = slice(gi * cg, (gi + 1) * cg)
            ws = jnp.where(mask, ws_ref[gi], 0.0).astype(BF16)
            vg = vln[:, cols]
            sg = _dot(ws, vg, NN) + bs_ref[:, gi:gi + 1]
            gate = gate_ref[:, cols]
            s = _sigmoid(gate)
            u = u_ref[:, cols]
            dyv = dy_ref[:, cols]
            dyu = dyv * u
            dp_ref[:, cols] = (dyv * sg * (gate * s)).astype(dp_ref.dtype)
            dp_ref[:, 2 * W + gi * cg:2 * W + (gi + 1) * cg] = (dyu * sg * _dsilu(gate, s)).astype(dp_ref.dtype)
            dsg = dyu * (gate * s)
            dsgb = dsg.astype(BF16)
            dvln_scr[:, cols] = _dot(ws, dsgb, TN)
            dws_ref[gi] += _dot(dsgb, vg, NT)
            dbs_ref[:, gi:gi + 1] += jnp.sum(dsg, axis=-1, keepdims=True)
        dvln = dvln_scr[...]
        dg_ref[...] += jnp.sum(dvln * xhat, axis=0, keepdims=True)
        db_ref[...] += jnp.sum(dvln, axis=0, keepdims=True)
        dxh = dvln * g_ref[...]
        dv = rstd * (dxh - jnp.mean(dxh, axis=-1, keepdims=True) - xhat * jnp.mean(dxh * xhat, axis=-1, keepdims=True))
        dp_ref[:, W:2 * W] = dv.astype(dp_ref.dtype)

    vec = pl.BlockSpec((1, W), lambda i: (0, 0))
    wsp = pl.BlockSpec((G, GMLP_CHUNK, GMLP_CHUNK), lambda i: (0, 0, 0))
    bsp = pl.BlockSpec((GMLP_CHUNK, G), lambda i: (0, 0))
    return pl.pallas_call(
        body, name="sgu_bwd", grid=(T // tb,),
        in_specs=[pl.BlockSpec((tb, W), lambda i: (i, 0)), pl.BlockSpec((tb, W), lambda i: (i, 1)),
                  pl.BlockSpec((tb, W), lambda i: (i, 2)), pl.BlockSpec((tb, W), lambda i: (i, 0)), vec, vec, wsp, bsp],
        out_specs=[pl.BlockSpec((tb, 3 * W), lambda i: (i, 0)), wsp, bsp, vec, vec],
        out_shape=[jax.ShapeDtypeStruct((T, 3 * W), BF16), jax.ShapeDtypeStruct((G, GMLP_CHUNK, GMLP_CHUNK), F32),
                   jax.ShapeDtypeStruct((GMLP_CHUNK, G), F32), jax.ShapeDtypeStruct((1, W), F32),
                   jax.ShapeDtypeStruct((1, W), F32)],
        scratch_shapes=[pltpu.VMEM((tb, W), F32)],
        compiler_params=_cp("arbitrary"))(proj, proj, proj, dy, ln_g, ln_b, w_s, b_s_t)


def _xattn_probs(q, k, hd):
    s = _dot(q, k, NT) * (hd ** -0.5)
    e = jnp.exp(s - jnp.max(s, axis=-1, keepdims=True))
    return e / jnp.sum(e, axis=-1, keepdims=True)


def _xattn_fwd(name, q, k, v):
    T, D = q.shape
    M = k.shape[0]
    hd = D // N_HEADS_X
    tb = _tile(T, 512)

    def body(q_ref, k_ref, v_ref, o_ref):
        for h in range(N_HEADS_X):
            cols = slice(h * hd, (h + 1) * hd)
            p = _xattn_probs(q_ref[:, cols], k_ref[:, cols], hd)
            o_ref[:, cols] = _dot(p.astype(BF16), v_ref[:, cols], NN).astype(o_ref.dtype)

    row = pl.BlockSpec((tb, D), lambda i: (i, 0))
    kv = pl.BlockSpec((M, D), lambda i: (0, 0))
    return pl.pallas_call(
        body, name=name, grid=(T // tb,), in_specs=[row, kv, kv], out_specs=row,
        out_shape=jax.ShapeDtypeStruct((T, D), BF16), compiler_params=_cp("parallel"))(q, k, v)


def _xattn_bwd(name, q, k, v, do):
    T, D = q.shape
    M = k.shape[0]
    hd = D // N_HEADS_X
    tb = _tile(T, 512)

    def body(q_ref, k_ref, v_ref, do_ref, dq_ref, dk_ref, dv_ref):
        @pl.when(pl.program_id(0) == 0)
        def _():
            dk_ref[...] = jnp.zeros_like(dk_ref)
            dv_ref[...] = jnp.zeros_like(dv_ref)

        for h in range(N_HEADS_X):
            cols = slice(h * hd, (h + 1) * hd)
            qh, kh, doh = q_ref[:, cols], k_ref[:, cols], do_ref[:, cols]
            p = _xattn_probs(qh, kh, hd)
            dp = _dot(doh, v_ref[:, cols], NT)
            ds = p * (dp - jnp.sum(dp * p, axis=-1, keepdims=True))
            dsb = (ds * (hd ** -0.5)).astype(BF16)
            dq_ref[:, cols] = _dot(dsb, kh, NN).astype(dq_ref.dtype)
            dk_ref[:, cols] += _dot(dsb, qh, TN)
            dv_ref[:, cols] += _dot(p.astype(BF16), doh, TN)

    row = pl.BlockSpec((tb, D), lambda i: (i, 0))
    kv = pl.BlockSpec((M, D), lambda i: (0, 0))
    return pl.pallas_call(
        body, name=name, grid=(T // tb,), in_specs=[row, kv, kv, row], out_specs=[row, kv, kv],
        out_shape=[jax.ShapeDtypeStruct((T, D), BF16), jax.ShapeDtypeStruct((M, D), F32),
                   jax.ShapeDtypeStruct((M, D), F32)],
        compiler_params=_cp("arbitrary"))(q, k, v, do)


def _adamw(name, contrib, w, m, v):
    R, C = w.shape
    tr = min(R, 128)
    while R % tr:
        tr -= SUBLANE

    def body(c_ref, w_ref, m_ref, v_ref, g_ref, d_ref, nm_ref, nv_ref):
        g = c_ref[0].astype(F32)
        for s in range(1, N_DEV):
            g = g + c_ref[s].astype(F32)
        nm = ADAM_B1 * m_ref[...] + (1.0 - ADAM_B1) * g
        nv = ADAM_B2 * v_ref[...] + (1.0 - ADAM_B2) * (g * g)
        m_hat = nm / (1.0 - ADAM_B1 ** ADAM_STEP)
        v_hat = nv / (1.0 - ADAM_B2 ** ADAM_STEP)
        g_ref[...] = g
        d_ref[...] = -ADAM_LR * (m_hat / (jnp.sqrt(v_hat) + ADAM_EPS) + ADAM_WD * w_ref[...])
        nm_ref[...] = nm
        nv_ref[...] = nv

    row = pl.BlockSpec((tr, C), lambda i: (i, 0))
    sds = jax.ShapeDtypeStruct((R, C), F32)
    return pl.pallas_call(
        body, name=name, grid=(R // tr,),
        in_specs=[pl.BlockSpec((N_DEV, tr, C), lambda i: (0, i, 0)), row, row, row], out_specs=[row] * 4,
        out_shape=[sds] * 4, compiler_params=_cp("parallel"))(contrib, w, m, v)


def _pack(arrs):
    unit = SUBLANE * LANE
    flat = [jnp.pad(a.reshape(-1), (0, -a.size % unit)) for a in arrs]
    return jnp.concatenate(flat).reshape(-1, LANE)


def _unpack(buf, shapes):
    unit = SUBLANE * LANE
    flat = buf.reshape(-1)
    out, off = [], 0
    for s in shapes:
        size = 1
        for d in s:
            size *= d
        out.append(flat[off:off + size].reshape(s))
        off += size + (-size % unit)
    return out


def _cross_attention_fwd(l, h, mem, g_x, g_mem, wq, wk, wv, wo):
    hx = _rms_fwd(f"rms_x{l}", h, g_x)
    memn = _rms_fwd(f"rms_mem{l}", mem, g_mem)
    q = _mm_nn(f"xq{l}", hx, wq, BF16)
    k = _mm_nn(f"xk{l}", memn, wk, BF16)
    v = _mm_nn(f"xv{l}", memn, wv, BF16)
    o = _xattn_fwd(f"xattn_fwd{l}", q, k, v)
    h_out = _mm_nn(f"xo{l}", o, wo, F32, add=h)
    return h_out, (hx, memn, q, k, v, o)


def _cross_attention_bwd(l, dh, h, mem, g_x, g_mem, wq, wk, wv, wo, saved):
    hx, memn, q, k, v, o = saved
    do = _mm_nt(f"xo_dx{l}", dh, wo, BF16)
    dwo = _mm_tn(f"xo_dw{l}", o, dh, BF16)
    dq, dk, dv = _xattn_bwd(f"xattn_bwd{l}", q, k, v, do)
    dwq = _mm_tn(f"xq_dw{l}", hx, dq, BF16)
    dwk = _mm_tn(f"xk_dw{l}", memn, dk, BF16)
    dwv = _mm_tn(f"xv_dw{l}", memn, dv, BF16)
    dmemn = _mm_nt(f"xk_dx{l}", dk, wk, F32)
    dmemn = _mm_nt(f"xv_dx{l}", dv, wv, F32, add=dmemn)
    _, dg_mem = _rms_bwd(f"rms_mem_bwd{l}", dmemn, mem, g_mem, jnp.zeros_like(mem))
    dhx = _mm_nt(f"xq_dx{l}", dq, wq, F32)
    dh_in, dg_x = _rms_bwd(f"rms_x_bwd{l}", dhx, h, g_x, dh)
    return dh_in, (dg_x, dg_mem, dwq, dwk, dwv, dwo)


def kernel(x, mem, norm_mix_g, norm_x_g, norm_mem_g, final_norm_g, w_in_ab, rel_bias, conv_w, conv_b, conv_ln_g, conv_ln_b, w_out_ab, w_in_c, sgu_ln_g, sgu_ln_b, w_s, b_s, w_out_c, w_xq, w_xk, w_xv, w_xo, loss_target, m_norm_mix_g, m_norm_x_g, m_norm_mem_g, m_final_norm_g, m_w_in_ab, m_rel_bias, m_conv_w, m_conv_b, m_conv_ln_g, m_conv_ln_b, m_w_out_ab, m_w_in_c, m_sgu_ln_g, m_sgu_ln_b, m_w_s, m_b_s, m_w_out_c, m_w_xq, m_w_xk, m_w_xv, m_w_xo, v_norm_mix_g, v_norm_x_g, v_norm_mem_g, v_final_norm_g, v_w_in_ab, v_rel_bias, v_conv_w, v_conv_b, v_conv_ln_g, v_conv_ln_b, v_w_out_ab, v_w_in_c, v_sgu_ln_g, v_sgu_ln_b, v_w_s, v_b_s, v_w_out_c, v_w_xq, v_w_xk, v_w_xv, v_w_xo):
    names = ["norm_mix_g", "norm_x_g", "norm_mem_g", "final_norm_g", "w_in_ab", "rel_bias", "conv_w", "conv_b",
             "conv_ln_g", "conv_ln_b", "w_out_ab", "w_in_c", "sgu_ln_g", "sgu_ln_b", "w_s", "b_s", "w_out_c",
             "w_xq", "w_xk", "w_xv", "w_xo"]
    W = dict(zip(names, (norm_mix_g, norm_x_g, norm_mem_g, final_norm_g, w_in_ab, rel_bias, conv_w, conv_b, conv_ln_g,
                         conv_ln_b, w_out_ab, w_in_c, sgu_ln_g, sgu_ln_b, w_s, b_s, w_out_c, w_xq, w_xk, w_xv, w_xo)))
    M1 = dict(zip(names, (m_norm_mix_g, m_norm_x_g, m_norm_mem_g, m_final_norm_g, m_w_in_ab, m_rel_bias, m_conv_w, m_conv_b,
                          m_conv_ln_g, m_conv_ln_b, m_w_out_ab, m_w_in_c, m_sgu_ln_g, m_sgu_ln_b, m_w_s, m_b_s, m_w_out_c,
                          m_w_xq, m_w_xk, m_w_xv, m_w_xo)))
    M2 = dict(zip(names, (v_norm_mix_g, v_norm_x_g, v_norm_mem_g, v_final_norm_g, v_w_in_ab, v_rel_bias, v_conv_w, v_conv_b,
                          v_conv_ln_g, v_conv_ln_b, v_w_out_ab, v_w_in_c, v_sgu_ln_g, v_sgu_ln_b, v_w_s, v_b_s, v_w_out_c,
                          v_w_xq, v_w_xk, v_w_xv, v_w_xo)))

    h0, memv, tgt = x[0], mem[0], loss_target[0]
    T, D = h0.shape
    n_rel = rel_bias.shape[2]
    xnames = ["w_xq", "w_xk", "w_xv", "w_xo"]
    bf = lambda a: a.astype(BF16)
    blocks = lambda g: g.reshape(N_DEV, -1, D)

    small = _pack([conv_w[0], sgu_ln_g[0], sgu_ln_b[0]])
    win_ab, small_g = _comm_call("gather_in_ab", _Carry("gather", [bf(w_in_ab[0]), small]))
    per_cw, per_ln = conv_w.shape[2], sgu_ln_g.shape[1]
    cw_s, lg_s, lb_s = zip(*[_unpack(small_g[d], [(CONV_K, per_cw), (1, per_ln), (1, per_ln)]) for d in range(N_DEV)])
    conv_w_full = jnp.pad(jnp.concatenate(cw_s, axis=1), ((0, HALO - CONV_K), (0, 0)))
    sgu_g_full = jnp.concatenate(lg_s, axis=1)
    sgu_b_full = jnp.concatenate(lb_s, axis=1)
    b_s_t = b_s[0].T

    hn0 = _rms_fwd("rms_mix0", h0, norm_mix_g[0:1])
    ag1 = _Carry("gather", [bf(w_out_ab[0])] + [bf(W[n][0]) for n in xnames])
    proj_ab = _mm_nn("in_ab", hn0, win_ab, F32, carry=ag1)
    wout_ab = ag1.result[0].reshape(-1, D)
    wx0 = [g.reshape(D, D) for g in ag1.result[1:]]
    btile = _bias_tile(rel_bias[0])
    ag2 = _Carry("gather", [bf(w_in_c[0]), bf(w_out_c[0])] + [bf(W[n][1]) for n in xnames])
    ya, y_ab = _band_attn_fwd(proj_ab, btile, carry=ag2)
    win_c = ag2.result[0]
    wout_c = ag2.result[1].reshape(-1, D)
    wx1 = [g.reshape(D, D) for g in ag2.result[2:]]
    y_ab, z_conv = _conv_fwd(proj_ab, y_ab, conv_w_full, conv_b, conv_ln_g, conv_ln_b)
    h1 = _mm_nn("out_ab", y_ab, wout_ab, F32, add=h0)
    h2, xs0 = _cross_attention_fwd(0, h1, memv, norm_x_g[0:1], norm_mem_g[0:1], *wx0)
    hn1 = _rms_fwd("rms_mix1", h2, norm_mix_g[1:2])
    proj_c = _mm_nn("in_c", hn1, win_c, F32)
    y_c = _sgu_fwd(proj_c, sgu_g_full, sgu_b_full, w_s[0], b_s_t)
    h3 = _mm_nn("out_c", y_c, wout_c, F32, add=h2)
    h4, xs1 = _cross_attention_fwd(1, h3, memv, norm_x_g[1:2], norm_mem_g[1:2], *wx1)
    loss_acc, dh4, dg_final = _loss_head(h4, tgt, final_norm_g.reshape(1, D))

    dh3, gx1 = _cross_attention_bwd(1, dh4, h3, memv, norm_x_g[1:2], norm_mem_g[1:2], *wx1, xs1)
    dy_c = _mm_nt("out_c_dx", dh3, wout_c, F32)
    dwout_c = _mm_tn("out_c_dw", y_c, dh3, BF16)
    dproj_c, dws, dbs_t, dsgu_g, dsgu_b = _sgu_bwd(proj_c, dy_c, sgu_g_full, sgu_b_full, w_s[0], b_s_t)
    ex_a = _Carry("exchange", [blocks(g) for g in gx1[2:]] + [blocks(dwout_c)])
    dwin_c = _mm_tn("in_c_dw", hn1, dproj_c, BF16, per=win_c.shape[2], carry=ex_a)
    ex_b = _Carry("exchange", [dwin_c])
    dhn1 = _mm_nt("in_c_dx", dproj_c, win_c, F32, carry=ex_b)
    dh2, dg_mix1 = _rms_bwd("rms_mix1_bwd", dhn1, h2, norm_mix_g[1:2], dh3)
    dh1, gx0 = _cross_attention_bwd(0, dh2, h1, memv, norm_x_g[0:1], norm_mem_g[0:1], *wx0, xs0)
    dy_ab = _mm_nt("out_ab_dx", dh1, wout_ab, F32)
    dwout_ab = _mm_tn("out_ab_dw", y_ab, dh1, BF16)
    ex_c = _Carry("exchange", [blocks(g) for g in gx0[2:]] + [blocks(dwout_ab)])
    dq, dk, dv, dgate_a, dbtile = _band_attn_bwd(proj_ab, btile, dy_ab, ya, carry=ex_c)
    drel = _bias_tile_grad(dbtile, n_rel)
    dz, dgate_b, dcln_g, dcln_b, dconv_b = _conv_bwd_ln(proj_ab, z_conv, dy_ab, conv_ln_g, conv_ln_b)
    dglu_a, dglu_b, dconv_w = _conv_bwd_taps(proj_ab, dz, conv_w_full)
    dproj_ab = jnp.concatenate([dq, dk, dv, dglu_a, dglu_b, dgate_a, dgate_b], axis=1)
    dwin_ab = _mm_tn("in_ab_dw", hn0, dproj_ab, BF16, per=win_ab.shape[2])
    ex_d = _Carry("exchange", [dwin_ab])
    dhn0 = _mm_nt("in_ab_dx", dproj_ab, win_ab, F32, carry=ex_d)
    dx, dg_mix0 = _rms_bwd("rms_mix0_bwd", dhn0, h0, norm_mix_g[0:1], dh1)

    sm = [_pack([dconv_w[:CONV_K, d * per_cw:(d + 1) * per_cw], dsgu_g[:, d * per_ln:(d + 1) * per_ln],
                 dsgu_b[:, d * per_ln:(d + 1) * per_ln]]) for d in range(N_DEV)]
    mask = (jnp.arange(GMLP_CHUNK)[:, None] // CHUNK >= jnp.arange(GMLP_CHUNK)[None, :] // CHUNK).astype(F32)
    rep_names = ["norm_mix_g", "norm_x_g", "norm_mem_g", "final_norm_g", "rel_bias", "conv_b", "conv_ln_g", "conv_ln_b",
                 "w_s", "b_s"]
    rep_grads = {
        "norm_mix_g": jnp.concatenate([dg_mix0, dg_mix1], axis=0),
        "norm_x_g": jnp.concatenate([gx0[0], gx1[0]], axis=0),
        "norm_mem_g": jnp.concatenate([gx0[1], gx1[1]], axis=0),
        "final_norm_g": dg_final.reshape(D),
        "rel_bias": drel[None], "conv_b": dconv_b, "conv_ln_g": dcln_g, "conv_ln_b": dcln_b,
        "w_s": (dws * mask[None])[None], "b_s": dbs_t.T[None],
    }
    ex_e = _Carry("exchange", [jnp.stack(sm), _pack([rep_grads[n] for n in rep_names])], bcast=[False, True])
    recv_small, recv_rep = _comm_call("grad_exchange_small", ex_e)

    out = {}
    kinds = ("grad", "delta", "new_m", "new_v")
    recv_big = {"w_in_ab": ex_d.result[0], "w_out_ab": ex_c.result[4], "w_in_c": ex_b.result[0], "w_out_c": ex_a.result[4]}
    for n, contrib in recv_big.items():
        for kind, r in zip(kinds, _adamw(f"adamw_{n}", contrib, W[n][0], M1[n][0], M2[n][0])):
            out[(kind, n)] = r[None]
    for j, n in enumerate(xnames):
        res = [_adamw(f"adamw_{n}{l}", ex.result[j], W[n][l], M1[n][l], M2[n][l]) for l, ex in enumerate((ex_c, ex_a))]
        for kind, r in zip(kinds, zip(*res)):
            out[(kind, n)] = jnp.stack(r)
    sm_names = ["conv_w", "sgu_ln_g", "sgu_ln_b"]
    res = _adamw("adamw_small", recv_small, *[_pack([D_[n][0] for n in sm_names]) for D_ in (W, M1, M2)])
    for kind, r in zip(kinds, res):
        for n, piece in zip(sm_names, _unpack(r, [W[n].shape for n in sm_names])):
            out[(kind, n)] = piece
    res = _adamw("adamw_replicated", recv_rep, *[_pack([D_[n] for n in rep_names]) for D_ in (W, M1, M2)])
    for kind, r in zip(kinds, res):
        for n, piece in zip(rep_names, _unpack(r, [W[n].shape for n in rep_names])):
            out[(kind, n)] = piece

    loss = lax.psum(loss_acc[0, 0], MESH_AXES)
    return (loss, dx[None]) + tuple(out[(kind, n)] for kind in kinds for n in names)
```

```python
import jax
import jax.numpy as jnp
from jax import lax
from jax.experimental import pallas as pl
from jax.experimental.pallas import tpu as pltpu

F32 = jnp.float32
BF16 = jnp.bfloat16
I32 = jnp.int32

N_DEV = 8
CHUNK = 64
N_PAST = 8
MAX_REL = 128
HEAD_A = 128
CONV_K = 31
GMLP_CHUNK = 128
N_HEADS_X = 4
EPS = 1e-6
NEG = -1e30

ADAM_LR, ADAM_B1, ADAM_B2, ADAM_EPS, ADAM_WD, ADAM_STEP = 0.001, 0.9, 0.999, 1e-08, 0.01, 10

LANE = 128
SUBLANE = 8
VMEM_LIMIT = 56 * 1024 * 1024
QB = 4 * CHUNK
KW = QB + N_PAST * CHUNK
ROLL_W = 1024
REL_PAD = 384
HB = 2
BAND_W = HB * HEAD_A
HALO = 32
CONV_LC = LANE
CONV_RC = 64
MESH_AXES = ("x", "y", "c")

NN = (((1,), (0,)), ((), ()))
NT = (((1,), (1,)), ((), ()))
TN = (((0,), (0,)), ((), ()))


def _cp(*sem):
    return pltpu.CompilerParams(dimension_semantics=sem, vmem_limit_bytes=VMEM_LIMIT)


def _tile(dim, pref):
    if dim <= pref:
        return dim
    t = (pref // LANE) * LANE
    while dim % t:
        t -= LANE
    return t


def _sigmoid(x):
    return 1.0 / (1.0 + jnp.exp(-x))


def _dsilu(x, s):
    return s * (1.0 + x * (1.0 - s))


def _dot(a, b, dims):
    return lax.dot_general(a, b, dims, preferred_element_type=F32)


def _mesh_pos():
    return lax.axis_index("x"), lax.axis_index("y"), lax.axis_index("c")


def _lin(x, y, c):
    return 4 * x + 2 * y + c


def _remote(src, dst, send_sem, recv_sem, to):
    return pltpu.make_async_remote_copy(src_ref=src, dst_ref=dst, send_sem=send_sem, recv_sem=recv_sem,
                                        device_id=to, device_id_type=pl.DeviceIdType.MESH)


class _Carry:
    def __init__(self, kind, arrs, bcast=None):
        n = len(arrs)
        self.kind, self.arrs, self.n = kind, list(arrs), n
        self.bcast = [kind == "gather"] * n if bcast is None else list(bcast)
        self.out_shape = [jax.ShapeDtypeStruct(((N_DEV,) + a.shape) if b else a.shape, a.dtype)
                          for a, b in zip(arrs, self.bcast)]
        self.scratch = [pltpu.SemaphoreType.DMA((n, 7)), pltpu.SemaphoreType.DMA((n, 7)), pltpu.SemaphoreType.DMA((n,))]
        self.result = None

    def _src(self, ins, a, d):
        return ins[a] if self.bcast[a] else ins[a].at[d]

    def _local(self, ins, outs, sems):
        me = _lin(*_mesh_pos())
        return [pltpu.make_async_copy(self._src(ins, a, me), outs[a].at[me], sems[2].at[a]) for a in range(self.n)]

    @staticmethod
    def _chips():
        x, y, _ = _mesh_pos()
        return [(1 - x, y), (x, 1 - y), (1 - x, 1 - y)]

    def _g_copy(self, ins, outs, sems, a, k, block, to, own=False):
        rows = outs[a].at[_lin(*block)]
        return _remote(ins[a] if own else rows, rows, sems[0].at[a, k], sems[1].at[a, k], to)

    def _g_first(self, ins, outs, sems):
        x, y, c = _mesh_pos()
        cps = []
        for a in range(self.n):
            cps.append(self._g_copy(ins, outs, sems, a, 0, (x, y, c), (x, y, 1 - c), own=True))
            cps += [self._g_copy(ins, outs, sems, a, 1 + j, (x, y, c), (*chip, c), own=True)
                    for j, chip in enumerate(self._chips())]
        return cps

    def _g_passed(self, ins, outs, sems):
        x, y, c = _mesh_pos()
        return [self._g_copy(ins, outs, sems, a, 4 + j, (*chip, c), (x, y, 1 - c))
                for j, chip in enumerate(self._chips()) for a in range(self.n)]

    @staticmethod
    def _peer(k):
        x, y, c = _mesh_pos()
        return (1 - x if k & 4 else x, 1 - y if k & 2 else y, 1 - c if k & 1 else c)

    def _x_sends(self, ins, outs, sems):
        me = _lin(*_mesh_pos())
        return [_remote(self._src(ins, a, _lin(*self._peer(k))), outs[a].at[me], sems[0].at[a, k - 1],
                        sems[1].at[a, k - 1], self._peer(k)) for k in range(1, N_DEV) for a in range(self.n)]

    def start(self, ins, outs, sems):
        for cp in self._local(ins, outs, sems):
            cp.start()
        for cp in (self._g_first if self.kind == "gather" else self._x_sends)(ins, outs, sems):
            cp.start()

    def mid(self, ins, outs, sems):
        if self.kind != "gather":
            return
        x, y, c = _mesh_pos()
        passed = self._g_passed(ins, outs, sems)
        for j, chip in enumerate(self._chips()):
            for a in range(self.n):
                self._g_copy(ins, outs, sems, a, 1 + j, (*chip, c), (x, y, c)).wait_recv()
                passed[j * self.n + a].start()

    def finish(self, ins, outs, sems):
        x, y, c = _mesh_pos()
        if self.kind == "gather":
            for a in range(self.n):
                self._g_copy(ins, outs, sems, a, 0, (x, y, 1 - c), (x, y, c)).wait_recv()
            for j, chip in enumerate(self._chips()):
                for a in range(self.n):
                    self._g_copy(ins, outs, sems, a, 4 + j, (*chip, 1 - c), (x, y, c)).wait_recv()
            sent = self._g_first(ins, outs, sems) + self._g_passed(ins, outs, sems)
        else:
            for k in range(1, N_DEV):
                for a in range(self.n):
                    got = outs[a].at[_lin(*self._peer(k))]
                    _remote(got, got, sems[0].at[a, k - 1], sems[1].at[a, k - 1], self._peer(k)).wait_recv()
            sent = self._x_sends(ins, outs, sems)
        for cp in sent:
            cp.wait_send()
        for cp in self._local(ins, outs, sems):
            cp.wait()


def _comm_call(name, carry):
    n = carry.n

    def body(*refs):
        ins, outs, sems = refs[:n], refs[n:2 * n], refs[2 * n:]
        carry.start(ins, outs, sems)
        carry.mid(ins, outs, sems)
        carry.finish(ins, outs, sems)

    hbm = pl.BlockSpec(memory_space=pl.ANY)
    return pl.pallas_call(body, name=name, in_specs=[hbm] * n, out_specs=[hbm] * n, out_shape=carry.out_shape,
                          scratch_shapes=carry.scratch)(*carry.arrs)


def _call(name, body, *, grid, in_specs, out_specs, out_shape, args, sem, scratch=(), aliases=None, carry=None):
    aliases = aliases or {}
    if carry is None:
        return pl.pallas_call(body, name=name, grid=grid, in_specs=in_specs, out_specs=out_specs, out_shape=out_shape,
                              scratch_shapes=list(scratch), input_output_aliases=aliases, compiler_params=_cp(*sem))(*args)
    ni, no, ns, nc = len(in_specs), len(out_specs), len(scratch), carry.n
    total = 1
    for g in grid:
        total *= g

    def full(*refs):
        ins, cins = refs[:ni], refs[ni:ni + nc]
        outs, couts = refs[ni + nc:ni + nc + no], refs[ni + nc + no:ni + 2 * nc + no]
        scr, sems = refs[ni + 2 * nc + no:ni + 2 * nc + no + ns], refs[ni + 2 * nc + no + ns:]
        step = pl.program_id(0)
        for d in range(1, len(grid)):
            step = step * grid[d] + pl.program_id(d)

        @pl.when(step == 0)
        def _():
            carry.start(cins, couts, sems)

        body(*ins, *outs, *scr)

        @pl.when(step == total // 2)
        def _():
            carry.mid(cins, couts, sems)

        @pl.when(step == total - 1)
        def _():
            carry.finish(cins, couts, sems)

    hbm = pl.BlockSpec(memory_space=pl.ANY)
    res = pl.pallas_call(
        full, name=name, grid=grid, in_specs=list(in_specs) + [hbm] * nc, out_specs=list(out_specs) + [hbm] * nc,
        out_shape=list(out_shape) + carry.out_shape, scratch_shapes=list(scratch) + carry.scratch,
        input_output_aliases=aliases, compiler_params=_cp(*["arbitrary"] * len(grid)))(*args, *carry.arrs)
    carry.result = list(res[no:])
    return list(res[:no])


def _matmul(name, a, b, *, dims, grid, a_spec, b_spec, out_sds, out_spec, acc_shape, add=None, add_spec=None, carry=None):
    nk = grid[2]
    has_add = add is not None

    def body(*refs):
        a_ref, b_ref = refs[0], refs[1]
        o_ref = refs[3] if has_add else refs[2]
        part = _dot(a_ref[...].astype(BF16), b_ref[...].astype(BF16), dims)

        def finish(r):
            if has_add:
                r = r + refs[2][...]
            o_ref[...] = r.astype(o_ref.dtype)

        if nk == 1:
            finish(part)
        else:
            acc_ref = refs[-1]
            k = pl.program_id(2)

            @pl.when(k == 0)
            def _():
                acc_ref[...] = part

            @pl.when(k > 0)
            def _():
                acc_ref[...] += part

            @pl.when(k == nk - 1)
            def _():
                finish(acc_ref[...])

    in_specs = [a_spec, b_spec] + ([add_spec] if has_add else [])
    args = (a, b) + ((add,) if has_add else ())
    return _call(name, body, grid=grid, in_specs=in_specs, out_specs=[out_spec], out_shape=[out_sds], args=args,
                 sem=("parallel", "parallel", "arbitrary"), scratch=[pltpu.VMEM(acc_shape, F32)] if nk > 1 else [],
                 carry=carry)[0]


def _blk(per):
    return per if per <= 1024 else per // 2


def _mm_nn(name, a, b, out_dtype, *, add=None, tm=1024, tn=1024, tk=2048, carry=None):
    M, K = a.shape
    tm, tk = _tile(M, tm), _tile(K, tk)
    if b.ndim == 3:
        per = b.shape[2]
        tn = _blk(per)
        q = per // tn
        N = N_DEV * per
        b_spec = pl.BlockSpec((None, tk, tn), lambda m, n, k: (n // q, k, n % q))
    else:
        N = b.shape[1]
        tn = _tile(N, tn)
        b_spec = pl.BlockSpec((tk, tn), lambda m, n, k: (k, n))
    return _matmul(
        name, a, b, dims=NN, grid=(M // tm, N // tn, K // tk),
        a_spec=pl.BlockSpec((tm, tk), lambda m, n, k: (m, k)), b_spec=b_spec,
        out_sds=jax.ShapeDtypeStruct((M, N), out_dtype), out_spec=pl.BlockSpec((tm, tn), lambda m, n, k: (m, n)),
        acc_shape=(tm, tn), add=add, add_spec=pl.BlockSpec((tm, tn), lambda m, n, k: (m, n)), carry=carry)


def _mm_nt(name, a, b, out_dtype, *, add=None, tm=1024, tn=1024, tk=2048, carry=None):
    M, K = a.shape
    tm = _tile(M, tm)
    if b.ndim == 3:
        N, tk = b.shape[1], b.shape[2]
        tn = _tile(N, tn)
        b_spec = pl.BlockSpec((None, tn, tk), lambda m, n, k: (k, n, 0))
    else:
        N = b.shape[0]
        tn, tk = _tile(N, tn), _tile(K, tk)
        b_spec = pl.BlockSpec((tn, tk), lambda m, n, k: (n, k))
    return _matmul(
        name, a, b, dims=NT, grid=(M // tm, N // tn, K // tk),
        a_spec=pl.BlockSpec((tm, tk), lambda m, n, k: (m, k)), b_spec=b_spec,
        out_sds=jax.ShapeDtypeStruct((M, N), out_dtype), out_spec=pl.BlockSpec((tm, tn), lambda m, n, k: (m, n)),
        acc_shape=(tm, tn), add=add, add_spec=pl.BlockSpec((tm, tn), lambda m, n, k: (m, n)), carry=carry)


def _mm_tn(name, a, b, out_dtype, *, per=None, tm=1024, tn=1024, tk=2048, carry=None):
    K, M = a.shape
    N = b.shape[1]
    tm, tk = _tile(M, tm), _tile(K, tk)
    if per is not None:
        tn = _blk(per)
        q = per // tn
        out_sds = jax.ShapeDtypeStruct((N_DEV, M, per), out_dtype)
        out_spec = pl.BlockSpec((None, tm, tn), lambda m, n, k: (n // q, m, n % q))
    else:
        tn = _tile(N, tn)
        out_sds = jax.ShapeDtypeStruct((M, N), out_dtype)
        out_spec = pl.BlockSpec((tm, tn), lambda m, n, k: (m, n))
    return _matmul(
        name, a, b, dims=TN, grid=(M // tm, N // tn, K // tk),
        a_spec=pl.BlockSpec((tk, tm), lambda m, n, k: (k, m)), b_spec=pl.BlockSpec((tk, tn), lambda m, n, k: (k, n)),
        out_sds=out_sds, out_spec=out_spec, acc_shape=(tm, tn), carry=carry)


def _rms_fwd(name, h, g):
    T, D = h.shape
    tb = _tile(T, 512)

    def body(h_ref, g_ref, o_ref):
        x = h_ref[...]
        r = lax.rsqrt(jnp.mean(x * x, axis=-1, keepdims=True) + EPS)
        o_ref[...] = (x * r * g_ref[...]).astype(o_ref.dtype)

    return pl.pallas_call(
        body, name=name, grid=(T // tb,),
        in_specs=[pl.BlockSpec((tb, D), lambda i: (i, 0)), pl.BlockSpec((1, D), lambda i: (0, 0))],
        out_specs=pl.BlockSpec((tb, D), lambda i: (i, 0)), out_shape=jax.ShapeDtypeStruct((T, D), BF16),
        compiler_params=_cp("parallel"))(h, g)


def _rms_bwd(name, dhn, h, g, dres):
    T, D = h.shape
    tb = _tile(T, 256)

    def body(dhn_ref, h_ref, g_ref, dres_ref, dh_ref, dg_ref):
        i = pl.program_id(0)
        x = h_ref[...]
        r = lax.rsqrt(jnp.mean(x * x, axis=-1, keepdims=True) + EPS)
        y = x * r
        d = dhn_ref[...]
        dy = d * g_ref[...]
        dh_ref[...] = dres_ref[...] + r * (dy - y * jnp.mean(dy * y, axis=-1, keepdims=True))
        part = jnp.sum(d * y, axis=0, keepdims=True)

        @pl.when(i == 0)
        def _():
            dg_ref[...] = part

        @pl.when(i > 0)
        def _():
            dg_ref[...] += part

    row = pl.BlockSpec((tb, D), lambda i: (i, 0))
    vec = pl.BlockSpec((1, D), lambda i: (0, 0))
    return pl.pallas_call(
        body, name=name, grid=(T // tb,), in_specs=[row, row, vec, row], out_specs=[row, vec],
        out_shape=[jax.ShapeDtypeStruct((T, D), F32), jax.ShapeDtypeStruct((1, D), F32)],
        compiler_params=_cp("arbitrary"))(dhn, h, g, dres)


def _loss_head(h, tgt, g):
    T, D = h.shape
    tb = _tile(T, 256)

    def body(h_ref, t_ref, g_ref, loss_ref, dh_ref, dg_ref):
        i = pl.program_id(0)
        x = h_ref[...]
        gg = g_ref[...]
        r = lax.rsqrt(jnp.mean(x * x, axis=-1, keepdims=True) + EPS)
        y0 = x * r
        err = y0 * gg - t_ref[...]
        tot = 0.5 * jnp.sum(jnp.mean(err * err, axis=-1, keepdims=True), axis=0, keepdims=True)
        dy = err * (1.0 / D)
        dyg = dy * gg
        dh_ref[...] = r * (dyg - y0 * jnp.mean(dyg * y0, axis=-1, keepdims=True))
        part = jnp.sum(dy * y0, axis=0, keepdims=True)
        tot = jnp.broadcast_to(tot, loss_ref.shape)

        @pl.when(i == 0)
        def _():
            dg_ref[...] = part
            loss_ref[...] = tot

        @pl.when(i > 0)
        def _():
            dg_ref[...] += part
            loss_ref[...] += tot

    row = pl.BlockSpec((tb, D), lambda i: (i, 0))
    vec = pl.BlockSpec((1, D), lambda i: (0, 0))
    return pl.pallas_call(
        body, name="loss_head", grid=(T // tb,), in_specs=[row, row, vec],
        out_specs=[pl.BlockSpec((SUBLANE, LANE), lambda i: (0, 0)), row, vec],
        out_shape=[jax.ShapeDtypeStruct((SUBLANE, LANE), F32), jax.ShapeDtypeStruct((T, D), F32),
                   jax.ShapeDtypeStruct((1, D), F32)],
        compiler_params=_cp("arbitrary"))(h, tgt, g)


def _rel_onehot(pos_axis, shape):
    pos = lax.broadcasted_iota(I32, shape, pos_axis)
    r = lax.broadcasted_iota(I32, shape, 1 - pos_axis)
    d = jnp.where(pos < KW, N_PAST * CHUNK - pos, N_PAST * CHUNK + ROLL_W - pos)
    return (jnp.clip(d, -MAX_REL, MAX_REL) + MAX_REL == r).astype(F32)


def _roll_rows(x, left):
    row = lax.broadcasted_iota(I32, x.shape, 0)
    for b in range(QB.bit_length() - 1):
        shift = (ROLL_W - (1 << b)) if left else (1 << b)
        x = jnp.where(((row >> b) & 1) == 1, pltpu.roll(x, shift, 1), x)
    return x


def _bias_tile(rel_bias):
    H = rel_bias.shape[0]
    rb = jnp.pad(rel_bias, ((0, 0), (0, REL_PAD - rel_bias.shape[1]))).reshape(H, 1, REL_PAD)

    def body(rb_ref, o_ref):
        row = jnp.broadcast_to(rb_ref[...], (SUBLANE, REL_PAD))
        base = jnp.dot(row, _rel_onehot(1, (REL_PAD, ROLL_W)), precision=lax.Precision.HIGHEST,
                       preferred_element_type=F32)[0:1]
        tile = _roll_rows(jnp.broadcast_to(base, (QB, ROLL_W)), left=False)[:, :KW]
        qc = lax.broadcasted_iota(I32, (QB, KW), 0) // CHUNK
        kc = lax.broadcasted_iota(I32, (QB, KW), 1) // CHUNK - N_PAST
        o_ref[...] = jnp.where((kc >= qc - N_PAST) & (kc <= qc), tile, NEG)

    return pl.pallas_call(
        body, name="bias_tile", grid=(H,),
        in_specs=[pl.BlockSpec((None, 1, REL_PAD), lambda h: (h, 0, 0))],
        out_specs=pl.BlockSpec((None, QB, KW), lambda h: (h, 0, 0)),
        out_shape=jax.ShapeDtypeStruct((H, QB, KW), F32), compiler_params=_cp("parallel"))(rb)


def _bias_tile_grad(dtile, n_rel):
    H = dtile.shape[0]

    def body(dt_ref, o_ref):
        x = jnp.concatenate([dt_ref[...], jnp.zeros((QB, ROLL_W - KW), F32)], axis=1)
        cs = jnp.sum(_roll_rows(x, left=True), axis=0, keepdims=True)
        o_ref[...] = jnp.dot(jnp.broadcast_to(cs, (SUBLANE, ROLL_W)), _rel_onehot(0, (ROLL_W, REL_PAD)),
                             precision=lax.Precision.HIGHEST, preferred_element_type=F32)[0:1]

    out = pl.pallas_call(
        body, name="bias_tile_grad", grid=(H,),
        in_specs=[pl.BlockSpec((None, QB, KW), lambda h: (h, 0, 0))],
        out_specs=pl.BlockSpec((None, 1, REL_PAD), lambda h: (h, 0, 0)),
        out_shape=jax.ShapeDtypeStruct((H, 1, REL_PAD), F32), compiler_params=_cp("parallel"))(dtile)
    return out.reshape(H, REL_PAD)[:, :n_rel]


def _band_specs(G):
    spec = lambda f: pl.BlockSpec((QB, BAND_W), f)
    q = spec(lambda h, i: (i, h))
    ks = [spec(lambda h, i, r=r: (jnp.maximum(i - 2 + r, 0), G + h)) for r in range(3)]
    vs = [spec(lambda h, i, r=r: (jnp.maximum(i - 2 + r, 0), 2 * G + h)) for r in range(3)]
    gate = spec(lambda h, i: (i, 5 * G + h))
    bias = pl.BlockSpec((HB, QB, KW), lambda h, i: (h, 0, 0))
    return [q] + ks + vs + [gate, bias]


def _band_probs(i, hh, q_ref, k_refs, v_refs, bias_ref):
    cols = slice(hh * HEAD_A, (hh + 1) * HEAD_A)
    q = q_ref[:, cols].astype(BF16)
    k = jnp.concatenate([r[:, cols] for r in k_refs], axis=0).astype(BF16)
    v = jnp.concatenate([r[:, cols] for r in v_refs], axis=0).astype(BF16)
    s = _dot(q, k, NT) * (HEAD_A ** -0.5) + bias_ref[hh]
    kpos = (i - 2) * QB + lax.broadcasted_iota(I32, (1, KW), 1)
    s = jnp.where(kpos >= 0, s, NEG)
    e = jnp.exp(s - jnp.max(s, axis=-1, keepdims=True))
    p = e / jnp.sum(e, axis=-1, keepdims=True)
    return p, q, k, v


def _band_attn_fwd(proj, bias_tile, carry=None):
    T = proj.shape[0]
    H = bias_tile.shape[0]

    def body(q_ref, k0, k1, k2, v0, v1, v2, gate_ref, bias_ref, ya_ref, y_ref):
        for hh in range(HB):
            cols = slice(hh * HEAD_A, (hh + 1) * HEAD_A)
            p, _, _, v = _band_probs(pl.program_id(1), hh, q_ref, (k0, k1, k2), (v0, v1, v2), bias_ref)
            o = _dot(p.astype(BF16), v, NN)
            g = gate_ref[:, cols]
            ya_ref[:, cols] = o
            y_ref[:, cols] = (o * (g * _sigmoid(g))).astype(y_ref.dtype)

    out = pl.BlockSpec((QB, BAND_W), lambda h, i: (i, h))
    return _call(
        "band_attn_fwd", body, grid=(H // HB, T // QB), in_specs=_band_specs(H // HB), out_specs=[out, out],
        out_shape=[jax.ShapeDtypeStruct((T, H * HEAD_A), F32), jax.ShapeDtypeStruct((T, 2 * H * HEAD_A), BF16)],
        args=[proj] * 8 + [bias_tile], sem=("parallel", "parallel"), carry=carry)


def _band_attn_bwd(proj, bias_tile, dy, ya, carry=None):
    T = proj.shape[0]
    H = bias_tile.shape[0]
    n_i = T // QB

    def body(q_ref, k0, k1, k2, v0, v1, v2, gate_ref, bias_ref, dy_ref, ya_ref,
             dq_ref, dk_ref, dv_ref, dgate_ref, dbias_ref, dk_acc, dv_acc):
        i = pl.program_id(1)

        @pl.when(i == 0)
        def _():
            dk_acc[...] = jnp.zeros_like(dk_acc)
            dv_acc[...] = jnp.zeros_like(dv_acc)
            dbias_ref[...] = jnp.zeros_like(dbias_ref)

        for hh in range(HB):
            cols = slice(hh * HEAD_A, (hh + 1) * HEAD_A)
            p, q, k, v = _band_probs(i, hh, q_ref, (k0, k1, k2), (v0, v1, v2), bias_ref)
            g = gate_ref[:, cols]
            sg = _sigmoid(g)
            dyv = dy_ref[:, cols]
            dgate_ref[:, cols] = (dyv * ya_ref[:, cols] * _dsilu(g, sg)).astype(dgate_ref.dtype)
            do = (dyv * (g * sg)).astype(BF16)
            dp = _dot(do, v, NT)
            ds = p * (dp - jnp.sum(dp * p, axis=-1, keepdims=True))
            dbias_ref[hh] += ds
            dsb = (ds * (HEAD_A ** -0.5)).astype(BF16)
            dq_ref[:, cols] = _dot(dsb, k, NN).astype(dq_ref.dtype)
            dkc = _dot(dsb, q, TN)
            dvc = _dot(p.astype(BF16), do, TN)
            for r in range(3):
                blk = i - 2 + r

                @pl.when(blk >= 0)
                def _(r=r, blk=blk, cols=cols, dkc=dkc, dvc=dvc):
                    rows = pl.ds(pl.multiple_of(blk * QB, QB), QB)
                    dk_acc[rows, cols] += dkc[r * QB:(r + 1) * QB]
                    dv_acc[rows, cols] += dvc[r * QB:(r + 1) * QB]

        @pl.when(i == n_i - 1)
        def _():
            dk_ref[...] = dk_acc[...].astype(dk_ref.dtype)
            dv_ref[...] = dv_acc[...].astype(dv_ref.dtype)

    blk = pl.BlockSpec((QB, BAND_W), lambda h, i: (i, h))
    col = pl.BlockSpec((T, BAND_W), lambda h, i: (0, h))
    sds = jax.ShapeDtypeStruct((T, H * HEAD_A), BF16)
    return _call(
        "band_attn_bwd", body, grid=(H // HB, n_i), in_specs=_band_specs(H // HB) + [blk, blk],
        out_specs=[blk, col, col, blk, pl.BlockSpec((HB, QB, KW), lambda h, i: (h, 0, 0))],
        out_shape=[sds, sds, sds, sds, jax.ShapeDtypeStruct((H, QB, KW), F32)],
        args=[proj] * 8 + [bias_tile, dy, ya], sem=("parallel", "arbitrary"),
        scratch=[pltpu.VMEM((T, BAND_W), F32), pltpu.VMEM((T, BAND_W), F32)], carry=carry)


def _conv_in_specs(tb, C):
    per = tb // HALO
    nb = C // CONV_LC
    prev = lambda i: jnp.maximum(i * per - 1, 0)
    return [pl.BlockSpec((tb, CONV_LC), lambda c, i: (i, 3 * nb + c)), pl.BlockSpec((tb, CONV_LC), lambda c, i: (i, 4 * nb + c)),
            pl.BlockSpec((HALO, CONV_LC), lambda c, i: (prev(i), 3 * nb + c)),
            pl.BlockSpec((HALO, CONV_LC), lambda c, i: (prev(i), 4 * nb + c))]


def _conv_tb(T):
    return _tile(T, 1024) if T > 1024 else T // 2


def _glu_with_halo(i, ga_ref, gb_ref, ha_ref, hb_ref, scr):
    tb = ga_ref.shape[0]
    halo = ha_ref[...] * _sigmoid(hb_ref[...])
    scr[0:HALO, :] = jnp.where(i > 0, halo, 0.0)
    scr[HALO:HALO + tb, :] = ga_ref[...] * _sigmoid(gb_ref[...])
    scr[HALO + tb:, :] = jnp.zeros((SUBLANE, scr.shape[1]), F32)


def _chunk_taps(src, w_ref, row0, tap_of, uscr):
    acc = None
    for r in range(SUBLANE):
        u = None
        for a in range(HALO // SUBLANE + 1):
            j = tap_of(SUBLANE * a + r)
            if 0 <= j < CONV_K:
                rows = pl.ds(pl.multiple_of(row0 + SUBLANE * a, SUBLANE), CONV_RC + SUBLANE)
                t = w_ref[j:j + 1, :] * src[rows, :]
                u = t if u is None else u + t
        if u is None:
            continue
        if r == 0:
            piece = u[0:CONV_RC]
        else:
            uscr[...] = u
            piece = uscr[pl.ds(r, CONV_RC), :]
        acc = piece if acc is None else acc + piece
    return acc


def _layernorm_stats(z):
    mu = jnp.mean(z, axis=-1, keepdims=True)
    zc = z - mu
    rstd = lax.rsqrt(jnp.mean(zc * zc, axis=-1, keepdims=True) + EPS)
    return zc * rstd, rstd


def _conv_z(proj, conv_w, conv_b):
    T = proj.shape[0]
    C = conv_w.shape[1]
    tb = _conv_tb(T)

    def body(ga_ref, gb_ref, ha_ref, hb_ref, w_ref, cb_ref, z_ref, scr, uscr):
        _glu_with_halo(pl.program_id(1), ga_ref, gb_ref, ha_ref, hb_ref, scr)
        cb = cb_ref[...]

        def chunk(rc, carry):
            row0 = pl.multiple_of(rc * CONV_RC, CONV_RC)
            z = _chunk_taps(scr, w_ref, row0, lambda o: o - (HALO - (CONV_K - 1)), uscr)
            z_ref[pl.ds(row0, CONV_RC), :] = z + cb
            return carry

        lax.fori_loop(0, tb // CONV_RC, chunk, 0)

    return pl.pallas_call(
        body, name="conv_z", grid=(C // CONV_LC, T // tb),
        in_specs=_conv_in_specs(tb, C) + [pl.BlockSpec((HALO, CONV_LC), lambda c, i: (0, c)),
                                          pl.BlockSpec((1, CONV_LC), lambda c, i: (0, c))],
        out_specs=pl.BlockSpec((tb, CONV_LC), lambda c, i: (i, c)), out_shape=jax.ShapeDtypeStruct((T, C), F32),
        scratch_shapes=[pltpu.VMEM((HALO + tb + SUBLANE, CONV_LC), F32), pltpu.VMEM((CONV_RC + SUBLANE, CONV_LC), F32)],
        compiler_params=_cp("parallel", "parallel"))(proj, proj, proj, proj, conv_w, conv_b)


def _conv_ln_fwd(proj, z, y, ln_g, ln_b):
    T, C = z.shape
    tb = _tile(T, 256)

    def body(z_ref, gate_ref, g_ref, b_ref, y_in, y_ref):
        xhat, _ = _layernorm_stats(z_ref[...])
        ln = xhat * g_ref[...] + b_ref[...]
        gate = gate_ref[...]
        y_ref[...] = (ln * _sigmoid(ln) * (gate * _sigmoid(gate))).astype(y_ref.dtype)

    vec = pl.BlockSpec((1, C), lambda i: (0, 0))
    return pl.pallas_call(
        body, name="conv_ln_fwd", grid=(T // tb,),
        in_specs=[pl.BlockSpec((tb, C), lambda i: (i, 0)), pl.BlockSpec((tb, C), lambda i: (i, 6)), vec, vec,
                  pl.BlockSpec(memory_space=pl.ANY)],
        out_specs=pl.BlockSpec((tb, C), lambda i: (i, 1)), out_shape=jax.ShapeDtypeStruct(y.shape, y.dtype),
        input_output_aliases={4: 0}, compiler_params=_cp("parallel"))(z, proj, ln_g, ln_b, y)


def _conv_bwd_ln(proj, z, dy, ln_g, ln_b):
    T, C = z.shape
    tb = _tile(T, 256)

    def body(z_ref, gate_ref, dy_ref, g_ref, b_ref, dz_ref, dgate_ref, dg_ref, db_ref, dcb_ref):
        i = pl.program_id(0)
        xhat, rstd = _layernorm_stats(z_ref[...])
        ln = xhat * g_ref[...] + b_ref[...]
        sl = _sigmoid(ln)
        gate = gate_ref[...]
        sg = _sigmoid(gate)
        dyv = dy_ref[...]
        dgate_ref[...] = (dyv * (ln * sl) * _dsilu(gate, sg)).astype(dgate_ref.dtype)
        dln = dyv * (gate * sg) * _dsilu(ln, sl)
        dxh = dln * g_ref[...]
        dz = rstd * (dxh - jnp.mean(dxh, axis=-1, keepdims=True) - xhat * jnp.mean(dxh * xhat, axis=-1, keepdims=True))
        dz_ref[...] = dz
        parts = (jnp.sum(dln * xhat, axis=0, keepdims=True), jnp.sum(dln, axis=0, keepdims=True),
                 jnp.sum(dz, axis=0, keepdims=True))

        @pl.when(i == 0)
        def _():
            for ref, part in zip((dg_ref, db_ref, dcb_ref), parts):
                ref[...] = part

        @pl.when(i > 0)
        def _():
            for ref, part in zip((dg_ref, db_ref, dcb_ref), parts):
                ref[...] += part

    vec = pl.BlockSpec((1, C), lambda i: (0, 0))
    row = pl.BlockSpec((tb, C), lambda i: (i, 0))
    vsd = jax.ShapeDtypeStruct((1, C), F32)
    return pl.pallas_call(
        body, name="conv_bwd_ln", grid=(T // tb,),
        in_specs=[row, pl.BlockSpec((tb, C), lambda i: (i, 6)), pl.BlockSpec((tb, C), lambda i: (i, 1)), vec, vec],
        out_specs=[row, row, vec, vec, vec],
        out_shape=[jax.ShapeDtypeStruct((T, C), F32), jax.ShapeDtypeStruct((T, C), BF16), vsd, vsd, vsd],
        compiler_params=_cp("arbitrary"))(z, proj, dy, ln_g, ln_b)


def _conv_bwd_taps(proj, dz, conv_w):
    T = proj.shape[0]
    C = conv_w.shape[1]
    tb = _conv_tb(T)
    per = tb // HALO
    n_i = T // tb
    first = HALO - (CONV_K - 1)

    def body(ga_ref, gb_ref, ha_ref, hb_ref, dz_ref, dzn_ref, w_ref, da_ref, db_ref, dw_ref, scr, dscr, uscr, zscr, dwacc):
        i = pl.program_id(1)
        _glu_with_halo(i, ga_ref, gb_ref, ha_ref, hb_ref, scr)
        dscr[0:tb, :] = dz_ref[...]
        dscr[tb:tb + HALO, :] = jnp.where(i < n_i - 1, dzn_ref[...], 0.0)
        dscr[tb + HALO:, :] = jnp.zeros((SUBLANE, CONV_LC), F32)
        zscr[0:SUBLANE, :] = jnp.zeros((SUBLANE, CONV_LC), F32)
        zscr[SUBLANE + CONV_RC:, :] = jnp.zeros((SUBLANE, CONV_LC), F32)

        @pl.when(i == 0)
        def _():
            dwacc[...] = jnp.zeros_like(dwacc)

        def chunk(rc, carry):
            row0 = pl.multiple_of(rc * CONV_RC, CONV_RC)
            rows = pl.ds(row0, CONV_RC)
            dglu = _chunk_taps(dscr, w_ref, row0, lambda o: CONV_K - 1 - o, uscr)
            ga = ga_ref[rows, :]
            sb = _sigmoid(gb_ref[rows, :])
            da_ref[rows, :] = (dglu * sb).astype(da_ref.dtype)
            db_ref[rows, :] = (dglu * ga * sb * (1.0 - sb)).astype(db_ref.dtype)
            zscr[SUBLANE:SUBLANE + CONV_RC, :] = dz_ref[rows, :]
            for r in range(SUBLANE):
                dzs = zscr[pl.ds(SUBLANE - r, CONV_RC + SUBLANE), :]
                for a in range(HALO // SUBLANE + 1):
                    j = SUBLANE * a + r - first
                    if 0 <= j < CONV_K:
                        src = pl.ds(pl.multiple_of(row0 + SUBLANE * a, SUBLANE), CONV_RC + SUBLANE)
                        p = dzs * scr[src, :]
                        f = p[0:SUBLANE]
                        for s in range(1, CONV_RC // SUBLANE + 1):
                            f = f + p[s * SUBLANE:(s + 1) * SUBLANE]
                        dwacc[j * SUBLANE:(j + 1) * SUBLANE, :] += f
            return carry

        lax.fori_loop(0, tb // CONV_RC, chunk, 0)

        @pl.when(i == n_i - 1)
        def _():
            dw_ref[...] = jnp.zeros_like(dw_ref)
            for j in range(CONV_K):
                dw_ref[j:j + 1, :] = jnp.sum(dwacc[j * SUBLANE:(j + 1) * SUBLANE, :], axis=0, keepdims=True)

    blk = pl.BlockSpec((tb, CONV_LC), lambda c, i: (i, c))
    wspec = pl.BlockSpec((HALO, CONV_LC), lambda c, i: (0, c))
    nxt = pl.BlockSpec((HALO, CONV_LC), lambda c, i: (jnp.minimum((i + 1) * per, T // HALO - 1), c))
    return pl.pallas_call(
        body, name="conv_bwd_taps", grid=(C // CONV_LC, n_i),
        in_specs=_conv_in_specs(tb, C) + [blk, nxt, wspec], out_specs=[blk, blk, wspec],
        out_shape=[jax.ShapeDtypeStruct((T, C), BF16), jax.ShapeDtypeStruct((T, C), BF16),
                   jax.ShapeDtypeStruct((HALO, C), F32)],
        scratch_shapes=[pltpu.VMEM((HALO + tb + SUBLANE, CONV_LC), F32), pltpu.VMEM((tb + HALO + SUBLANE, CONV_LC), F32),
                        pltpu.VMEM((CONV_RC + SUBLANE, CONV_LC), F32), pltpu.VMEM((CONV_RC + 2 * SUBLANE, CONV_LC), F32),
                        pltpu.VMEM((HALO * SUBLANE, CONV_LC), F32)],
        compiler_params=_cp("parallel", "arbitrary"))(proj, proj, proj, proj, dz, dz, conv_w)


def _sgu_mask():
    r = lax.broadcasted_iota(I32, (GMLP_CHUNK, GMLP_CHUNK), 0) // CHUNK
    c = lax.broadcasted_iota(I32, (GMLP_CHUNK, GMLP_CHUNK), 1) // CHUNK
    return r >= c


def _sgu_fwd(proj, ln_g, ln_b, w_s, b_s_t):
    T = proj.shape[0]
    W = ln_g.shape[1]
    G = w_s.shape[0]
    cg = W // G
    tb = GMLP_CHUNK

    def body(u_ref, v_ref, gate_ref, g_ref, b_ref, ws_ref, bs_ref, y_ref):
        xhat, _ = _layernorm_stats(v_ref[...])
        vln = (xhat * g_ref[...] + b_ref[...]).astype(BF16)
        mask = _sgu_mask()
        for gi in range(G):
            cols = slice(gi * cg, (gi + 1) * cg)
            ws = jnp.where(mask, ws_ref[gi], 0.0).astype(BF16)
            sg = _dot(ws, vln[:, cols], NN) + bs_ref[:, gi:gi + 1]
            gate = gate_ref[:, cols]
            y_ref[:, cols] = (u_ref[:, cols] * sg * (gate * _sigmoid(gate))).astype(y_ref.dtype)

    vec = pl.BlockSpec((1, W), lambda i: (0, 0))
    return pl.pallas_call(
        body, name="sgu_fwd", grid=(T // tb,),
        in_specs=[pl.BlockSpec((tb, W), lambda i: (i, 0)), pl.BlockSpec((tb, W), lambda i: (i, 1)),
                  pl.BlockSpec((tb, W), lambda i: (i, 2)), vec, vec,
                  pl.BlockSpec((G, GMLP_CHUNK, GMLP_CHUNK), lambda i: (0, 0, 0)),
                  pl.BlockSpec((GMLP_CHUNK, G), lambda i: (0, 0))],
        out_specs=pl.BlockSpec((tb, W), lambda i: (i, 0)), out_shape=jax.ShapeDtypeStruct((T, W), BF16),
        compiler_params=_cp("parallel"))(proj, proj, proj, ln_g, ln_b, w_s, b_s_t)


def _sgu_bwd(proj, dy, ln_g, ln_b, w_s, b_s_t):
    T = proj.shape[0]
    W = ln_g.shape[1]
    G = w_s.shape[0]
    cg = W // G
    tb = GMLP_CHUNK

    def body(u_ref, v_ref, gate_ref, dy_ref, g_ref, b_ref, ws_ref, bs_ref,
             dp_ref, dws_ref, dbs_ref, dg_ref, db_ref, dvln_scr):
        i = pl.program_id(0)

        @pl.when(i == 0)
        def _():
            dws_ref[...] = jnp.zeros_like(dws_ref)
            dbs_ref[...] = jnp.zeros_like(dbs_ref)
            dg_ref[...] = jnp.zeros_like(dg_ref)
            db_ref[...] = jnp.zeros_like(db_ref)

        xhat, rstd = _layernorm_stats(v_ref[...])
        vln = (xhat * g_ref[...] + b_ref[...]).astype(BF16)
        mask = _sgu_mask()
        for gi in range(G):
            cols = slice(gi * cg, (gi + 1) * cg)
            ws = jnp.where(mask, ws_ref[gi], 0.0).astype(BF16)
            vg = vln[:, cols]
            sg = _dot(ws, vg, NN) + bs_ref[:, gi:gi + 1]
            gate = gate_ref[:, cols]
            s = _sigmoid(gate)
            u = u_ref[:, cols]
            dyv = dy_ref[:, cols]
            dyu = dyv * u
            dp_ref[:, cols] = (dyv * sg * (gate * s)).astype(dp_ref.dtype)
            dp_ref[:, 2 * W + gi * cg:2 * W + (gi + 1) * cg] = (dyu * sg * _dsilu(gate, s)).astype(dp_ref.dtype)
            dsg = dyu * (gate * s)
            dsgb = dsg.astype(BF16)
            dvln_scr[:, cols] = _dot(ws, dsgb, TN)
            dws_ref[gi] += _dot(dsgb, vg, NT)
            dbs_ref[:, gi:gi + 1] += jnp.sum(dsg, axis=-1, keepdims=True)
        dvln = dvln_scr[...]
        dg_ref[...] += jnp.sum(dvln * xhat, axis=0, keepdims=True)
        db_ref[...] += jnp.sum(dvln, axis=0, keepdims=True)
        dxh = dvln * g_ref[...]
        dv = rstd * (dxh - jnp.mean(dxh, axis=-1, keepdims=True) - xhat * jnp.mean(dxh * xhat, axis=-1, keepdims=True))
        dp_ref[:, W:2 * W] = dv.astype(dp_ref.dtype)

    vec = pl.BlockSpec((1, W), lambda i: (0, 0))
    wsp = pl.BlockSpec((G, GMLP_CHUNK, GMLP_CHUNK), lambda i: (0, 0, 0))
    bsp = pl.BlockSpec((GMLP_CHUNK, G), lambda i: (0, 0))
    return pl.pallas_call(
        body, name="sgu_bwd", grid=(T // tb,),
        in_specs=[pl.BlockSpec((tb, W), lambda i: (i, 0)), pl.BlockSpec((tb, W), lambda i: (i, 1)),
                  pl.BlockSpec((tb, W), lambda i: (i, 2)), pl.BlockSpec((tb, W), lambda i: (i, 0)), vec, vec, wsp, bsp],
        out_specs=[pl.BlockSpec((tb, 3 * W), lambda i: (i, 0)), wsp, bsp, vec, vec],
        out_shape=[jax.ShapeDtypeStruct((T, 3 * W), BF16), jax.ShapeDtypeStruct((G, GMLP_CHUNK, GMLP_CHUNK), F32),
                   jax.ShapeDtypeStruct((GMLP_CHUNK, G), F32), jax.ShapeDtypeStruct((1, W), F32),
                   jax.ShapeDtypeStruct((1, W), F32)],
        scratch_shapes=[pltpu.VMEM((tb, W), F32)],
        compiler_params=_cp("arbitrary"))(proj, proj, proj, dy, ln_g, ln_b, w_s, b_s_t)


def _xattn_probs(q, k, hd):
    s = _dot(q, k, NT) * (hd ** -0.5)
    e = jnp.exp(s - jnp.max(s, axis=-1, keepdims=True))
    return e / jnp.sum(e, axis=-1, keepdims=True)


def _xattn_fwd(name, q, k, v):
    T, D = q.shape
    M = k.shape[0]
    hd = D // N_HEADS_X
    tb = _tile(T, 512)

    def body(q_ref, k_ref, v_ref, o_ref):
        for h in range(N_HEADS_X):
            cols = slice(h * hd, (h + 1) * hd)
            p = _xattn_probs(q_ref[:, cols], k_ref[:, cols], hd)
            o_ref[:, cols] = _dot(p.astype(BF16), v_ref[:, cols], NN).astype(o_ref.dtype)

    row = pl.BlockSpec((tb, D), lambda i: (i, 0))
    kv = pl.BlockSpec((M, D), lambda i: (0, 0))
    return pl.pallas_call(
        body, name=name, grid=(T // tb,), in_specs=[row, kv, kv], out_specs=row,
        out_shape=jax.ShapeDtypeStruct((T, D), BF16), compiler_params=_cp("parallel"))(q, k, v)


def _xattn_bwd(name, q, k, v, do):
    T, D = q.shape
    M = k.shape[0]
    hd = D // N_HEADS_X
    tb = _tile(T, 512)

    def body(q_ref, k_ref, v_ref, do_ref, dq_ref, dk_ref, dv_ref):
        @pl.when(pl.program_id(0) == 0)
        def _():
            dk_ref[...] = jnp.zeros_like(dk_ref)
            dv_ref[...] = jnp.zeros_like(dv_ref)

        for h in range(N_HEADS_X):
            cols = slice(h * hd, (h + 1) * hd)
            qh, kh, doh = q_ref[:, cols], k_ref[:, cols], do_ref[:, cols]
            p = _xattn_probs(qh, kh, hd)
            dp = _dot(doh, v_ref[:, cols], NT)
            ds = p * (dp - jnp.sum(dp * p, axis=-1, keepdims=True))
            dsb = (ds * (hd ** -0.5)).astype(BF16)
            dq_ref[:, cols] = _dot(dsb, kh, NN).astype(dq_ref.dtype)
            dk_ref[:, cols] += _dot(dsb, qh, TN)
            dv_ref[:, cols] += _dot(p.astype(BF16), doh, TN)

    row = pl.BlockSpec((tb, D), lambda i: (i, 0))
    kv = pl.BlockSpec((M, D), lambda i: (0, 0))
    return pl.pallas_call(
        body, name=name, grid=(T // tb,), in_specs=[row, kv, kv, row], out_specs=[row, kv, kv],
        out_shape=[jax.ShapeDtypeStruct((T, D), BF16), jax.ShapeDtypeStruct((M, D), F32),
                   jax.ShapeDtypeStruct((M, D), F32)],
        compiler_params=_cp("arbitrary"))(q, k, v, do)


def _adamw(name, contrib, w, m, v):
    R, C = w.shape
    tr = min(R, 128)
    while R % tr:
        tr -= SUBLANE

    def body(c_ref, w_ref, m_ref, v_ref, g_ref, d_ref, nm_ref, nv_ref):
        g = c_ref[0].astype(F32)
        for s in range(1, N_DEV):
            g = g + c_ref[s].astype(F32)
        nm = ADAM_B1 * m_ref[...] + (1.0 - ADAM_B1) * g
        nv = ADAM_B2 * v_ref[...] + (1.0 - ADAM_B2) * (g * g)
        m_hat = nm / (1.0 - ADAM_B1 ** ADAM_STEP)
        v_hat = nv / (1.0 - ADAM_B2 ** ADAM_STEP)
        g_ref[...] = g
        d_ref[...] = -ADAM_LR * (m_hat / (jnp.sqrt(v_hat) + ADAM_EPS) + ADAM_WD * w_ref[...])
        nm_ref[...] = nm
        nv_ref[...] = nv

    row = pl.BlockSpec((tr, C), lambda i: (i, 0))
    sds = jax.ShapeDtypeStruct((R, C), F32)
    return pl.pallas_call(
        body, name=name, grid=(R // tr,),
        in_specs=[pl.BlockSpec((N_DEV, tr, C), lambda i: (0, i, 0)), row, row, row], out_specs=[row] * 4,
        out_shape=[sds] * 4, compiler_params=_cp("parallel"))(contrib, w, m, v)


def _pack(arrs):
    unit = SUBLANE * LANE
    flat = [jnp.pad(a.reshape(-1), (0, -a.size % unit)) for a in arrs]
    return jnp.concatenate(flat).reshape(-1, LANE)


def _unpack(buf, shapes):
    unit = SUBLANE * LANE
    flat = buf.reshape(-1)
    out, off = [], 0
    for s in shapes:
        size = 1
        for d in s:
            size *= d
        out.append(flat[off:off + size].reshape(s))
        off += size + (-size % unit)
    return out


def _cross_attention_fwd(l, h, mem, g_x, g_mem, wq, wk, wv, wo):
    hx = _rms_fwd(f"rms_x{l}", h, g_x)
    memn = _rms_fwd(f"rms_mem{l}", mem, g_mem)
    q = _mm_nn(f"xq{l}", hx, wq, BF16)
    k = _mm_nn(f"xk{l}", memn, wk, BF16)
    v = _mm_nn(f"xv{l}", memn, wv, BF16)
    o = _xattn_fwd(f"xattn_fwd{l}", q, k, v)
    h_out = _mm_nn(f"xo{l}", o, wo, F32, add=h)
    return h_out, (hx, memn, q, k, v, o)


def _cross_attention_bwd(l, dh, h, mem, g_x, g_mem, wq, wk, wv, wo, saved):
    hx, memn, q, k, v, o = saved
    do = _mm_nt(f"xo_dx{l}", dh, wo, BF16)
    dwo = _mm_tn(f"xo_dw{l}", o, dh, BF16)
    dq, dk, dv = _xattn_bwd(f"xattn_bwd{l}", q, k, v, do)
    dwq = _mm_tn(f"xq_dw{l}", hx, dq, BF16)
    dwk = _mm_tn(f"xk_dw{l}", memn, dk, BF16)
    dwv = _mm_tn(f"xv_dw{l}", memn, dv, BF16)
    dmemn = _mm_nt(f"xk_dx{l}", dk, wk, F32)
    dmemn = _mm_nt(f"xv_dx{l}", dv, wv, F32, add=dmemn)
    _, dg_mem = _rms_bwd(f"rms_mem_bwd{l}", dmemn, mem, g_mem, jnp.zeros_like(mem))
    dhx = _mm_nt(f"xq_dx{l}", dq, wq, F32)
    dh_in, dg_x = _rms_bwd(f"rms_x_bwd{l}", dhx, h, g_x, dh)
    return dh_in, (dg_x, dg_mem, dwq, dwk, dwv, dwo)


def kernel(x, mem, norm_mix_g, norm_x_g, norm_mem_g, final_norm_g, w_in_ab, rel_bias, conv_w, conv_b, conv_ln_g, conv_ln_b, w_out_ab, w_in_c, sgu_ln_g, sgu_ln_b, w_s, b_s, w_out_c, w_xq, w_xk, w_xv, w_xo, loss_target, m_norm_mix_g, m_norm_x_g, m_norm_mem_g, m_final_norm_g, m_w_in_ab, m_rel_bias, m_conv_w, m_conv_b, m_conv_ln_g, m_conv_ln_b, m_w_out_ab, m_w_in_c, m_sgu_ln_g, m_sgu_ln_b, m_w_s, m_b_s, m_w_out_c, m_w_xq, m_w_xk, m_w_xv, m_w_xo, v_norm_mix_g, v_norm_x_g, v_norm_mem_g, v_final_norm_g, v_w_in_ab, v_rel_bias, v_conv_w, v_conv_b, v_conv_ln_g, v_conv_ln_b, v_w_out_ab, v_w_in_c, v_sgu_ln_g, v_sgu_ln_b, v_w_s, v_b_s, v_w_out_c, v_w_xq, v_w_xk, v_w_xv, v_w_xo):
    names = ["norm_mix_g", "norm_x_g", "norm_mem_g", "final_norm_g", "w_in_ab", "rel_bias", "conv_w", "conv_b",
             "conv_ln_g", "conv_ln_b", "w_out_ab", "w_in_c", "sgu_ln_g", "sgu_ln_b", "w_s", "b_s", "w_out_c",
             "w_xq", "w_xk", "w_xv", "w_xo"]
    W = dict(zip(names, (norm_mix_g, norm_x_g, norm_mem_g, final_norm_g, w_in_ab, rel_bias, conv_w, conv_b, conv_ln_g,
                         conv_ln_b, w_out_ab, w_in_c, sgu_ln_g, sgu_ln_b, w_s, b_s, w_out_c, w_xq, w_xk, w_xv, w_xo)))
    M1 = dict(zip(names, (m_norm_mix_g, m_norm_x_g, m_norm_mem_g, m_final_norm_g, m_w_in_ab, m_rel_bias, m_conv_w, m_conv_b,
                          m_conv_ln_g, m_conv_ln_b, m_w_out_ab, m_w_in_c, m_sgu_ln_g, m_sgu_ln_b, m_w_s, m_b_s, m_w_out_c,
                          m_w_xq, m_w_xk, m_w_xv, m_w_xo)))
    M2 = dict(zip(names, (v_norm_mix_g, v_norm_x_g, v_norm_mem_g, v_final_norm_g, v_w_in_ab, v_rel_bias, v_conv_w, v_conv_b,
                          v_conv_ln_g, v_conv_ln_b, v_w_out_ab, v_w_in_c, v_sgu_ln_g, v_sgu_ln_b, v_w_s, v_b_s, v_w_out_c,
                          v_w_xq, v_w_xk, v_w_xv, v_w_xo)))

    h0, memv, tgt = x[0], mem[0], loss_target[0]
    T, D = h0.shape
    n_rel = rel_bias.shape[2]
    xnames = ["w_xq", "w_xk", "w_xv", "w_xo"]
    bf = lambda a: a.astype(BF16)
    blocks = lambda g: g.reshape(N_DEV, -1, D)

    small = _pack([conv_w[0], sgu_ln_g[0], sgu_ln_b[0]])
    win_ab, small_g = _comm_call("gather_in_ab", _Carry("gather", [bf(w_in_ab[0]), small]))
    per_cw, per_ln = conv_w.shape[2], sgu_ln_g.shape[1]
    cw_s, lg_s, lb_s = zip(*[_unpack(small_g[d], [(CONV_K, per_cw), (1, per_ln), (1, per_ln)]) for d in range(N_DEV)])
    conv_w_full = jnp.pad(jnp.concatenate(cw_s, axis=1), ((0, HALO - CONV_K), (0, 0)))
    sgu_g_full = jnp.concatenate(lg_s, axis=1)
    sgu_b_full = jnp.concatenate(lb_s, axis=1)
    b_s_t = b_s[0].T

    hn0 = _rms_fwd("rms_mix0", h0, norm_mix_g[0:1])
    ag1 = _Carry("gather", [bf(w_out_ab[0])] + [bf(W[n][0]) for n in xnames])
    proj_ab = _mm_nn("in_ab", hn0, win_ab, F32, carry=ag1)
    wout_ab = ag1.result[0].reshape(-1, D)
    wx0 = [g.reshape(D, D) for g in ag1.result[1:]]
    btile = _bias_tile(rel_bias[0])
    ag2 = _Carry("gather", [bf(w_in_c[0])])
    ya, y_ab = _band_attn_fwd(proj_ab, btile, carry=ag2)
    win_c = ag2.result[0]
    z_conv = _conv_z(proj_ab, conv_w_full, conv_b)
    y_ab = _conv_ln_fwd(proj_ab, z_conv, y_ab, conv_ln_g, conv_ln_b)
    h1 = _mm_nn("out_ab", y_ab, wout_ab, F32, add=h0)
    h2, xs0 = _cross_attention_fwd(0, h1, memv, norm_x_g[0:1], norm_mem_g[0:1], *wx0)
    hn1 = _rms_fwd("rms_mix1", h2, norm_mix_g[1:2])
    ag3 = _Carry("gather", [bf(w_out_c[0])] + [bf(W[n][1]) for n in xnames])
    proj_c = _mm_nn("in_c", hn1, win_c, F32, carry=ag3)
    wout_c = ag3.result[0].reshape(-1, D)
    wx1 = [g.reshape(D, D) for g in ag3.result[1:]]
    y_c = _sgu_fwd(proj_c, sgu_g_full, sgu_b_full, w_s[0], b_s_t)
    h3 = _mm_nn("out_c", y_c, wout_c, F32, add=h2)
    h4, xs1 = _cross_attention_fwd(1, h3, memv, norm_x_g[1:2], norm_mem_g[1:2], *wx1)
    loss_acc, dh4, dg_final = _loss_head(h4, tgt, final_norm_g.reshape(1, D))

    dh3, gx1 = _cross_attention_bwd(1, dh4, h3, memv, norm_x_g[1:2], norm_mem_g[1:2], *wx1, xs1)
    dy_c = _mm_nt("out_c_dx", dh3, wout_c, F32)
    dwout_c = _mm_tn("out_c_dw", y_c, dh3, BF16)
    dproj_c, dws, dbs_t, dsgu_g, dsgu_b = _sgu_bwd(proj_c, dy_c, sgu_g_full, sgu_b_full, w_s[0], b_s_t)
    ex_a = _Carry("exchange", [blocks(g) for g in gx1[2:]] + [blocks(dwout_c)])
    dwin_c = _mm_tn("in_c_dw", hn1, dproj_c, BF16, per=win_c.shape[2], carry=ex_a)
    ex_b = _Carry("exchange", [dwin_c])
    dhn1 = _mm_nt("in_c_dx", dproj_c, win_c, F32, carry=ex_b)
    dh2, dg_mix1 = _rms_bwd("rms_mix1_bwd", dhn1, h2, norm_mix_g[1:2], dh3)
    dh1, gx0 = _cross_attention_bwd(0, dh2, h1, memv, norm_x_g[0:1], norm_mem_g[0:1], *wx0, xs0)
    dy_ab = _mm_nt("out_ab_dx", dh1, wout_ab, F32)
    dwout_ab = _mm_tn("out_ab_dw", y_ab, dh1, BF16)
    ex_c = _Carry("exchange", [blocks(g) for g in gx0[2:]] + [blocks(dwout_ab)])
    dq, dk, dv, dgate_a, dbtile = _band_attn_bwd(proj_ab, btile, dy_ab, ya, carry=ex_c)
    drel = _bias_tile_grad(dbtile, n_rel)
    dz, dgate_b, dcln_g, dcln_b, dconv_b = _conv_bwd_ln(proj_ab, z_conv, dy_ab, conv_ln_g, conv_ln_b)
    dglu_a, dglu_b, dconv_w = _conv_bwd_taps(proj_ab, dz, conv_w_full)
    dproj_ab = jnp.concatenate([dq, dk, dv, dglu_a, dglu_b, dgate_a, dgate_b], axis=1)
    dwin_ab = _mm_tn("in_ab_dw", hn0, dproj_ab, BF16, per=win_ab.shape[2])
    ex_d = _Carry("exchange", [dwin_ab])
    dhn0 = _mm_nt("in_ab_dx", dproj_ab, win_ab, F32, carry=ex_d)
    dx, dg_mix0 = _rms_bwd("rms_mix0_bwd", dhn0, h0, norm_mix_g[0:1], dh1)

    sm = [_pack([dconv_w[:CONV_K, d * per_cw:(d + 1) * per_cw], dsgu_g[:, d * per_ln:(d + 1) * per_ln],
                 dsgu_b[:, d * per_ln:(d + 1) * per_ln]]) for d in range(N_DEV)]
    mask = (jnp.arange(GMLP_CHUNK)[:, None] // CHUNK >= jnp.arange(GMLP_CHUNK)[None, :] // CHUNK).astype(F32)
    rep_names = ["norm_mix_g", "norm_x_g", "norm_mem_g", "final_norm_g", "rel_bias", "conv_b", "conv_ln_g", "conv_ln_b",
                 "w_s", "b_s"]
    rep_grads = {
        "norm_mix_g": jnp.concatenate([dg_mix0, dg_mix1], axis=0),
        "norm_x_g": jnp.concatenate([gx0[0], gx1[0]], axis=0),
        "norm_mem_g": jnp.concatenate([gx0[1], gx1[1]], axis=0),
        "final_norm_g": dg_final.reshape(D),
        "rel_bias": drel[None], "conv_b": dconv_b, "conv_ln_g": dcln_g, "conv_ln_b": dcln_b,
        "w_s": (dws * mask[None])[None], "b_s": dbs_t.T[None],
    }
    ex_e = _Carry("exchange", [jnp.stack(sm), _pack([rep_grads[n] for n in rep_names])], bcast=[False, True])
    recv_small, recv_rep = _comm_call("grad_exchange_small", ex_e)

    out = {}
    kinds = ("grad", "delta", "new_m", "new_v")
    recv_big = {"w_in_ab": ex_d.result[0], "w_out_ab": ex_c.result[4], "w_in_c": ex_b.result[0], "w_out_c": ex_a.result[4]}
    for n, contrib in recv_big.items():
        for kind, r in zip(kinds, _adamw(f"adamw_{n}", contrib, W[n][0], M1[n][0], M2[n][0])):
            out[(kind, n)] = r[None]
    for j, n in enumerate(xnames):
        res = [_adamw(f"adamw_{n}{l}", ex.result[j], W[n][l], M1[n][l], M2[n][l]) for l, ex in enumerate((ex_c, ex_a))]
        for kind, r in zip(kinds, zip(*res)):
            out[(kind, n)] = jnp.stack(r)
    sm_names = ["conv_w", "sgu_ln_g", "sgu_ln_b"]
    res = _adamw("adamw_small", recv_small, *[_pack([D_[n][0] for n in sm_names]) for D_ in (W, M1, M2)])
    for kind, r in zip(kinds, res):
        for n, piece in zip(sm_names, _unpack(r, [W[n].shape for n in sm_names])):
            out[(kind, n)] = piece
    res = _adamw("adamw_replicated", recv_rep, *[_pack([D_[n] for n in rep_names]) for D_ in (W, M1, M2)])
    for kind, r in zip(kinds, res):
        for n, piece in zip(rep_names, _unpack(r, [W[n].shape for n in rep_names])):
            out[(kind, n)] = piece

    loss = lax.psum(loss_acc[0, 0], MESH_AXES)
    return (loss, dx[None]) + tuple(out[(kind, n)] for kind in kinds for n in names)
```

```python
import jax
import jax.numpy as jnp
from jax import lax
from jax.experimental import pallas as pl
from jax.experimental.pallas import tpu as pltpu

F32 = jnp.float32
BF16 = jnp.bfloat16
I32 = jnp.int32

N_DEV = 8
CHUNK = 64
N_PAST = 8
MAX_REL = 128
HEAD_A = 128
CONV_K = 31
GMLP_CHUNK = 128
N_HEADS_X = 4
EPS = 1e-6
NEG = -1e30

ADAM_LR, ADAM_B1, ADAM_B2, ADAM_EPS, ADAM_WD, ADAM_STEP = 0.001, 0.9, 0.999, 1e-08, 0.01, 10

LANE = 128
SUBLANE = 8
VMEM_LIMIT = 56 * 1024 * 1024
MATMUL_VMEM = 44 * 1024 * 1024
QB = 4 * CHUNK
KW = QB + N_PAST * CHUNK
ROLL_W = 1024
REL_PAD = 384
HB = 2
BAND_W = HB * HEAD_A
HALO = 32
CONV_LC = LANE
CONV_RC = 64
MESH_AXES = ("x", "y", "c")

NN = (((1,), (0,)), ((), ()))
NT = (((1,), (1,)), ((), ()))
TN = (((0,), (0,)), ((), ()))


def _cp(*sem):
    return pltpu.CompilerParams(dimension_semantics=sem, vmem_limit_bytes=VMEM_LIMIT)


def _tile(dim, pref):
    if dim <= pref:
        return dim
    t = (pref // LANE) * LANE
    while dim % t:
        t -= LANE
    return t


def _sigmoid(x):
    return 1.0 / (1.0 + jnp.exp(-x))


def _dsilu(x, s):
    return s * (1.0 + x * (1.0 - s))


def _dot(a, b, dims):
    return lax.dot_general(a, b, dims, preferred_element_type=F32)


def _mesh_pos():
    return lax.axis_index("x"), lax.axis_index("y"), lax.axis_index("c")


def _lin(x, y, c):
    return 4 * x + 2 * y + c


def _remote(src, dst, send_sem, recv_sem, to):
    return pltpu.make_async_remote_copy(src_ref=src, dst_ref=dst, send_sem=send_sem, recv_sem=recv_sem,
                                        device_id=to, device_id_type=pl.DeviceIdType.MESH)


class _Carry:
    def __init__(self, kind, arrs, bcast=None):
        n = len(arrs)
        self.kind, self.arrs, self.n = kind, list(arrs), n
        self.bcast = [kind == "gather"] * n if bcast is None else list(bcast)
        self.out_shape = [jax.ShapeDtypeStruct(((N_DEV,) + a.shape) if b else a.shape, a.dtype)
                          for a, b in zip(arrs, self.bcast)]
        self.scratch = [pltpu.SemaphoreType.DMA((n, 7)), pltpu.SemaphoreType.DMA((n, 7)), pltpu.SemaphoreType.DMA((n,))]
        self.result = None

    def _src(self, ins, a, d):
        return ins[a] if self.bcast[a] else ins[a].at[d]

    def _local(self, ins, outs, sems):
        me = _lin(*_mesh_pos())
        return [pltpu.make_async_copy(self._src(ins, a, me), outs[a].at[me], sems[2].at[a]) for a in range(self.n)]

    @staticmethod
    def _chips():
        x, y, _ = _mesh_pos()
        return [(1 - x, y), (x, 1 - y), (1 - x, 1 - y)]

    def _g_copy(self, ins, outs, sems, a, k, block, to, own=False):
        rows = outs[a].at[_lin(*block)]
        return _remote(ins[a] if own else rows, rows, sems[0].at[a, k], sems[1].at[a, k], to)

    def _g_first(self, ins, outs, sems):
        x, y, c = _mesh_pos()
        cps = []
        for a in range(self.n):
            cps.append(self._g_copy(ins, outs, sems, a, 0, (x, y, c), (x, y, 1 - c), own=True))
            cps += [self._g_copy(ins, outs, sems, a, 1 + j, (x, y, c), (*chip, c), own=True)
                    for j, chip in enumerate(self._chips())]
        return cps

    def _g_passed(self, ins, outs, sems):
        x, y, c = _mesh_pos()
        return [self._g_copy(ins, outs, sems, a, 4 + j, (*chip, c), (x, y, 1 - c))
                for j, chip in enumerate(self._chips()) for a in range(self.n)]

    @staticmethod
    def _peer(k):
        x, y, c = _mesh_pos()
        return (1 - x if k & 4 else x, 1 - y if k & 2 else y, 1 - c if k & 1 else c)

    def _x_sends(self, ins, outs, sems):
        me = _lin(*_mesh_pos())
        return [_remote(self._src(ins, a, _lin(*self._peer(k))), outs[a].at[me], sems[0].at[a, k - 1],
                        sems[1].at[a, k - 1], self._peer(k)) for k in range(1, N_DEV) for a in range(self.n)]

    def start(self, ins, outs, sems):
        for cp in self._local(ins, outs, sems):
            cp.start()
        for cp in (self._g_first if self.kind == "gather" else self._x_sends)(ins, outs, sems):
            cp.start()

    def mid(self, ins, outs, sems):
        if self.kind != "gather":
            return
        x, y, c = _mesh_pos()
        passed = self._g_passed(ins, outs, sems)
        for j, chip in enumerate(self._chips()):
            for a in range(self.n):
                self._g_copy(ins, outs, sems, a, 1 + j, (*chip, c), (x, y, c)).wait_recv()
                passed[j * self.n + a].start()

    def finish(self, ins, outs, sems):
        x, y, c = _mesh_pos()
        if self.kind == "gather":
            for a in range(self.n):
                self._g_copy(ins, outs, sems, a, 0, (x, y, 1 - c), (x, y, c)).wait_recv()
            for j, chip in enumerate(self._chips()):
                for a in range(self.n):
                    self._g_copy(ins, outs, sems, a, 4 + j, (*chip, 1 - c), (x, y, c)).wait_recv()
            sent = self._g_first(ins, outs, sems) + self._g_passed(ins, outs, sems)
        else:
            for k in range(1, N_DEV):
                for a in range(self.n):
                    got = outs[a].at[_lin(*self._peer(k))]
                    _remote(got, got, sems[0].at[a, k - 1], sems[1].at[a, k - 1], self._peer(k)).wait_recv()
            sent = self._x_sends(ins, outs, sems)
        for cp in sent:
            cp.wait_send()
        for cp in self._local(ins, outs, sems):
            cp.wait()


def _comm_call(name, carry):
    n = carry.n

    def body(*refs):
        ins, outs, sems = refs[:n], refs[n:2 * n], refs[2 * n:]
        carry.start(ins, outs, sems)
        carry.mid(ins, outs, sems)
        carry.finish(ins, outs, sems)

    hbm = pl.BlockSpec(memory_space=pl.ANY)
    return pl.pallas_call(body, name=name, in_specs=[hbm] * n, out_specs=[hbm] * n, out_shape=carry.out_shape,
                          scratch_shapes=carry.scratch)(*carry.arrs)


def _call(name, body, *, grid, in_specs, out_specs, out_shape, args, sem, scratch=(), aliases=None, carry=None):
    aliases = aliases or {}
    if carry is None:
        return pl.pallas_call(body, name=name, grid=grid, in_specs=in_specs, out_specs=out_specs, out_shape=out_shape,
                              scratch_shapes=list(scratch), input_output_aliases=aliases, compiler_params=_cp(*sem))(*args)
    ni, no, ns, nc = len(in_specs), len(out_specs), len(scratch), carry.n
    total = 1
    for g in grid:
        total *= g

    def full(*refs):
        ins, cins = refs[:ni], refs[ni:ni + nc]
        outs, couts = refs[ni + nc:ni + nc + no], refs[ni + nc + no:ni + 2 * nc + no]
        scr, sems = refs[ni + 2 * nc + no:ni + 2 * nc + no + ns], refs[ni + 2 * nc + no + ns:]
        step = pl.program_id(0)
        for d in range(1, len(grid)):
            step = step * grid[d] + pl.program_id(d)

        @pl.when(step == 0)
        def _():
            carry.start(cins, couts, sems)

        body(*ins, *outs, *scr)

        @pl.when(step == total // 2)
        def _():
            carry.mid(cins, couts, sems)

        @pl.when(step == total - 1)
        def _():
            carry.finish(cins, couts, sems)

    hbm = pl.BlockSpec(memory_space=pl.ANY)
    res = pl.pallas_call(
        full, name=name, grid=grid, in_specs=list(in_specs) + [hbm] * nc, out_specs=list(out_specs) + [hbm] * nc,
        out_shape=list(out_shape) + carry.out_shape, scratch_shapes=list(scratch) + carry.scratch,
        input_output_aliases=aliases, compiler_params=_cp(*["arbitrary"] * len(grid)))(*args, *carry.arrs)
    carry.result = list(res[no:])
    return list(res[:no])


def _matmul(name, a, b, *, dims, grid, a_spec, b_spec, out_sds, out_spec, acc_shape, add=None, add_spec=None, into=None,
            carry=None):
    nk = grid[2]
    has_add = add is not None
    has_into = into is not None

    def body(*refs):
        a_ref, b_ref = refs[0], refs[1]
        o_ref = refs[2 + has_add + has_into]
        part = _dot(a_ref[...].astype(BF16), b_ref[...].astype(BF16), dims)

        def finish(r):
            if has_add:
                r = r + refs[2][...]
            o_ref[...] = r.astype(o_ref.dtype)

        if nk == 1:
            finish(part)
        else:
            acc_ref = refs[-1]
            k = pl.program_id(2)

            @pl.when(k == 0)
            def _():
                acc_ref[...] = part

            @pl.when(k > 0)
            def _():
                acc_ref[...] += part

            @pl.when(k == nk - 1)
            def _():
                finish(acc_ref[...])

    in_specs = [a_spec, b_spec] + ([add_spec] if has_add else [])
    args = (a, b) + ((add,) if has_add else ())
    aliases = None
    if has_into:
        aliases = {len(in_specs): 0}
        in_specs = in_specs + [pl.BlockSpec(memory_space=pl.ANY)]
        args = args + (into,)
    return _call(name, body, grid=grid, in_specs=in_specs, out_specs=[out_spec], out_shape=[out_sds], args=args,
                 sem=("parallel", "parallel", "arbitrary"), scratch=[pltpu.VMEM(acc_shape, F32)] if nk > 1 else [],
                 aliases=aliases, carry=carry)[0]


def _blk(per):
    return per if per <= 1024 else per // 2


def _pick_tiles(M, N, K, a_dtype, b_dtype, out_dtype, has_add, tn=None, tk=None):
    isz = lambda dt: jnp.dtype(dt).itemsize
    tms = sorted({_tile(M, t) for t in (2048, 1024, 512)}, reverse=True)
    tks = [tk] if tk else sorted({_tile(K, t) for t in (2048, 1024, 512)}, reverse=True)
    tns = [tn] if tn else sorted({_tile(N, t) for t in (1024, 512)}, reverse=True)
    for m in tms:
        for k in tks:
            for n in tns:
                blocks = m * k * isz(a_dtype) + k * n * isz(b_dtype) + m * n * (isz(out_dtype) + (4 if has_add else 0))
                if 2 * blocks + (m * n * 4 if K > k else 0) <= MATMUL_VMEM:
                    return m, n, k
    raise ValueError("no matmul tiling fits")


def _mm_nn(name, a, b, out_dtype, *, add=None, half=None, into=None, carry=None):
    M, K = a.shape
    if half is not None:
        tn = b.shape[2]
        N = N_DEV * 2 * tn
        tm, tn, tk = _pick_tiles(M, N, K, a.dtype, b.dtype, out_dtype, False, tn=tn)
        return _matmul(
            name, a, b, dims=NN, grid=(M // tm, N_DEV, K // tk),
            a_spec=pl.BlockSpec((tm, tk), lambda m, n, k: (m, k)), b_spec=pl.BlockSpec((None, tk, tn), lambda m, n, k: (n, k, 0)),
            out_sds=jax.ShapeDtypeStruct((M, N), out_dtype), out_spec=pl.BlockSpec((tm, tn), lambda m, n, k: (m, 2 * n + half)),
            acc_shape=(tm, tn), into=into, carry=carry)
    if b.ndim == 3:
        per = b.shape[2]
        N = N_DEV * per
        tm, tn, tk = _pick_tiles(M, N, K, a.dtype, b.dtype, out_dtype, add is not None, tn=_blk(per))
        q = per // tn
        b_spec = pl.BlockSpec((None, tk, tn), lambda m, n, k: (n // q, k, n % q))
    else:
        N = b.shape[1]
        tm, tn, tk = _pick_tiles(M, N, K, a.dtype, b.dtype, out_dtype, add is not None)
        b_spec = pl.BlockSpec((tk, tn), lambda m, n, k: (k, n))
    return _matmul(
        name, a, b, dims=NN, grid=(M // tm, N // tn, K // tk),
        a_spec=pl.BlockSpec((tm, tk), lambda m, n, k: (m, k)), b_spec=b_spec,
        out_sds=jax.ShapeDtypeStruct((M, N), out_dtype), out_spec=pl.BlockSpec((tm, tn), lambda m, n, k: (m, n)),
        acc_shape=(tm, tn), add=add, add_spec=pl.BlockSpec((tm, tn), lambda m, n, k: (m, n)), carry=carry)


def _mm_nt(name, a, b, out_dtype, *, add=None, carry=None):
    M, K = a.shape
    if b.ndim == 3:
        N = b.shape[1]
        tm, tn, tk = _pick_tiles(M, N, K, a.dtype, b.dtype, out_dtype, add is not None, tk=b.shape[2])
        b_spec = pl.BlockSpec((None, tn, tk), lambda m, n, k: (k, n, 0))
    else:
        N = b.shape[0]
        tm, tn, tk = _pick_tiles(M, N, K, a.dtype, b.dtype, out_dtype, add is not None)
        b_spec = pl.BlockSpec((tn, tk), lambda m, n, k: (n, k))
    return _matmul(
        name, a, b, dims=NT, grid=(M // tm, N // tn, K // tk),
        a_spec=pl.BlockSpec((tm, tk), lambda m, n, k: (m, k)), b_spec=b_spec,
        out_sds=jax.ShapeDtypeStruct((M, N), out_dtype), out_spec=pl.BlockSpec((tm, tn), lambda m, n, k: (m, n)),
        acc_shape=(tm, tn), add=add, add_spec=pl.BlockSpec((tm, tn), lambda m, n, k: (m, n)), carry=carry)


def _mm_tn(name, a, b, out_dtype, *, per=None, carry=None):
    K, M = a.shape
    N = b.shape[1]
    tm, tn, tk = _pick_tiles(M, N, K, a.dtype, b.dtype, out_dtype, False, tn=_blk(per) if per else None)
    if per is not None:
        q = per // tn
        out_sds = jax.ShapeDtypeStruct((N_DEV, M, per), out_dtype)
        out_spec = pl.BlockSpec((None, tm, tn), lambda m, n, k: (n // q, m, n % q))
    else:
        out_sds = jax.ShapeDtypeStruct((M, N), out_dtype)
        out_spec = pl.BlockSpec((tm, tn), lambda m, n, k: (m, n))
    return _matmul(
        name, a, b, dims=TN, grid=(M // tm, N // tn, K // tk),
        a_spec=pl.BlockSpec((tk, tm), lambda m, n, k: (k, m)), b_spec=pl.BlockSpec((tk, tn), lambda m, n, k: (k, n)),
        out_sds=out_sds, out_spec=out_spec, acc_shape=(tm, tn), carry=carry)


def _rms_fwd(name, h, g):
    T, D = h.shape
    tb = _tile(T, 512)

    def body(h_ref, g_ref, o_ref):
        x = h_ref[...]
        r = lax.rsqrt(jnp.mean(x * x, axis=-1, keepdims=True) + EPS)
        o_ref[...] = (x * r * g_ref[...]).astype(o_ref.dtype)

    return pl.pallas_call(
        body, name=name, grid=(T // tb,),
        in_specs=[pl.BlockSpec((tb, D), lambda i: (i, 0)), pl.BlockSpec((1, D), lambda i: (0, 0))],
        out_specs=pl.BlockSpec((tb, D), lambda i: (i, 0)), out_shape=jax.ShapeDtypeStruct((T, D), BF16),
        compiler_params=_cp("parallel"))(h, g)


def _rms_bwd(name, dhn, h, g, dres):
    T, D = h.shape
    tb = _tile(T, 256)

    def body(dhn_ref, h_ref, g_ref, dres_ref, dh_ref, dg_ref):
        i = pl.program_id(0)
        x = h_ref[...]
        r = lax.rsqrt(jnp.mean(x * x, axis=-1, keepdims=True) + EPS)
        y = x * r
        d = dhn_ref[...]
        dy = d * g_ref[...]
        dh_ref[...] = dres_ref[...] + r * (dy - y * jnp.mean(dy * y, axis=-1, keepdims=True))
        part = jnp.sum(d * y, axis=0, keepdims=True)

        @pl.when(i == 0)
        def _():
            dg_ref[...] = part

        @pl.when(i > 0)
        def _():
            dg_ref[...] += part

    row = pl.BlockSpec((tb, D), lambda i: (i, 0))
    vec = pl.BlockSpec((1, D), lambda i: (0, 0))
    return pl.pallas_call(
        body, name=name, grid=(T // tb,), in_specs=[row, row, vec, row], out_specs=[row, vec],
        out_shape=[jax.ShapeDtypeStruct((T, D), F32), jax.ShapeDtypeStruct((1, D), F32)],
        compiler_params=_cp("arbitrary"))(dhn, h, g, dres)


def _loss_head(h, tgt, g):
    T, D = h.shape
    tb = _tile(T, 256)

    def body(h_ref, t_ref, g_ref, loss_ref, dh_ref, dg_ref):
        i = pl.program_id(0)
        x = h_ref[...]
        gg = g_ref[...]
        r = lax.rsqrt(jnp.mean(x * x, axis=-1, keepdims=True) + EPS)
        y0 = x * r
        err = y0 * gg - t_ref[...]
        tot = 0.5 * jnp.sum(jnp.mean(err * err, axis=-1, keepdims=True), axis=0, keepdims=True)
        dy = err * (1.0 / D)
        dyg = dy * gg
        dh_ref[...] = r * (dyg - y0 * jnp.mean(dyg * y0, axis=-1, keepdims=True))
        part = jnp.sum(dy * y0, axis=0, keepdims=True)
        tot = jnp.broadcast_to(tot, loss_ref.shape)

        @pl.when(i == 0)
        def _():
            dg_ref[...] = part
            loss_ref[...] = tot

        @pl.when(i > 0)
        def _():
            dg_ref[...] += part
            loss_ref[...] += tot

    row = pl.BlockSpec((tb, D), lambda i: (i, 0))
    vec = pl.BlockSpec((1, D), lambda i: (0, 0))
    return pl.pallas_call(
        body, name="loss_head", grid=(T // tb,), in_specs=[row, row, vec],
        out_specs=[pl.BlockSpec((SUBLANE, LANE), lambda i: (0, 0)), row, vec],
        out_shape=[jax.ShapeDtypeStruct((SUBLANE, LANE), F32), jax.ShapeDtypeStruct((T, D), F32),
                   jax.ShapeDtypeStruct((1, D), F32)],
        compiler_params=_cp("arbitrary"))(h, tgt, g)


def _rel_onehot(pos_axis, shape):
    pos = lax.broadcasted_iota(I32, shape, pos_axis)
    r = lax.broadcasted_iota(I32, shape, 1 - pos_axis)
    d = jnp.where(pos < KW, N_PAST * CHUNK - pos, N_PAST * CHUNK + ROLL_W - pos)
    return (jnp.clip(d, -MAX_REL, MAX_REL) + MAX_REL == r).astype(F32)


def _roll_rows(x, left):
    row = lax.broadcasted_iota(I32, x.shape, 0)
    for b in range(QB.bit_length() - 1):
        shift = (ROLL_W - (1 << b)) if left else (1 << b)
        x = jnp.where(((row >> b) & 1) == 1, pltpu.roll(x, shift, 1), x)
    return x


def _bias_tile(rel_bias):
    H = rel_bias.shape[0]
    rb = jnp.pad(rel_bias, ((0, 0), (0, REL_PAD - rel_bias.shape[1]))).reshape(H, 1, REL_PAD)

    def body(rb_ref, o_ref):
        row = jnp.broadcast_to(rb_ref[...], (SUBLANE, REL_PAD))
        base = jnp.dot(row, _rel_onehot(1, (REL_PAD, ROLL_W)), precision=lax.Precision.HIGHEST,
                       preferred_element_type=F32)[0:1]
        tile = _roll_rows(jnp.broadcast_to(base, (QB, ROLL_W)), left=False)[:, :KW]
        qc = lax.broadcasted_iota(I32, (QB, KW), 0) // CHUNK
        kc = lax.broadcasted_iota(I32, (QB, KW), 1) // CHUNK - N_PAST
        o_ref[...] = jnp.where((kc >= qc - N_PAST) & (kc <= qc), tile, NEG)

    return pl.pallas_call(
        body, name="bias_tile", grid=(H,),
        in_specs=[pl.BlockSpec((None, 1, REL_PAD), lambda h: (h, 0, 0))],
        out_specs=pl.BlockSpec((None, QB, KW), lambda h: (h, 0, 0)),
        out_shape=jax.ShapeDtypeStruct((H, QB, KW), F32), compiler_params=_cp("parallel"))(rb)


def _bias_tile_grad(dtile, n_rel):
    H = dtile.shape[0]

    def body(dt_ref, o_ref):
        x = jnp.concatenate([dt_ref[...], jnp.zeros((QB, ROLL_W - KW), F32)], axis=1)
        cs = jnp.sum(_roll_rows(x, left=True), axis=0, keepdims=True)
        o_ref[...] = jnp.dot(jnp.broadcast_to(cs, (SUBLANE, ROLL_W)), _rel_onehot(0, (ROLL_W, REL_PAD)),
                             precision=lax.Precision.HIGHEST, preferred_element_type=F32)[0:1]

    out = pl.pallas_call(
        body, name="bias_tile_grad", grid=(H,),
        in_specs=[pl.BlockSpec((None, QB, KW), lambda h: (h, 0, 0))],
        out_specs=pl.BlockSpec((None, 1, REL_PAD), lambda h: (h, 0, 0)),
        out_shape=jax.ShapeDtypeStruct((H, 1, REL_PAD), F32), compiler_params=_cp("parallel"))(dtile)
    return out.reshape(H, REL_PAD)[:, :n_rel]


def _band_specs(G):
    spec = lambda f: pl.BlockSpec((QB, BAND_W), f)
    q = spec(lambda h, i: (i, h))
    ks = [spec(lambda h, i, r=r: (jnp.maximum(i - 2 + r, 0), G + h)) for r in range(3)]
    vs = [spec(lambda h, i, r=r: (jnp.maximum(i - 2 + r, 0), 2 * G + h)) for r in range(3)]
    gate = spec(lambda h, i: (i, 5 * G + h))
    bias = pl.BlockSpec((HB, QB, KW), lambda h, i: (h, 0, 0))
    return [q] + ks + vs + [gate, bias]


def _band_probs(i, hh, q_ref, k_refs, v_refs, bias_ref):
    cols = slice(hh * HEAD_A, (hh + 1) * HEAD_A)
    q = q_ref[:, cols].astype(BF16)
    k = jnp.concatenate([r[:, cols] for r in k_refs], axis=0).astype(BF16)
    v = jnp.concatenate([r[:, cols] for r in v_refs], axis=0).astype(BF16)
    s = _dot(q, k, NT) * (HEAD_A ** -0.5) + bias_ref[hh]
    kpos = (i - 2) * QB + lax.broadcasted_iota(I32, (1, KW), 1)
    s = jnp.where(kpos >= 0, s, NEG)
    e = jnp.exp(s - jnp.max(s, axis=-1, keepdims=True))
    p = e * (1.0 / jnp.sum(e, axis=-1, keepdims=True))
    return p, q, k, v


def _band_attn_fwd(proj, bias_tile, carry=None):
    T = proj.shape[0]
    H = bias_tile.shape[0]

    def body(q_ref, k0, k1, k2, v0, v1, v2, gate_ref, bias_ref, ya_ref, y_ref):
        for hh in range(HB):
            cols = slice(hh * HEAD_A, (hh + 1) * HEAD_A)
            p, _, _, v = _band_probs(pl.program_id(1), hh, q_ref, (k0, k1, k2), (v0, v1, v2), bias_ref)
            o = _dot(p.astype(BF16), v, NN)
            g = gate_ref[:, cols]
            ya_ref[:, cols] = o
            y_ref[:, cols] = (o * (g * _sigmoid(g))).astype(y_ref.dtype)

    out = pl.BlockSpec((QB, BAND_W), lambda h, i: (i, h))
    return _call(
        "band_attn_fwd", body, grid=(H // HB, T // QB), in_specs=_band_specs(H // HB), out_specs=[out, out],
        out_shape=[jax.ShapeDtypeStruct((T, H * HEAD_A), F32), jax.ShapeDtypeStruct((T, 2 * H * HEAD_A), BF16)],
        args=[proj] * 8 + [bias_tile], sem=("parallel", "parallel"), carry=carry)


def _band_attn_bwd(proj, bias_tile, dy, ya, carry=None):
    T = proj.shape[0]
    H = bias_tile.shape[0]
    n_i = T // QB

    def body(q_ref, k0, k1, k2, v0, v1, v2, gate_ref, bias_ref, dy_ref, ya_ref,
             dq_ref, dk_ref, dv_ref, dgate_ref, dbias_ref, dk_acc, dv_acc):
        i = pl.program_id(1)

        @pl.when(i == 0)
        def _():
            dk_acc[...] = jnp.zeros_like(dk_acc)
            dv_acc[...] = jnp.zeros_like(dv_acc)
            dbias_ref[...] = jnp.zeros_like(dbias_ref)

        for hh in range(HB):
            cols = slice(hh * HEAD_A, (hh + 1) * HEAD_A)
            p, q, k, v = _band_probs(i, hh, q_ref, (k0, k1, k2), (v0, v1, v2), bias_ref)
            g = gate_ref[:, cols]
            sg = _sigmoid(g)
            dyv = dy_ref[:, cols]
            dgate_ref[:, cols] = (dyv * ya_ref[:, cols] * _dsilu(g, sg)).astype(dgate_ref.dtype)
            do = (dyv * (g * sg)).astype(BF16)
            dp = _dot(do, v, NT)
            ds = p * (dp - jnp.sum(dp * p, axis=-1, keepdims=True))
            dbias_ref[hh] += ds
            dsb = (ds * (HEAD_A ** -0.5)).astype(BF16)
            dq_ref[:, cols] = _dot(dsb, k, NN).astype(dq_ref.dtype)
            dkc = _dot(dsb, q, TN)
            dvc = _dot(p.astype(BF16), do, TN)
            for r in range(3):
                blk = i - 2 + r

                @pl.when(blk >= 0)
                def _(r=r, blk=blk, cols=cols, dkc=dkc, dvc=dvc):
                    rows = pl.ds(pl.multiple_of(blk * QB, QB), QB)
                    dk_acc[rows, cols] += dkc[r * QB:(r + 1) * QB]
                    dv_acc[rows, cols] += dvc[r * QB:(r + 1) * QB]

        @pl.when(i == n_i - 1)
        def _():
            dk_ref[...] = dk_acc[...].astype(dk_ref.dtype)
            dv_ref[...] = dv_acc[...].astype(dv_ref.dtype)

    blk = pl.BlockSpec((QB, BAND_W), lambda h, i: (i, h))
    col = pl.BlockSpec((T, BAND_W), lambda h, i: (0, h))
    sds = jax.ShapeDtypeStruct((T, H * HEAD_A), BF16)
    return _call(
        "band_attn_bwd", body, grid=(H // HB, n_i), in_specs=_band_specs(H // HB) + [blk, blk],
        out_specs=[blk, col, col, blk, pl.BlockSpec((HB, QB, KW), lambda h, i: (h, 0, 0))],
        out_shape=[sds, sds, sds, sds, jax.ShapeDtypeStruct((H, QB, KW), F32)],
        args=[proj] * 8 + [bias_tile, dy, ya], sem=("parallel", "arbitrary"),
        scratch=[pltpu.VMEM((T, BAND_W), F32), pltpu.VMEM((T, BAND_W), F32)], carry=carry)


def _conv_in_specs(tb, C):
    per = tb // HALO
    nb = C // CONV_LC
    prev = lambda i: jnp.maximum(i * per - 1, 0)
    return [pl.BlockSpec((tb, CONV_LC), lambda c, i: (i, 3 * nb + c)), pl.BlockSpec((tb, CONV_LC), lambda c, i: (i, 4 * nb + c)),
            pl.BlockSpec((HALO, CONV_LC), lambda c, i: (prev(i), 3 * nb + c)),
            pl.BlockSpec((HALO, CONV_LC), lambda c, i: (prev(i), 4 * nb + c))]


def _conv_tb(T):
    return _tile(T, 1024) if T > 1024 else T // 2


def _glu_with_halo(i, ga_ref, gb_ref, ha_ref, hb_ref, scr):
    tb = ga_ref.shape[0]
    halo = ha_ref[...] * _sigmoid(hb_ref[...])
    scr[0:HALO, :] = jnp.where(i > 0, halo, 0.0)
    scr[HALO:HALO + tb, :] = ga_ref[...] * _sigmoid(gb_ref[...])
    scr[HALO + tb:, :] = jnp.zeros((SUBLANE, scr.shape[1]), F32)


def _chunk_taps(src, w_ref, row0, tap_of, uscr):
    acc = None
    for r in range(SUBLANE):
        u = None
        for a in range(HALO // SUBLANE + 1):
            j = tap_of(SUBLANE * a + r)
            if 0 <= j < CONV_K:
                rows = pl.ds(pl.multiple_of(row0 + SUBLANE * a, SUBLANE), CONV_RC + SUBLANE)
                t = w_ref[j:j + 1, :] * src[rows, :]
                u = t if u is None else u + t
        if u is None:
            continue
        if r == 0:
            piece = u[0:CONV_RC]
        else:
            uscr[...] = u
            piece = uscr[pl.ds(r, CONV_RC), :]
        acc = piece if acc is None else acc + piece
    return acc


def _layernorm_stats(z):
    mu = jnp.mean(z, axis=-1, keepdims=True)
    zc = z - mu
    rstd = lax.rsqrt(jnp.mean(zc * zc, axis=-1, keepdims=True) + EPS)
    return zc * rstd, rstd


def _conv_z(proj, conv_w, conv_b):
    T = proj.shape[0]
    C = conv_w.shape[1]
    tb = _conv_tb(T)

    def body(ga_ref, gb_ref, ha_ref, hb_ref, w_ref, cb_ref, z_ref, scr, uscr):
        _glu_with_halo(pl.program_id(1), ga_ref, gb_ref, ha_ref, hb_ref, scr)
        cb = cb_ref[...]

        def chunk(rc, carry):
            row0 = pl.multiple_of(rc * CONV_RC, CONV_RC)
            z = _chunk_taps(scr, w_ref, row0, lambda o: o - (HALO - (CONV_K - 1)), uscr)
            z_ref[pl.ds(row0, CONV_RC), :] = z + cb
            return carry

        lax.fori_loop(0, tb // CONV_RC, chunk, 0)

    return pl.pallas_call(
        body, name="conv_z", grid=(C // CONV_LC, T // tb),
        in_specs=_conv_in_specs(tb, C) + [pl.BlockSpec((HALO, CONV_LC), lambda c, i: (0, c)),
                                          pl.BlockSpec((1, CONV_LC), lambda c, i: (0, c))],
        out_specs=pl.BlockSpec((tb, CONV_LC), lambda c, i: (i, c)), out_shape=jax.ShapeDtypeStruct((T, C), F32),
        scratch_shapes=[pltpu.VMEM((HALO + tb + SUBLANE, CONV_LC), F32), pltpu.VMEM((CONV_RC + SUBLANE, CONV_LC), F32)],
        compiler_params=_cp("parallel", "parallel"))(proj, proj, proj, proj, conv_w, conv_b)


def _conv_ln_fwd(proj, z, y, ln_g, ln_b):
    T, C = z.shape
    tb = _tile(T, 256)

    def body(z_ref, gate_ref, g_ref, b_ref, y_in, y_ref):
        xhat, _ = _layernorm_stats(z_ref[...])
        ln = xhat * g_ref[...] + b_ref[...]
        gate = gate_ref[...]
        y_ref[...] = (ln * _sigmoid(ln) * (gate * _sigmoid(gate))).astype(y_ref.dtype)

    vec = pl.BlockSpec((1, C), lambda i: (0, 0))
    return pl.pallas_call(
        body, name="conv_ln_fwd", grid=(T // tb,),
        in_specs=[pl.BlockSpec((tb, C), lambda i: (i, 0)), pl.BlockSpec((tb, C), lambda i: (i, 6)), vec, vec,
                  pl.BlockSpec(memory_space=pl.ANY)],
        out_specs=pl.BlockSpec((tb, C), lambda i: (i, 1)), out_shape=jax.ShapeDtypeStruct(y.shape, y.dtype),
        input_output_aliases={4: 0}, compiler_params=_cp("parallel"))(z, proj, ln_g, ln_b, y)


def _conv_bwd_ln(proj, z, dy, ln_g, ln_b):
    T, C = z.shape
    tb = _tile(T, 256)

    def body(z_ref, gate_ref, dy_ref, g_ref, b_ref, dz_ref, dgate_ref, dg_ref, db_ref, dcb_ref):
        i = pl.program_id(0)
        xhat, rstd = _layernorm_stats(z_ref[...])
        ln = xhat * g_ref[...] + b_ref[...]
        sl = _sigmoid(ln)
        gate = gate_ref[...]
        sg = _sigmoid(gate)
        dyv = dy_ref[...]
        dgate_ref[...] = (dyv * (ln * sl) * _dsilu(gate, sg)).astype(dgate_ref.dtype)
        dln = dyv * (gate * sg) * _dsilu(ln, sl)
        dxh = dln * g_ref[...]
        dz = rstd * (dxh - jnp.mean(dxh, axis=-1, keepdims=True) - xhat * jnp.mean(dxh * xhat, axis=-1, keepdims=True))
        dz_ref[...] = dz
        parts = (jnp.sum(dln * xhat, axis=0, keepdims=True), jnp.sum(dln, axis=0, keepdims=True),
                 jnp.sum(dz, axis=0, keepdims=True))

        @pl.when(i == 0)
        def _():
            for ref, part in zip((dg_ref, db_ref, dcb_ref), parts):
                ref[...] = part

        @pl.when(i > 0)
        def _():
            for ref, part in zip((dg_ref, db_ref, dcb_ref), parts):
                ref[...] += part

    vec = pl.BlockSpec((1, C), lambda i: (0, 0))
    row = pl.BlockSpec((tb, C), lambda i: (i, 0))
    vsd = jax.ShapeDtypeStruct((1, C), F32)
    return pl.pallas_call(
        body, name="conv_bwd_ln", grid=(T // tb,),
        in_specs=[row, pl.BlockSpec((tb, C), lambda i: (i, 6)), pl.BlockSpec((tb, C), lambda i: (i, 1)), vec, vec],
        out_specs=[row, row, vec, vec, vec],
        out_shape=[jax.ShapeDtypeStruct((T, C), F32), jax.ShapeDtypeStruct((T, C), BF16), vsd, vsd, vsd],
        compiler_params=_cp("arbitrary"))(z, proj, dy, ln_g, ln_b)


def _conv_bwd_taps(proj, dz, conv_w):
    T = proj.shape[0]
    C = conv_w.shape[1]
    tb = _conv_tb(T)
    per = tb // HALO
    n_i = T // tb
    first = HALO - (CONV_K - 1)

    def body(ga_ref, gb_ref, ha_ref, hb_ref, dz_ref, dzn_ref, w_ref, da_ref, db_ref, dw_ref, scr, dscr, uscr, zscr, dwacc):
        i = pl.program_id(1)
        _glu_with_halo(i, ga_ref, gb_ref, ha_ref, hb_ref, scr)
        dscr[0:tb, :] = dz_ref[...]
        dscr[tb:tb + HALO, :] = jnp.where(i < n_i - 1, dzn_ref[...], 0.0)
        dscr[tb + HALO:, :] = jnp.zeros((SUBLANE, CONV_LC), F32)
        zscr[0:SUBLANE, :] = jnp.zeros((SUBLANE, CONV_LC), F32)
        zscr[SUBLANE + CONV_RC:, :] = jnp.zeros((SUBLANE, CONV_LC), F32)

        @pl.when(i == 0)
        def _():
            dwacc[...] = jnp.zeros_like(dwacc)

        def chunk(rc, carry):
            row0 = pl.multiple_of(rc * CONV_RC, CONV_RC)
            rows = pl.ds(row0, CONV_RC)
            dglu = _chunk_taps(dscr, w_ref, row0, lambda o: CONV_K - 1 - o, uscr)
            ga = ga_ref[rows, :]
            sb = _sigmoid(gb_ref[rows, :])
            da_ref[rows, :] = (dglu * sb).astype(da_ref.dtype)
            db_ref[rows, :] = (dglu * ga * sb * (1.0 - sb)).astype(db_ref.dtype)
            zscr[SUBLANE:SUBLANE + CONV_RC, :] = dz_ref[rows, :]
            for r in range(SUBLANE):
                dzs = zscr[pl.ds(SUBLANE - r, CONV_RC + SUBLANE), :]
                for a in range(HALO // SUBLANE + 1):
                    j = SUBLANE * a + r - first
                    if 0 <= j < CONV_K:
                        src = pl.ds(pl.multiple_of(row0 + SUBLANE * a, SUBLANE), CONV_RC + SUBLANE)
                        p = dzs * scr[src, :]
                        f = p[0:SUBLANE]
                        for s in range(1, CONV_RC // SUBLANE + 1):
                            f = f + p[s * SUBLANE:(s + 1) * SUBLANE]
                        dwacc[j * SUBLANE:(j + 1) * SUBLANE, :] += f
            return carry

        lax.fori_loop(0, tb // CONV_RC, chunk, 0)

        @pl.when(i == n_i - 1)
        def _():
            dw_ref[...] = jnp.zeros_like(dw_ref)
            for j in range(CONV_K):
                dw_ref[j:j + 1, :] = jnp.sum(dwacc[j * SUBLANE:(j + 1) * SUBLANE, :], axis=0, keepdims=True)

    blk = pl.BlockSpec((tb, CONV_LC), lambda c, i: (i, c))
    wspec = pl.BlockSpec((HALO, CONV_LC), lambda c, i: (0, c))
    nxt = pl.BlockSpec((HALO, CONV_LC), lambda c, i: (jnp.minimum((i + 1) * per, T // HALO - 1), c))
    return pl.pallas_call(
        body, name="conv_bwd_taps", grid=(C // CONV_LC, n_i),
        in_specs=_conv_in_specs(tb, C) + [blk, nxt, wspec], out_specs=[blk, blk, wspec],
        out_shape=[jax.ShapeDtypeStruct((T, C), BF16), jax.ShapeDtypeStruct((T, C), BF16),
                   jax.ShapeDtypeStruct((HALO, C), F32)],
        scratch_shapes=[pltpu.VMEM((HALO + tb + SUBLANE, CONV_LC), F32), pltpu.VMEM((tb + HALO + SUBLANE, CONV_LC), F32),
                        pltpu.VMEM((CONV_RC + SUBLANE, CONV_LC), F32), pltpu.VMEM((CONV_RC + 2 * SUBLANE, CONV_LC), F32),
                        pltpu.VMEM((HALO * SUBLANE, CONV_LC), F32)],
        compiler_params=_cp("parallel", "arbitrary"))(proj, proj, proj, proj, dz, dz, conv_w)


def _sgu_mask():
    r = lax.broadcasted_iota(I32, (GMLP_CHUNK, GMLP_CHUNK), 0) // CHUNK
    c = lax.broadcasted_iota(I32, (GMLP_CHUNK, GMLP_CHUNK), 1) // CHUNK
    return r >= c


def _sgu_fwd(proj, ln_g, ln_b, w_s, b_s_t):
    T = proj.shape[0]
    W = ln_g.shape[1]
    G = w_s.shape[0]
    cg = W // G
    tb = GMLP_CHUNK

    def body(u_ref, v_ref, gate_ref, g_ref, b_ref, ws_ref, bs_ref, y_ref):
        xhat, _ = _layernorm_stats(v_ref[...])
        vln = (xhat * g_ref[...] + b_ref[...]).astype(BF16)
        mask = _sgu_mask()
        for gi in range(G):
            cols = slice(gi * cg, (gi + 1) * cg)
            ws = jnp.where(mask, ws_ref[gi], 0.0).astype(BF16)
            sg = _dot(ws, vln[:, cols], NN) + bs_ref[:, gi:gi + 1]
            gate = gate_ref[:, cols]
            y_ref[:, cols] = (u_ref[:, cols] * sg * (gate * _sigmoid(gate))).astype(y_ref.dtype)

    vec = pl.BlockSpec((1, W), lambda i: (0, 0))
    return pl.pallas_call(
        body, name="sgu_fwd", grid=(T // tb,),
        in_specs=[pl.BlockSpec((tb, W), lambda i: (i, 0)), pl.BlockSpec((tb, W), lambda i: (i, 1)),
                  pl.BlockSpec((tb, W), lambda i: (i, 2)), vec, vec,
                  pl.BlockSpec((G, GMLP_CHUNK, GMLP_CHUNK), lambda i: (0, 0, 0)),
                  pl.BlockSpec((GMLP_CHUNK, G), lambda i: (0, 0))],
        out_specs=pl.BlockSpec((tb, W), lambda i: (i, 0)), out_shape=jax.ShapeDtypeStruct((T, W), BF16),
        compiler_params=_cp("parallel"))(proj, proj, proj, ln_g, ln_b, w_s, b_s_t)


def _sgu_bwd(proj, dy, ln_g, ln_b, w_s, b_s_t):
    T = proj.shape[0]
    W = ln_g.shape[1]
    G = w_s.shape[0]
    cg = W // G
    tb = GMLP_CHUNK

    def body(u_ref, v_ref, gate_ref, dy_ref, g_ref, b_ref, ws_ref, bs_ref,
             dp_ref, dws_ref, dbs_ref, dg_ref, db_ref, dvln_scr):
        i = pl.program_id(0)

        @pl.when(i == 0)
        def _():
            dws_ref[...] = jnp.zeros_like(dws_ref)
            dbs_ref[...] = jnp.zeros_like(dbs_ref)
            dg_ref[...] = jnp.zeros_like(dg_ref)
            db_ref[...] = jnp.zeros_like(db_ref)

        xhat, rstd = _layernorm_stats(v_ref[...])
        vln = (xhat * g_ref[...] + b_ref[...]).astype(BF16)
        mask = _sgu_mask()
        for gi in range(G):
            cols = slice(gi * cg, (gi + 1) * cg)
            ws = jnp.where(mask, ws_ref[gi], 0.0).astype(BF16)
            vg = vln[:, cols]
            sg = _dot(ws, vg, NN) + bs_ref[:, gi:gi + 1]
            gate = gate_ref[:, cols]
            s = _sigmoid(gate)
            u = u_ref[:, cols]
            dyv = dy_ref[:, cols]
            dyu = dyv * u
            dp_ref[:, cols] = (dyv * sg * (gate * s)).astype(dp_ref.dtype)
            dp_ref[:, 2 * W + gi * cg:2 * W + (gi + 1) * cg] = (dyu * sg * _dsilu(gate, s)).astype(dp_ref.dtype)
            dsg = dyu * (gate * s)
            dsgb = dsg.astype(BF16)
            dvln_scr[:, cols] = _dot(ws, dsgb, TN)
            dws_ref[gi] += _dot(dsgb, vg, NT)
            dbs_ref[:, gi:gi + 1] += jnp.sum(dsg, axis=-1, keepdims=True)
        dvln = dvln_scr[...]
        dg_ref[...] += jnp.sum(dvln * xhat, axis=0, keepdims=True)
        db_ref[...] += jnp.sum(dvln, axis=0, keepdims=True)
        dxh = dvln * g_ref[...]
        dv = rstd * (dxh - jnp.mean(dxh, axis=-1, keepdims=True) - xhat * jnp.mean(dxh * xhat, axis=-1, keepdims=True))
        dp_ref[:, W:2 * W] = dv.astype(dp_ref.dtype)

    vec = pl.BlockSpec((1, W), lambda i: (0, 0))
    wsp = pl.BlockSpec((G, GMLP_CHUNK, GMLP_CHUNK), lambda i: (0, 0, 0))
    bsp = pl.BlockSpec((GMLP_CHUNK, G), lambda i: (0, 0))
    return pl.pallas_call(
        body, name="sgu_bwd", grid=(T // tb,),
        in_specs=[pl.BlockSpec((tb, W), lambda i: (i, 0)), pl.BlockSpec((tb, W), lambda i: (i, 1)),
                  pl.BlockSpec((tb, W), lambda i: (i, 2)), pl.BlockSpec((tb, W), lambda i: (i, 0)), vec, vec, wsp, bsp],
        out_specs=[pl.BlockSpec((tb, 3 * W), lambda i: (i, 0)), wsp, bsp, vec, vec],
        out_shape=[jax.ShapeDtypeStruct((T, 3 * W), BF16), jax.ShapeDtypeStruct((G, GMLP_CHUNK, GMLP_CHUNK), F32),
                   jax.ShapeDtypeStruct((GMLP_CHUNK, G), F32), jax.ShapeDtypeStruct((1, W), F32),
                   jax.ShapeDtypeStruct((1, W), F32)],
        scratch_shapes=[pltpu.VMEM((tb, W), F32)],
        compiler_params=_cp("arbitrary"))(proj, proj, proj, dy, ln_g, ln_b, w_s, b_s_t)


def _xattn_probs(q, k, hd):
    s = _dot(q, k, NT) * (hd ** -0.5)
    e = jnp.exp(s - jnp.max(s, axis=-1, keepdims=True))
    return e * (1.0 / jnp.sum(e, axis=-1, keepdims=True))


def _xattn_fwd(name, q, k, v):
    T, D = q.shape
    M = k.shape[0]
    hd = D // N_HEADS_X
    tb = _tile(T, 512)

    def body(q_ref, k_ref, v_ref, o_ref):
        for h in range(N_HEADS_X):
            cols = slice(h * hd, (h + 1) * hd)
            p = _xattn_probs(q_ref[:, cols], k_ref[:, cols], hd)
            o_ref[:, cols] = _dot(p.astype(BF16), v_ref[:, cols], NN).astype(o_ref.dtype)

    row = pl.BlockSpec((tb, D), lambda i: (i, 0))
    kv = pl.BlockSpec((M, D), lambda i: (0, 0))
    return pl.pallas_call(
        body, name=name, grid=(T // tb,), in_specs=[row, kv, kv], out_specs=row,
        out_shape=jax.ShapeDtypeStruct((T, D), BF16), compiler_params=_cp("parallel"))(q, k, v)


def _xattn_bwd(name, q, k, v, do):
    T, D = q.shape
    M = k.shape[0]
    hd = D // N_HEADS_X
    tb = _tile(T, 512)

    def body(q_ref, k_ref, v_ref, do_ref, dq_ref, dk_ref, dv_ref):
        @pl.when(pl.program_id(0) == 0)
        def _():
            dk_ref[...] = jnp.zeros_like(dk_ref)
            dv_ref[...] = jnp.zeros_like(dv_ref)

        for h in range(N_HEADS_X):
            cols = slice(h * hd, (h + 1) * hd)
            qh, kh, doh = q_ref[:, cols], k_ref[:, cols], do_ref[:, cols]
            p = _xattn_probs(qh, kh, hd)
            dp = _dot(doh, v_ref[:, cols], NT)
            ds = p * (dp - jnp.sum(dp * p, axis=-1, keepdims=True))
            dsb = (ds * (hd ** -0.5)).astype(BF16)
            dq_ref[:, cols] = _dot(dsb, kh, NN).astype(dq_ref.dtype)
            dk_ref[:, cols] += _dot(dsb, qh, TN)
            dv_ref[:, cols] += _dot(p.astype(BF16), doh, TN)

    row = pl.BlockSpec((tb, D), lambda i: (i, 0))
    kv = pl.BlockSpec((M, D), lambda i: (0, 0))
    return pl.pallas_call(
        body, name=name, grid=(T // tb,), in_specs=[row, kv, kv, row], out_specs=[row, kv, kv],
        out_shape=[jax.ShapeDtypeStruct((T, D), BF16), jax.ShapeDtypeStruct((M, D), F32),
                   jax.ShapeDtypeStruct((M, D), F32)],
        compiler_params=_cp("arbitrary"))(q, k, v, do)


def _adamw(name, contrib, w, m, v):
    R, C = w.shape
    tr = min(R, 128)
    while R % tr:
        tr -= SUBLANE

    def body(c_ref, w_ref, m_ref, v_ref, g_ref, d_ref, nm_ref, nv_ref):
        g = c_ref[0].astype(F32)
        for s in range(1, N_DEV):
            g = g + c_ref[s].astype(F32)
        nm = ADAM_B1 * m_ref[...] + (1.0 - ADAM_B1) * g
        nv = ADAM_B2 * v_ref[...] + (1.0 - ADAM_B2) * (g * g)
        m_hat = nm / (1.0 - ADAM_B1 ** ADAM_STEP)
        v_hat = nv / (1.0 - ADAM_B2 ** ADAM_STEP)
        g_ref[...] = g
        d_ref[...] = -ADAM_LR * (m_hat / (jnp.sqrt(v_hat) + ADAM_EPS) + ADAM_WD * w_ref[...])
        nm_ref[...] = nm
        nv_ref[...] = nv

    row = pl.BlockSpec((tr, C), lambda i: (i, 0))
    sds = jax.ShapeDtypeStruct((R, C), F32)
    return pl.pallas_call(
        body, name=name, grid=(R // tr,),
        in_specs=[pl.BlockSpec((N_DEV, tr, C), lambda i: (0, i, 0)), row, row, row], out_specs=[row] * 4,
        out_shape=[sds] * 4, compiler_params=_cp("parallel"))(contrib, w, m, v)


def _pack(arrs):
    unit = SUBLANE * LANE
    flat = [jnp.pad(a.reshape(-1), (0, -a.size % unit)) for a in arrs]
    return jnp.concatenate(flat).reshape(-1, LANE)


def _unpack(buf, shapes):
    unit = SUBLANE * LANE
    flat = buf.reshape(-1)
    out, off = [], 0
    for s in shapes:
        size = 1
        for d in s:
            size *= d
        out.append(flat[off:off + size].reshape(s))
        off += size + (-size % unit)
    return out


def _cross_attention_fwd(l, h, mem, g_x, g_mem, wq, wk, wv, wo):
    hx = _rms_fwd(f"rms_x{l}", h, g_x)
    memn = _rms_fwd(f"rms_mem{l}", mem, g_mem)
    q = _mm_nn(f"xq{l}", hx, wq, BF16)
    k = _mm_nn(f"xk{l}", memn, wk, BF16)
    v = _mm_nn(f"xv{l}", memn, wv, BF16)
    o = _xattn_fwd(f"xattn_fwd{l}", q, k, v)
    h_out = _mm_nn(f"xo{l}", o, wo, F32, add=h)
    return h_out, (hx, memn, q, k, v, o)


def _cross_attention_bwd(l, dh, h, mem, g_x, g_mem, wq, wk, wv, wo, saved):
    hx, memn, q, k, v, o = saved
    do = _mm_nt(f"xo_dx{l}", dh, wo, BF16)
    dwo = _mm_tn(f"xo_dw{l}", o, dh, BF16)
    dq, dk, dv = _xattn_bwd(f"xattn_bwd{l}", q, k, v, do)
    dwq = _mm_tn(f"xq_dw{l}", hx, dq, BF16)
    dwk = _mm_tn(f"xk_dw{l}", memn, dk, BF16)
    dwv = _mm_tn(f"xv_dw{l}", memn, dv, BF16)
    dmemn = _mm_nt(f"xk_dx{l}", dk, wk, F32)
    dmemn = _mm_nt(f"xv_dx{l}", dv, wv, F32, add=dmemn)
    _, dg_mem = _rms_bwd(f"rms_mem_bwd{l}", dmemn, mem, g_mem, jnp.zeros_like(mem))
    dhx = _mm_nt(f"xq_dx{l}", dq, wq, F32)
    dh_in, dg_x = _rms_bwd(f"rms_x_bwd{l}", dhx, h, g_x, dh)
    return dh_in, (dg_x, dg_mem, dwq, dwk, dwv, dwo)


def kernel(x, mem, norm_mix_g, norm_x_g, norm_mem_g, final_norm_g, w_in_ab, rel_bias, conv_w, conv_b, conv_ln_g, conv_ln_b, w_out_ab, w_in_c, sgu_ln_g, sgu_ln_b, w_s, b_s, w_out_c, w_xq, w_xk, w_xv, w_xo, loss_target, m_norm_mix_g, m_norm_x_g, m_norm_mem_g, m_final_norm_g, m_w_in_ab, m_rel_bias, m_conv_w, m_conv_b, m_conv_ln_g, m_conv_ln_b, m_w_out_ab, m_w_in_c, m_sgu_ln_g, m_sgu_ln_b, m_w_s, m_b_s, m_w_out_c, m_w_xq, m_w_xk, m_w_xv, m_w_xo, v_norm_mix_g, v_norm_x_g, v_norm_mem_g, v_final_norm_g, v_w_in_ab, v_rel_bias, v_conv_w, v_conv_b, v_conv_ln_g, v_conv_ln_b, v_w_out_ab, v_w_in_c, v_sgu_ln_g, v_sgu_ln_b, v_w_s, v_b_s, v_w_out_c, v_w_xq, v_w_xk, v_w_xv, v_w_xo):
    names = ["norm_mix_g", "norm_x_g", "norm_mem_g", "final_norm_g", "w_in_ab", "rel_bias", "conv_w", "conv_b",
             "conv_ln_g", "conv_ln_b", "w_out_ab", "w_in_c", "sgu_ln_g", "sgu_ln_b", "w_s", "b_s", "w_out_c",
             "w_xq", "w_xk", "w_xv", "w_xo"]
    W = dict(zip(names, (norm_mix_g, norm_x_g, norm_mem_g, final_norm_g, w_in_ab, rel_bias, conv_w, conv_b, conv_ln_g,
                         conv_ln_b, w_out_ab, w_in_c, sgu_ln_g, sgu_ln_b, w_s, b_s, w_out_c, w_xq, w_xk, w_xv, w_xo)))
    M1 = dict(zip(names, (m_norm_mix_g, m_norm_x_g, m_norm_mem_g, m_final_norm_g, m_w_in_ab, m_rel_bias, m_conv_w, m_conv_b,
                          m_conv_ln_g, m_conv_ln_b, m_w_out_ab, m_w_in_c, m_sgu_ln_g, m_sgu_ln_b, m_w_s, m_b_s, m_w_out_c,
                          m_w_xq, m_w_xk, m_w_xv, m_w_xo)))
    M2 = dict(zip(names, (v_norm_mix_g, v_norm_x_g, v_norm_mem_g, v_final_norm_g, v_w_in_ab, v_rel_bias, v_conv_w, v_conv_b,
                          v_conv_ln_g, v_conv_ln_b, v_w_out_ab, v_w_in_c, v_sgu_ln_g, v_sgu_ln_b, v_w_s, v_b_s, v_w_out_c,
                          v_w_xq, v_w_xk, v_w_xv, v_w_xo)))

    h0, memv, tgt = x[0], mem[0], loss_target[0]
    T, D = h0.shape
    n_rel = rel_bias.shape[2]
    xnames = ["w_xq", "w_xk", "w_xv", "w_xo"]
    bf = lambda a: a.astype(BF16)
    blocks = lambda g: g.reshape(N_DEV, -1, D)

    small = _pack([conv_w[0], sgu_ln_g[0], sgu_ln_b[0]])
    half = w_in_ab.shape[2] // 2
    win_lo, small_g = _comm_call("gather_in_ab_lo", _Carry("gather", [bf(w_in_ab[0][:, :half]), small]))
    per_cw, per_ln = conv_w.shape[2], sgu_ln_g.shape[1]
    cw_s, lg_s, lb_s = zip(*[_unpack(small_g[d], [(CONV_K, per_cw), (1, per_ln), (1, per_ln)]) for d in range(N_DEV)])
    conv_w_full = jnp.pad(jnp.concatenate(cw_s, axis=1), ((0, HALO - CONV_K), (0, 0)))
    sgu_g_full = jnp.concatenate(lg_s, axis=1)
    sgu_b_full = jnp.concatenate(lb_s, axis=1)
    b_s_t = b_s[0].T

    hn0 = _rms_fwd("rms_mix0", h0, norm_mix_g[0:1])
    ag0 = _Carry("gather", [bf(w_in_ab[0][:, half:])])
    proj_ab = _mm_nn("in_ab_lo", hn0, win_lo, F32, half=0, carry=ag0)
    ag1 = _Carry("gather", [bf(w_out_ab[0])] + [bf(W[n][0]) for n in xnames])
    proj_ab = _mm_nn("in_ab_hi", hn0, ag0.result[0], F32, half=1, into=proj_ab, carry=ag1)
    win_ab = jnp.concatenate([win_lo, ag0.result[0]], axis=2)
    wout_ab = ag1.result[0].reshape(-1, D)
    wx0 = [g.reshape(D, D) for g in ag1.result[1:]]
    btile = _bias_tile(rel_bias[0])
    ag2 = _Carry("gather", [bf(w_in_c[0])])
    ya, y_ab = _band_attn_fwd(proj_ab, btile, carry=ag2)
    win_c = ag2.result[0]
    z_conv = _conv_z(proj_ab, conv_w_full, conv_b)
    y_ab = _conv_ln_fwd(proj_ab, z_conv, y_ab, conv_ln_g, conv_ln_b)
    h1 = _mm_nn("out_ab", y_ab, wout_ab, F32, add=h0)
    h2, xs0 = _cross_attention_fwd(0, h1, memv, norm_x_g[0:1], norm_mem_g[0:1], *wx0)
    hn1 = _rms_fwd("rms_mix1", h2, norm_mix_g[1:2])
    ag3 = _Carry("gather", [bf(w_out_c[0])] + [bf(W[n][1]) for n in xnames])
    proj_c = _mm_nn("in_c", hn1, win_c, F32, carry=ag3)
    wout_c = ag3.result[0].reshape(-1, D)
    wx1 = [g.reshape(D, D) for g in ag3.result[1:]]
    y_c = _sgu_fwd(proj_c, sgu_g_full, sgu_b_full, w_s[0], b_s_t)
    h3 = _mm_nn("out_c", y_c, wout_c, F32, add=h2)
    h4, xs1 = _cross_attention_fwd(1, h3, memv, norm_x_g[1:2], norm_mem_g[1:2], *wx1)
    loss_acc, dh4, dg_final = _loss_head(h4, tgt, final_norm_g.reshape(1, D))

    dh3, gx1 = _cross_attention_bwd(1, dh4, h3, memv, norm_x_g[1:2], norm_mem_g[1:2], *wx1, xs1)
    dy_c = _mm_nt("out_c_dx", dh3, wout_c, F32)
    dwout_c = _mm_tn("out_c_dw", y_c, dh3, BF16)
    dproj_c, dws, dbs_t, dsgu_g, dsgu_b = _sgu_bwd(proj_c, dy_c, sgu_g_full, sgu_b_full, w_s[0], b_s_t)
    ex_a = _Carry("exchange", [blocks(g) for g in gx1[2:]] + [blocks(dwout_c)])
    dwin_c = _mm_tn("in_c_dw", hn1, dproj_c, BF16, per=win_c.shape[2], carry=ex_a)
    ex_b = _Carry("exchange", [dwin_c])
    dhn1 = _mm_nt("in_c_dx", dproj_c, win_c, F32, carry=ex_b)
    dh2, dg_mix1 = _rms_bwd("rms_mix1_bwd", dhn1, h2, norm_mix_g[1:2], dh3)
    dh1, gx0 = _cross_attention_bwd(0, dh2, h1, memv, norm_x_g[0:1], norm_mem_g[0:1], *wx0, xs0)
    dy_ab = _mm_nt("out_ab_dx", dh1, wout_ab, F32)
    dwout_ab = _mm_tn("out_ab_dw", y_ab, dh1, BF16)
    ex_c = _Carry("exchange", [blocks(g) for g in gx0[2:]] + [blocks(dwout_ab)])
    dq, dk, dv, dgate_a, dbtile = _band_attn_bwd(proj_ab, btile, dy_ab, ya, carry=ex_c)
    drel = _bias_tile_grad(dbtile, n_rel)
    dz, dgate_b, dcln_g, dcln_b, dconv_b = _conv_bwd_ln(proj_ab, z_conv, dy_ab, conv_ln_g, conv_ln_b)
    dglu_a, dglu_b, dconv_w = _conv_bwd_taps(proj_ab, dz, conv_w_full)
    dproj_ab = jnp.concatenate([dq, dk, dv, dglu_a, dglu_b, dgate_a, dgate_b], axis=1)
    dwin_ab = _mm_tn("in_ab_dw", hn0, dproj_ab, BF16, per=win_ab.shape[2])
    ex_d = _Carry("exchange", [dwin_ab])
    dhn0 = _mm_nt("in_ab_dx", dproj_ab, win_ab, F32, carry=ex_d)
    dx, dg_mix0 = _rms_bwd("rms_mix0_bwd", dhn0, h0, norm_mix_g[0:1], dh1)

    sm = [_pack([dconv_w[:CONV_K, d * per_cw:(d + 1) * per_cw], dsgu_g[:, d * per_ln:(d + 1) * per_ln],
                 dsgu_b[:, d * per_ln:(d + 1) * per_ln]]) for d in range(N_DEV)]
    mask = (jnp.arange(GMLP_CHUNK)[:, None] // CHUNK >= jnp.arange(GMLP_CHUNK)[None, :] // CHUNK).astype(F32)
    rep_names = ["norm_mix_g", "norm_x_g", "norm_mem_g", "final_norm_g", "rel_bias", "conv_b", "conv_ln_g", "conv_ln_b",
                 "w_s", "b_s"]
    rep_grads = {
        "norm_mix_g": jnp.concatenate([dg_mix0, dg_mix1], axis=0),
        "norm_x_g": jnp.concatenate([gx0[0], gx1[0]], axis=0),
        "norm_mem_g": jnp.concatenate([gx0[1], gx1[1]], axis=0),
        "final_norm_g": dg_final.reshape(D),
        "rel_bias": drel[None], "conv_b": dconv_b, "conv_ln_g": dcln_g, "conv_ln_b": dcln_b,
        "w_s": (dws * mask[None])[None], "b_s": dbs_t.T[None],
    }
    ex_e = _Carry("exchange", [jnp.stack(sm), _pack([rep_grads[n] for n in rep_names])], bcast=[False, True])
    recv_small, recv_rep = _comm_call("grad_exchange_small", ex_e)

    out = {}
    kinds = ("grad", "delta", "new_m", "new_v")
    recv_big = {"w_in_ab": ex_d.result[0], "w_out_ab": ex_c.result[4], "w_in_c": ex_b.result[0], "w_out_c": ex_a.result[4]}
    for n, contrib in recv_big.items():
        for kind, r in zip(kinds, _adamw(f"adamw_{n}", contrib, W[n][0], M1[n][0], M2[n][0])):
            out[(kind, n)] = r[None]
    for j, n in enumerate(xnames):
        res = [_adamw(f"adamw_{n}{l}", ex.result[j], W[n][l], M1[n][l], M2[n][l]) for l, ex in enumerate((ex_c, ex_a))]
        for kind, r in zip(kinds, zip(*res)):
            out[(kind, n)] = jnp.stack(r)
    sm_names = ["conv_w", "sgu_ln_g", "sgu_ln_b"]
    res = _adamw("adamw_small", recv_small, *[_pack([D_[n][0] for n in sm_names]) for D_ in (W, M1, M2)])
    for kind, r in zip(kinds, res):
        for n, piece in zip(sm_names, _unpack(r, [W[n].shape for n in sm_names])):
            out[(kind, n)] = piece
    res = _adamw("adamw_replicated", recv_rep, *[_pack([D_[n] for n in rep_names]) for D_ in (W, M1, M2)])
    for kind, r in zip(kinds, res):
        for n, piece in zip(rep_names, _unpack(r, [W[n].shape for n in rep_names])):
            out[(kind, n)] = piece

    loss = lax.psum(loss_acc[0, 0], MESH_AXES)
    return (loss, dx[None]) + tuple(out[(kind, n)] for kind in kinds for n in names)
```

```python
import jax
import jax.numpy as jnp
from jax import lax
from jax.experimental import pallas as pl
from jax.experimental.pallas import tpu as pltpu

F32 = jnp.float32
BF16 = jnp.bfloat16
I32 = jnp.int32

N_DEV = 8
CHUNK = 64
N_PAST = 8
MAX_REL = 128
HEAD_A = 128
CONV_K = 31
GMLP_CHUNK = 128
N_HEADS_X = 4
EPS = 1e-6
NEG = -1e30

ADAM_LR, ADAM_B1, ADAM_B2, ADAM_EPS, ADAM_WD, ADAM_STEP = 0.001, 0.9, 0.999, 1e-08, 0.01, 10

LANE = 128
SUBLANE = 8
VMEM_LIMIT = 56 * 1024 * 1024
MATMUL_VMEM = 44 * 1024 * 1024
QB = 4 * CHUNK
KW = QB + N_PAST * CHUNK
ROLL_W = 1024
REL_PAD = 384
HB = 2
BAND_W = HB * HEAD_A
HALO = 32
CONV_LC = LANE
CONV_RC = 64
MESH_AXES = ("x", "y", "c")
GATHER_PIECES = 4

NN = (((1,), (0,)), ((), ()))
NT = (((1,), (1,)), ((), ()))
TN = (((0,), (0,)), ((), ()))


def _cp(*sem):
    return pltpu.CompilerParams(dimension_semantics=sem, vmem_limit_bytes=VMEM_LIMIT)


def _tile(dim, pref):
    if dim <= pref:
        return dim
    t = (pref // LANE) * LANE
    while dim % t:
        t -= LANE
    return t


def _sigmoid(x):
    return 1.0 / (1.0 + jnp.exp(-x))


def _dsilu(x, s):
    return s * (1.0 + x * (1.0 - s))


def _dot(a, b, dims):
    return lax.dot_general(a, b, dims, preferred_element_type=F32)


def _mesh_pos():
    return lax.axis_index("x"), lax.axis_index("y"), lax.axis_index("c")


def _lin(x, y, c):
    return 4 * x + 2 * y + c


def _remote(src, dst, send_sem, recv_sem, to):
    return pltpu.make_async_remote_copy(src_ref=src, dst_ref=dst, send_sem=send_sem, recv_sem=recv_sem,
                                        device_id=to, device_id_type=pl.DeviceIdType.MESH)


class _Carry:
    def __init__(self, kind, arrs, bcast=None):
        n = len(arrs)
        self.kind, self.arrs, self.n = kind, list(arrs), n
        self.bcast = [kind == "gather"] * n if bcast is None else list(bcast)
        self.out_shape = [jax.ShapeDtypeStruct(((N_DEV,) + a.shape) if b else a.shape, a.dtype)
                          for a, b in zip(arrs, self.bcast)]
        self.units = [(a, None, None) for a in range(n)]
        if kind == "gather":
            self.units = []
            for a, arr in enumerate(arrs):
                pieces = GATHER_PIECES if arr.shape[0] >= GATHER_PIECES * 256 else 1
                rows = arr.shape[0] // pieces
                self.units += [(a, p * rows, rows) if pieces > 1 else (a, None, None) for p in range(pieces)]
        nu = len(self.units)
        self.scratch = [pltpu.SemaphoreType.DMA((nu, 7)), pltpu.SemaphoreType.DMA((nu, 7)), pltpu.SemaphoreType.DMA((n,))]
        self.result = None

    def _src(self, ins, a, d):
        return ins[a] if self.bcast[a] else ins[a].at[d]

    def _local(self, ins, outs, sems):
        me = _lin(*_mesh_pos())
        return [pltpu.make_async_copy(self._src(ins, a, me), outs[a].at[me], sems[2].at[a]) for a in range(self.n)]

    @staticmethod
    def _chips():
        x, y, _ = _mesh_pos()
        return [(1 - x, y), (x, 1 - y), (1 - x, 1 - y)]

    def _g_copy(self, ins, outs, sems, u, k, block, to, own=False):
        a, row0, rows = self.units[u]
        piece = (lambda r: r) if row0 is None else (lambda r: r.at[pl.ds(row0, rows)])
        dst = piece(outs[a].at[_lin(*block)])
        return _remote(piece(ins[a]) if own else dst, dst, sems[0].at[u, k], sems[1].at[u, k], to)

    def _g_first(self, ins, outs, sems):
        x, y, c = _mesh_pos()
        cps = []
        for u in range(len(self.units)):
            cps.append(self._g_copy(ins, outs, sems, u, 0, (x, y, c), (x, y, 1 - c), own=True))
            cps += [self._g_copy(ins, outs, sems, u, 1 + j, (x, y, c), (*chip, c), own=True)
                    for j, chip in enumerate(self._chips())]
        return cps

    def _g_passed(self, ins, outs, sems):
        x, y, c = _mesh_pos()
        return [self._g_copy(ins, outs, sems, u, 4 + j, (*chip, c), (x, y, 1 - c))
                for u in range(len(self.units)) for j, chip in enumerate(self._chips())]

    @staticmethod
    def _peer(k):
        x, y, c = _mesh_pos()
        return (1 - x if k & 4 else x, 1 - y if k & 2 else y, 1 - c if k & 1 else c)

    def _x_sends(self, ins, outs, sems):
        me = _lin(*_mesh_pos())
        return [_remote(self._src(ins, a, _lin(*self._peer(k))), outs[a].at[me], sems[0].at[a, k - 1],
                        sems[1].at[a, k - 1], self._peer(k)) for k in range(1, N_DEV) for a in range(self.n)]

    def start(self, ins, outs, sems):
        for cp in self._local(ins, outs, sems):
            cp.start()
        for cp in (self._g_first if self.kind == "gather" else self._x_sends)(ins, outs, sems):
            cp.start()

    def mid(self, ins, outs, sems):
        if self.kind != "gather":
            return
        x, y, c = _mesh_pos()
        passed = self._g_passed(ins, outs, sems)
        for u in range(len(self.units)):
            for j, chip in enumerate(self._chips()):
                self._g_copy(ins, outs, sems, u, 1 + j, (*chip, c), (x, y, c)).wait_recv()
                passed[3 * u + j].start()

    def finish(self, ins, outs, sems):
        x, y, c = _mesh_pos()
        if self.kind == "gather":
            for u in range(len(self.units)):
                self._g_copy(ins, outs, sems, u, 0, (x, y, 1 - c), (x, y, c)).wait_recv()
                for j, chip in enumerate(self._chips()):
                    self._g_copy(ins, outs, sems, u, 4 + j, (*chip, 1 - c), (x, y, c)).wait_recv()
            sent = self._g_first(ins, outs, sems) + self._g_passed(ins, outs, sems)
        else:
            for k in range(1, N_DEV):
                for a in range(self.n):
                    got = outs[a].at[_lin(*self._peer(k))]
                    _remote(got, got, sems[0].at[a, k - 1], sems[1].at[a, k - 1], self._peer(k)).wait_recv()
            sent = self._x_sends(ins, outs, sems)
        for cp in sent:
            cp.wait_send()
        for cp in self._local(ins, outs, sems):
            cp.wait()


def _comm_call(name, carry):
    n = carry.n

    def body(*refs):
        ins, outs, sems = refs[:n], refs[n:2 * n], refs[2 * n:]
        carry.start(ins, outs, sems)
        carry.mid(ins, outs, sems)
        carry.finish(ins, outs, sems)

    hbm = pl.BlockSpec(memory_space=pl.ANY)
    return pl.pallas_call(body, name=name, in_specs=[hbm] * n, out_specs=[hbm] * n, out_shape=carry.out_shape,
                          scratch_shapes=carry.scratch)(*carry.arrs)


def _call(name, body, *, grid, in_specs, out_specs, out_shape, args, sem, scratch=(), aliases=None, carry=None):
    aliases = aliases or {}
    if carry is None:
        return pl.pallas_call(body, name=name, grid=grid, in_specs=in_specs, out_specs=out_specs, out_shape=out_shape,
                              scratch_shapes=list(scratch), input_output_aliases=aliases, compiler_params=_cp(*sem))(*args)
    ni, no, ns, nc = len(in_specs), len(out_specs), len(scratch), carry.n
    total = 1
    for g in grid:
        total *= g

    def full(*refs):
        ins, cins = refs[:ni], refs[ni:ni + nc]
        outs, couts = refs[ni + nc:ni + nc + no], refs[ni + nc + no:ni + 2 * nc + no]
        scr, sems = refs[ni + 2 * nc + no:ni + 2 * nc + no + ns], refs[ni + 2 * nc + no + ns:]
        step = pl.program_id(0)
        for d in range(1, len(grid)):
            step = step * grid[d] + pl.program_id(d)

        @pl.when(step == 0)
        def _():
            carry.start(cins, couts, sems)

        body(*ins, *outs, *scr)

        @pl.when(step == total // 2)
        def _():
            carry.mid(cins, couts, sems)

        @pl.when(step == total - 1)
        def _():
            carry.finish(cins, couts, sems)

    hbm = pl.BlockSpec(memory_space=pl.ANY)
    res = pl.pallas_call(
        full, name=name, grid=grid, in_specs=list(in_specs) + [hbm] * nc, out_specs=list(out_specs) + [hbm] * nc,
        out_shape=list(out_shape) + carry.out_shape, scratch_shapes=list(scratch) + carry.scratch,
        input_output_aliases=aliases, compiler_params=_cp(*["arbitrary"] * len(grid)))(*args, *carry.arrs)
    carry.result = list(res[no:])
    return list(res[:no])


def _matmul(name, a, b, *, dims, grid, a_spec, b_spec, out_sds, out_spec, acc_shape, add=None, add_spec=None, carry=None):
    nk = grid[2]
    has_add = add is not None

    def body(*refs):
        a_ref, b_ref = refs[0], refs[1]
        o_ref = refs[2 + has_add]
        def finish(r):
            if has_add:
                r = r + refs[2][...]
            o_ref[...] = r.astype(o_ref.dtype)

        if nk == 1:
            finish(_dot(a_ref[...].astype(BF16), b_ref[...].astype(BF16), dims))
        else:
            acc_ref = refs[-1]
            k = pl.program_id(2)

            @pl.when(k == 0)
            def _():
                acc_ref[...] = jnp.zeros_like(acc_ref)

            acc_ref[...] += _dot(a_ref[...].astype(BF16), b_ref[...].astype(BF16), dims)

            @pl.when(k == nk - 1)
            def _():
                finish(acc_ref[...])

    in_specs = [a_spec, b_spec] + ([add_spec] if has_add else [])
    args = (a, b) + ((add,) if has_add else ())
    return _call(name, body, grid=grid, in_specs=in_specs, out_specs=[out_spec], out_shape=[out_sds], args=args,
                 sem=("parallel", "parallel", "arbitrary"), scratch=[pltpu.VMEM(acc_shape, F32)] if nk > 1 else [],
                 carry=carry)[0]


def _blk(per):
    return per if per <= 1024 else per // 2


def _pick_tiles(M, N, K, a_dtype, b_dtype, out_dtype, has_add, tn=None, tk=None):
    isz = lambda dt: jnp.dtype(dt).itemsize
    tms = sorted({_tile(M, t) for t in (2048, 1024, 512)}, reverse=True)
    tks = [tk] if tk else sorted({_tile(K, t) for t in (2048, 1024, 512)}, reverse=True)
    tns = [tn] if tn else sorted({_tile(N, t) for t in (1024, 512)}, reverse=True)
    for m in tms:
        for k in tks:
            for n in tns:
                blocks = m * k * isz(a_dtype) + k * n * isz(b_dtype) + m * n * (isz(out_dtype) + (4 if has_add else 0))
                if 2 * blocks + (m * n * 4 if K > k else 0) <= MATMUL_VMEM:
                    return m, n, k
    raise ValueError("no matmul tiling fits")


def _mm_nn(name, a, b, out_dtype, *, add=None, carry=None):
    M, K = a.shape
    if b.ndim == 3:
        per = b.shape[2]
        N = N_DEV * per
        tm, tn, tk = _pick_tiles(M, N, K, a.dtype, b.dtype, out_dtype, add is not None, tn=_blk(per))
        q = per // tn
        b_spec = pl.BlockSpec((None, tk, tn), lambda m, n, k: (n // q, k, n % q))
    else:
        N = b.shape[1]
        tm, tn, tk = _pick_tiles(M, N, K, a.dtype, b.dtype, out_dtype, add is not None)
        b_spec = pl.BlockSpec((tk, tn), lambda m, n, k: (k, n))
    return _matmul(
        name, a, b, dims=NN, grid=(M // tm, N // tn, K // tk),
        a_spec=pl.BlockSpec((tm, tk), lambda m, n, k: (m, k)), b_spec=b_spec,
        out_sds=jax.ShapeDtypeStruct((M, N), out_dtype), out_spec=pl.BlockSpec((tm, tn), lambda m, n, k: (m, n)),
        acc_shape=(tm, tn), add=add, add_spec=pl.BlockSpec((tm, tn), lambda m, n, k: (m, n)), carry=carry)


def _mm_nt(name, a, b, out_dtype, *, add=None, carry=None):
    M, K = a.shape
    if b.ndim == 3:
        N = b.shape[1]
        tm, tn, tk = _pick_tiles(M, N, K, a.dtype, b.dtype, out_dtype, add is not None, tk=b.shape[2])
        b_spec = pl.BlockSpec((None, tn, tk), lambda m, n, k: (k, n, 0))
    else:
        N = b.shape[0]
        tm, tn, tk = _pick_tiles(M, N, K, a.dtype, b.dtype, out_dtype, add is not None)
        b_spec = pl.BlockSpec((tn, tk), lambda m, n, k: (n, k))
    return _matmul(
        name, a, b, dims=NT, grid=(M // tm, N // tn, K // tk),
        a_spec=pl.BlockSpec((tm, tk), lambda m, n, k: (m, k)), b_spec=b_spec,
        out_sds=jax.ShapeDtypeStruct((M, N), out_dtype), out_spec=pl.BlockSpec((tm, tn), lambda m, n, k: (m, n)),
        acc_shape=(tm, tn), add=add, add_spec=pl.BlockSpec((tm, tn), lambda m, n, k: (m, n)), carry=carry)


def _mm_tn(name, a, b, out_dtype, *, per=None, carry=None):
    K, M = a.shape
    N = b.shape[1]
    tm, tn, tk = _pick_tiles(M, N, K, a.dtype, b.dtype, out_dtype, False, tn=_blk(per) if per else None)
    if per is not None:
        q = per // tn
        out_sds = jax.ShapeDtypeStruct((N_DEV, M, per), out_dtype)
        out_spec = pl.BlockSpec((None, tm, tn), lambda m, n, k: (n // q, m, n % q))
    else:
        out_sds = jax.ShapeDtypeStruct((M, N), out_dtype)
        out_spec = pl.BlockSpec((tm, tn), lambda m, n, k: (m, n))
    return _matmul(
        name, a, b, dims=TN, grid=(M // tm, N // tn, K // tk),
        a_spec=pl.BlockSpec((tk, tm), lambda m, n, k: (k, m)), b_spec=pl.BlockSpec((tk, tn), lambda m, n, k: (k, n)),
        out_sds=out_sds, out_spec=out_spec, acc_shape=(tm, tn), carry=carry)


def _rms_fwd(name, h, g):
    T, D = h.shape
    tb = _tile(T, 512)

    def body(h_ref, g_ref, o_ref):
        x = h_ref[...]
        r = lax.rsqrt(jnp.mean(x * x, axis=-1, keepdims=True) + EPS)
        o_ref[...] = (x * r * g_ref[...]).astype(o_ref.dtype)

    return pl.pallas_call(
        body, name=name, grid=(T // tb,),
        in_specs=[pl.BlockSpec((tb, D), lambda i: (i, 0)), pl.BlockSpec((1, D), lambda i: (0, 0))],
        out_specs=pl.BlockSpec((tb, D), lambda i: (i, 0)), out_shape=jax.ShapeDtypeStruct((T, D), BF16),
        compiler_params=_cp("parallel"))(h, g)


def _rms_bwd(name, dhn, h, g, dres):
    T, D = h.shape
    tb = _tile(T, 256)

    def body(dhn_ref, h_ref, g_ref, dres_ref, dh_ref, dg_ref):
        i = pl.program_id(0)
        x = h_ref[...]
        r = lax.rsqrt(jnp.mean(x * x, axis=-1, keepdims=True) + EPS)
        y = x * r
        d = dhn_ref[...]
        dy = d * g_ref[...]
        dh_ref[...] = dres_ref[...] + r * (dy - y * jnp.mean(dy * y, axis=-1, keepdims=True))
        part = jnp.sum(d * y, axis=0, keepdims=True)

        @pl.when(i == 0)
        def _():
            dg_ref[...] = part

        @pl.when(i > 0)
        def _():
            dg_ref[...] += part

    row = pl.BlockSpec((tb, D), lambda i: (i, 0))
    vec = pl.BlockSpec((1, D), lambda i: (0, 0))
    return pl.pallas_call(
        body, name=name, grid=(T // tb,), in_specs=[row, row, vec, row], out_specs=[row, vec],
        out_shape=[jax.ShapeDtypeStruct((T, D), F32), jax.ShapeDtypeStruct((1, D), F32)],
        compiler_params=_cp("arbitrary"))(dhn, h, g, dres)


def _loss_head(h, tgt, g):
    T, D = h.shape
    tb = _tile(T, 256)

    def body(h_ref, t_ref, g_ref, loss_ref, dh_ref, dg_ref):
        i = pl.program_id(0)
        x = h_ref[...]
        gg = g_ref[...]
        r = lax.rsqrt(jnp.mean(x * x, axis=-1, keepdims=True) + EPS)
        y0 = x * r
        err = y0 * gg - t_ref[...]
        tot = 0.5 * jnp.sum(jnp.mean(err * err, axis=-1, keepdims=True), axis=0, keepdims=True)
        dy = err * (1.0 / D)
        dyg = dy * gg
        dh_ref[...] = r * (dyg - y0 * jnp.mean(dyg * y0, axis=-1, keepdims=True))
        part = jnp.sum(dy * y0, axis=0, keepdims=True)
        tot = jnp.broadcast_to(tot, loss_ref.shape)

        @pl.when(i == 0)
        def _():
            dg_ref[...] = part
            loss_ref[...] = tot

        @pl.when(i > 0)
        def _():
            dg_ref[...] += part
            loss_ref[...] += tot

    row = pl.BlockSpec((tb, D), lambda i: (i, 0))
    vec = pl.BlockSpec((1, D), lambda i: (0, 0))
    return pl.pallas_call(
        body, name="loss_head", grid=(T // tb,), in_specs=[row, row, vec],
        out_specs=[pl.BlockSpec((SUBLANE, LANE), lambda i: (0, 0)), row, vec],
        out_shape=[jax.ShapeDtypeStruct((SUBLANE, LANE), F32), jax.ShapeDtypeStruct((T, D), F32),
                   jax.ShapeDtypeStruct((1, D), F32)],
        compiler_params=_cp("arbitrary"))(h, tgt, g)


def _rel_onehot(pos_axis, shape):
    pos = lax.broadcasted_iota(I32, shape, pos_axis)
    r = lax.broadcasted_iota(I32, shape, 1 - pos_axis)
    d = jnp.where(pos < KW, N_PAST * CHUNK - pos, N_PAST * CHUNK + ROLL_W - pos)
    return (jnp.clip(d, -MAX_REL, MAX_REL) + MAX_REL == r).astype(F32)


def _roll_rows(x, left):
    row = lax.broadcasted_iota(I32, x.shape, 0)
    for b in range(QB.bit_length() - 1):
        shift = (ROLL_W - (1 << b)) if left else (1 << b)
        x = jnp.where(((row >> b) & 1) == 1, pltpu.roll(x, shift, 1), x)
    return x


def _bias_tile(rel_bias):
    H = rel_bias.shape[0]
    rb = jnp.pad(rel_bias, ((0, 0), (0, REL_PAD - rel_bias.shape[1]))).reshape(H, 1, REL_PAD)

    def body(rb_ref, o_ref):
        row = jnp.broadcast_to(rb_ref[...], (SUBLANE, REL_PAD))
        base = jnp.dot(row, _rel_onehot(1, (REL_PAD, ROLL_W)), precision=lax.Precision.HIGHEST,
                       preferred_element_type=F32)[0:1]
        tile = _roll_rows(jnp.broadcast_to(base, (QB, ROLL_W)), left=False)[:, :KW]
        qc = lax.broadcasted_iota(I32, (QB, KW), 0) // CHUNK
        kc = lax.broadcasted_iota(I32, (QB, KW), 1) // CHUNK - N_PAST
        o_ref[...] = jnp.where((kc >= qc - N_PAST) & (kc <= qc), tile, NEG)

    return pl.pallas_call(
        body, name="bias_tile", grid=(H,),
        in_specs=[pl.BlockSpec((None, 1, REL_PAD), lambda h: (h, 0, 0))],
        out_specs=pl.BlockSpec((None, QB, KW), lambda h: (h, 0, 0)),
        out_shape=jax.ShapeDtypeStruct((H, QB, KW), F32), compiler_params=_cp("parallel"))(rb)


def _bias_tile_grad(dtile, n_rel):
    H = dtile.shape[0]

    def body(dt_ref, o_ref):
        x = jnp.concatenate([dt_ref[...], jnp.zeros((QB, ROLL_W - KW), F32)], axis=1)
        cs = jnp.sum(_roll_rows(x, left=True), axis=0, keepdims=True)
        o_ref[...] = jnp.dot(jnp.broadcast_to(cs, (SUBLANE, ROLL_W)), _rel_onehot(0, (ROLL_W, REL_PAD)),
                             precision=lax.Precision.HIGHEST, preferred_element_type=F32)[0:1]

    out = pl.pallas_call(
        body, name="bias_tile_grad", grid=(H,),
        in_specs=[pl.BlockSpec((None, QB, KW), lambda h: (h, 0, 0))],
        out_specs=pl.BlockSpec((None, 1, REL_PAD), lambda h: (h, 0, 0)),
        out_shape=jax.ShapeDtypeStruct((H, 1, REL_PAD), F32), compiler_params=_cp("parallel"))(dtile)
    return out.reshape(H, REL_PAD)[:, :n_rel]


def _band_specs(G):
    spec = lambda f: pl.BlockSpec((QB, BAND_W), f)
    q = spec(lambda h, i: (i, h))
    ks = [spec(lambda h, i, r=r: (jnp.maximum(i - 2 + r, 0), G + h)) for r in range(3)]
    vs = [spec(lambda h, i, r=r: (jnp.maximum(i - 2 + r, 0), 2 * G + h)) for r in range(3)]
    gate = spec(lambda h, i: (i, 5 * G + h))
    bias = pl.BlockSpec((HB, QB, KW), lambda h, i: (h, 0, 0))
    return [q] + ks + vs + [gate, bias]


def _band_probs(i, hh, q_ref, k_refs, v_refs, bias_ref):
    cols = slice(hh * HEAD_A, (hh + 1) * HEAD_A)
    q = q_ref[:, cols].astype(BF16)
    k = jnp.concatenate([r[:, cols] for r in k_refs], axis=0).astype(BF16)
    v = jnp.concatenate([r[:, cols] for r in v_refs], axis=0).astype(BF16)
    s = _dot(q, k, NT) * (HEAD_A ** -0.5) + bias_ref[hh]
    kpos = (i - 2) * QB + lax.broadcasted_iota(I32, (1, KW), 1)
    s = jnp.where(kpos >= 0, s, NEG)
    e = jnp.exp(s - jnp.max(s, axis=-1, keepdims=True))
    p = e * (1.0 / jnp.sum(e, axis=-1, keepdims=True))
    return p, q, k, v


def _band_attn_fwd(proj, bias_tile, carry=None):
    T = proj.shape[0]
    H = bias_tile.shape[0]

    def body(q_ref, k0, k1, k2, v0, v1, v2, gate_ref, bias_ref, ya_ref, y_ref):
        for hh in range(HB):
            cols = slice(hh * HEAD_A, (hh + 1) * HEAD_A)
            p, _, _, v = _band_probs(pl.program_id(1), hh, q_ref, (k0, k1, k2), (v0, v1, v2), bias_ref)
            o = _dot(p.astype(BF16), v, NN)
            g = gate_ref[:, cols]
            ya_ref[:, cols] = o
            y_ref[:, cols] = (o * (g * _sigmoid(g))).astype(y_ref.dtype)

    out = pl.BlockSpec((QB, BAND_W), lambda h, i: (i, h))
    return _call(
        "band_attn_fwd", body, grid=(H // HB, T // QB), in_specs=_band_specs(H // HB), out_specs=[out, out],
        out_shape=[jax.ShapeDtypeStruct((T, H * HEAD_A), F32), jax.ShapeDtypeStruct((T, 2 * H * HEAD_A), BF16)],
        args=[proj] * 8 + [bias_tile], sem=("parallel", "parallel"), carry=carry)


def _band_attn_bwd(proj, bias_tile, dy, ya, carry=None):
    T = proj.shape[0]
    H = bias_tile.shape[0]
    n_i = T // QB

    def body(q_ref, k0, k1, k2, v0, v1, v2, gate_ref, bias_ref, dy_ref, ya_ref,
             dq_ref, dk_ref, dv_ref, dgate_ref, dbias_ref, dk_acc, dv_acc):
        i = pl.program_id(1)

        @pl.when(i == 0)
        def _():
            dk_acc[...] = jnp.zeros_like(dk_acc)
            dv_acc[...] = jnp.zeros_like(dv_acc)
            dbias_ref[...] = jnp.zeros_like(dbias_ref)

        for hh in range(HB):
            cols = slice(hh * HEAD_A, (hh + 1) * HEAD_A)
            p, q, k, v = _band_probs(i, hh, q_ref, (k0, k1, k2), (v0, v1, v2), bias_ref)
            g = gate_ref[:, cols]
            sg = _sigmoid(g)
            dyv = dy_ref[:, cols]
            dgate_ref[:, cols] = (dyv * ya_ref[:, cols] * _dsilu(g, sg)).astype(dgate_ref.dtype)
            do = (dyv * (g * sg)).astype(BF16)
            dp = _dot(do, v, NT)
            ds = p * (dp - jnp.sum(dp * p, axis=-1, keepdims=True))
            dbias_ref[hh] += ds
            dsb = (ds * (HEAD_A ** -0.5)).astype(BF16)
            dq_ref[:, cols] = _dot(dsb, k, NN).astype(dq_ref.dtype)
            dkc = _dot(dsb, q, TN)
            dvc = _dot(p.astype(BF16), do, TN)
            for r in range(3):
                blk = i - 2 + r

                @pl.when(blk >= 0)
                def _(r=r, blk=blk, cols=cols, dkc=dkc, dvc=dvc):
                    rows = pl.ds(pl.multiple_of(blk * QB, QB), QB)
                    dk_acc[rows, cols] += dkc[r * QB:(r + 1) * QB]
                    dv_acc[rows, cols] += dvc[r * QB:(r + 1) * QB]

        @pl.when(i == n_i - 1)
        def _():
            dk_ref[...] = dk_acc[...].astype(dk_ref.dtype)
            dv_ref[...] = dv_acc[...].astype(dv_ref.dtype)

    blk = pl.BlockSpec((QB, BAND_W), lambda h, i: (i, h))
    col = pl.BlockSpec((T, BAND_W), lambda h, i: (0, h))
    sds = jax.ShapeDtypeStruct((T, H * HEAD_A), BF16)
    return _call(
        "band_attn_bwd", body, grid=(H // HB, n_i), in_specs=_band_specs(H // HB) + [blk, blk],
        out_specs=[blk, col, col, blk, pl.BlockSpec((HB, QB, KW), lambda h, i: (h, 0, 0))],
        out_shape=[sds, sds, sds, sds, jax.ShapeDtypeStruct((H, QB, KW), F32)],
        args=[proj] * 8 + [bias_tile, dy, ya], sem=("parallel", "arbitrary"),
        scratch=[pltpu.VMEM((T, BAND_W), F32), pltpu.VMEM((T, BAND_W), F32)], carry=carry)


def _conv_in_specs(tb, C):
    per = tb // HALO
    nb = C // CONV_LC
    prev = lambda i: jnp.maximum(i * per - 1, 0)
    return [pl.BlockSpec((tb, CONV_LC), lambda c, i: (i, 3 * nb + c)), pl.BlockSpec((tb, CONV_LC), lambda c, i: (i, 4 * nb + c)),
            pl.BlockSpec((HALO, CONV_LC), lambda c, i: (prev(i), 3 * nb + c)),
            pl.BlockSpec((HALO, CONV_LC), lambda c, i: (prev(i), 4 * nb + c))]


def _conv_tb(T):
    return _tile(T, 1024) if T > 1024 else T // 2


def _glu_with_halo(i, ga_ref, gb_ref, ha_ref, hb_ref, scr):
    tb = ga_ref.shape[0]
    halo = ha_ref[...] * _sigmoid(hb_ref[...])
    scr[0:HALO, :] = jnp.where(i > 0, halo, 0.0)
    scr[HALO:HALO + tb, :] = ga_ref[...] * _sigmoid(gb_ref[...])
    scr[HALO + tb:, :] = jnp.zeros((SUBLANE, scr.shape[1]), F32)


def _chunk_taps(src, w_ref, row0, tap_of, uscr):
    acc = None
    for r in range(SUBLANE):
        u = None
        for a in range(HALO // SUBLANE + 1):
            j = tap_of(SUBLANE * a + r)
            if 0 <= j < CONV_K:
                rows = pl.ds(pl.multiple_of(row0 + SUBLANE * a, SUBLANE), CONV_RC + SUBLANE)
                t = w_ref[j:j + 1, :] * src[rows, :]
                u = t if u is None else u + t
        if u is None:
            continue
        if r == 0:
            piece = u[0:CONV_RC]
        else:
            uscr[...] = u
            piece = uscr[pl.ds(r, CONV_RC), :]
        acc = piece if acc is None else acc + piece
    return acc


def _layernorm_stats(z):
    mu = jnp.mean(z, axis=-1, keepdims=True)
    zc = z - mu
    rstd = lax.rsqrt(jnp.mean(zc * zc, axis=-1, keepdims=True) + EPS)
    return zc * rstd, rstd


def _conv_z(proj, conv_w, conv_b):
    T = proj.shape[0]
    C = conv_w.shape[1]
    tb = _conv_tb(T)

    def body(ga_ref, gb_ref, ha_ref, hb_ref, w_ref, cb_ref, z_ref, scr, uscr):
        _glu_with_halo(pl.program_id(1), ga_ref, gb_ref, ha_ref, hb_ref, scr)
        cb = cb_ref[...]

        def chunk(rc, carry):
            row0 = pl.multiple_of(rc * CONV_RC, CONV_RC)
            z = _chunk_taps(scr, w_ref, row0, lambda o: o - (HALO - (CONV_K - 1)), uscr)
            z_ref[pl.ds(row0, CONV_RC), :] = z + cb
            return carry

        lax.fori_loop(0, tb // CONV_RC, chunk, 0)

    return pl.pallas_call(
        body, name="conv_z", grid=(C // CONV_LC, T // tb),
        in_specs=_conv_in_specs(tb, C) + [pl.BlockSpec((HALO, CONV_LC), lambda c, i: (0, c)),
                                          pl.BlockSpec((1, CONV_LC), lambda c, i: (0, c))],
        out_specs=pl.BlockSpec((tb, CONV_LC), lambda c, i: (i, c)), out_shape=jax.ShapeDtypeStruct((T, C), F32),
        scratch_shapes=[pltpu.VMEM((HALO + tb + SUBLANE, CONV_LC), F32), pltpu.VMEM((CONV_RC + SUBLANE, CONV_LC), F32)],
        compiler_params=_cp("parallel", "parallel"))(proj, proj, proj, proj, conv_w, conv_b)


def _conv_ln_fwd(proj, z, y, ln_g, ln_b):
    T, C = z.shape
    tb = _tile(T, 256)

    def body(z_ref, gate_ref, g_ref, b_ref, y_in, y_ref):
        xhat, _ = _layernorm_stats(z_ref[...])
        ln = xhat * g_ref[...] + b_ref[...]
        gate = gate_ref[...]
        y_ref[...] = (ln * _sigmoid(ln) * (gate * _sigmoid(gate))).astype(y_ref.dtype)

    vec = pl.BlockSpec((1, C), lambda i: (0, 0))
    return pl.pallas_call(
        body, name="conv_ln_fwd", grid=(T // tb,),
        in_specs=[pl.BlockSpec((tb, C), lambda i: (i, 0)), pl.BlockSpec((tb, C), lambda i: (i, 6)), vec, vec,
                  pl.BlockSpec(memory_space=pl.ANY)],
        out_specs=pl.BlockSpec((tb, C), lambda i: (i, 1)), out_shape=jax.ShapeDtypeStruct(y.shape, y.dtype),
        input_output_aliases={4: 0}, compiler_params=_cp("parallel"))(z, proj, ln_g, ln_b, y)


def _conv_bwd_ln(proj, z, dy, ln_g, ln_b):
    T, C = z.shape
    tb = _tile(T, 256)

    def body(z_ref, gate_ref, dy_ref, g_ref, b_ref, dz_ref, dgate_ref, dg_ref, db_ref, dcb_ref):
        i = pl.program_id(0)
        xhat, rstd = _layernorm_stats(z_ref[...])
        ln = xhat * g_ref[...] + b_ref[...]
        sl = _sigmoid(ln)
        gate = gate_ref[...]
        sg = _sigmoid(gate)
        dyv = dy_ref[...]
        dgate_ref[...] = (dyv * (ln * sl) * _dsilu(gate, sg)).astype(dgate_ref.dtype)
        dln = dyv * (gate * sg) * _dsilu(ln, sl)
        dxh = dln * g_ref[...]
        dz = rstd * (dxh - jnp.mean(dxh, axis=-1, keepdims=True) - xhat * jnp.mean(dxh * xhat, axis=-1, keepdims=True))
        dz_ref[...] = dz
        parts = (jnp.sum(dln * xhat, axis=0, keepdims=True), jnp.sum(dln, axis=0, keepdims=True),
                 jnp.sum(dz, axis=0, keepdims=True))

        @pl.when(i == 0)
        def _():
            for ref, part in zip((dg_ref, db_ref, dcb_ref), parts):
                ref[...] = part

        @pl.when(i > 0)
        def _():
            for ref, part in zip((dg_ref, db_ref, dcb_ref), parts):
                ref[...] += part

    vec = pl.BlockSpec((1, C), lambda i: (0, 0))
    row = pl.BlockSpec((tb, C), lambda i: (i, 0))
    vsd = jax.ShapeDtypeStruct((1, C), F32)
    return pl.pallas_call(
        body, name="conv_bwd_ln", grid=(T // tb,),
        in_specs=[row, pl.BlockSpec((tb, C), lambda i: (i, 6)), pl.BlockSpec((tb, C), lambda i: (i, 1)), vec, vec],
        out_specs=[row, row, vec, vec, vec],
        out_shape=[jax.ShapeDtypeStruct((T, C), F32), jax.ShapeDtypeStruct((T, C), BF16), vsd, vsd, vsd],
        compiler_params=_cp("arbitrary"))(z, proj, dy, ln_g, ln_b)


def _conv_bwd_taps(proj, dz, conv_w):
    T = proj.shape[0]
    C = conv_w.shape[1]
    tb = _conv_tb(T)
    per = tb // HALO
    n_i = T // tb
    first = HALO - (CONV_K - 1)

    def body(ga_ref, gb_ref, ha_ref, hb_ref, dz_ref, dzn_ref, w_ref, da_ref, db_ref, dw_ref, scr, dscr, uscr, zscr, dwacc):
        i = pl.program_id(1)
        _glu_with_halo(i, ga_ref, gb_ref, ha_ref, hb_ref, scr)
        dscr[0:tb, :] = dz_ref[...]
        dscr[tb:tb + HALO, :] = jnp.where(i < n_i - 1, dzn_ref[...], 0.0)
        dscr[tb + HALO:, :] = jnp.zeros((SUBLANE, CONV_LC), F32)
        zscr[0:SUBLANE, :] = jnp.zeros((SUBLANE, CONV_LC), F32)
        zscr[SUBLANE + CONV_RC:, :] = jnp.zeros((SUBLANE, CONV_LC), F32)

        @pl.when(i == 0)
        def _():
            dwacc[...] = jnp.zeros_like(dwacc)

        def chunk(rc, carry):
            row0 = pl.multiple_of(rc * CONV_RC, CONV_RC)
            rows = pl.ds(row0, CONV_RC)
            dglu = _chunk_taps(dscr, w_ref, row0, lambda o: CONV_K - 1 - o, uscr)
            ga = ga_ref[rows, :]
            sb = _sigmoid(gb_ref[rows, :])
            da_ref[rows, :] = (dglu * sb).astype(da_ref.dtype)
            db_ref[rows, :] = (dglu * ga * sb * (1.0 - sb)).astype(db_ref.dtype)
            zscr[SUBLANE:SUBLANE + CONV_RC, :] = dz_ref[rows, :]
            for r in range(SUBLANE):
                dzs = zscr[pl.ds(SUBLANE - r, CONV_RC + SUBLANE), :]
                for a in range(HALO // SUBLANE + 1):
                    j = SUBLANE * a + r - first
                    if 0 <= j < CONV_K:
                        src = pl.ds(pl.multiple_of(row0 + SUBLANE * a, SUBLANE), CONV_RC + SUBLANE)
                        p = dzs * scr[src, :]
                        f = p[0:SUBLANE]
                        for s in range(1, CONV_RC // SUBLANE + 1):
                            f = f + p[s * SUBLANE:(s + 1) * SUBLANE]
                        dwacc[j * SUBLANE:(j + 1) * SUBLANE, :] += f
            return carry

        lax.fori_loop(0, tb // CONV_RC, chunk, 0)

        @pl.when(i == n_i - 1)
        def _():
            dw_ref[...] = jnp.zeros_like(dw_ref)
            for j in range(CONV_K):
                dw_ref[j:j + 1, :] = jnp.sum(dwacc[j * SUBLANE:(j + 1) * SUBLANE, :], axis=0, keepdims=True)

    blk = pl.BlockSpec((tb, CONV_LC), lambda c, i: (i, c))
    wspec = pl.BlockSpec((HALO, CONV_LC), lambda c, i: (0, c))
    nxt = pl.BlockSpec((HALO, CONV_LC), lambda c, i: (jnp.minimum((i + 1) * per, T // HALO - 1), c))
    return pl.pallas_call(
        body, name="conv_bwd_taps", grid=(C // CONV_LC, n_i),
        in_specs=_conv_in_specs(tb, C) + [blk, nxt, wspec], out_specs=[blk, blk, wspec],
        out_shape=[jax.ShapeDtypeStruct((T, C), BF16), jax.ShapeDtypeStruct((T, C), BF16),
                   jax.ShapeDtypeStruct((HALO, C), F32)],
        scratch_shapes=[pltpu.VMEM((HALO + tb + SUBLANE, CONV_LC), F32), pltpu.VMEM((tb + HALO + SUBLANE, CONV_LC), F32),
                        pltpu.VMEM((CONV_RC + SUBLANE, CONV_LC), F32), pltpu.VMEM((CONV_RC + 2 * SUBLANE, CONV_LC), F32),
                        pltpu.VMEM((HALO * SUBLANE, CONV_LC), F32)],
        compiler_params=_cp("parallel", "arbitrary"))(proj, proj, proj, proj, dz, dz, conv_w)


def _sgu_mask():
    r = lax.broadcasted_iota(I32, (GMLP_CHUNK, GMLP_CHUNK), 0) // CHUNK
    c = lax.broadcasted_iota(I32, (GMLP_CHUNK, GMLP_CHUNK), 1) // CHUNK
    return r >= c


def _sgu_fwd(proj, ln_g, ln_b, w_s, b_s_t):
    T = proj.shape[0]
    W = ln_g.shape[1]
    G = w_s.shape[0]
    cg = W // G
    tb = GMLP_CHUNK

    def body(u_ref, v_ref, gate_ref, g_ref, b_ref, ws_ref, bs_ref, y_ref):
        xhat, _ = _layernorm_stats(v_ref[...])
        vln = (xhat * g_ref[...] + b_ref[...]).astype(BF16)
        mask = _sgu_mask()
        for gi in range(G):
            cols = slice(gi * cg, (gi + 1) * cg)
            ws = jnp.where(mask, ws_ref[gi], 0.0).astype(BF16)
            sg = _dot(ws, vln[:, cols], NN) + bs_ref[:, gi:gi + 1]
            gate = gate_ref[:, cols]
            y_ref[:, cols] = (u_ref[:, cols] * sg * (gate * _sigmoid(gate))).astype(y_ref.dtype)

    vec = pl.BlockSpec((1, W), lambda i: (0, 0))
    return pl.pallas_call(
        body, name="sgu_fwd", grid=(T // tb,),
        in_specs=[pl.BlockSpec((tb, W), lambda i: (i, 0)), pl.BlockSpec((tb, W), lambda i: (i, 1)),
                  pl.BlockSpec((tb, W), lambda i: (i, 2)), vec, vec,
                  pl.BlockSpec((G, GMLP_CHUNK, GMLP_CHUNK), lambda i: (0, 0, 0)),
                  pl.BlockSpec((GMLP_CHUNK, G), lambda i: (0, 0))],
        out_specs=pl.BlockSpec((tb, W), lambda i: (i, 0)), out_shape=jax.ShapeDtypeStruct((T, W), BF16),
        compiler_params=_cp("parallel"))(proj, proj, proj, ln_g, ln_b, w_s, b_s_t)


def _sgu_bwd(proj, dy, ln_g, ln_b, w_s, b_s_t):
    T = proj.shape[0]
    W = ln_g.shape[1]
    G = w_s.shape[0]
    cg = W // G
    tb = GMLP_CHUNK

    def body(u_ref, v_ref, gate_ref, dy_ref, g_ref, b_ref, ws_ref, bs_ref,
             dp_ref, dws_ref, dbs_ref, dg_ref, db_ref, dvln_scr):
        i = pl.program_id(0)

        @pl.when(i == 0)
        def _():
            dws_ref[...] = jnp.zeros_like(dws_ref)
            dbs_ref[...] = jnp.zeros_like(dbs_ref)
            dg_ref[...] = jnp.zeros_like(dg_ref)
            db_ref[...] = jnp.zeros_like(db_ref)

        xhat, rstd = _layernorm_stats(v_ref[...])
        vln = (xhat * g_ref[...] + b_ref[...]).astype(BF16)
        mask = _sgu_mask()
        for gi in range(G):
            cols = slice(gi * cg, (gi + 1) * cg)
            ws = jnp.where(mask, ws_ref[gi], 0.0).astype(BF16)
            vg = vln[:, cols]
            sg = _dot(ws, vg, NN) + bs_ref[:, gi:gi + 1]
            gate = gate_ref[:, cols]
            s = _sigmoid(gate)
            u = u_ref[:, cols]
            dyv = dy_ref[:, cols]
            dyu = dyv * u
            dp_ref[:, cols] = (dyv * sg * (gate * s)).astype(dp_ref.dtype)
            dp_ref[:, 2 * W + gi * cg:2 * W + (gi + 1) * cg] = (dyu * sg * _dsilu(gate, s)).astype(dp_ref.dtype)
            dsg = dyu * (gate * s)
            dsgb = dsg.astype(BF16)
            dvln_scr[:, cols] = _dot(ws, dsgb, TN)
            dws_ref[gi] += _dot(dsgb, vg, NT)
            dbs_ref[:, gi:gi + 1] += jnp.sum(dsg, axis=-1, keepdims=True)
        dvln = dvln_scr[...]
        dg_ref[...] += jnp.sum(dvln * xhat, axis=0, keepdims=True)
        db_ref[...] += jnp.sum(dvln, axis=0, keepdims=True)
        dxh = dvln * g_ref[...]
        dv = rstd * (dxh - jnp.mean(dxh, axis=-1, keepdims=True) - xhat * jnp.mean(dxh * xhat, axis=-1, keepdims=True))
        dp_ref[:, W:2 * W] = dv.astype(dp_ref.dtype)

    vec = pl.BlockSpec((1, W), lambda i: (0, 0))
    wsp = pl.BlockSpec((G, GMLP_CHUNK, GMLP_CHUNK), lambda i: (0, 0, 0))
    bsp = pl.BlockSpec((GMLP_CHUNK, G), lambda i: (0, 0))
    return pl.pallas_call(
        body, name="sgu_bwd", grid=(T // tb,),
        in_specs=[pl.BlockSpec((tb, W), lambda i: (i, 0)), pl.BlockSpec((tb, W), lambda i: (i, 1)),
                  pl.BlockSpec((tb, W), lambda i: (i, 2)), pl.BlockSpec((tb, W), lambda i: (i, 0)), vec, vec, wsp, bsp],
        out_specs=[pl.BlockSpec((tb, 3 * W), lambda i: (i, 0)), wsp, bsp, vec, vec],
        out_shape=[jax.ShapeDtypeStruct((T, 3 * W), BF16), jax.ShapeDtypeStruct((G, GMLP_CHUNK, GMLP_CHUNK), F32),
                   jax.ShapeDtypeStruct((GMLP_CHUNK, G), F32), jax.ShapeDtypeStruct((1, W), F32),
                   jax.ShapeDtypeStruct((1, W), F32)],
        scratch_shapes=[pltpu.VMEM((tb, W), F32)],
        compiler_params=_cp("arbitrary"))(proj, proj, proj, dy, ln_g, ln_b, w_s, b_s_t)


def _xattn_probs(q, k, hd):
    s = _dot(q, k, NT) * (hd ** -0.5)
    e = jnp.exp(s - jnp.max(s, axis=-1, keepdims=True))
    return e * (1.0 / jnp.sum(e, axis=-1, keepdims=True))


def _xattn_fwd(name, q, k, v):
    T, D = q.shape
    M = k.shape[0]
    hd = D // N_HEADS_X
    tb = _tile(T, 512)

    def body(q_ref, k_ref, v_ref, o_ref):
        for h in range(N_HEADS_X):
            cols = slice(h * hd, (h + 1) * hd)
            p = _xattn_probs(q_ref[:, cols], k_ref[:, cols], hd)
            o_ref[:, cols] = _dot(p.astype(BF16), v_ref[:, cols], NN).astype(o_ref.dtype)

    row = pl.BlockSpec((tb, D), lambda i: (i, 0))
    kv = pl.BlockSpec((M, D), lambda i: (0, 0))
    return pl.pallas_call(
        body, name=name, grid=(T // tb,), in_specs=[row, kv, kv], out_specs=row,
        out_shape=jax.ShapeDtypeStruct((T, D), BF16), compiler_params=_cp("parallel"))(q, k, v)


def _xattn_bwd(name, q, k, v, do):
    T, D = q.shape
    M = k.shape[0]
    hd = D // N_HEADS_X
    tb = _tile(T, 512)

    def body(q_ref, k_ref, v_ref, do_ref, dq_ref, dk_ref, dv_ref):
        @pl.when(pl.program_id(0) == 0)
        def _():
            dk_ref[...] = jnp.zeros_like(dk_ref)
            dv_ref[...] = jnp.zeros_like(dv_ref)

        for h in range(N_HEADS_X):
            cols = slice(h * hd, (h + 1) * hd)
            qh, kh, doh = q_ref[:, cols], k_ref[:, cols], do_ref[:, cols]
            p = _xattn_probs(qh, kh, hd)
            dp = _dot(doh, v_ref[:, cols], NT)
            ds = p * (dp - jnp.sum(dp * p, axis=-1, keepdims=True))
            dsb = (ds * (hd ** -0.5)).astype(BF16)
            dq_ref[:, cols] = _dot(dsb, kh, NN).astype(dq_ref.dtype)
            dk_ref[:, cols] += _dot(dsb, qh, TN)
            dv_ref[:, cols] += _dot(p.astype(BF16), doh, TN)

    row = pl.BlockSpec((tb, D), lambda i: (i, 0))
    kv = pl.BlockSpec((M, D), lambda i: (0, 0))
    return pl.pallas_call(
        body, name=name, grid=(T // tb,), in_specs=[row, kv, kv, row], out_specs=[row, kv, kv],
        out_shape=[jax.ShapeDtypeStruct((T, D), BF16), jax.ShapeDtypeStruct((M, D), F32),
                   jax.ShapeDtypeStruct((M, D), F32)],
        compiler_params=_cp("arbitrary"))(q, k, v, do)


def _adamw(name, contrib, w, m, v, carry=None):
    R, C = w.shape
    tr = min(R, 128)
    while R % tr:
        tr -= SUBLANE

    def body(c_ref, w_ref, m_ref, v_ref, g_ref, d_ref, nm_ref, nv_ref):
        g = c_ref[0].astype(F32)
        for s in range(1, N_DEV):
            g = g + c_ref[s].astype(F32)
        nm = ADAM_B1 * m_ref[...] + (1.0 - ADAM_B1) * g
        nv = ADAM_B2 * v_ref[...] + (1.0 - ADAM_B2) * (g * g)
        m_hat = nm / (1.0 - ADAM_B1 ** ADAM_STEP)
        v_hat = nv / (1.0 - ADAM_B2 ** ADAM_STEP)
        g_ref[...] = g
        d_ref[...] = -ADAM_LR * (m_hat / (jnp.sqrt(v_hat) + ADAM_EPS) + ADAM_WD * w_ref[...])
        nm_ref[...] = nm
        nv_ref[...] = nv

    row = pl.BlockSpec((tr, C), lambda i: (i, 0))
    sds = jax.ShapeDtypeStruct((R, C), F32)
    return _call(name, body, grid=(R // tr,),
                 in_specs=[pl.BlockSpec((N_DEV, tr, C), lambda i: (0, i, 0)), row, row, row], out_specs=[row] * 4,
                 out_shape=[sds] * 4, args=(contrib, w, m, v), sem=("parallel",), carry=carry)


def _pack(arrs):
    unit = SUBLANE * LANE
    flat = [jnp.pad(a.reshape(-1), (0, -a.size % unit)) for a in arrs]
    return jnp.concatenate(flat).reshape(-1, LANE)


def _unpack(buf, shapes):
    unit = SUBLANE * LANE
    flat = buf.reshape(-1)
    out, off = [], 0
    for s in shapes:
        size = 1
        for d in s:
            size *= d
        out.append(flat[off:off + size].reshape(s))
        off += size + (-size % unit)
    return out


def _cross_attention_fwd(l, h, mem, g_x, g_mem, wq, wk, wv, wo):
    hx = _rms_fwd(f"rms_x{l}", h, g_x)
    memn = _rms_fwd(f"rms_mem{l}", mem, g_mem)
    q = _mm_nn(f"xq{l}", hx, wq, BF16)
    k = _mm_nn(f"xk{l}", memn, wk, BF16)
    v = _mm_nn(f"xv{l}", memn, wv, BF16)
    o = _xattn_fwd(f"xattn_fwd{l}", q, k, v)
    h_out = _mm_nn(f"xo{l}", o, wo, F32, add=h)
    return h_out, (hx, memn, q, k, v, o)


def _cross_attention_bwd(l, dh, h, mem, g_x, g_mem, wq, wk, wv, wo, saved):
    hx, memn, q, k, v, o = saved
    do = _mm_nt(f"xo_dx{l}", dh, wo, BF16)
    dwo = _mm_tn(f"xo_dw{l}", o, dh, BF16)
    dq, dk, dv = _xattn_bwd(f"xattn_bwd{l}", q, k, v, do)
    dwq = _mm_tn(f"xq_dw{l}", hx, dq, BF16)
    dwk = _mm_tn(f"xk_dw{l}", memn, dk, BF16)
    dwv = _mm_tn(f"xv_dw{l}", memn, dv, BF16)
    dmemn = _mm_nt(f"xk_dx{l}", dk, wk, F32)
    dmemn = _mm_nt(f"xv_dx{l}", dv, wv, F32, add=dmemn)
    _, dg_mem = _rms_bwd(f"rms_mem_bwd{l}", dmemn, mem, g_mem, jnp.zeros_like(mem))
    dhx = _mm_nt(f"xq_dx{l}", dq, wq, F32)
    dh_in, dg_x = _rms_bwd(f"rms_x_bwd{l}", dhx, h, g_x, dh)
    return dh_in, (dg_x, dg_mem, dwq, dwk, dwv, dwo)


def kernel(x, mem, norm_mix_g, norm_x_g, norm_mem_g, final_norm_g, w_in_ab, rel_bias, conv_w, conv_b, conv_ln_g, conv_ln_b, w_out_ab, w_in_c, sgu_ln_g, sgu_ln_b, w_s, b_s, w_out_c, w_xq, w_xk, w_xv, w_xo, loss_target, m_norm_mix_g, m_norm_x_g, m_norm_mem_g, m_final_norm_g, m_w_in_ab, m_rel_bias, m_conv_w, m_conv_b, m_conv_ln_g, m_conv_ln_b, m_w_out_ab, m_w_in_c, m_sgu_ln_g, m_sgu_ln_b, m_w_s, m_b_s, m_w_out_c, m_w_xq, m_w_xk, m_w_xv, m_w_xo, v_norm_mix_g, v_norm_x_g, v_norm_mem_g, v_final_norm_g, v_w_in_ab, v_rel_bias, v_conv_w, v_conv_b, v_conv_ln_g, v_conv_ln_b, v_w_out_ab, v_w_in_c, v_sgu_ln_g, v_sgu_ln_b, v_w_s, v_b_s, v_w_out_c, v_w_xq, v_w_xk, v_w_xv, v_w_xo):
    names = ["norm_mix_g", "norm_x_g", "norm_mem_g", "final_norm_g", "w_in_ab", "rel_bias", "conv_w", "conv_b",
             "conv_ln_g", "conv_ln_b", "w_out_ab", "w_in_c", "sgu_ln_g", "sgu_ln_b", "w_s", "b_s", "w_out_c",
             "w_xq", "w_xk", "w_xv", "w_xo"]
    W = dict(zip(names, (norm_mix_g, norm_x_g, norm_mem_g, final_norm_g, w_in_ab, rel_bias, conv_w, conv_b, conv_ln_g,
                         conv_ln_b, w_out_ab, w_in_c, sgu_ln_g, sgu_ln_b, w_s, b_s, w_out_c, w_xq, w_xk, w_xv, w_xo)))
    M1 = dict(zip(names, (m_norm_mix_g, m_norm_x_g, m_norm_mem_g, m_final_norm_g, m_w_in_ab, m_rel_bias, m_conv_w, m_conv_b,
                          m_conv_ln_g, m_conv_ln_b, m_w_out_ab, m_w_in_c, m_sgu_ln_g, m_sgu_ln_b, m_w_s, m_b_s, m_w_out_c,
                          m_w_xq, m_w_xk, m_w_xv, m_w_xo)))
    M2 = dict(zip(names, (v_norm_mix_g, v_norm_x_g, v_norm_mem_g, v_final_norm_g, v_w_in_ab, v_rel_bias, v_conv_w, v_conv_b,
                          v_conv_ln_g, v_conv_ln_b, v_w_out_ab, v_w_in_c, v_sgu_ln_g, v_sgu_ln_b, v_w_s, v_b_s, v_w_out_c,
                          v_w_xq, v_w_xk, v_w_xv, v_w_xo)))

    h0, memv, tgt = x[0], mem[0], loss_target[0]
    T, D = h0.shape
    n_rel = rel_bias.shape[2]
    xnames = ["w_xq", "w_xk", "w_xv", "w_xo"]
    bf = lambda a: a.astype(BF16)
    blocks = lambda g: g.reshape(N_DEV, -1, D)

    small = _pack([conv_w[0], sgu_ln_g[0], sgu_ln_b[0]])
    win_ab, small_g = _comm_call("gather_in_ab", _Carry("gather", [bf(w_in_ab[0]), small]))
    per_cw, per_ln = conv_w.shape[2], sgu_ln_g.shape[1]
    cw_s, lg_s, lb_s = zip(*[_unpack(small_g[d], [(CONV_K, per_cw), (1, per_ln), (1, per_ln)]) for d in range(N_DEV)])
    conv_w_full = jnp.pad(jnp.concatenate(cw_s, axis=1), ((0, HALO - CONV_K), (0, 0)))
    sgu_g_full = jnp.concatenate(lg_s, axis=1)
    sgu_b_full = jnp.concatenate(lb_s, axis=1)
    b_s_t = b_s[0].T

    hn0 = _rms_fwd("rms_mix0", h0, norm_mix_g[0:1])
    ag1 = _Carry("gather", [bf(w_out_ab[0])] + [bf(W[n][0]) for n in xnames])
    proj_ab = _mm_nn("in_ab", hn0, win_ab, F32, carry=ag1)
    wout_ab = ag1.result[0].reshape(-1, D)
    wx0 = [g.reshape(D, D) for g in ag1.result[1:]]
    btile = _bias_tile(rel_bias[0])
    ag2 = _Carry("gather", [bf(w_in_c[0])])
    ya, y_ab = _band_attn_fwd(proj_ab, btile, carry=ag2)
    win_c = ag2.result[0]
    z_conv = _conv_z(proj_ab, conv_w_full, conv_b)
    y_ab = _conv_ln_fwd(proj_ab, z_conv, y_ab, conv_ln_g, conv_ln_b)
    h1 = _mm_nn("out_ab", y_ab, wout_ab, F32, add=h0)
    h2, xs0 = _cross_attention_fwd(0, h1, memv, norm_x_g[0:1], norm_mem_g[0:1], *wx0)
    hn1 = _rms_fwd("rms_mix1", h2, norm_mix_g[1:2])
    ag3 = _Carry("gather", [bf(w_out_c[0])] + [bf(W[n][1]) for n in xnames])
    proj_c = _mm_nn("in_c", hn1, win_c, F32, carry=ag3)
    wout_c = ag3.result[0].reshape(-1, D)
    wx1 = [g.reshape(D, D) for g in ag3.result[1:]]
    y_c = _sgu_fwd(proj_c, sgu_g_full, sgu_b_full, w_s[0], b_s_t)
    h3 = _mm_nn("out_c", y_c, wout_c, F32, add=h2)
    h4, xs1 = _cross_attention_fwd(1, h3, memv, norm_x_g[1:2], norm_mem_g[1:2], *wx1)
    loss_acc, dh4, dg_final = _loss_head(h4, tgt, final_norm_g.reshape(1, D))

    dh3, gx1 = _cross_attention_bwd(1, dh4, h3, memv, norm_x_g[1:2], norm_mem_g[1:2], *wx1, xs1)
    dy_c = _mm_nt("out_c_dx", dh3, wout_c, F32)
    dwout_c = _mm_tn("out_c_dw", y_c, dh3, BF16)
    dproj_c, dws, dbs_t, dsgu_g, dsgu_b = _sgu_bwd(proj_c, dy_c, sgu_g_full, sgu_b_full, w_s[0], b_s_t)
    ex_a = _Carry("exchange", [blocks(g) for g in gx1[2:]] + [blocks(dwout_c)])
    dwin_c = _mm_tn("in_c_dw", hn1, dproj_c, BF16, per=win_c.shape[2], carry=ex_a)
    ex_b = _Carry("exchange", [dwin_c])
    dhn1 = _mm_nt("in_c_dx", dproj_c, win_c, F32, carry=ex_b)
    dh2, dg_mix1 = _rms_bwd("rms_mix1_bwd", dhn1, h2, norm_mix_g[1:2], dh3)
    dh1, gx0 = _cross_attention_bwd(0, dh2, h1, memv, norm_x_g[0:1], norm_mem_g[0:1], *wx0, xs0)
    dy_ab = _mm_nt("out_ab_dx", dh1, wout_ab, F32)
    dwout_ab = _mm_tn("out_ab_dw", y_ab, dh1, BF16)
    ex_c = _Carry("exchange", [blocks(g) for g in gx0[2:]] + [blocks(dwout_ab)])
    dq, dk, dv, dgate_a, dbtile = _band_attn_bwd(proj_ab, btile, dy_ab, ya, carry=ex_c)
    drel = _bias_tile_grad(dbtile, n_rel)
    dz, dgate_b, dcln_g, dcln_b, dconv_b = _conv_bwd_ln(proj_ab, z_conv, dy_ab, conv_ln_g, conv_ln_b)
    dglu_a, dglu_b, dconv_w = _conv_bwd_taps(proj_ab, dz, conv_w_full)
    dproj_ab = jnp.concatenate([dq, dk, dv, dglu_a, dglu_b, dgate_a, dgate_b], axis=1)
    dwin_ab = _mm_tn("in_ab_dw", hn0, dproj_ab, BF16, per=win_ab.shape[2])
    ex_d = _Carry("exchange", [dwin_ab])
    dhn0 = _mm_nt("in_ab_dx", dproj_ab, win_ab, F32, carry=ex_d)
    dx, dg_mix0 = _rms_bwd("rms_mix0_bwd", dhn0, h0, norm_mix_g[0:1], dh1)

    sm = [_pack([dconv_w[:CONV_K, d * per_cw:(d + 1) * per_cw], dsgu_g[:, d * per_ln:(d + 1) * per_ln],
                 dsgu_b[:, d * per_ln:(d + 1) * per_ln]]) for d in range(N_DEV)]
    mask = (jnp.arange(GMLP_CHUNK)[:, None] // CHUNK >= jnp.arange(GMLP_CHUNK)[None, :] // CHUNK).astype(F32)
    rep_names = ["norm_mix_g", "norm_x_g", "norm_mem_g", "final_norm_g", "rel_bias", "conv_b", "conv_ln_g", "conv_ln_b",
                 "w_s", "b_s"]
    rep_grads = {
        "norm_mix_g": jnp.concatenate([dg_mix0, dg_mix1], axis=0),
        "norm_x_g": jnp.concatenate([gx0[0], gx1[0]], axis=0),
        "norm_mem_g": jnp.concatenate([gx0[1], gx1[1]], axis=0),
        "final_norm_g": dg_final.reshape(D),
        "rel_bias": drel[None], "conv_b": dconv_b, "conv_ln_g": dcln_g, "conv_ln_b": dcln_b,
        "w_s": (dws * mask[None])[None], "b_s": dbs_t.T[None],
    }
    ex_e = _Carry("exchange", [jnp.stack(sm), _pack([rep_grads[n] for n in rep_names])], bcast=[False, True])

    out = {}
    kinds = ("grad", "delta", "new_m", "new_v")
    recv_big = {"w_in_ab": ex_d.result[0], "w_out_ab": ex_c.result[4], "w_in_c": ex_b.result[0], "w_out_c": ex_a.result[4]}
    for n, contrib in recv_big.items():
        res = _adamw(f"adamw_{n}", contrib, W[n][0], M1[n][0], M2[n][0], carry=ex_e if n == "w_in_ab" else None)
        for kind, r in zip(kinds, res):
            out[(kind, n)] = r[None]
    recv_small, recv_rep = ex_e.result
    for j, n in enumerate(xnames):
        res = [_adamw(f"adamw_{n}{l}", ex.result[j], W[n][l], M1[n][l], M2[n][l]) for l, ex in enumerate((ex_c, ex_a))]
        for kind, r in zip(kinds, zip(*res)):
            out[(kind, n)] = jnp.stack(r)
    sm_names = ["conv_w", "sgu_ln_g", "sgu_ln_b"]
    res = _adamw("adamw_small", recv_small, *[_pack([D_[n][0] for n in sm_names]) for D_ in (W, M1, M2)])
    for kind, r in zip(kinds, res):
        for n, piece in zip(sm_names, _unpack(r, [W[n].shape for n in sm_names])):
            out[(kind, n)] = piece
    res = _adamw("adamw_replicated", recv_rep, *[_pack([D_[n] for n in rep_names]) for D_ in (W, M1, M2)])
    for kind, r in zip(kinds, res):
        for n, piece in zip(rep_names, _unpack(r, [W[n].shape for n in rep_names])):
            out[(kind, n)] = piece

    loss = lax.psum(loss_acc[0, 0], MESH_AXES)
    return (loss, dx[None]) + tuple(out[(kind, n)] for kind in kinds for n in names)
```

```python
import jax
import jax.numpy as jnp
from jax import lax
from jax.experimental import pallas as pl
from jax.experimental.pallas import tpu as pltpu

F32 = jnp.float32
BF16 = jnp.bfloat16
I32 = jnp.int32

N_DEV = 8
CHUNK = 64
N_PAST = 8
MAX_REL = 128
HEAD_A = 128
CONV_K = 31
GMLP_CHUNK = 128
N_HEADS_X = 4
EPS = 1e-6
NEG = -1e30

ADAM_LR, ADAM_B1, ADAM_B2, ADAM_EPS, ADAM_WD, ADAM_STEP = 0.001, 0.9, 0.999, 1e-08, 0.01, 10

LANE = 128
SUBLANE = 8
VMEM_LIMIT = 56 * 1024 * 1024
MATMUL_VMEM = 44 * 1024 * 1024
QB = 4 * CHUNK
KW = QB + N_PAST * CHUNK
ROLL_W = 1024
REL_PAD = 384
HB = 2
BAND_W = HB * HEAD_A
HALO = 32
CONV_LC = LANE
CONV_RC = 64
MESH_AXES = ("x", "y", "c")
GATHER_PIECES = 4

NN = (((1,), (0,)), ((), ()))
NT = (((1,), (1,)), ((), ()))
TN = (((0,), (0,)), ((), ()))


def _cp(*sem):
    return pltpu.CompilerParams(dimension_semantics=sem, vmem_limit_bytes=VMEM_LIMIT)


def _tile(dim, pref):
    if dim <= pref:
        return dim
    t = (pref // LANE) * LANE
    while dim % t:
        t -= LANE
    return t


def _sigmoid(x):
    return 1.0 / (1.0 + jnp.exp(-x))


def _dsilu(x, s):
    return s * (1.0 + x * (1.0 - s))


def _dot(a, b, dims):
    return lax.dot_general(a, b, dims, preferred_element_type=F32)


def _mesh_pos():
    return lax.axis_index("x"), lax.axis_index("y"), lax.axis_index("c")


def _lin(x, y, c):
    return 4 * x + 2 * y + c


def _remote(src, dst, send_sem, recv_sem, to):
    return pltpu.make_async_remote_copy(src_ref=src, dst_ref=dst, send_sem=send_sem, recv_sem=recv_sem,
                                        device_id=to, device_id_type=pl.DeviceIdType.MESH)


class _Carry:
    def __init__(self, kind, arrs, bcast=None):
        n = len(arrs)
        self.kind, self.arrs, self.n = kind, list(arrs), n
        self.bcast = [kind == "gather"] * n if bcast is None else list(bcast)
        self.out_shape = [jax.ShapeDtypeStruct(((N_DEV,) + a.shape) if b else a.shape, a.dtype)
                          for a, b in zip(arrs, self.bcast)]
        self.units = [(a, None, None) for a in range(n)]
        if kind == "gather":
            self.units = []
            for a, arr in enumerate(arrs):
                pieces = GATHER_PIECES if arr.shape[0] >= GATHER_PIECES * 256 else 1
                rows = arr.shape[0] // pieces
                self.units += [(a, p * rows, rows) if pieces > 1 else (a, None, None) for p in range(pieces)]
        nu = len(self.units)
        self.scratch = [pltpu.SemaphoreType.DMA((nu, 7)), pltpu.SemaphoreType.DMA((nu, 7)), pltpu.SemaphoreType.DMA((n,))]
        self.result = None

    def _src(self, ins, a, d):
        return ins[a] if self.bcast[a] else ins[a].at[d]

    def _local(self, ins, outs, sems):
        me = _lin(*_mesh_pos())
        return [pltpu.make_async_copy(self._src(ins, a, me), outs[a].at[me], sems[2].at[a]) for a in range(self.n)]

    @staticmethod
    def _chips():
        x, y, _ = _mesh_pos()
        return [(1 - x, y), (x, 1 - y), (1 - x, 1 - y)]

    def _g_copy(self, ins, outs, sems, u, k, block, to, own=False):
        a, row0, rows = self.units[u]
        piece = (lambda r: r) if row0 is None else (lambda r: r.at[pl.ds(row0, rows)])
        dst = piece(outs[a].at[_lin(*block)])
        return _remote(piece(ins[a]) if own else dst, dst, sems[0].at[u, k], sems[1].at[u, k], to)

    def _g_first(self, ins, outs, sems):
        x, y, c = _mesh_pos()
        cps = []
        for u in range(len(self.units)):
            cps.append(self._g_copy(ins, outs, sems, u, 0, (x, y, c), (x, y, 1 - c), own=True))
            cps += [self._g_copy(ins, outs, sems, u, 1 + j, (x, y, c), (*chip, c), own=True)
                    for j, chip in enumerate(self._chips())]
        return cps

    def _g_passed(self, ins, outs, sems):
        x, y, c = _mesh_pos()
        return [self._g_copy(ins, outs, sems, u, 4 + j, (*chip, c), (x, y, 1 - c))
                for u in range(len(self.units)) for j, chip in enumerate(self._chips())]

    @staticmethod
    def _peer(k):
        x, y, c = _mesh_pos()
        return (1 - x if k & 4 else x, 1 - y if k & 2 else y, 1 - c if k & 1 else c)

    def _x_sends(self, ins, outs, sems):
        me = _lin(*_mesh_pos())
        return [_remote(self._src(ins, a, _lin(*self._peer(k))), outs[a].at[me], sems[0].at[a, k - 1],
                        sems[1].at[a, k - 1], self._peer(k)) for k in range(1, N_DEV) for a in range(self.n)]

    def start(self, ins, outs, sems):
        for cp in self._local(ins, outs, sems):
            cp.start()
        for cp in (self._g_first if self.kind == "gather" else self._x_sends)(ins, outs, sems):
            cp.start()

    def mid(self, ins, outs, sems):
        if self.kind != "gather":
            return
        x, y, c = _mesh_pos()
        passed = self._g_passed(ins, outs, sems)
        for u in range(len(self.units)):
            for j, chip in enumerate(self._chips()):
                self._g_copy(ins, outs, sems, u, 1 + j, (*chip, c), (x, y, c)).wait_recv()
                passed[3 * u + j].start()

    def finish(self, ins, outs, sems):
        x, y, c = _mesh_pos()
        if self.kind == "gather":
            for u in range(len(self.units)):
                self._g_copy(ins, outs, sems, u, 0, (x, y, 1 - c), (x, y, c)).wait_recv()
                for j, chip in enumerate(self._chips()):
                    self._g_copy(ins, outs, sems, u, 4 + j, (*chip, 1 - c), (x, y, c)).wait_recv()
            sent = self._g_first(ins, outs, sems) + self._g_passed(ins, outs, sems)
        else:
            for k in range(1, N_DEV):
                for a in range(self.n):
                    got = outs[a].at[_lin(*self._peer(k))]
                    _remote(got, got, sems[0].at[a, k - 1], sems[1].at[a, k - 1], self._peer(k)).wait_recv()
            sent = self._x_sends(ins, outs, sems)
        for cp in sent:
            cp.wait_send()
        for cp in self._local(ins, outs, sems):
            cp.wait()


def _comm_call(name, carry):
    n = carry.n

    def body(*refs):
        ins, outs, sems = refs[:n], refs[n:2 * n], refs[2 * n:]
        carry.start(ins, outs, sems)
        carry.mid(ins, outs, sems)
        carry.finish(ins, outs, sems)

    hbm = pl.BlockSpec(memory_space=pl.ANY)
    return pl.pallas_call(body, name=name, in_specs=[hbm] * n, out_specs=[hbm] * n, out_shape=carry.out_shape,
                          scratch_shapes=carry.scratch)(*carry.arrs)


def _call(name, body, *, grid, in_specs, out_specs, out_shape, args, sem, scratch=(), aliases=None, carry=None):
    aliases = aliases or {}
    if carry is None:
        return pl.pallas_call(body, name=name, grid=grid, in_specs=in_specs, out_specs=out_specs, out_shape=out_shape,
                              scratch_shapes=list(scratch), input_output_aliases=aliases, compiler_params=_cp(*sem))(*args)
    ni, no, ns, nc = len(in_specs), len(out_specs), len(scratch), carry.n
    total = 1
    for g in grid:
        total *= g

    def full(*refs):
        ins, cins = refs[:ni], refs[ni:ni + nc]
        outs, couts = refs[ni + nc:ni + nc + no], refs[ni + nc + no:ni + 2 * nc + no]
        scr, sems = refs[ni + 2 * nc + no:ni + 2 * nc + no + ns], refs[ni + 2 * nc + no + ns:]
        step = pl.program_id(0)
        for d in range(1, len(grid)):
            step = step * grid[d] + pl.program_id(d)

        @pl.when(step == 0)
        def _():
            carry.start(cins, couts, sems)

        body(*ins, *outs, *scr)

        @pl.when(step == total // 2)
        def _():
            carry.mid(cins, couts, sems)

        @pl.when(step == total - 1)
        def _():
            carry.finish(cins, couts, sems)

    hbm = pl.BlockSpec(memory_space=pl.ANY)
    res = pl.pallas_call(
        full, name=name, grid=grid, in_specs=list(in_specs) + [hbm] * nc, out_specs=list(out_specs) + [hbm] * nc,
        out_shape=list(out_shape) + carry.out_shape, scratch_shapes=list(scratch) + carry.scratch,
        input_output_aliases=aliases, compiler_params=_cp(*["arbitrary"] * len(grid)))(*args, *carry.arrs)
    carry.result = list(res[no:])
    return list(res[:no])


def _matmul(name, a, b, *, dims, grid, a_spec, b_spec, out_sds, out_spec, acc_shape, add=None, add_spec=None, carry=None):
    nk = grid[2]
    has_add = add is not None

    def body(*refs):
        a_ref, b_ref = refs[0], refs[1]
        o_ref = refs[2 + has_add]
        def finish(r):
            if has_add:
                r = r + refs[2][...]
            o_ref[...] = r.astype(o_ref.dtype)

        if nk == 1:
            finish(_dot(a_ref[...].astype(BF16), b_ref[...].astype(BF16), dims))
        else:
            acc_ref = refs[-1]
            k = pl.program_id(2)

            @pl.when(k == 0)
            def _():
                acc_ref[...] = jnp.zeros_like(acc_ref)

            acc_ref[...] += _dot(a_ref[...].astype(BF16), b_ref[...].astype(BF16), dims)

            @pl.when(k == nk - 1)
            def _():
                finish(acc_ref[...])

    in_specs = [a_spec, b_spec] + ([add_spec] if has_add else [])
    args = (a, b) + ((add,) if has_add else ())
    return _call(name, body, grid=grid, in_specs=in_specs, out_specs=[out_spec], out_shape=[out_sds], args=args,
                 sem=("parallel", "parallel", "arbitrary"), scratch=[pltpu.VMEM(acc_shape, F32)] if nk > 1 else [],
                 carry=carry)[0]


def _blk(per):
    return per if per <= 1024 else per // 2


def _pick_tiles(M, N, K, a_dtype, b_dtype, out_dtype, has_add, tn=None, tk=None):
    isz = lambda dt: jnp.dtype(dt).itemsize
    tms = sorted({_tile(M, t) for t in (2048, 1024, 512)}, reverse=True)
    tks = [tk] if tk else sorted({_tile(K, t) for t in (2048, 1024, 512)}, reverse=True)
    tns = [tn] if tn else sorted({_tile(N, t) for t in (1024, 512)}, reverse=True)
    for m in tms:
        for k in tks:
            for n in tns:
                blocks = m * k * isz(a_dtype) + k * n * isz(b_dtype) + m * n * (isz(out_dtype) + (4 if has_add else 0))
                if 2 * blocks + (m * n * 4 if K > k else 0) <= MATMUL_VMEM:
                    return m, n, k
    raise ValueError("no matmul tiling fits")


def _mm_nn(name, a, b, out_dtype, *, add=None, carry=None):
    M, K = a.shape
    if b.ndim == 3:
        per = b.shape[2]
        N = N_DEV * per
        tm, tn, tk = _pick_tiles(M, N, K, a.dtype, b.dtype, out_dtype, add is not None, tn=_blk(per))
        q = per // tn
        b_spec = pl.BlockSpec((None, tk, tn), lambda m, n, k: (n // q, k, n % q))
    else:
        N = b.shape[1]
        tm, tn, tk = _pick_tiles(M, N, K, a.dtype, b.dtype, out_dtype, add is not None)
        b_spec = pl.BlockSpec((tk, tn), lambda m, n, k: (k, n))
    return _matmul(
        name, a, b, dims=NN, grid=(M // tm, N // tn, K // tk),
        a_spec=pl.BlockSpec((tm, tk), lambda m, n, k: (m, k)), b_spec=b_spec,
        out_sds=jax.ShapeDtypeStruct((M, N), out_dtype), out_spec=pl.BlockSpec((tm, tn), lambda m, n, k: (m, n)),
        acc_shape=(tm, tn), add=add, add_spec=pl.BlockSpec((tm, tn), lambda m, n, k: (m, n)), carry=carry)


def _mm_nt(name, a, b, out_dtype, *, add=None, carry=None):
    M, K = a.shape
    if b.ndim == 3:
        N = b.shape[1]
        tm, tn, tk = _pick_tiles(M, N, K, a.dtype, b.dtype, out_dtype, add is not None, tk=b.shape[2])
        b_spec = pl.BlockSpec((None, tn, tk), lambda m, n, k: (k, n, 0))
    else:
        N = b.shape[0]
        tm, tn, tk = _pick_tiles(M, N, K, a.dtype, b.dtype, out_dtype, add is not None)
        b_spec = pl.BlockSpec((tn, tk), lambda m, n, k: (n, k))
    return _matmul(
        name, a, b, dims=NT, grid=(M // tm, N // tn, K // tk),
        a_spec=pl.BlockSpec((tm, tk), lambda m, n, k: (m, k)), b_spec=b_spec,
        out_sds=jax.ShapeDtypeStruct((M, N), out_dtype), out_spec=pl.BlockSpec((tm, tn), lambda m, n, k: (m, n)),
        acc_shape=(tm, tn), add=add, add_spec=pl.BlockSpec((tm, tn), lambda m, n, k: (m, n)), carry=carry)


def _mm_tn(name, a, b, out_dtype, *, per=None, carry=None):
    K, M = a.shape
    N = b.shape[1]
    tm, tn, tk = _pick_tiles(M, N, K, a.dtype, b.dtype, out_dtype, False, tn=_blk(per) if per else None)
    if per is not None:
        q = per // tn
        out_sds = jax.ShapeDtypeStruct((N_DEV, M, per), out_dtype)
        out_spec = pl.BlockSpec((None, tm, tn), lambda m, n, k: (n // q, m, n % q))
    else:
        out_sds = jax.ShapeDtypeStruct((M, N), out_dtype)
        out_spec = pl.BlockSpec((tm, tn), lambda m, n, k: (m, n))
    return _matmul(
        name, a, b, dims=TN, grid=(M // tm, N // tn, K // tk),
        a_spec=pl.BlockSpec((tk, tm), lambda m, n, k: (k, m)), b_spec=pl.BlockSpec((tk, tn), lambda m, n, k: (k, n)),
        out_sds=out_sds, out_spec=out_spec, acc_shape=(tm, tn), carry=carry)


def _rms_fwd(name, h, g):
    T, D = h.shape
    tb = _tile(T, 512)

    def body(h_ref, g_ref, o_ref):
        x = h_ref[...]
        r = lax.rsqrt(jnp.mean(x * x, axis=-1, keepdims=True) + EPS)
        o_ref[...] = (x * r * g_ref[...]).astype(o_ref.dtype)

    return pl.pallas_call(
        body, name=name, grid=(T // tb,),
        in_specs=[pl.BlockSpec((tb, D), lambda i: (i, 0)), pl.BlockSpec((1, D), lambda i: (0, 0))],
        out_specs=pl.BlockSpec((tb, D), lambda i: (i, 0)), out_shape=jax.ShapeDtypeStruct((T, D), BF16),
        compiler_params=_cp("parallel"))(h, g)


def _rms_bwd(name, dhn, h, g, dres):
    T, D = h.shape
    tb = _tile(T, 256)

    def body(dhn_ref, h_ref, g_ref, dres_ref, dh_ref, dg_ref):
        i = pl.program_id(0)
        x = h_ref[...]
        r = lax.rsqrt(jnp.mean(x * x, axis=-1, keepdims=True) + EPS)
        y = x * r
        d = dhn_ref[...]
        dy = d * g_ref[...]
        dh_ref[...] = dres_ref[...] + r * (dy - y * jnp.mean(dy * y, axis=-1, keepdims=True))
        part = jnp.sum(d * y, axis=0, keepdims=True)

        @pl.when(i == 0)
        def _():
            dg_ref[...] = part

        @pl.when(i > 0)
        def _():
            dg_ref[...] += part

    row = pl.BlockSpec((tb, D), lambda i: (i, 0))
    vec = pl.BlockSpec((1, D), lambda i: (0, 0))
    return pl.pallas_call(
        body, name=name, grid=(T // tb,), in_specs=[row, row, vec, row], out_specs=[row, vec],
        out_shape=[jax.ShapeDtypeStruct((T, D), F32), jax.ShapeDtypeStruct((1, D), F32)],
        compiler_params=_cp("arbitrary"))(dhn, h, g, dres)


def _loss_head(h, tgt, g):
    T, D = h.shape
    tb = _tile(T, 256)

    def body(h_ref, t_ref, g_ref, loss_ref, dh_ref, dg_ref):
        i = pl.program_id(0)
        x = h_ref[...]
        gg = g_ref[...]
        r = lax.rsqrt(jnp.mean(x * x, axis=-1, keepdims=True) + EPS)
        y0 = x * r
        err = y0 * gg - t_ref[...]
        tot = 0.5 * jnp.sum(jnp.mean(err * err, axis=-1, keepdims=True), axis=0, keepdims=True)
        dy = err * (1.0 / D)
        dyg = dy * gg
        dh_ref[...] = r * (dyg - y0 * jnp.mean(dyg * y0, axis=-1, keepdims=True))
        part = jnp.sum(dy * y0, axis=0, keepdims=True)
        tot = jnp.broadcast_to(tot, loss_ref.shape)

        @pl.when(i == 0)
        def _():
            dg_ref[...] = part
            loss_ref[...] = tot

        @pl.when(i > 0)
        def _():
            dg_ref[...] += part
            loss_ref[...] += tot

    row = pl.BlockSpec((tb, D), lambda i: (i, 0))
    vec = pl.BlockSpec((1, D), lambda i: (0, 0))
    return pl.pallas_call(
        body, name="loss_head", grid=(T // tb,), in_specs=[row, row, vec],
        out_specs=[pl.BlockSpec((SUBLANE, LANE), lambda i: (0, 0)), row, vec],
        out_shape=[jax.ShapeDtypeStruct((SUBLANE, LANE), F32), jax.ShapeDtypeStruct((T, D), F32),
                   jax.ShapeDtypeStruct((1, D), F32)],
        compiler_params=_cp("arbitrary"))(h, tgt, g)


def _rel_onehot(pos_axis, shape):
    pos = lax.broadcasted_iota(I32, shape, pos_axis)
    r = lax.broadcasted_iota(I32, shape, 1 - pos_axis)
    d = jnp.where(pos < KW, N_PAST * CHUNK - pos, N_PAST * CHUNK + ROLL_W - pos)
    return (jnp.clip(d, -MAX_REL, MAX_REL) + MAX_REL == r).astype(F32)


def _roll_rows(x, left):
    row = lax.broadcasted_iota(I32, x.shape, 0)
    for b in range(QB.bit_length() - 1):
        shift = (ROLL_W - (1 << b)) if left else (1 << b)
        x = jnp.where(((row >> b) & 1) == 1, pltpu.roll(x, shift, 1), x)
    return x


def _bias_tile(rel_bias):
    H = rel_bias.shape[0]
    rb = jnp.pad(rel_bias, ((0, 0), (0, REL_PAD - rel_bias.shape[1]))).reshape(H, 1, REL_PAD)

    def body(rb_ref, o_ref):
        row = jnp.broadcast_to(rb_ref[...], (SUBLANE, REL_PAD))
        base = jnp.dot(row, _rel_onehot(1, (REL_PAD, ROLL_W)), precision=lax.Precision.HIGHEST,
                       preferred_element_type=F32)[0:1]
        tile = _roll_rows(jnp.broadcast_to(base, (QB, ROLL_W)), left=False)[:, :KW]
        qc = lax.broadcasted_iota(I32, (QB, KW), 0) // CHUNK
        kc = lax.broadcasted_iota(I32, (QB, KW), 1) // CHUNK - N_PAST
        o_ref[...] = jnp.where((kc >= qc - N_PAST) & (kc <= qc), tile, NEG)

    return pl.pallas_call(
        body, name="bias_tile", grid=(H,),
        in_specs=[pl.BlockSpec((None, 1, REL_PAD), lambda h: (h, 0, 0))],
        out_specs=pl.BlockSpec((None, QB, KW), lambda h: (h, 0, 0)),
        out_shape=jax.ShapeDtypeStruct((H, QB, KW), F32), compiler_params=_cp("parallel"))(rb)


def _bias_tile_grad(dtile, n_rel):
    H = dtile.shape[0]

    def body(dt_ref, o_ref):
        x = jnp.concatenate([dt_ref[...], jnp.zeros((QB, ROLL_W - KW), F32)], axis=1)
        cs = jnp.sum(_roll_rows(x, left=True), axis=0, keepdims=True)
        o_ref[...] = jnp.dot(jnp.broadcast_to(cs, (SUBLANE, ROLL_W)), _rel_onehot(0, (ROLL_W, REL_PAD)),
                             precision=lax.Precision.HIGHEST, preferred_element_type=F32)[0:1]

    out = pl.pallas_call(
        body, name="bias_tile_grad", grid=(H,),
        in_specs=[pl.BlockSpec((None, QB, KW), lambda h: (h, 0, 0))],
        out_specs=pl.BlockSpec((None, 1, REL_PAD), lambda h: (h, 0, 0)),
        out_shape=jax.ShapeDtypeStruct((H, 1, REL_PAD), F32), compiler_params=_cp("parallel"))(dtile)
    return out.reshape(H, REL_PAD)[:, :n_rel]


def _band_specs(G):
    spec = lambda f: pl.BlockSpec((QB, BAND_W), f)
    q = spec(lambda h, i: (i, h))
    ks = [spec(lambda h, i, r=r: (jnp.maximum(i - 2 + r, 0), G + h)) for r in range(3)]
    vs = [spec(lambda h, i, r=r: (jnp.maximum(i - 2 + r, 0), 2 * G + h)) for r in range(3)]
    gate = spec(lambda h, i: (i, 5 * G + h))
    bias = pl.BlockSpec((HB, QB, KW), lambda h, i: (h, 0, 0))
    return [q] + ks + vs + [gate, bias]


def _band_probs(i, hh, q_ref, k_refs, v_refs, bias_ref):
    cols = slice(hh * HEAD_A, (hh + 1) * HEAD_A)
    q = q_ref[:, cols].astype(BF16)
    k = jnp.concatenate([r[:, cols] for r in k_refs], axis=0).astype(BF16)
    v = jnp.concatenate([r[:, cols] for r in v_refs], axis=0).astype(BF16)
    s = _dot(q, k, NT) * (HEAD_A ** -0.5) + bias_ref[hh]
    kpos = (i - 2) * QB + lax.broadcasted_iota(I32, (1, KW), 1)
    s = jnp.where(kpos >= 0, s, NEG)
    e = jnp.exp(s - jnp.max(s, axis=-1, keepdims=True))
    p = e * (1.0 / jnp.sum(e, axis=-1, keepdims=True))
    return p, q, k, v


def _band_attn_fwd(proj, bias_tile, carry=None):
    T = proj.shape[0]
    H = bias_tile.shape[0]

    def body(q_ref, k0, k1, k2, v0, v1, v2, gate_ref, bias_ref, ya_ref, y_ref):
        for hh in range(HB):
            cols = slice(hh * HEAD_A, (hh + 1) * HEAD_A)
            p, _, _, v = _band_probs(pl.program_id(1), hh, q_ref, (k0, k1, k2), (v0, v1, v2), bias_ref)
            o = _dot(p.astype(BF16), v, NN)
            g = gate_ref[:, cols]
            ya_ref[:, cols] = o
            y_ref[:, cols] = (o * (g * _sigmoid(g))).astype(y_ref.dtype)

    out = pl.BlockSpec((QB, BAND_W), lambda h, i: (i, h))
    return _call(
        "band_attn_fwd", body, grid=(H // HB, T // QB), in_specs=_band_specs(H // HB), out_specs=[out, out],
        out_shape=[jax.ShapeDtypeStruct((T, H * HEAD_A), F32), jax.ShapeDtypeStruct((T, 2 * H * HEAD_A), BF16)],
        args=[proj] * 8 + [bias_tile], sem=("parallel", "parallel"), carry=carry)


def _band_attn_bwd(proj, bias_tile, dy, ya, carry=None):
    T = proj.shape[0]
    H = bias_tile.shape[0]
    n_i = T // QB

    def body(q_ref, k0, k1, k2, v0, v1, v2, gate_ref, bias_ref, dy_ref, ya_ref,
             dq_ref, dk_ref, dv_ref, dgate_ref, dbias_ref, dk_acc, dv_acc):
        i = pl.program_id(1)

        @pl.when(i == 0)
        def _():
            dk_acc[...] = jnp.zeros_like(dk_acc)
            dv_acc[...] = jnp.zeros_like(dv_acc)
            dbias_ref[...] = jnp.zeros_like(dbias_ref)

        for hh in range(HB):
            cols = slice(hh * HEAD_A, (hh + 1) * HEAD_A)
            p, q, k, v = _band_probs(i, hh, q_ref, (k0, k1, k2), (v0, v1, v2), bias_ref)
            g = gate_ref[:, cols]
            sg = _sigmoid(g)
            dyv = dy_ref[:, cols]
            dgate_ref[:, cols] = (dyv * ya_ref[:, cols] * _dsilu(g, sg)).astype(dgate_ref.dtype)
            do = (dyv * (g * sg)).astype(BF16)
            dp = _dot(do, v, NT)
            ds = p * (dp - jnp.sum(dp * p, axis=-1, keepdims=True))
            dbias_ref[hh] += ds
            dsb = (ds * (HEAD_A ** -0.5)).astype(BF16)
            dq_ref[:, cols] = _dot(dsb, k, NN).astype(dq_ref.dtype)
            dkc = _dot(dsb, q, TN)
            dvc = _dot(p.astype(BF16), do, TN)
            for r in range(3):
                blk = i - 2 + r

                @pl.when(blk >= 0)
                def _(r=r, blk=blk, cols=cols, dkc=dkc, dvc=dvc):
                    rows = pl.ds(pl.multiple_of(blk * QB, QB), QB)
                    dk_acc[rows, cols] += dkc[r * QB:(r + 1) * QB]
                    dv_acc[rows, cols] += dvc[r * QB:(r + 1) * QB]

        @pl.when(i == n_i - 1)
        def _():
            dk_ref[...] = dk_acc[...].astype(dk_ref.dtype)
            dv_ref[...] = dv_acc[...].astype(dv_ref.dtype)

    blk = pl.BlockSpec((QB, BAND_W), lambda h, i: (i, h))
    col = pl.BlockSpec((T, BAND_W), lambda h, i: (0, h))
    sds = jax.ShapeDtypeStruct((T, H * HEAD_A), BF16)
    return _call(
        "band_attn_bwd", body, grid=(H // HB, n_i), in_specs=_band_specs(H // HB) + [blk, blk],
        out_specs=[blk, col, col, blk, pl.BlockSpec((HB, QB, KW), lambda h, i: (h, 0, 0))],
        out_shape=[sds, sds, sds, sds, jax.ShapeDtypeStruct((H, QB, KW), F32)],
        args=[proj] * 8 + [bias_tile, dy, ya], sem=("parallel", "arbitrary"),
        scratch=[pltpu.VMEM((T, BAND_W), F32), pltpu.VMEM((T, BAND_W), F32)], carry=carry)


def _conv_in_specs(tb, C):
    per = tb // HALO
    nb = C // CONV_LC
    prev = lambda i: jnp.maximum(i * per - 1, 0)
    return [pl.BlockSpec((tb, CONV_LC), lambda c, i: (i, 3 * nb + c)), pl.BlockSpec((tb, CONV_LC), lambda c, i: (i, 4 * nb + c)),
            pl.BlockSpec((HALO, CONV_LC), lambda c, i: (prev(i), 3 * nb + c)),
            pl.BlockSpec((HALO, CONV_LC), lambda c, i: (prev(i), 4 * nb + c))]


def _conv_tb(T):
    return _tile(T, 1024) if T > 1024 else T // 2


def _glu_with_halo(i, ga_ref, gb_ref, ha_ref, hb_ref, scr):
    tb = ga_ref.shape[0]
    halo = ha_ref[...] * _sigmoid(hb_ref[...])
    scr[0:HALO, :] = jnp.where(i > 0, halo, 0.0)
    scr[HALO:HALO + tb, :] = ga_ref[...] * _sigmoid(gb_ref[...])
    scr[HALO + tb:, :] = jnp.zeros((SUBLANE, scr.shape[1]), F32)


def _chunk_taps(src, w_ref, row0, tap_of, uscr):
    acc = None
    for r in range(SUBLANE):
        u = None
        for a in range(HALO // SUBLANE + 1):
            j = tap_of(SUBLANE * a + r)
            if 0 <= j < CONV_K:
                rows = pl.ds(pl.multiple_of(row0 + SUBLANE * a, SUBLANE), CONV_RC + SUBLANE)
                t = w_ref[j:j + 1, :] * src[rows, :]
                u = t if u is None else u + t
        if u is None:
            continue
        if r == 0:
            piece = u[0:CONV_RC]
        else:
            uscr[...] = u
            piece = uscr[pl.ds(r, CONV_RC), :]
        acc = piece if acc is None else acc + piece
    return acc


def _layernorm_stats(z):
    mu = jnp.mean(z, axis=-1, keepdims=True)
    zc = z - mu
    rstd = lax.rsqrt(jnp.mean(zc * zc, axis=-1, keepdims=True) + EPS)
    return zc * rstd, rstd


def _conv_z(proj, conv_w, conv_b):
    T = proj.shape[0]
    C = conv_w.shape[1]
    tb = _conv_tb(T)

    def body(ga_ref, gb_ref, ha_ref, hb_ref, w_ref, cb_ref, z_ref, scr, uscr):
        _glu_with_halo(pl.program_id(1), ga_ref, gb_ref, ha_ref, hb_ref, scr)
        cb = cb_ref[...]

        def chunk(rc, carry):
            row0 = pl.multiple_of(rc * CONV_RC, CONV_RC)
            z = _chunk_taps(scr, w_ref, row0, lambda o: o - (HALO - (CONV_K - 1)), uscr)
            z_ref[pl.ds(row0, CONV_RC), :] = z + cb
            return carry

        lax.fori_loop(0, tb // CONV_RC, chunk, 0)

    return pl.pallas_call(
        body, name="conv_z", grid=(C // CONV_LC, T // tb),
        in_specs=_conv_in_specs(tb, C) + [pl.BlockSpec((HALO, CONV_LC), lambda c, i: (0, c)),
                                          pl.BlockSpec((1, CONV_LC), lambda c, i: (0, c))],
        out_specs=pl.BlockSpec((tb, CONV_LC), lambda c, i: (i, c)), out_shape=jax.ShapeDtypeStruct((T, C), F32),
        scratch_shapes=[pltpu.VMEM((HALO + tb + SUBLANE, CONV_LC), F32), pltpu.VMEM((CONV_RC + SUBLANE, CONV_LC), F32)],
        compiler_params=_cp("parallel", "parallel"))(proj, proj, proj, proj, conv_w, conv_b)


def _conv_ln_fwd(proj, z, y, ln_g, ln_b):
    T, C = z.shape
    tb = _tile(T, 256)

    def body(z_ref, gate_ref, g_ref, b_ref, y_in, y_ref):
        xhat, _ = _layernorm_stats(z_ref[...])
        ln = xhat * g_ref[...] + b_ref[...]
        gate = gate_ref[...]
        y_ref[...] = (ln * _sigmoid(ln) * (gate * _sigmoid(gate))).astype(y_ref.dtype)

    vec = pl.BlockSpec((1, C), lambda i: (0, 0))
    return pl.pallas_call(
        body, name="conv_ln_fwd", grid=(T // tb,),
        in_specs=[pl.BlockSpec((tb, C), lambda i: (i, 0)), pl.BlockSpec((tb, C), lambda i: (i, 6)), vec, vec,
                  pl.BlockSpec(memory_space=pl.ANY)],
        out_specs=pl.BlockSpec((tb, C), lambda i: (i, 1)), out_shape=jax.ShapeDtypeStruct(y.shape, y.dtype),
        input_output_aliases={4: 0}, compiler_params=_cp("parallel"))(z, proj, ln_g, ln_b, y)


def _conv_bwd_ln(proj, z, dy, ln_g, ln_b):
    T, C = z.shape
    tb = _tile(T, 256)

    def body(z_ref, gate_ref, dy_ref, g_ref, b_ref, dz_ref, dgate_ref, dg_ref, db_ref, dcb_ref):
        i = pl.program_id(0)
        xhat, rstd = _layernorm_stats(z_ref[...])
        ln = xhat * g_ref[...] + b_ref[...]
        sl = _sigmoid(ln)
        gate = gate_ref[...]
        sg = _sigmoid(gate)
        dyv = dy_ref[...]
        dgate_ref[...] = (dyv * (ln * sl) * _dsilu(gate, sg)).astype(dgate_ref.dtype)
        dln = dyv * (gate * sg) * _dsilu(ln, sl)
        dxh = dln * g_ref[...]
        dz = rstd * (dxh - jnp.mean(dxh, axis=-1, keepdims=True) - xhat * jnp.mean(dxh * xhat, axis=-1, keepdims=True))
        dz_ref[...] = dz
        parts = (jnp.sum(dln * xhat, axis=0, keepdims=True), jnp.sum(dln, axis=0, keepdims=True),
                 jnp.sum(dz, axis=0, keepdims=True))

        @pl.when(i == 0)
        def _():
            for ref, part in zip((dg_ref, db_ref, dcb_ref), parts):
                ref[...] = part

        @pl.when(i > 0)
        def _():
            for ref, part in zip((dg_ref, db_ref, dcb_ref), parts):
                ref[...] += part

    vec = pl.BlockSpec((1, C), lambda i: (0, 0))
    row = pl.BlockSpec((tb, C), lambda i: (i, 0))
    vsd = jax.ShapeDtypeStruct((1, C), F32)
    return pl.pallas_call(
        body, name="conv_bwd_ln", grid=(T // tb,),
        in_specs=[row, pl.BlockSpec((tb, C), lambda i: (i, 6)), pl.BlockSpec((tb, C), lambda i: (i, 1)), vec, vec],
        out_specs=[row, row, vec, vec, vec],
        out_shape=[jax.ShapeDtypeStruct((T, C), F32), jax.ShapeDtypeStruct((T, C), BF16), vsd, vsd, vsd],
        compiler_params=_cp("arbitrary"))(z, proj, dy, ln_g, ln_b)


def _conv_bwd_taps(proj, dz, conv_w):
    T = proj.shape[0]
    C = conv_w.shape[1]
    tb = _conv_tb(T)
    per = tb // HALO
    n_i = T // tb
    first = HALO - (CONV_K - 1)

    def body(ga_ref, gb_ref, ha_ref, hb_ref, dz_ref, dzn_ref, w_ref, da_ref, db_ref, dw_ref, scr, dscr, uscr, zscr, dwacc):
        i = pl.program_id(1)
        _glu_with_halo(i, ga_ref, gb_ref, ha_ref, hb_ref, scr)
        dscr[0:tb, :] = dz_ref[...]
        dscr[tb:tb + HALO, :] = jnp.where(i < n_i - 1, dzn_ref[...], 0.0)
        dscr[tb + HALO:, :] = jnp.zeros((SUBLANE, CONV_LC), F32)
        zscr[0:SUBLANE, :] = jnp.zeros((SUBLANE, CONV_LC), F32)
        zscr[SUBLANE + CONV_RC:, :] = jnp.zeros((SUBLANE, CONV_LC), F32)

        @pl.when(i == 0)
        def _():
            dwacc[...] = jnp.zeros_like(dwacc)

        def chunk(rc, carry):
            row0 = pl.multiple_of(rc * CONV_RC, CONV_RC)
            rows = pl.ds(row0, CONV_RC)
            dglu = _chunk_taps(dscr, w_ref, row0, lambda o: CONV_K - 1 - o, uscr)
            ga = ga_ref[rows, :]
            sb = _sigmoid(gb_ref[rows, :])
            da_ref[rows, :] = (dglu * sb).astype(da_ref.dtype)
            db_ref[rows, :] = (dglu * ga * sb * (1.0 - sb)).astype(db_ref.dtype)
            zscr[SUBLANE:SUBLANE + CONV_RC, :] = dz_ref[rows, :]
            for r in range(SUBLANE):
                dzs = zscr[pl.ds(SUBLANE - r, CONV_RC + SUBLANE), :]
                for a in range(HALO // SUBLANE + 1):
                    j = SUBLANE * a + r - first
                    if 0 <= j < CONV_K:
                        src = pl.ds(pl.multiple_of(row0 + SUBLANE * a, SUBLANE), CONV_RC + SUBLANE)
                        p = dzs * scr[src, :]
                        f = p[0:SUBLANE]
                        for s in range(1, CONV_RC // SUBLANE + 1):
                            f = f + p[s * SUBLANE:(s + 1) * SUBLANE]
                        dwacc[j * SUBLANE:(j + 1) * SUBLANE, :] += f
            return carry

        lax.fori_loop(0, tb // CONV_RC, chunk, 0)

        @pl.when(i == n_i - 1)
        def _():
            dw_ref[...] = jnp.zeros_like(dw_ref)
            for j in range(CONV_K):
                dw_ref[j:j + 1, :] = jnp.sum(dwacc[j * SUBLANE:(j + 1) * SUBLANE, :], axis=0, keepdims=True)

    blk = pl.BlockSpec((tb, CONV_LC), lambda c, i: (i, c))
    wspec = pl.BlockSpec((HALO, CONV_LC), lambda c, i: (0, c))
    nxt = pl.BlockSpec((HALO, CONV_LC), lambda c, i: (jnp.minimum((i + 1) * per, T // HALO - 1), c))
    return pl.pallas_call(
        body, name="conv_bwd_taps", grid=(C // CONV_LC, n_i),
        in_specs=_conv_in_specs(tb, C) + [blk, nxt, wspec], out_specs=[blk, blk, wspec],
        out_shape=[jax.ShapeDtypeStruct((T, C), BF16), jax.ShapeDtypeStruct((T, C), BF16),
                   jax.ShapeDtypeStruct((HALO, C), F32)],
        scratch_shapes=[pltpu.VMEM((HALO + tb + SUBLANE, CONV_LC), F32), pltpu.VMEM((tb + HALO + SUBLANE, CONV_LC), F32),
                        pltpu.VMEM((CONV_RC + SUBLANE, CONV_LC), F32), pltpu.VMEM((CONV_RC + 2 * SUBLANE, CONV_LC), F32),
                        pltpu.VMEM((HALO * SUBLANE, CONV_LC), F32)],
        compiler_params=_cp("parallel", "arbitrary"))(proj, proj, proj, proj, dz, dz, conv_w)


def _sgu_mask():
    r = lax.broadcasted_iota(I32, (GMLP_CHUNK, GMLP_CHUNK), 0) // CHUNK
    c = lax.broadcasted_iota(I32, (GMLP_CHUNK, GMLP_CHUNK), 1) // CHUNK
    return r >= c


def _sgu_fwd(proj, ln_g, ln_b, w_s, b_s_t):
    T = proj.shape[0]
    W = ln_g.shape[1]
    G = w_s.shape[0]
    cg = W // G
    tb = GMLP_CHUNK

    def body(u_ref, v_ref, gate_ref, g_ref, b_ref, ws_ref, bs_ref, y_ref):
        xhat, _ = _layernorm_stats(v_ref[...])
        vln = (xhat * g_ref[...] + b_ref[...]).astype(BF16)
        mask = _sgu_mask()
        for gi in range(G):
            cols = slice(gi * cg, (gi + 1) * cg)
            ws = jnp.where(mask, ws_ref[gi], 0.0).astype(BF16)
            sg = _dot(ws, vln[:, cols], NN) + bs_ref[:, gi:gi + 1]
            gate = gate_ref[:, cols]
            y_ref[:, cols] = (u_ref[:, cols] * sg * (gate * _sigmoid(gate))).astype(y_ref.dtype)

    vec = pl.BlockSpec((1, W), lambda i: (0, 0))
    return pl.pallas_call(
        body, name="sgu_fwd", grid=(T // tb,),
        in_specs=[pl.BlockSpec((tb, W), lambda i: (i, 0)), pl.BlockSpec((tb, W), lambda i: (i, 1)),
                  pl.BlockSpec((tb, W), lambda i: (i, 2)), vec, vec,
                  pl.BlockSpec((G, GMLP_CHUNK, GMLP_CHUNK), lambda i: (0, 0, 0)),
                  pl.BlockSpec((GMLP_CHUNK, G), lambda i: (0, 0))],
        out_specs=pl.BlockSpec((tb, W), lambda i: (i, 0)), out_shape=jax.ShapeDtypeStruct((T, W), BF16),
        compiler_params=_cp("parallel"))(proj, proj, proj, ln_g, ln_b, w_s, b_s_t)


def _sgu_bwd(proj, dy, ln_g, ln_b, w_s, b_s_t):
    T = proj.shape[0]
    W = ln_g.shape[1]
    G = w_s.shape[0]
    cg = W // G
    tb = GMLP_CHUNK

    def body(u_ref, v_ref, gate_ref, dy_ref, g_ref, b_ref, ws_ref, bs_ref,
             dp_ref, dws_ref, dbs_ref, dg_ref, db_ref, dvln_scr):
        i = pl.program_id(0)

        @pl.when(i == 0)
        def _():
            dws_ref[...] = jnp.zeros_like(dws_ref)
            dbs_ref[...] = jnp.zeros_like(dbs_ref)
            dg_ref[...] = jnp.zeros_like(dg_ref)
            db_ref[...] = jnp.zeros_like(db_ref)

        xhat, rstd = _layernorm_stats(v_ref[...])
        vln = (xhat * g_ref[...] + b_ref[...]).astype(BF16)
        mask = _sgu_mask()
        for gi in range(G):
            cols = slice(gi * cg, (gi + 1) * cg)
            ws = jnp.where(mask, ws_ref[gi], 0.0).astype(BF16)
            vg = vln[:, cols]
            sg = _dot(ws, vg, NN) + bs_ref[:, gi:gi + 1]
            gate = gate_ref[:, cols]
            s = _sigmoid(gate)
            u = u_ref[:, cols]
            dyv = dy_ref[:, cols]
            dyu = dyv * u
            dp_ref[:, cols] = (dyv * sg * (gate * s)).astype(dp_ref.dtype)
            dp_ref[:, 2 * W + gi * cg:2 * W + (gi + 1) * cg] = (dyu * sg * _dsilu(gate, s)).astype(dp_ref.dtype)
            dsg = dyu * (gate * s)
            dsgb = dsg.astype(BF16)
            dvln_scr[:, cols] = _dot(ws, dsgb, TN)
            dws_ref[gi] += _dot(dsgb, vg, NT)
            dbs_ref[:, gi:gi + 1] += jnp.sum(dsg, axis=-1, keepdims=True)
        dvln = dvln_scr[...]
        dg_ref[...] += jnp.sum(dvln * xhat, axis=0, keepdims=True)
        db_ref[...] += jnp.sum(dvln, axis=0, keepdims=True)
        dxh = dvln * g_ref[...]
        dv = rstd * (dxh - jnp.mean(dxh, axis=-1, keepdims=True) - xhat * jnp.mean(dxh * xhat, axis=-1, keepdims=True))
        dp_ref[:, W:2 * W] = dv.astype(dp_ref.dtype)

    vec = pl.BlockSpec((1, W), lambda i: (0, 0))
    wsp = pl.BlockSpec((G, GMLP_CHUNK, GMLP_CHUNK), lambda i: (0, 0, 0))
    bsp = pl.BlockSpec((GMLP_CHUNK, G), lambda i: (0, 0))
    return pl.pallas_call(
        body, name="sgu_bwd", grid=(T // tb,),
        in_specs=[pl.BlockSpec((tb, W), lambda i: (i, 0)), pl.BlockSpec((tb, W), lambda i: (i, 1)),
                  pl.BlockSpec((tb, W), lambda i: (i, 2)), pl.BlockSpec((tb, W), lambda i: (i, 0)), vec, vec, wsp, bsp],
        out_specs=[pl.BlockSpec((tb, 3 * W), lambda i: (i, 0)), wsp, bsp, vec, vec],
        out_shape=[jax.ShapeDtypeStruct((T, 3 * W), BF16), jax.ShapeDtypeStruct((G, GMLP_CHUNK, GMLP_CHUNK), F32),
                   jax.ShapeDtypeStruct((GMLP_CHUNK, G), F32), jax.ShapeDtypeStruct((1, W), F32),
                   jax.ShapeDtypeStruct((1, W), F32)],
        scratch_shapes=[pltpu.VMEM((tb, W), F32)],
        compiler_params=_cp("arbitrary"))(proj, proj, proj, dy, ln_g, ln_b, w_s, b_s_t)


def _xattn_probs(q, k, hd):
    s = _dot(q, k, NT) * (hd ** -0.5)
    e = jnp.exp(s - jnp.max(s, axis=-1, keepdims=True))
    return e * (1.0 / jnp.sum(e, axis=-1, keepdims=True))


def _xattn_fwd(name, q, k, v):
    T, D = q.shape
    M = k.shape[0]
    hd = D // N_HEADS_X
    tb = _tile(T, 512)

    def body(q_ref, k_ref, v_ref, o_ref):
        for h in range(N_HEADS_X):
            cols = slice(h * hd, (h + 1) * hd)
            p = _xattn_probs(q_ref[:, cols], k_ref[:, cols], hd)
            o_ref[:, cols] = _dot(p.astype(BF16), v_ref[:, cols], NN).astype(o_ref.dtype)

    row = pl.BlockSpec((tb, D), lambda i: (i, 0))
    kv = pl.BlockSpec((M, D), lambda i: (0, 0))
    return pl.pallas_call(
        body, name=name, grid=(T // tb,), in_specs=[row, kv, kv], out_specs=row,
        out_shape=jax.ShapeDtypeStruct((T, D), BF16), compiler_params=_cp("parallel"))(q, k, v)


def _xattn_bwd(name, q, k, v, do):
    T, D = q.shape
    M = k.shape[0]
    hd = D // N_HEADS_X
    tb = _tile(T, 512)

    def body(q_ref, k_ref, v_ref, do_ref, dq_ref, dk_ref, dv_ref):
        @pl.when(pl.program_id(0) == 0)
        def _():
            dk_ref[...] = jnp.zeros_like(dk_ref)
            dv_ref[...] = jnp.zeros_like(dv_ref)

        for h in range(N_HEADS_X):
            cols = slice(h * hd, (h + 1) * hd)
            qh, kh, doh = q_ref[:, cols], k_ref[:, cols], do_ref[:, cols]
            p = _xattn_probs(qh, kh, hd)
            dp = _dot(doh, v_ref[:, cols], NT)
            ds = p * (dp - jnp.sum(dp * p, axis=-1, keepdims=True))
            dsb = (ds * (hd ** -0.5)).astype(BF16)
            dq_ref[:, cols] = _dot(dsb, kh, NN).astype(dq_ref.dtype)
            dk_ref[:, cols] += _dot(dsb, qh, TN)
            dv_ref[:, cols] += _dot(p.astype(BF16), doh, TN)

    row = pl.BlockSpec((tb, D), lambda i: (i, 0))
    kv = pl.BlockSpec((M, D), lambda i: (0, 0))
    return pl.pallas_call(
        body, name=name, grid=(T // tb,), in_specs=[row, kv, kv, row], out_specs=[row, kv, kv],
        out_shape=[jax.ShapeDtypeStruct((T, D), BF16), jax.ShapeDtypeStruct((M, D), F32),
                   jax.ShapeDtypeStruct((M, D), F32)],
        compiler_params=_cp("arbitrary"))(q, k, v, do)


def _adamw(name, contrib, w, m, v, carry=None):
    R, C = w.shape
    tr = min(R, 128)
    while R % tr:
        tr -= SUBLANE

    def body(c_ref, w_ref, m_ref, v_ref, g_ref, d_ref, nm_ref, nv_ref):
        g = c_ref[0].astype(F32)
        for s in range(1, N_DEV):
            g = g + c_ref[s].astype(F32)
        nm = ADAM_B1 * m_ref[...] + (1.0 - ADAM_B1) * g
        nv = ADAM_B2 * v_ref[...] + (1.0 - ADAM_B2) * (g * g)
        m_hat = nm / (1.0 - ADAM_B1 ** ADAM_STEP)
        v_hat = nv / (1.0 - ADAM_B2 ** ADAM_STEP)
        g_ref[...] = g
        d_ref[...] = -ADAM_LR * (m_hat / (jnp.sqrt(v_hat) + ADAM_EPS) + ADAM_WD * w_ref[...])
        nm_ref[...] = nm
        nv_ref[...] = nv

    row = pl.BlockSpec((tr, C), lambda i: (i, 0))
    sds = jax.ShapeDtypeStruct((R, C), F32)
    return _call(name, body, grid=(R // tr,),
                 in_specs=[pl.BlockSpec((N_DEV, tr, C), lambda i: (0, i, 0)), row, row, row], out_specs=[row] * 4,
                 out_shape=[sds] * 4, args=(contrib, w, m, v), sem=("parallel",), carry=carry)


def _pack(arrs):
    unit = SUBLANE * LANE
    flat = [jnp.pad(a.reshape(-1), (0, -a.size % unit)) for a in arrs]
    return jnp.concatenate(flat).reshape(-1, LANE)


def _unpack(buf, shapes):
    unit = SUBLANE * LANE
    flat = buf.reshape(-1)
    out, off = [], 0
    for s in shapes:
        size = 1
        for d in s:
            size *= d
        out.append(flat[off:off + size].reshape(s))
        off += size + (-size % unit)
    return out


def _cross_attention_fwd(l, h, mem, g_x, g_mem, wq, wk, wv, wo):
    hx = _rms_fwd(f"rms_x{l}", h, g_x)
    memn = _rms_fwd(f"rms_mem{l}", mem, g_mem)
    q = _mm_nn(f"xq{l}", hx, wq, BF16)
    k = _mm_nn(f"xk{l}", memn, wk, BF16)
    v = _mm_nn(f"xv{l}", memn, wv, BF16)
    o = _xattn_fwd(f"xattn_fwd{l}", q, k, v)
    h_out = _mm_nn(f"xo{l}", o, wo, F32, add=h)
    return h_out, (hx, memn, q, k, v, o)


def _cross_attention_bwd(l, dh, h, mem, g_x, g_mem, wq, wk, wv, wo, saved):
    hx, memn, q, k, v, o = saved
    do = _mm_nt(f"xo_dx{l}", dh, wo, BF16)
    dwo = _mm_tn(f"xo_dw{l}", o, dh, BF16)
    dq, dk, dv = _xattn_bwd(f"xattn_bwd{l}", q, k, v, do)
    dwq = _mm_tn(f"xq_dw{l}", hx, dq, BF16)
    dwk = _mm_tn(f"xk_dw{l}", memn, dk, BF16)
    dwv = _mm_tn(f"xv_dw{l}", memn, dv, BF16)
    dmemn = _mm_nt(f"xk_dx{l}", dk, wk, F32)
    dmemn = _mm_nt(f"xv_dx{l}", dv, wv, F32, add=dmemn)
    _, dg_mem = _rms_bwd(f"rms_mem_bwd{l}", dmemn, mem, g_mem, jnp.zeros_like(mem))
    dhx = _mm_nt(f"xq_dx{l}", dq, wq, F32)
    dh_in, dg_x = _rms_bwd(f"rms_x_bwd{l}", dhx, h, g_x, dh)
    return dh_in, (dg_x, dg_mem, dwq, dwk, dwv, dwo)


def kernel(x, mem, norm_mix_g, norm_x_g, norm_mem_g, final_norm_g, w_in_ab, rel_bias, conv_w, conv_b, conv_ln_g, conv_ln_b, w_out_ab, w_in_c, sgu_ln_g, sgu_ln_b, w_s, b_s, w_out_c, w_xq, w_xk, w_xv, w_xo, loss_target, m_norm_mix_g, m_norm_x_g, m_norm_mem_g, m_final_norm_g, m_w_in_ab, m_rel_bias, m_conv_w, m_conv_b, m_conv_ln_g, m_conv_ln_b, m_w_out_ab, m_w_in_c, m_sgu_ln_g, m_sgu_ln_b, m_w_s, m_b_s, m_w_out_c, m_w_xq, m_w_xk, m_w_xv, m_w_xo, v_norm_mix_g, v_norm_x_g, v_norm_mem_g, v_final_norm_g, v_w_in_ab, v_rel_bias, v_conv_w, v_conv_b, v_conv_ln_g, v_conv_ln_b, v_w_out_ab, v_w_in_c, v_sgu_ln_g, v_sgu_ln_b, v_w_s, v_b_s, v_w_out_c, v_w_xq, v_w_xk, v_w_xv, v_w_xo):
    names = ["norm_mix_g", "norm_x_g", "norm_mem_g", "final_norm_g", "w_in_ab", "rel_bias", "conv_w", "conv_b",
             "conv_ln_g", "conv_ln_b", "w_out_ab", "w_in_c", "sgu_ln_g", "sgu_ln_b", "w_s", "b_s", "w_out_c",
             "w_xq", "w_xk", "w_xv", "w_xo"]
    W = dict(zip(names, (norm_mix_g, norm_x_g, norm_mem_g, final_norm_g, w_in_ab, rel_bias, conv_w, conv_b, conv_ln_g,
                         conv_ln_b, w_out_ab, w_in_c, sgu_ln_g, sgu_ln_b, w_s, b_s, w_out_c, w_xq, w_xk, w_xv, w_xo)))
    M1 = dict(zip(names, (m_norm_mix_g, m_norm_x_g, m_norm_mem_g, m_final_norm_g, m_w_in_ab, m_rel_bias, m_conv_w, m_conv_b,
                          m_conv_ln_g, m_conv_ln_b, m_w_out_ab, m_w_in_c, m_sgu_ln_g, m_sgu_ln_b, m_w_s, m_b_s, m_w_out_c,
                          m_w_xq, m_w_xk, m_w_xv, m_w_xo)))
    M2 = dict(zip(names, (v_norm_mix_g, v_norm_x_g, v_norm_mem_g, v_final_norm_g, v_w_in_ab, v_rel_bias, v_conv_w, v_conv_b,
                          v_conv_ln_g, v_conv_ln_b, v_w_out_ab, v_w_in_c, v_sgu_ln_g, v_sgu_ln_b, v_w_s, v_b_s, v_w_out_c,
                          v_w_xq, v_w_xk, v_w_xv, v_w_xo)))

    h0, memv, tgt = x[0], mem[0], loss_target[0]
    T, D = h0.shape
    n_rel = rel_bias.shape[2]
    xnames = ["w_xq", "w_xk", "w_xv", "w_xo"]
    bf = lambda a: a.astype(BF16)
    blocks = lambda g: g.reshape(N_DEV, -1, D)

    small = _pack([conv_w[0], sgu_ln_g[0], sgu_ln_b[0]])
    win_ab, small_g = _comm_call("gather_in_ab", _Carry("gather", [bf(w_in_ab[0]), small]))
    per_cw, per_ln = conv_w.shape[2], sgu_ln_g.shape[1]
    cw_s, lg_s, lb_s = zip(*[_unpack(small_g[d], [(CONV_K, per_cw), (1, per_ln), (1, per_ln)]) for d in range(N_DEV)])
    conv_w_full = jnp.pad(jnp.concatenate(cw_s, axis=1), ((0, HALO - CONV_K), (0, 0)))
    sgu_g_full = jnp.concatenate(lg_s, axis=1)
    sgu_b_full = jnp.concatenate(lb_s, axis=1)
    b_s_t = b_s[0].T

    hn0 = _rms_fwd("rms_mix0", h0, norm_mix_g[0:1])
    ag1 = _Carry("gather", [bf(w_out_ab[0])] + [bf(W[n][0]) for n in xnames])
    proj_ab = _mm_nn("in_ab", hn0, win_ab, F32, carry=ag1)
    wout_ab = ag1.result[0].reshape(-1, D)
    wx0 = [g.reshape(D, D) for g in ag1.result[1:]]
    btile = _bias_tile(rel_bias[0])
    ag2 = _Carry("gather", [bf(w_in_c[0])])
    ya, y_ab = _band_attn_fwd(proj_ab, btile, carry=ag2)
    win_c = ag2.result[0]
    z_conv = _conv_z(proj_ab, conv_w_full, conv_b)
    y_ab = _conv_ln_fwd(proj_ab, z_conv, y_ab, conv_ln_g, conv_ln_b)
    h1 = _mm_nn("out_ab", y_ab, wout_ab, F32, add=h0)
    h2, xs0 = _cross_attention_fwd(0, h1, memv, norm_x_g[0:1], norm_mem_g[0:1], *wx0)
    hn1 = _rms_fwd("rms_mix1", h2, norm_mix_g[1:2])
    ag3 = _Carry("gather", [bf(w_out_c[0])] + [bf(W[n][1]) for n in xnames])
    proj_c = _mm_nn("in_c", hn1, win_c, F32, carry=ag3)
    wout_c = ag3.result[0].reshape(-1, D)
    wx1 = [g.reshape(D, D) for g in ag3.result[1:]]
    y_c = _sgu_fwd(proj_c, sgu_g_full, sgu_b_full, w_s[0], b_s_t)
    h3 = _mm_nn("out_c", y_c, wout_c, F32, add=h2)
    h4, xs1 = _cross_attention_fwd(1, h3, memv, norm_x_g[1:2], norm_mem_g[1:2], *wx1)
    loss_acc, dh4, dg_final = _loss_head(h4, tgt, final_norm_g.reshape(1, D))

    sent = {}

    def ride(**grads):
        carry = _Carry("exchange", list(grads.values()))
        sent.update({name: (carry, i) for i, name in enumerate(grads)})
        return carry

    dh3, (dg_x1, dg_mem1, dwq1, dwk1, dwv1, dwo1) = _cross_attention_bwd(
        1, dh4, h3, memv, norm_x_g[1:2], norm_mem_g[1:2], *wx1, xs1)
    dy_c = _mm_nt("out_c_dx", dh3, wout_c, F32, carry=ride(w_xo1=blocks(dwo1)))
    dwout_c = _mm_tn("out_c_dw", y_c, dh3, BF16, carry=ride(w_xq1=blocks(dwq1)))
    dproj_c, dws, dbs_t, dsgu_g, dsgu_b = _sgu_bwd(proj_c, dy_c, sgu_g_full, sgu_b_full, w_s[0], b_s_t)
    dwin_c = _mm_tn("in_c_dw", hn1, dproj_c, BF16, per=win_c.shape[2],
                    carry=ride(w_xk1=blocks(dwk1), w_out_c=blocks(dwout_c)))
    dhn1 = _mm_nt("in_c_dx", dproj_c, win_c, F32, carry=ride(w_xv1=blocks(dwv1)))
    dh2, dg_mix1 = _rms_bwd("rms_mix1_bwd", dhn1, h2, norm_mix_g[1:2], dh3)
    dh1, (dg_x0, dg_mem0, dwq0, dwk0, dwv0, dwo0) = _cross_attention_bwd(
        0, dh2, h1, memv, norm_x_g[0:1], norm_mem_g[0:1], *wx0, xs0)
    dy_ab = _mm_nt("out_ab_dx", dh1, wout_ab, F32, carry=ride(w_xo0=blocks(dwo0)))
    dwout_ab = _mm_tn("out_ab_dw", y_ab, dh1, BF16, carry=ride(w_xq0=blocks(dwq0)))
    dq, dk, dv, dgate_a, dbtile = _band_attn_bwd(proj_ab, btile, dy_ab, ya, carry=ride(w_in_c=dwin_c))
    drel = _bias_tile_grad(dbtile, n_rel)
    dz, dgate_b, dcln_g, dcln_b, dconv_b = _conv_bwd_ln(proj_ab, z_conv, dy_ab, conv_ln_g, conv_ln_b)
    dglu_a, dglu_b, dconv_w = _conv_bwd_taps(proj_ab, dz, conv_w_full)
    dproj_ab = jnp.concatenate([dq, dk, dv, dglu_a, dglu_b, dgate_a, dgate_b], axis=1)
    dwin_ab = _mm_tn("in_ab_dw", hn0, dproj_ab, BF16, per=win_ab.shape[2],
                     carry=ride(w_xk0=blocks(dwk0), w_xv0=blocks(dwv0), w_out_ab=blocks(dwout_ab)))
    dhn0 = _mm_nt("in_ab_dx", dproj_ab, win_ab, F32, carry=ride(w_in_ab=dwin_ab))
    dx, dg_mix0 = _rms_bwd("rms_mix0_bwd", dhn0, h0, norm_mix_g[0:1], dh1)
    received = lambda name: sent[name][0].result[sent[name][1]]

    sm = [_pack([dconv_w[:CONV_K, d * per_cw:(d + 1) * per_cw], dsgu_g[:, d * per_ln:(d + 1) * per_ln],
                 dsgu_b[:, d * per_ln:(d + 1) * per_ln]]) for d in range(N_DEV)]
    mask = (jnp.arange(GMLP_CHUNK)[:, None] // CHUNK >= jnp.arange(GMLP_CHUNK)[None, :] // CHUNK).astype(F32)
    rep_names = ["norm_mix_g", "norm_x_g", "norm_mem_g", "final_norm_g", "rel_bias", "conv_b", "conv_ln_g", "conv_ln_b",
                 "w_s", "b_s"]
    rep_grads = {
        "norm_mix_g": jnp.concatenate([dg_mix0, dg_mix1], axis=0),
        "norm_x_g": jnp.concatenate([dg_x0, dg_x1], axis=0),
        "norm_mem_g": jnp.concatenate([dg_mem0, dg_mem1], axis=0),
        "final_norm_g": dg_final.reshape(D),
        "rel_bias": drel[None], "conv_b": dconv_b, "conv_ln_g": dcln_g, "conv_ln_b": dcln_b,
        "w_s": (dws * mask[None])[None], "b_s": dbs_t.T[None],
    }
    ex_e = _Carry("exchange", [jnp.stack(sm), _pack([rep_grads[n] for n in rep_names])], bcast=[False, True])

    out = {}
    kinds = ("grad", "delta", "new_m", "new_v")
    for n in ("w_in_ab", "w_out_ab", "w_in_c", "w_out_c"):
        res = _adamw(f"adamw_{n}", received(n), W[n][0], M1[n][0], M2[n][0], carry=ex_e if n == "w_in_ab" else None)
        for kind, r in zip(kinds, res):
            out[(kind, n)] = r[None]
    recv_small, recv_rep = ex_e.result
    for n in xnames:
        res = [_adamw(f"adamw_{n}{l}", received(f"{n}{l}"), W[n][l], M1[n][l], M2[n][l]) for l in range(2)]
        for kind, r in zip(kinds, zip(*res)):
            out[(kind, n)] = jnp.stack(r)
    sm_names = ["conv_w", "sgu_ln_g", "sgu_ln_b"]
    res = _adamw("adamw_small", recv_small, *[_pack([D_[n][0] for n in sm_names]) for D_ in (W, M1, M2)])
    for kind, r in zip(kinds, res):
        for n, piece in zip(sm_names, _unpack(r, [W[n].shape for n in sm_names])):
            out[(kind, n)] = piece
    res = _adamw("adamw_replicated", recv_rep, *[_pack([D_[n] for n in rep_names]) for D_ in (W, M1, M2)])
    for kind, r in zip(kinds, res):
        for n, piece in zip(rep_names, _unpack(r, [W[n].shape for n in rep_names])):
            out[(kind, n)] = piece

    loss = lax.psum(loss_acc[0, 0], MESH_AXES)
    return (loss, dx[None]) + tuple(out[(kind, n)] for kind in kinds for n in names)
```

```python
import jax
import jax.numpy as jnp
from jax import lax
from jax.experimental import pallas as pl
from jax.experimental.pallas import tpu as pltpu

F32 = jnp.float32
BF16 = jnp.bfloat16
I32 = jnp.int32

N_DEV = 8
CHUNK = 64
N_PAST = 8
MAX_REL = 128
HEAD_A = 128
CONV_K = 31
GMLP_CHUNK = 128
N_HEADS_X = 4
EPS = 1e-6
NEG = -1e30

ADAM_LR, ADAM_B1, ADAM_B2, ADAM_EPS, ADAM_WD, ADAM_STEP = 0.001, 0.9, 0.999, 1e-08, 0.01, 10

LANE = 128
SUBLANE = 8
VMEM_LIMIT = 56 * 1024 * 1024
MATMUL_VMEM = 44 * 1024 * 1024
QB = 4 * CHUNK
KW = QB + N_PAST * CHUNK
ROLL_W = 1024
REL_PAD = 384
HB = 2
BAND_W = HB * HEAD_A
HALO = 32
CONV_LC = LANE
CONV_RC = 64
MESH_AXES = ("x", "y", "c")
GATHER_PIECES = 4

NN = (((1,), (0,)), ((), ()))
NT = (((1,), (1,)), ((), ()))
TN = (((0,), (0,)), ((), ()))


def _cp(*sem):
    return pltpu.CompilerParams(dimension_semantics=sem, vmem_limit_bytes=VMEM_LIMIT)


def _tile(dim, pref):
    if dim <= pref:
        return dim
    t = (pref // LANE) * LANE
    while dim % t:
        t -= LANE
    return t


def _sigmoid(x):
    return 1.0 / (1.0 + jnp.exp(-x))


def _dsilu(x, s):
    return s * (1.0 + x * (1.0 - s))


def _dot(a, b, dims):
    return lax.dot_general(a, b, dims, preferred_element_type=F32)


def _mesh_pos():
    return lax.axis_index("x"), lax.axis_index("y"), lax.axis_index("c")


def _lin(x, y, c):
    return 4 * x + 2 * y + c


def _remote(src, dst, send_sem, recv_sem, to):
    return pltpu.make_async_remote_copy(src_ref=src, dst_ref=dst, send_sem=send_sem, recv_sem=recv_sem,
                                        device_id=to, device_id_type=pl.DeviceIdType.MESH)


class _Carry:
    def __init__(self, kind, arrs, bcast=None):
        n = len(arrs)
        self.kind, self.arrs, self.n = kind, list(arrs), n
        self.bcast = [kind == "gather"] * n if bcast is None else list(bcast)
        self.out_shape = [jax.ShapeDtypeStruct(((N_DEV,) + a.shape) if b else a.shape, a.dtype)
                          for a, b in zip(arrs, self.bcast)]
        self.units = [(a, None, None) for a in range(n)]
        if kind == "gather":
            self.units = []
            for a, arr in enumerate(arrs):
                pieces = GATHER_PIECES if arr.shape[0] >= GATHER_PIECES * 256 else 1
                rows = arr.shape[0] // pieces
                self.units += [(a, p * rows, rows) if pieces > 1 else (a, None, None) for p in range(pieces)]
        nu = len(self.units)
        self.scratch = [pltpu.SemaphoreType.DMA((nu, 7)), pltpu.SemaphoreType.DMA((nu, 7)), pltpu.SemaphoreType.DMA((n,))]
        self.result = None

    def _src(self, ins, a, d):
        return ins[a] if self.bcast[a] else ins[a].at[d]

    def _local(self, ins, outs, sems):
        me = _lin(*_mesh_pos())
        return [pltpu.make_async_copy(self._src(ins, a, me), outs[a].at[me], sems[2].at[a]) for a in range(self.n)]

    @staticmethod
    def _chips():
        x, y, _ = _mesh_pos()
        return [(1 - x, y), (x, 1 - y), (1 - x, 1 - y)]

    def _g_copy(self, ins, outs, sems, u, k, block, to, own=False):
        a, row0, rows = self.units[u]
        piece = (lambda r: r) if row0 is None else (lambda r: r.at[pl.ds(row0, rows)])
        dst = piece(outs[a].at[_lin(*block)])
        return _remote(piece(ins[a]) if own else dst, dst, sems[0].at[u, k], sems[1].at[u, k], to)

    def _g_first(self, ins, outs, sems):
        x, y, c = _mesh_pos()
        cps = []
        for u in range(len(self.units)):
            cps.append(self._g_copy(ins, outs, sems, u, 0, (x, y, c), (x, y, 1 - c), own=True))
            cps += [self._g_copy(ins, outs, sems, u, 1 + j, (x, y, c), (*chip, c), own=True)
                    for j, chip in enumerate(self._chips())]
        return cps

    def _g_passed(self, ins, outs, sems):
        x, y, c = _mesh_pos()
        return [self._g_copy(ins, outs, sems, u, 4 + j, (*chip, c), (x, y, 1 - c))
                for u in range(len(self.units)) for j, chip in enumerate(self._chips())]

    @staticmethod
    def _peer(k):
        x, y, c = _mesh_pos()
        return (1 - x if k & 4 else x, 1 - y if k & 2 else y, 1 - c if k & 1 else c)

    def _x_sends(self, ins, outs, sems):
        me = _lin(*_mesh_pos())
        return [_remote(self._src(ins, a, _lin(*self._peer(k))), outs[a].at[me], sems[0].at[a, k - 1],
                        sems[1].at[a, k - 1], self._peer(k)) for k in range(1, N_DEV) for a in range(self.n)]

    def start(self, ins, outs, sems):
        for cp in self._local(ins, outs, sems):
            cp.start()
        for cp in (self._g_first if self.kind == "gather" else self._x_sends)(ins, outs, sems):
            cp.start()

    def mid(self, ins, outs, sems):
        if self.kind != "gather":
            return
        x, y, c = _mesh_pos()
        passed = self._g_passed(ins, outs, sems)
        for u in range(len(self.units)):
            for j, chip in enumerate(self._chips()):
                self._g_copy(ins, outs, sems, u, 1 + j, (*chip, c), (x, y, c)).wait_recv()
                passed[3 * u + j].start()

    def finish(self, ins, outs, sems):
        x, y, c = _mesh_pos()
        if self.kind == "gather":
            for u in range(len(self.units)):
                self._g_copy(ins, outs, sems, u, 0, (x, y, 1 - c), (x, y, c)).wait_recv()
                for j, chip in enumerate(self._chips()):
                    self._g_copy(ins, outs, sems, u, 4 + j, (*chip, 1 - c), (x, y, c)).wait_recv()
            sent = self._g_first(ins, outs, sems) + self._g_passed(ins, outs, sems)
        else:
            for k in range(1, N_DEV):
                for a in range(self.n):
                    got = outs[a].at[_lin(*self._peer(k))]
                    _remote(got, got, sems[0].at[a, k - 1], sems[1].at[a, k - 1], self._peer(k)).wait_recv()
            sent = self._x_sends(ins, outs, sems)
        for cp in sent:
            cp.wait_send()
        for cp in self._local(ins, outs, sems):
            cp.wait()


def _comm_call(name, carry):
    n = carry.n

    def body(*refs):
        ins, outs, sems = refs[:n], refs[n:2 * n], refs[2 * n:]
        carry.start(ins, outs, sems)
        carry.mid(ins, outs, sems)
        carry.finish(ins, outs, sems)

    hbm = pl.BlockSpec(memory_space=pl.ANY)
    return pl.pallas_call(body, name=name, in_specs=[hbm] * n, out_specs=[hbm] * n, out_shape=carry.out_shape,
                          scratch_shapes=carry.scratch)(*carry.arrs)


def _call(name, body, *, grid, in_specs, out_specs, out_shape, args, sem, scratch=(), aliases=None, carry=None):
    aliases = aliases or {}
    if carry is None:
        return pl.pallas_call(body, name=name, grid=grid, in_specs=in_specs, out_specs=out_specs, out_shape=out_shape,
                              scratch_shapes=list(scratch), input_output_aliases=aliases, compiler_params=_cp(*sem))(*args)
    ni, no, ns, nc = len(in_specs), len(out_specs), len(scratch), carry.n
    total = 1
    for g in grid:
        total *= g

    def full(*refs):
        ins, cins = refs[:ni], refs[ni:ni + nc]
        outs, couts = refs[ni + nc:ni + nc + no], refs[ni + nc + no:ni + 2 * nc + no]
        scr, sems = refs[ni + 2 * nc + no:ni + 2 * nc + no + ns], refs[ni + 2 * nc + no + ns:]
        step = pl.program_id(0)
        for d in range(1, len(grid)):
            step = step * grid[d] + pl.program_id(d)

        @pl.when(step == 0)
        def _():
            carry.start(cins, couts, sems)

        body(*ins, *outs, *scr)

        @pl.when(step == total // 2)
        def _():
            carry.mid(cins, couts, sems)

        @pl.when(step == total - 1)
        def _():
            carry.finish(cins, couts, sems)

    hbm = pl.BlockSpec(memory_space=pl.ANY)
    res = pl.pallas_call(
        full, name=name, grid=grid, in_specs=list(in_specs) + [hbm] * nc, out_specs=list(out_specs) + [hbm] * nc,
        out_shape=list(out_shape) + carry.out_shape, scratch_shapes=list(scratch) + carry.scratch,
        input_output_aliases=aliases, compiler_params=_cp(*["arbitrary"] * len(grid)))(*args, *carry.arrs)
    carry.result = list(res[no:])
    return list(res[:no])


def _matmul(name, a, b, *, dims, grid, a_spec, b_spec, out_sds, out_spec, acc_shape, add=None, add_spec=None, carry=None):
    nk = grid[2]
    has_add = add is not None

    def body(*refs):
        a_ref, b_ref = refs[0], refs[1]
        o_ref = refs[2 + has_add]
        def finish(r):
            if has_add:
                r = r + refs[2][...]
            o_ref[...] = r.astype(o_ref.dtype)

        if nk == 1:
            finish(_dot(a_ref[...].astype(BF16), b_ref[...].astype(BF16), dims))
        else:
            acc_ref = refs[-1]
            k = pl.program_id(2)

            @pl.when(k == 0)
            def _():
                acc_ref[...] = jnp.zeros_like(acc_ref)

            acc_ref[...] += _dot(a_ref[...].astype(BF16), b_ref[...].astype(BF16), dims)

            @pl.when(k == nk - 1)
            def _():
                finish(acc_ref[...])

    in_specs = [a_spec, b_spec] + ([add_spec] if has_add else [])
    args = (a, b) + ((add,) if has_add else ())
    return _call(name, body, grid=grid, in_specs=in_specs, out_specs=[out_spec], out_shape=[out_sds], args=args,
                 sem=("parallel", "parallel", "arbitrary"), scratch=[pltpu.VMEM(acc_shape, F32)] if nk > 1 else [],
                 carry=carry)[0]


def _blk(per):
    return per if per <= 1024 else per // 2


def _pick_tiles(M, N, K, a_dtype, b_dtype, out_dtype, has_add, tn=None, tk=None):
    isz = lambda dt: jnp.dtype(dt).itemsize
    tms = sorted({_tile(M, t) for t in (2048, 1024, 512)}, reverse=True)
    tks = [tk] if tk else sorted({_tile(K, t) for t in (2048, 1024, 512)}, reverse=True)
    tns = [tn] if tn else sorted({_tile(N, t) for t in (1024, 512)}, reverse=True)
    for m in tms:
        for k in tks:
            for n in tns:
                blocks = m * k * isz(a_dtype) + k * n * isz(b_dtype) + m * n * (isz(out_dtype) + (4 if has_add else 0))
                if 2 * blocks + (m * n * 4 if K > k else 0) <= MATMUL_VMEM:
                    return m, n, k
    raise ValueError("no matmul tiling fits")


def _mm_nn(name, a, b, out_dtype, *, add=None, carry=None):
    M, K = a.shape
    if b.ndim == 3:
        per = b.shape[2]
        N = N_DEV * per
        tm, tn, tk = _pick_tiles(M, N, K, a.dtype, b.dtype, out_dtype, add is not None, tn=_blk(per))
        q = per // tn
        b_spec = pl.BlockSpec((None, tk, tn), lambda m, n, k: (n // q, k, n % q))
    else:
        N = b.shape[1]
        tm, tn, tk = _pick_tiles(M, N, K, a.dtype, b.dtype, out_dtype, add is not None)
        b_spec = pl.BlockSpec((tk, tn), lambda m, n, k: (k, n))
    return _matmul(
        name, a, b, dims=NN, grid=(M // tm, N // tn, K // tk),
        a_spec=pl.BlockSpec((tm, tk), lambda m, n, k: (m, k)), b_spec=b_spec,
        out_sds=jax.ShapeDtypeStruct((M, N), out_dtype), out_spec=pl.BlockSpec((tm, tn), lambda m, n, k: (m, n)),
        acc_shape=(tm, tn), add=add, add_spec=pl.BlockSpec((tm, tn), lambda m, n, k: (m, n)), carry=carry)


def _mm_nt(name, a, b, out_dtype, *, add=None, carry=None):
    M, K = a.shape
    if b.ndim == 3:
        N = b.shape[1]
        tm, tn, tk = _pick_tiles(M, N, K, a.dtype, b.dtype, out_dtype, add is not None, tk=b.shape[2])
        b_spec = pl.BlockSpec((None, tn, tk), lambda m, n, k: (k, n, 0))
    else:
        N = b.shape[0]
        tm, tn, tk = _pick_tiles(M, N, K, a.dtype, b.dtype, out_dtype, add is not None)
        b_spec = pl.BlockSpec((tn, tk), lambda m, n, k: (n, k))
    return _matmul(
        name, a, b, dims=NT, grid=(M // tm, N // tn, K // tk),
        a_spec=pl.BlockSpec((tm, tk), lambda m, n, k: (m, k)), b_spec=b_spec,
        out_sds=jax.ShapeDtypeStruct((M, N), out_dtype), out_spec=pl.BlockSpec((tm, tn), lambda m, n, k: (m, n)),
        acc_shape=(tm, tn), add=add, add_spec=pl.BlockSpec((tm, tn), lambda m, n, k: (m, n)), carry=carry)


def _mm_tn(name, a, b, out_dtype, *, per=None, rows=None, carry=None):
    K, M = a.shape
    N = b.shape[1]
    row0 = 0
    if rows is not None:
        row0, M = rows
    tm, tn, tk = _pick_tiles(M, N, K, a.dtype, b.dtype, out_dtype, False, tn=_blk(per) if per else None)
    m0 = row0 // tm
    if per is not None:
        q = per // tn
        out_sds = jax.ShapeDtypeStruct((N_DEV, M, per), out_dtype)
        out_spec = pl.BlockSpec((None, tm, tn), lambda m, n, k: (n // q, m, n % q))
    else:
        out_sds = jax.ShapeDtypeStruct((M, N), out_dtype)
        out_spec = pl.BlockSpec((tm, tn), lambda m, n, k: (m, n))
    return _matmul(
        name, a, b, dims=TN, grid=(M // tm, N // tn, K // tk),
        a_spec=pl.BlockSpec((tk, tm), lambda m, n, k: (k, m + m0)), b_spec=pl.BlockSpec((tk, tn), lambda m, n, k: (k, n)),
        out_sds=out_sds, out_spec=out_spec, acc_shape=(tm, tn), carry=carry)


def _rms_fwd(name, h, g):
    T, D = h.shape
    tb = _tile(T, 512)

    def body(h_ref, g_ref, o_ref):
        x = h_ref[...]
        r = lax.rsqrt(jnp.mean(x * x, axis=-1, keepdims=True) + EPS)
        o_ref[...] = (x * r * g_ref[...]).astype(o_ref.dtype)

    return pl.pallas_call(
        body, name=name, grid=(T // tb,),
        in_specs=[pl.BlockSpec((tb, D), lambda i: (i, 0)), pl.BlockSpec((1, D), lambda i: (0, 0))],
        out_specs=pl.BlockSpec((tb, D), lambda i: (i, 0)), out_shape=jax.ShapeDtypeStruct((T, D), BF16),
        compiler_params=_cp("parallel"))(h, g)


def _rms_bwd(name, dhn, h, g, dres):
    T, D = h.shape
    tb = _tile(T, 256)

    def body(dhn_ref, h_ref, g_ref, dres_ref, dh_ref, dg_ref):
        i = pl.program_id(0)
        x = h_ref[...]
        r = lax.rsqrt(jnp.mean(x * x, axis=-1, keepdims=True) + EPS)
        y = x * r
        d = dhn_ref[...]
        dy = d * g_ref[...]
        dh_ref[...] = dres_ref[...] + r * (dy - y * jnp.mean(dy * y, axis=-1, keepdims=True))
        part = jnp.sum(d * y, axis=0, keepdims=True)

        @pl.when(i == 0)
        def _():
            dg_ref[...] = part

        @pl.when(i > 0)
        def _():
            dg_ref[...] += part

    row = pl.BlockSpec((tb, D), lambda i: (i, 0))
    vec = pl.BlockSpec((1, D), lambda i: (0, 0))
    return pl.pallas_call(
        body, name=name, grid=(T // tb,), in_specs=[row, row, vec, row], out_specs=[row, vec],
        out_shape=[jax.ShapeDtypeStruct((T, D), F32), jax.ShapeDtypeStruct((1, D), F32)],
        compiler_params=_cp("arbitrary"))(dhn, h, g, dres)


def _loss_head(h, tgt, g):
    T, D = h.shape
    tb = _tile(T, 256)

    def body(h_ref, t_ref, g_ref, loss_ref, dh_ref, dg_ref):
        i = pl.program_id(0)
        x = h_ref[...]
        gg = g_ref[...]
        r = lax.rsqrt(jnp.mean(x * x, axis=-1, keepdims=True) + EPS)
        y0 = x * r
        err = y0 * gg - t_ref[...]
        tot = 0.5 * jnp.sum(jnp.mean(err * err, axis=-1, keepdims=True), axis=0, keepdims=True)
        dy = err * (1.0 / D)
        dyg = dy * gg
        dh_ref[...] = r * (dyg - y0 * jnp.mean(dyg * y0, axis=-1, keepdims=True))
        part = jnp.sum(dy * y0, axis=0, keepdims=True)
        tot = jnp.broadcast_to(tot, loss_ref.shape)

        @pl.when(i == 0)
        def _():
            dg_ref[...] = part
            loss_ref[...] = tot

        @pl.when(i > 0)
        def _():
            dg_ref[...] += part
            loss_ref[...] += tot

    row = pl.BlockSpec((tb, D), lambda i: (i, 0))
    vec = pl.BlockSpec((1, D), lambda i: (0, 0))
    return pl.pallas_call(
        body, name="loss_head", grid=(T // tb,), in_specs=[row, row, vec],
        out_specs=[pl.BlockSpec((SUBLANE, LANE), lambda i: (0, 0)), row, vec],
        out_shape=[jax.ShapeDtypeStruct((SUBLANE, LANE), F32), jax.ShapeDtypeStruct((T, D), F32),
                   jax.ShapeDtypeStruct((1, D), F32)],
        compiler_params=_cp("arbitrary"))(h, tgt, g)


def _rel_onehot(pos_axis, shape):
    pos = lax.broadcasted_iota(I32, shape, pos_axis)
    r = lax.broadcasted_iota(I32, shape, 1 - pos_axis)
    d = jnp.where(pos < KW, N_PAST * CHUNK - pos, N_PAST * CHUNK + ROLL_W - pos)
    return (jnp.clip(d, -MAX_REL, MAX_REL) + MAX_REL == r).astype(F32)


def _roll_rows(x, left):
    row = lax.broadcasted_iota(I32, x.shape, 0)
    for b in range(QB.bit_length() - 1):
        shift = (ROLL_W - (1 << b)) if left else (1 << b)
        x = jnp.where(((row >> b) & 1) == 1, pltpu.roll(x, shift, 1), x)
    return x


def _bias_tile(rel_bias):
    H = rel_bias.shape[0]
    rb = jnp.pad(rel_bias, ((0, 0), (0, REL_PAD - rel_bias.shape[1]))).reshape(H, 1, REL_PAD)

    def body(rb_ref, o_ref):
        row = jnp.broadcast_to(rb_ref[...], (SUBLANE, REL_PAD))
        base = jnp.dot(row, _rel_onehot(1, (REL_PAD, ROLL_W)), precision=lax.Precision.HIGHEST,
                       preferred_element_type=F32)[0:1]
        tile = _roll_rows(jnp.broadcast_to(base, (QB, ROLL_W)), left=False)[:, :KW]
        qc = lax.broadcasted_iota(I32, (QB, KW), 0) // CHUNK
        kc = lax.broadcasted_iota(I32, (QB, KW), 1) // CHUNK - N_PAST
        o_ref[...] = jnp.where((kc >= qc - N_PAST) & (kc <= qc), tile, NEG)

    return pl.pallas_call(
        body, name="bias_tile", grid=(H,),
        in_specs=[pl.BlockSpec((None, 1, REL_PAD), lambda h: (h, 0, 0))],
        out_specs=pl.BlockSpec((None, QB, KW), lambda h: (h, 0, 0)),
        out_shape=jax.ShapeDtypeStruct((H, QB, KW), F32), compiler_params=_cp("parallel"))(rb)


def _bias_tile_grad(dtile, n_rel):
    H = dtile.shape[0]

    def body(dt_ref, o_ref):
        x = jnp.concatenate([dt_ref[...], jnp.zeros((QB, ROLL_W - KW), F32)], axis=1)
        cs = jnp.sum(_roll_rows(x, left=True), axis=0, keepdims=True)
        o_ref[...] = jnp.dot(jnp.broadcast_to(cs, (SUBLANE, ROLL_W)), _rel_onehot(0, (ROLL_W, REL_PAD)),
                             precision=lax.Precision.HIGHEST, preferred_element_type=F32)[0:1]

    out = pl.pallas_call(
        body, name="bias_tile_grad", grid=(H,),
        in_specs=[pl.BlockSpec((None, QB, KW), lambda h: (h, 0, 0))],
        out_specs=pl.BlockSpec((None, 1, REL_PAD), lambda h: (h, 0, 0)),
        out_shape=jax.ShapeDtypeStruct((H, 1, REL_PAD), F32), compiler_params=_cp("parallel"))(dtile)
    return out.reshape(H, REL_PAD)[:, :n_rel]


def _band_specs(G):
    spec = lambda f: pl.BlockSpec((QB, BAND_W), f)
    q = spec(lambda h, i: (i, h))
    ks = [spec(lambda h, i, r=r: (jnp.maximum(i - 2 + r, 0), G + h)) for r in range(3)]
    vs = [spec(lambda h, i, r=r: (jnp.maximum(i - 2 + r, 0), 2 * G + h)) for r in range(3)]
    gate = spec(lambda h, i: (i, 5 * G + h))
    bias = pl.BlockSpec((HB, QB, KW), lambda h, i: (h, 0, 0))
    return [q] + ks + vs + [gate, bias]


def _band_probs(i, hh, q_ref, k_refs, v_refs, bias_ref):
    cols = slice(hh * HEAD_A, (hh + 1) * HEAD_A)
    q = q_ref[:, cols].astype(BF16)
    k = jnp.concatenate([r[:, cols] for r in k_refs], axis=0).astype(BF16)
    v = jnp.concatenate([r[:, cols] for r in v_refs], axis=0).astype(BF16)
    s = _dot(q, k, NT) * (HEAD_A ** -0.5) + bias_ref[hh]
    kpos = (i - 2) * QB + lax.broadcasted_iota(I32, (1, KW), 1)
    s = jnp.where(kpos >= 0, s, NEG)
    e = jnp.exp(s - jnp.max(s, axis=-1, keepdims=True))
    p = e * (1.0 / jnp.sum(e, axis=-1, keepdims=True))
    return p, q, k, v


def _band_attn_fwd(proj, bias_tile, carry=None):
    T = proj.shape[0]
    H = bias_tile.shape[0]

    def body(q_ref, k0, k1, k2, v0, v1, v2, gate_ref, bias_ref, ya_ref, y_ref):
        for hh in range(HB):
            cols = slice(hh * HEAD_A, (hh + 1) * HEAD_A)
            p, _, _, v = _band_probs(pl.program_id(1), hh, q_ref, (k0, k1, k2), (v0, v1, v2), bias_ref)
            o = _dot(p.astype(BF16), v, NN)
            g = gate_ref[:, cols]
            ya_ref[:, cols] = o
            y_ref[:, cols] = (o * (g * _sigmoid(g))).astype(y_ref.dtype)

    out = pl.BlockSpec((QB, BAND_W), lambda h, i: (i, h))
    return _call(
        "band_attn_fwd", body, grid=(H // HB, T // QB), in_specs=_band_specs(H // HB), out_specs=[out, out],
        out_shape=[jax.ShapeDtypeStruct((T, H * HEAD_A), F32), jax.ShapeDtypeStruct((T, 2 * H * HEAD_A), BF16)],
        args=[proj] * 8 + [bias_tile], sem=("parallel", "parallel"), carry=carry)


def _band_attn_bwd(proj, bias_tile, dy, ya, carry=None):
    T = proj.shape[0]
    H = bias_tile.shape[0]
    n_i = T // QB

    def body(q_ref, k0, k1, k2, v0, v1, v2, gate_ref, bias_ref, dy_ref, ya_ref,
             dq_ref, dk_ref, dv_ref, dgate_ref, dbias_ref, dk_acc, dv_acc):
        i = pl.program_id(1)

        @pl.when(i == 0)
        def _():
            dk_acc[...] = jnp.zeros_like(dk_acc)
            dv_acc[...] = jnp.zeros_like(dv_acc)
            dbias_ref[...] = jnp.zeros_like(dbias_ref)

        for hh in range(HB):
            cols = slice(hh * HEAD_A, (hh + 1) * HEAD_A)
            p, q, k, v = _band_probs(i, hh, q_ref, (k0, k1, k2), (v0, v1, v2), bias_ref)
            g = gate_ref[:, cols]
            sg = _sigmoid(g)
            dyv = dy_ref[:, cols]
            dgate_ref[:, cols] = (dyv * ya_ref[:, cols] * _dsilu(g, sg)).astype(dgate_ref.dtype)
            do = (dyv * (g * sg)).astype(BF16)
            dp = _dot(do, v, NT)
            ds = p * (dp - jnp.sum(dp * p, axis=-1, keepdims=True))
            dbias_ref[hh] += ds
            dsb = (ds * (HEAD_A ** -0.5)).astype(BF16)
            dq_ref[:, cols] = _dot(dsb, k, NN).astype(dq_ref.dtype)
            dkc = _dot(dsb, q, TN)
            dvc = _dot(p.astype(BF16), do, TN)
            for r in range(3):
                blk = i - 2 + r

                @pl.when(blk >= 0)
                def _(r=r, blk=blk, cols=cols, dkc=dkc, dvc=dvc):
                    rows = pl.ds(pl.multiple_of(blk * QB, QB), QB)
                    dk_acc[rows, cols] += dkc[r * QB:(r + 1) * QB]
                    dv_acc[rows, cols] += dvc[r * QB:(r + 1) * QB]

        @pl.when(i == n_i - 1)
        def _():
            dk_ref[...] = dk_acc[...].astype(dk_ref.dtype)
            dv_ref[...] = dv_acc[...].astype(dv_ref.dtype)

    blk = pl.BlockSpec((QB, BAND_W), lambda h, i: (i, h))
    col = pl.BlockSpec((T, BAND_W), lambda h, i: (0, h))
    sds = jax.ShapeDtypeStruct((T, H * HEAD_A), BF16)
    return _call(
        "band_attn_bwd", body, grid=(H // HB, n_i), in_specs=_band_specs(H // HB) + [blk, blk],
        out_specs=[blk, col, col, blk, pl.BlockSpec((HB, QB, KW), lambda h, i: (h, 0, 0))],
        out_shape=[sds, sds, sds, sds, jax.ShapeDtypeStruct((H, QB, KW), F32)],
        args=[proj] * 8 + [bias_tile, dy, ya], sem=("parallel", "arbitrary"),
        scratch=[pltpu.VMEM((T, BAND_W), F32), pltpu.VMEM((T, BAND_W), F32)], carry=carry)


def _conv_in_specs(tb, C):
    per = tb // HALO
    nb = C // CONV_LC
    prev = lambda i: jnp.maximum(i * per - 1, 0)
    return [pl.BlockSpec((tb, CONV_LC), lambda c, i: (i, 3 * nb + c)), pl.BlockSpec((tb, CONV_LC), lambda c, i: (i, 4 * nb + c)),
            pl.BlockSpec((HALO, CONV_LC), lambda c, i: (prev(i), 3 * nb + c)),
            pl.BlockSpec((HALO, CONV_LC), lambda c, i: (prev(i), 4 * nb + c))]


def _conv_tb(T):
    return _tile(T, 1024) if T > 1024 else T // 2


def _glu_with_halo(i, ga_ref, gb_ref, ha_ref, hb_ref, scr):
    tb = ga_ref.shape[0]
    halo = ha_ref[...] * _sigmoid(hb_ref[...])
    scr[0:HALO, :] = jnp.where(i > 0, halo, 0.0)
    scr[HALO:HALO + tb, :] = ga_ref[...] * _sigmoid(gb_ref[...])
    scr[HALO + tb:, :] = jnp.zeros((SUBLANE, scr.shape[1]), F32)


def _chunk_taps(src, w_ref, row0, tap_of, uscr):
    acc = None
    for r in range(SUBLANE):
        u = None
        for a in range(HALO // SUBLANE + 1):
            j = tap_of(SUBLANE * a + r)
            if 0 <= j < CONV_K:
                rows = pl.ds(pl.multiple_of(row0 + SUBLANE * a, SUBLANE), CONV_RC + SUBLANE)
                t = w_ref[j:j + 1, :] * src[rows, :]
                u = t if u is None else u + t
        if u is None:
            continue
        if r == 0:
            piece = u[0:CONV_RC]
        else:
            uscr[...] = u
            piece = uscr[pl.ds(r, CONV_RC), :]
        acc = piece if acc is None else acc + piece
    return acc


def _layernorm_stats(z):
    mu = jnp.mean(z, axis=-1, keepdims=True)
    zc = z - mu
    rstd = lax.rsqrt(jnp.mean(zc * zc, axis=-1, keepdims=True) + EPS)
    return zc * rstd, rstd


def _conv_z(proj, conv_w, conv_b):
    T = proj.shape[0]
    C = conv_w.shape[1]
    tb = _conv_tb(T)

    def body(ga_ref, gb_ref, ha_ref, hb_ref, w_ref, cb_ref, z_ref, scr, uscr):
        _glu_with_halo(pl.program_id(1), ga_ref, gb_ref, ha_ref, hb_ref, scr)
        cb = cb_ref[...]

        def chunk(rc, carry):
            row0 = pl.multiple_of(rc * CONV_RC, CONV_RC)
            z = _chunk_taps(scr, w_ref, row0, lambda o: o - (HALO - (CONV_K - 1)), uscr)
            z_ref[pl.ds(row0, CONV_RC), :] = z + cb
            return carry

        lax.fori_loop(0, tb // CONV_RC, chunk, 0)

    return pl.pallas_call(
        body, name="conv_z", grid=(C // CONV_LC, T // tb),
        in_specs=_conv_in_specs(tb, C) + [pl.BlockSpec((HALO, CONV_LC), lambda c, i: (0, c)),
                                          pl.BlockSpec((1, CONV_LC), lambda c, i: (0, c))],
        out_specs=pl.BlockSpec((tb, CONV_LC), lambda c, i: (i, c)), out_shape=jax.ShapeDtypeStruct((T, C), F32),
        scratch_shapes=[pltpu.VMEM((HALO + tb + SUBLANE, CONV_LC), F32), pltpu.VMEM((CONV_RC + SUBLANE, CONV_LC), F32)],
        compiler_params=_cp("parallel", "parallel"))(proj, proj, proj, proj, conv_w, conv_b)


def _conv_ln_fwd(proj, z, y, ln_g, ln_b):
    T, C = z.shape
    tb = _tile(T, 256)

    def body(z_ref, gate_ref, g_ref, b_ref, y_in, y_ref):
        xhat, _ = _layernorm_stats(z_ref[...])
        ln = xhat * g_ref[...] + b_ref[...]
        gate = gate_ref[...]
        y_ref[...] = (ln * _sigmoid(ln) * (gate * _sigmoid(gate))).astype(y_ref.dtype)

    vec = pl.BlockSpec((1, C), lambda i: (0, 0))
    return pl.pallas_call(
        body, name="conv_ln_fwd", grid=(T // tb,),
        in_specs=[pl.BlockSpec((tb, C), lambda i: (i, 0)), pl.BlockSpec((tb, C), lambda i: (i, 6)), vec, vec,
                  pl.BlockSpec(memory_space=pl.ANY)],
        out_specs=pl.BlockSpec((tb, C), lambda i: (i, 1)), out_shape=jax.ShapeDtypeStruct(y.shape, y.dtype),
        input_output_aliases={4: 0}, compiler_params=_cp("parallel"))(z, proj, ln_g, ln_b, y)


def _conv_bwd_ln(proj, z, dy, ln_g, ln_b):
    T, C = z.shape
    tb = _tile(T, 256)

    def body(z_ref, gate_ref, dy_ref, g_ref, b_ref, dz_ref, dgate_ref, dg_ref, db_ref, dcb_ref):
        i = pl.program_id(0)
        xhat, rstd = _layernorm_stats(z_ref[...])
        ln = xhat * g_ref[...] + b_ref[...]
        sl = _sigmoid(ln)
        gate = gate_ref[...]
        sg = _sigmoid(gate)
        dyv = dy_ref[...]
        dgate_ref[...] = (dyv * (ln * sl) * _dsilu(gate, sg)).astype(dgate_ref.dtype)
        dln = dyv * (gate * sg) * _dsilu(ln, sl)
        dxh = dln * g_ref[...]
        dz = rstd * (dxh - jnp.mean(dxh, axis=-1, keepdims=True) - xhat * jnp.mean(dxh * xhat, axis=-1, keepdims=True))
        dz_ref[...] = dz
        parts = (jnp.sum(dln * xhat, axis=0, keepdims=True), jnp.sum(dln, axis=0, keepdims=True),
                 jnp.sum(dz, axis=0, keepdims=True))

        @pl.when(i == 0)
        def _():
            for ref, part in zip((dg_ref, db_ref, dcb_ref), parts):
                ref[...] = part

        @pl.when(i > 0)
        def _():
            for ref, part in zip((dg_ref, db_ref, dcb_ref), parts):
                ref[...] += part

    vec = pl.BlockSpec((1, C), lambda i: (0, 0))
    row = pl.BlockSpec((tb, C), lambda i: (i, 0))
    vsd = jax.ShapeDtypeStruct((1, C), F32)
    return pl.pallas_call(
        body, name="conv_bwd_ln", grid=(T // tb,),
        in_specs=[row, pl.BlockSpec((tb, C), lambda i: (i, 6)), pl.BlockSpec((tb, C), lambda i: (i, 1)), vec, vec],
        out_specs=[row, row, vec, vec, vec],
        out_shape=[jax.ShapeDtypeStruct((T, C), F32), jax.ShapeDtypeStruct((T, C), BF16), vsd, vsd, vsd],
        compiler_params=_cp("arbitrary"))(z, proj, dy, ln_g, ln_b)


def _conv_bwd_taps(proj, dz, conv_w):
    T = proj.shape[0]
    C = conv_w.shape[1]
    tb = _conv_tb(T)
    per = tb // HALO
    n_i = T // tb
    first = HALO - (CONV_K - 1)

    def body(ga_ref, gb_ref, ha_ref, hb_ref, dz_ref, dzn_ref, w_ref, da_ref, db_ref, dw_ref, scr, dscr, uscr, zscr, dwacc):
        i = pl.program_id(1)
        _glu_with_halo(i, ga_ref, gb_ref, ha_ref, hb_ref, scr)
        dscr[0:tb, :] = dz_ref[...]
        dscr[tb:tb + HALO, :] = jnp.where(i < n_i - 1, dzn_ref[...], 0.0)
        dscr[tb + HALO:, :] = jnp.zeros((SUBLANE, CONV_LC), F32)
        zscr[0:SUBLANE, :] = jnp.zeros((SUBLANE, CONV_LC), F32)
        zscr[SUBLANE + CONV_RC:, :] = jnp.zeros((SUBLANE, CONV_LC), F32)

        @pl.when(i == 0)
        def _():
            dwacc[...] = jnp.zeros_like(dwacc)

        def chunk(rc, carry):
            row0 = pl.multiple_of(rc * CONV_RC, CONV_RC)
            rows = pl.ds(row0, CONV_RC)
            dglu = _chunk_taps(dscr, w_ref, row0, lambda o: CONV_K - 1 - o, uscr)
            ga = ga_ref[rows, :]
            sb = _sigmoid(gb_ref[rows, :])
            da_ref[rows, :] = (dglu * sb).astype(da_ref.dtype)
            db_ref[rows, :] = (dglu * ga * sb * (1.0 - sb)).astype(db_ref.dtype)
            zscr[SUBLANE:SUBLANE + CONV_RC, :] = dz_ref[rows, :]
            for r in range(SUBLANE):
                dzs = zscr[pl.ds(SUBLANE - r, CONV_RC + SUBLANE), :]
                for a in range(HALO // SUBLANE + 1):
                    j = SUBLANE * a + r - first
                    if 0 <= j < CONV_K:
                        src = pl.ds(pl.multiple_of(row0 + SUBLANE * a, SUBLANE), CONV_RC + SUBLANE)
                        p = dzs * scr[src, :]
                        f = p[0:SUBLANE]
                        for s in range(1, CONV_RC // SUBLANE + 1):
                            f = f + p[s * SUBLANE:(s + 1) * SUBLANE]
                        dwacc[j * SUBLANE:(j + 1) * SUBLANE, :] += f
            return carry

        lax.fori_loop(0, tb // CONV_RC, chunk, 0)

        @pl.when(i == n_i - 1)
        def _():
            dw_ref[...] = jnp.zeros_like(dw_ref)
            for j in range(CONV_K):
                dw_ref[j:j + 1, :] = jnp.sum(dwacc[j * SUBLANE:(j + 1) * SUBLANE, :], axis=0, keepdims=True)

    blk = pl.BlockSpec((tb, CONV_LC), lambda c, i: (i, c))
    wspec = pl.BlockSpec((HALO, CONV_LC), lambda c, i: (0, c))
    nxt = pl.BlockSpec((HALO, CONV_LC), lambda c, i: (jnp.minimum((i + 1) * per, T // HALO - 1), c))
    return pl.pallas_call(
        body, name="conv_bwd_taps", grid=(C // CONV_LC, n_i),
        in_specs=_conv_in_specs(tb, C) + [blk, nxt, wspec], out_specs=[blk, blk, wspec],
        out_shape=[jax.ShapeDtypeStruct((T, C), BF16), jax.ShapeDtypeStruct((T, C), BF16),
                   jax.ShapeDtypeStruct((HALO, C), F32)],
        scratch_shapes=[pltpu.VMEM((HALO + tb + SUBLANE, CONV_LC), F32), pltpu.VMEM((tb + HALO + SUBLANE, CONV_LC), F32),
                        pltpu.VMEM((CONV_RC + SUBLANE, CONV_LC), F32), pltpu.VMEM((CONV_RC + 2 * SUBLANE, CONV_LC), F32),
                        pltpu.VMEM((HALO * SUBLANE, CONV_LC), F32)],
        compiler_params=_cp("parallel", "arbitrary"))(proj, proj, proj, proj, dz, dz, conv_w)


def _sgu_mask():
    r = lax.broadcasted_iota(I32, (GMLP_CHUNK, GMLP_CHUNK), 0) // CHUNK
    c = lax.broadcasted_iota(I32, (GMLP_CHUNK, GMLP_CHUNK), 1) // CHUNK
    return r >= c


def _sgu_fwd(proj, ln_g, ln_b, w_s, b_s_t):
    T = proj.shape[0]
    W = ln_g.shape[1]
    G = w_s.shape[0]
    cg = W // G
    tb = GMLP_CHUNK

    def body(u_ref, v_ref, gate_ref, g_ref, b_ref, ws_ref, bs_ref, y_ref):
        xhat, _ = _layernorm_stats(v_ref[...])
        vln = (xhat * g_ref[...] + b_ref[...]).astype(BF16)
        mask = _sgu_mask()
        for gi in range(G):
            cols = slice(gi * cg, (gi + 1) * cg)
            ws = jnp.where(mask, ws_ref[gi], 0.0).astype(BF16)
            sg = _dot(ws, vln[:, cols], NN) + bs_ref[:, gi:gi + 1]
            gate = gate_ref[:, cols]
            y_ref[:, cols] = (u_ref[:, cols] * sg * (gate * _sigmoid(gate))).astype(y_ref.dtype)

    vec = pl.BlockSpec((1, W), lambda i: (0, 0))
    return pl.pallas_call(
        body, name="sgu_fwd", grid=(T // tb,),
        in_specs=[pl.BlockSpec((tb, W), lambda i: (i, 0)), pl.BlockSpec((tb, W), lambda i: (i, 1)),
                  pl.BlockSpec((tb, W), lambda i: (i, 2)), vec, vec,
                  pl.BlockSpec((G, GMLP_CHUNK, GMLP_CHUNK), lambda i: (0, 0, 0)),
                  pl.BlockSpec((GMLP_CHUNK, G), lambda i: (0, 0))],
        out_specs=pl.BlockSpec((tb, W), lambda i: (i, 0)), out_shape=jax.ShapeDtypeStruct((T, W), BF16),
        compiler_params=_cp("parallel"))(proj, proj, proj, ln_g, ln_b, w_s, b_s_t)


def _sgu_bwd(proj, dy, ln_g, ln_b, w_s, b_s_t):
    T = proj.shape[0]
    W = ln_g.shape[1]
    G = w_s.shape[0]
    cg = W // G
    tb = GMLP_CHUNK

    def body(u_ref, v_ref, gate_ref, dy_ref, g_ref, b_ref, ws_ref, bs_ref,
             dp_ref, dws_ref, dbs_ref, dg_ref, db_ref, dvln_scr):
        i = pl.program_id(0)

        @pl.when(i == 0)
        def _():
            dws_ref[...] = jnp.zeros_like(dws_ref)
            dbs_ref[...] = jnp.zeros_like(dbs_ref)
            dg_ref[...] = jnp.zeros_like(dg_ref)
            db_ref[...] = jnp.zeros_like(db_ref)

        xhat, rstd = _layernorm_stats(v_ref[...])
        vln = (xhat * g_ref[...] + b_ref[...]).astype(BF16)
        mask = _sgu_mask()
        for gi in range(G):
            cols = slice(gi * cg, (gi + 1) * cg)
            ws = jnp.where(mask, ws_ref[gi], 0.0).astype(BF16)
            vg = vln[:, cols]
            sg = _dot(ws, vg, NN) + bs_ref[:, gi:gi + 1]
            gate = gate_ref[:, cols]
            s = _sigmoid(gate)
            u = u_ref[:, cols]
            dyv = dy_ref[:, cols]
            dyu = dyv * u
            dp_ref[:, cols] = (dyv * sg * (gate * s)).astype(dp_ref.dtype)
            dp_ref[:, 2 * W + gi * cg:2 * W + (gi + 1) * cg] = (dyu * sg * _dsilu(gate, s)).astype(dp_ref.dtype)
            dsg = dyu * (gate * s)
            dsgb = dsg.astype(BF16)
            dvln_scr[:, cols] = _dot(ws, dsgb, TN)
            dws_ref[gi] += _dot(dsgb, vg, NT)
            dbs_ref[:, gi:gi + 1] += jnp.sum(dsg, axis=-1, keepdims=True)
        dvln = dvln_scr[...]
        dg_ref[...] += jnp.sum(dvln * xhat, axis=0, keepdims=True)
        db_ref[...] += jnp.sum(dvln, axis=0, keepdims=True)
        dxh = dvln * g_ref[...]
        dv = rstd * (dxh - jnp.mean(dxh, axis=-1, keepdims=True) - xhat * jnp.mean(dxh * xhat, axis=-1, keepdims=True))
        dp_ref[:, W:2 * W] = dv.astype(dp_ref.dtype)

    vec = pl.BlockSpec((1, W), lambda i: (0, 0))
    wsp = pl.BlockSpec((G, GMLP_CHUNK, GMLP_CHUNK), lambda i: (0, 0, 0))
    bsp = pl.BlockSpec((GMLP_CHUNK, G), lambda i: (0, 0))
    return pl.pallas_call(
        body, name="sgu_bwd", grid=(T // tb,),
        in_specs=[pl.BlockSpec((tb, W), lambda i: (i, 0)), pl.BlockSpec((tb, W), lambda i: (i, 1)),
                  pl.BlockSpec((tb, W), lambda i: (i, 2)), pl.BlockSpec((tb, W), lambda i: (i, 0)), vec, vec, wsp, bsp],
        out_specs=[pl.BlockSpec((tb, 3 * W), lambda i: (i, 0)), wsp, bsp, vec, vec],
        out_shape=[jax.ShapeDtypeStruct((T, 3 * W), BF16), jax.ShapeDtypeStruct((G, GMLP_CHUNK, GMLP_CHUNK), F32),
                   jax.ShapeDtypeStruct((GMLP_CHUNK, G), F32), jax.ShapeDtypeStruct((1, W), F32),
                   jax.ShapeDtypeStruct((1, W), F32)],
        scratch_shapes=[pltpu.VMEM((tb, W), F32)],
        compiler_params=_cp("arbitrary"))(proj, proj, proj, dy, ln_g, ln_b, w_s, b_s_t)


def _xattn_probs(q, k, hd):
    s = _dot(q, k, NT) * (hd ** -0.5)
    e = jnp.exp(s - jnp.max(s, axis=-1, keepdims=True))
    return e * (1.0 / jnp.sum(e, axis=-1, keepdims=True))


def _xattn_fwd(name, q, k, v):
    T, D = q.shape
    M = k.shape[0]
    hd = D // N_HEADS_X
    tb = _tile(T, 512)

    def body(q_ref, k_ref, v_ref, o_ref):
        for h in range(N_HEADS_X):
            cols = slice(h * hd, (h + 1) * hd)
            p = _xattn_probs(q_ref[:, cols], k_ref[:, cols], hd)
            o_ref[:, cols] = _dot(p.astype(BF16), v_ref[:, cols], NN).astype(o_ref.dtype)

    row = pl.BlockSpec((tb, D), lambda i: (i, 0))
    kv = pl.BlockSpec((M, D), lambda i: (0, 0))
    return pl.pallas_call(
        body, name=name, grid=(T // tb,), in_specs=[row, kv, kv], out_specs=row,
        out_shape=jax.ShapeDtypeStruct((T, D), BF16), compiler_params=_cp("parallel"))(q, k, v)


def _xattn_bwd(name, q, k, v, do):
    T, D = q.shape
    M = k.shape[0]
    hd = D // N_HEADS_X
    tb = _tile(T, 512)

    def body(q_ref, k_ref, v_ref, do_ref, dq_ref, dk_ref, dv_ref):
        @pl.when(pl.program_id(0) == 0)
        def _():
            dk_ref[...] = jnp.zeros_like(dk_ref)
            dv_ref[...] = jnp.zeros_like(dv_ref)

        for h in range(N_HEADS_X):
            cols = slice(h * hd, (h + 1) * hd)
            qh, kh, doh = q_ref[:, cols], k_ref[:, cols], do_ref[:, cols]
            p = _xattn_probs(qh, kh, hd)
            dp = _dot(doh, v_ref[:, cols], NT)
            ds = p * (dp - jnp.sum(dp * p, axis=-1, keepdims=True))
            dsb = (ds * (hd ** -0.5)).astype(BF16)
            dq_ref[:, cols] = _dot(dsb, kh, NN).astype(dq_ref.dtype)
            dk_ref[:, cols] += _dot(dsb, qh, TN)
            dv_ref[:, cols] += _dot(p.astype(BF16), doh, TN)

    row = pl.BlockSpec((tb, D), lambda i: (i, 0))
    kv = pl.BlockSpec((M, D), lambda i: (0, 0))
    return pl.pallas_call(
        body, name=name, grid=(T // tb,), in_specs=[row, kv, kv, row], out_specs=[row, kv, kv],
        out_shape=[jax.ShapeDtypeStruct((T, D), BF16), jax.ShapeDtypeStruct((M, D), F32),
                   jax.ShapeDtypeStruct((M, D), F32)],
        compiler_params=_cp("arbitrary"))(q, k, v, do)


def _adamw(name, contrib, w, m, v, carry=None):
    parts = list(contrib) if isinstance(contrib, (list, tuple)) else [contrib]
    n_parts = len(parts)
    R, C = w.shape
    tr = min(min(p.shape[1] for p in parts), 128)
    while any(p.shape[1] % tr for p in parts):
        tr -= SUBLANE
    tiles = [p.shape[1] // tr for p in parts]
    first = [sum(tiles[:p]) for p in range(n_parts + 1)]

    def body(*refs):
        c_refs = refs[:n_parts]
        w_ref, m_ref, v_ref, g_ref, d_ref, nm_ref, nv_ref = refs[n_parts:]

        def update(c_ref):
            g = c_ref[0].astype(F32)
            for s in range(1, N_DEV):
                g = g + c_ref[s].astype(F32)
            nm = ADAM_B1 * m_ref[...] + (1.0 - ADAM_B1) * g
            nv = ADAM_B2 * v_ref[...] + (1.0 - ADAM_B2) * (g * g)
            m_hat = nm / (1.0 - ADAM_B1 ** ADAM_STEP)
            v_hat = nv / (1.0 - ADAM_B2 ** ADAM_STEP)
            g_ref[...] = g
            d_ref[...] = -ADAM_LR * (m_hat / (jnp.sqrt(v_hat) + ADAM_EPS) + ADAM_WD * w_ref[...])
            nm_ref[...] = nm
            nv_ref[...] = nv

        if n_parts == 1:
            update(c_refs[0])
        else:
            i = pl.program_id(0)
            for p in range(n_parts):
                pl.when((i >= first[p]) & (i < first[p + 1]))(lambda p=p: update(c_refs[p]))

    row = pl.BlockSpec((tr, C), lambda i: (i, 0))
    sds = jax.ShapeDtypeStruct((R, C), F32)
    c_specs = [pl.BlockSpec((N_DEV, tr, C), lambda i, p=p: (0, jnp.clip(i - first[p], 0, tiles[p] - 1), 0))
               for p in range(n_parts)]
    return _call(name, body, grid=(R // tr,), in_specs=c_specs + [row, row, row], out_specs=[row] * 4,
                 out_shape=[sds] * 4, args=(*parts, w, m, v), sem=("parallel",), carry=carry)


def _pack(arrs):
    unit = SUBLANE * LANE
    flat = [jnp.pad(a.reshape(-1), (0, -a.size % unit)) for a in arrs]
    return jnp.concatenate(flat).reshape(-1, LANE)


def _unpack(buf, shapes):
    unit = SUBLANE * LANE
    flat = buf.reshape(-1)
    out, off = [], 0
    for s in shapes:
        size = 1
        for d in s:
            size *= d
        out.append(flat[off:off + size].reshape(s))
        off += size + (-size % unit)
    return out


def _cross_attention_fwd(l, h, mem, g_x, g_mem, wq, wk, wv, wo):
    hx = _rms_fwd(f"rms_x{l}", h, g_x)
    memn = _rms_fwd(f"rms_mem{l}", mem, g_mem)
    q = _mm_nn(f"xq{l}", hx, wq, BF16)
    k = _mm_nn(f"xk{l}", memn, wk, BF16)
    v = _mm_nn(f"xv{l}", memn, wv, BF16)
    o = _xattn_fwd(f"xattn_fwd{l}", q, k, v)
    h_out = _mm_nn(f"xo{l}", o, wo, F32, add=h)
    return h_out, (hx, memn, q, k, v, o)


def _cross_attention_bwd(l, dh, h, mem, g_x, g_mem, wq, wk, wv, wo, saved):
    hx, memn, q, k, v, o = saved
    do = _mm_nt(f"xo_dx{l}", dh, wo, BF16)
    dwo = _mm_tn(f"xo_dw{l}", o, dh, BF16)
    dq, dk, dv = _xattn_bwd(f"xattn_bwd{l}", q, k, v, do)
    dwq = _mm_tn(f"xq_dw{l}", hx, dq, BF16)
    dwk = _mm_tn(f"xk_dw{l}", memn, dk, BF16)
    dwv = _mm_tn(f"xv_dw{l}", memn, dv, BF16)
    dmemn = _mm_nt(f"xk_dx{l}", dk, wk, F32)
    dmemn = _mm_nt(f"xv_dx{l}", dv, wv, F32, add=dmemn)
    _, dg_mem = _rms_bwd(f"rms_mem_bwd{l}", dmemn, mem, g_mem, jnp.zeros_like(mem))
    dhx = _mm_nt(f"xq_dx{l}", dq, wq, F32)
    dh_in, dg_x = _rms_bwd(f"rms_x_bwd{l}", dhx, h, g_x, dh)
    return dh_in, (dg_x, dg_mem, dwq, dwk, dwv, dwo)


def kernel(x, mem, norm_mix_g, norm_x_g, norm_mem_g, final_norm_g, w_in_ab, rel_bias, conv_w, conv_b, conv_ln_g, conv_ln_b, w_out_ab, w_in_c, sgu_ln_g, sgu_ln_b, w_s, b_s, w_out_c, w_xq, w_xk, w_xv, w_xo, loss_target, m_norm_mix_g, m_norm_x_g, m_norm_mem_g, m_final_norm_g, m_w_in_ab, m_rel_bias, m_conv_w, m_conv_b, m_conv_ln_g, m_conv_ln_b, m_w_out_ab, m_w_in_c, m_sgu_ln_g, m_sgu_ln_b, m_w_s, m_b_s, m_w_out_c, m_w_xq, m_w_xk, m_w_xv, m_w_xo, v_norm_mix_g, v_norm_x_g, v_norm_mem_g, v_final_norm_g, v_w_in_ab, v_rel_bias, v_conv_w, v_conv_b, v_conv_ln_g, v_conv_ln_b, v_w_out_ab, v_w_in_c, v_sgu_ln_g, v_sgu_ln_b, v_w_s, v_b_s, v_w_out_c, v_w_xq, v_w_xk, v_w_xv, v_w_xo):
    names = ["norm_mix_g", "norm_x_g", "norm_mem_g", "final_norm_g", "w_in_ab", "rel_bias", "conv_w", "conv_b",
             "conv_ln_g", "conv_ln_b", "w_out_ab", "w_in_c", "sgu_ln_g", "sgu_ln_b", "w_s", "b_s", "w_out_c",
             "w_xq", "w_xk", "w_xv", "w_xo"]
    W = dict(zip(names, (norm_mix_g, norm_x_g, norm_mem_g, final_norm_g, w_in_ab, rel_bias, conv_w, conv_b, conv_ln_g,
                         conv_ln_b, w_out_ab, w_in_c, sgu_ln_g, sgu_ln_b, w_s, b_s, w_out_c, w_xq, w_xk, w_xv, w_xo)))
    M1 = dict(zip(names, (m_norm_mix_g, m_norm_x_g, m_norm_mem_g, m_final_norm_g, m_w_in_ab, m_rel_bias, m_conv_w, m_conv_b,
                          m_conv_ln_g, m_conv_ln_b, m_w_out_ab, m_w_in_c, m_sgu_ln_g, m_sgu_ln_b, m_w_s, m_b_s, m_w_out_c,
                          m_w_xq, m_w_xk, m_w_xv, m_w_xo)))
    M2 = dict(zip(names, (v_norm_mix_g, v_norm_x_g, v_norm_mem_g, v_final_norm_g, v_w_in_ab, v_rel_bias, v_conv_w, v_conv_b,
                          v_conv_ln_g, v_conv_ln_b, v_w_out_ab, v_w_in_c, v_sgu_ln_g, v_sgu_ln_b, v_w_s, v_b_s, v_w_out_c,
                          v_w_xq, v_w_xk, v_w_xv, v_w_xo)))

    h0, memv, tgt = x[0], mem[0], loss_target[0]
    T, D = h0.shape
    n_rel = rel_bias.shape[2]
    xnames = ["w_xq", "w_xk", "w_xv", "w_xo"]
    bf = lambda a: a.astype(BF16)
    blocks = lambda g: g.reshape(N_DEV, -1, D)

    small = _pack([conv_w[0], sgu_ln_g[0], sgu_ln_b[0]])
    win_ab, small_g = _comm_call("gather_in_ab", _Carry("gather", [bf(w_in_ab[0]), small]))
    per_cw, per_ln = conv_w.shape[2], sgu_ln_g.shape[1]
    cw_s, lg_s, lb_s = zip(*[_unpack(small_g[d], [(CONV_K, per_cw), (1, per_ln), (1, per_ln)]) for d in range(N_DEV)])
    conv_w_full = jnp.pad(jnp.concatenate(cw_s, axis=1), ((0, HALO - CONV_K), (0, 0)))
    sgu_g_full = jnp.concatenate(lg_s, axis=1)
    sgu_b_full = jnp.concatenate(lb_s, axis=1)
    b_s_t = b_s[0].T

    hn0 = _rms_fwd("rms_mix0", h0, norm_mix_g[0:1])
    ag1 = _Carry("gather", [bf(w_out_ab[0])] + [bf(W[n][0]) for n in xnames])
    proj_ab = _mm_nn("in_ab", hn0, win_ab, F32, carry=ag1)
    wout_ab = ag1.result[0].reshape(-1, D)
    wx0 = [g.reshape(D, D) for g in ag1.result[1:]]
    btile = _bias_tile(rel_bias[0])
    ag2 = _Carry("gather", [bf(w_in_c[0])])
    ya, y_ab = _band_attn_fwd(proj_ab, btile, carry=ag2)
    win_c = ag2.result[0]
    z_conv = _conv_z(proj_ab, conv_w_full, conv_b)
    y_ab = _conv_ln_fwd(proj_ab, z_conv, y_ab, conv_ln_g, conv_ln_b)
    h1 = _mm_nn("out_ab", y_ab, wout_ab, F32, add=h0)
    h2, xs0 = _cross_attention_fwd(0, h1, memv, norm_x_g[0:1], norm_mem_g[0:1], *wx0)
    hn1 = _rms_fwd("rms_mix1", h2, norm_mix_g[1:2])
    ag3 = _Carry("gather", [bf(w_out_c[0])] + [bf(W[n][1]) for n in xnames])
    proj_c = _mm_nn("in_c", hn1, win_c, F32, carry=ag3)
    wout_c = ag3.result[0].reshape(-1, D)
    wx1 = [g.reshape(D, D) for g in ag3.result[1:]]
    y_c = _sgu_fwd(proj_c, sgu_g_full, sgu_b_full, w_s[0], b_s_t)
    h3 = _mm_nn("out_c", y_c, wout_c, F32, add=h2)
    h4, xs1 = _cross_attention_fwd(1, h3, memv, norm_x_g[1:2], norm_mem_g[1:2], *wx1)
    loss_acc, dh4, dg_final = _loss_head(h4, tgt, final_norm_g.reshape(1, D))

    sent = {}

    def ride(**grads):
        carry = _Carry("exchange", list(grads.values()))
        sent.update({name: (carry, i) for i, name in enumerate(grads)})
        return carry

    dh3, (dg_x1, dg_mem1, dwq1, dwk1, dwv1, dwo1) = _cross_attention_bwd(
        1, dh4, h3, memv, norm_x_g[1:2], norm_mem_g[1:2], *wx1, xs1)
    dy_c = _mm_nt("out_c_dx", dh3, wout_c, F32, carry=ride(w_xo1=blocks(dwo1)))
    dwout_c = _mm_tn("out_c_dw", y_c, dh3, BF16, carry=ride(w_xq1=blocks(dwq1)))
    dproj_c, dws, dbs_t, dsgu_g, dsgu_b = _sgu_bwd(proj_c, dy_c, sgu_g_full, sgu_b_full, w_s[0], b_s_t)
    dwin_c = _mm_tn("in_c_dw", hn1, dproj_c, BF16, per=win_c.shape[2],
                    carry=ride(w_xk1=blocks(dwk1), w_out_c=blocks(dwout_c)))
    dhn1 = _mm_nt("in_c_dx", dproj_c, win_c, F32, carry=ride(w_xv1=blocks(dwv1)))
    dh2, dg_mix1 = _rms_bwd("rms_mix1_bwd", dhn1, h2, norm_mix_g[1:2], dh3)
    dh1, (dg_x0, dg_mem0, dwq0, dwk0, dwv0, dwo0) = _cross_attention_bwd(
        0, dh2, h1, memv, norm_x_g[0:1], norm_mem_g[0:1], *wx0, xs0)
    dy_ab = _mm_nt("out_ab_dx", dh1, wout_ab, F32, carry=ride(w_xo0=blocks(dwo0)))
    dwout_ab = _mm_tn("out_ab_dw", y_ab, dh1, BF16, carry=ride(w_xq0=blocks(dwq0)))
    dq, dk, dv, dgate_a, dbtile = _band_attn_bwd(proj_ab, btile, dy_ab, ya, carry=ride(w_in_c=dwin_c))
    drel = _bias_tile_grad(dbtile, n_rel)
    dz, dgate_b, dcln_g, dcln_b, dconv_b = _conv_bwd_ln(proj_ab, z_conv, dy_ab, conv_ln_g, conv_ln_b)
    dglu_a, dglu_b, dconv_w = _conv_bwd_taps(proj_ab, dz, conv_w_full)
    dproj_ab = jnp.concatenate([dq, dk, dv, dglu_a, dglu_b, dgate_a, dgate_b], axis=1)
    dwin_ab_lo = _mm_tn("in_ab_dw_lo", hn0, dproj_ab, BF16, per=win_ab.shape[2], rows=(0, D // 2),
                        carry=ride(w_xk0=blocks(dwk0), w_out_ab=blocks(dwout_ab)))
    dwin_ab_hi = _mm_tn("in_ab_dw_hi", hn0, dproj_ab, BF16, per=win_ab.shape[2], rows=(D // 2, D // 2),
                        carry=ride(w_in_ab_lo=dwin_ab_lo))
    dhn0 = _mm_nt("in_ab_dx", dproj_ab, win_ab, F32, carry=ride(w_in_ab_hi=dwin_ab_hi, w_xv0=blocks(dwv0)))
    dx, dg_mix0 = _rms_bwd("rms_mix0_bwd", dhn0, h0, norm_mix_g[0:1], dh1)
    received = lambda name: sent[name][0].result[sent[name][1]]

    sm = [_pack([dconv_w[:CONV_K, d * per_cw:(d + 1) * per_cw], dsgu_g[:, d * per_ln:(d + 1) * per_ln],
                 dsgu_b[:, d * per_ln:(d + 1) * per_ln]]) for d in range(N_DEV)]
    mask = (jnp.arange(GMLP_CHUNK)[:, None] // CHUNK >= jnp.arange(GMLP_CHUNK)[None, :] // CHUNK).astype(F32)
    rep_names = ["norm_mix_g", "norm_x_g", "norm_mem_g", "final_norm_g", "rel_bias", "conv_b", "conv_ln_g", "conv_ln_b",
                 "w_s", "b_s"]
    rep_grads = {
        "norm_mix_g": jnp.concatenate([dg_mix0, dg_mix1], axis=0),
        "norm_x_g": jnp.concatenate([dg_x0, dg_x1], axis=0),
        "norm_mem_g": jnp.concatenate([dg_mem0, dg_mem1], axis=0),
        "final_norm_g": dg_final.reshape(D),
        "rel_bias": drel[None], "conv_b": dconv_b, "conv_ln_g": dcln_g, "conv_ln_b": dcln_b,
        "w_s": (dws * mask[None])[None], "b_s": dbs_t.T[None],
    }
    ex_e = _Carry("exchange", [jnp.stack(sm), _pack([rep_grads[n] for n in rep_names])], bcast=[False, True])

    out = {}
    kinds = ("grad", "delta", "new_m", "new_v")
    for n in ("w_in_ab", "w_out_ab", "w_in_c", "w_out_c"):
        contrib = [received("w_in_ab_lo"), received("w_in_ab_hi")] if n == "w_in_ab" else received(n)
        res = _adamw(f"adamw_{n}", contrib, W[n][0], M1[n][0], M2[n][0], carry=ex_e if n == "w_in_ab" else None)
        for kind, r in zip(kinds, res):
            out[(kind, n)] = r[None]
    recv_small, recv_rep = ex_e.result
    for n in xnames:
        res = [_adamw(f"adamw_{n}{l}", received(f"{n}{l}"), W[n][l], M1[n][l], M2[n][l]) for l in range(2)]
        for kind, r in zip(kinds, zip(*res)):
            out[(kind, n)] = jnp.stack(r)
    sm_names = ["conv_w", "sgu_ln_g", "sgu_ln_b"]
    res = _adamw("adamw_small", recv_small, *[_pack([D_[n][0] for n in sm_names]) for D_ in (W, M1, M2)])
    for kind, r in zip(kinds, res):
        for n, piece in zip(sm_names, _unpack(r, [W[n].shape for n in sm_names])):
            out[(kind, n)] = piece
    res = _adamw("adamw_replicated", recv_rep, *[_pack([D_[n] for n in rep_names]) for D_ in (W, M1, M2)])
    for kind, r in zip(kinds, res):
        for n, piece in zip(rep_names, _unpack(r, [W[n].shape for n in rep_names])):
            out[(kind, n)] = piece

    loss = lax.psum(loss_acc[0, 0], MESH_AXES)
    return (loss, dx[None]) + tuple(out[(kind, n)] for kind in kinds for n in names)
```

```python
import jax
import jax.numpy as jnp
from jax import lax
from jax.experimental import pallas as pl
from jax.experimental.pallas import tpu as pltpu

F32 = jnp.float32
BF16 = jnp.bfloat16
I32 = jnp.int32

N_DEV = 8
CHUNK = 64
N_PAST = 8
MAX_REL = 128
HEAD_A = 128
CONV_K = 31
GMLP_CHUNK = 128
N_HEADS_X = 4
EPS = 1e-6
NEG = -1e30

ADAM_LR, ADAM_B1, ADAM_B2, ADAM_EPS, ADAM_WD, ADAM_STEP = 0.001, 0.9, 0.999, 1e-08, 0.01, 10

LANE = 128
SUBLANE = 8
VMEM_LIMIT = 56 * 1024 * 1024
MATMUL_VMEM = 44 * 1024 * 1024
QB = 4 * CHUNK
KW = QB + N_PAST * CHUNK
ROLL_W = 1024
REL_PAD = 384
HB = 2
BAND_W = HB * HEAD_A
HALO = 32
CONV_LC = LANE
CONV_RC = 64
MESH_AXES = ("x", "y", "c")
GATHER_PIECES = 4

NN = (((1,), (0,)), ((), ()))
NT = (((1,), (1,)), ((), ()))
TN = (((0,), (0,)), ((), ()))


def _cp(*sem):
    return pltpu.CompilerParams(dimension_semantics=sem, vmem_limit_bytes=VMEM_LIMIT)


def _tile(dim, pref):
    if dim <= pref:
        return dim
    t = (pref // LANE) * LANE
    while dim % t:
        t -= LANE
    return t


def _sigmoid(x):
    return 1.0 / (1.0 + jnp.exp(-x))


def _dsilu(x, s):
    return s * (1.0 + x * (1.0 - s))


def _dot(a, b, dims):
    return lax.dot_general(a, b, dims, preferred_element_type=F32)


def _mesh_pos():
    return lax.axis_index("x"), lax.axis_index("y"), lax.axis_index("c")


def _lin(x, y, c):
    return 4 * x + 2 * y + c


def _remote(src, dst, send_sem, recv_sem, to):
    return pltpu.make_async_remote_copy(src_ref=src, dst_ref=dst, send_sem=send_sem, recv_sem=recv_sem,
                                        device_id=to, device_id_type=pl.DeviceIdType.MESH)


class _Carry:
    def __init__(self, kind, arrs, bcast=None):
        n = len(arrs)
        self.kind, self.arrs, self.n = kind, list(arrs), n
        self.bcast = [kind == "gather"] * n if bcast is None else list(bcast)
        self.out_shape = [jax.ShapeDtypeStruct(((N_DEV,) + a.shape) if b else a.shape, a.dtype)
                          for a, b in zip(arrs, self.bcast)]
        self.units = [(a, None, None) for a in range(n)]
        if kind == "gather":
            self.units = []
            for a, arr in enumerate(arrs):
                pieces = GATHER_PIECES if arr.shape[0] >= GATHER_PIECES * 256 else 1
                rows = arr.shape[0] // pieces
                self.units += [(a, p * rows, rows) if pieces > 1 else (a, None, None) for p in range(pieces)]
        nu = len(self.units)
        self.scratch = [pltpu.SemaphoreType.DMA((nu, 7)), pltpu.SemaphoreType.DMA((nu, 7)), pltpu.SemaphoreType.DMA((n,))]
        self.result = None

    def _src(self, ins, a, d):
        return ins[a] if self.bcast[a] else ins[a].at[d]

    def _local(self, ins, outs, sems):
        me = _lin(*_mesh_pos())
        return [pltpu.make_async_copy(self._src(ins, a, me), outs[a].at[me], sems[2].at[a]) for a in range(self.n)]

    @staticmethod
    def _chips():
        x, y, _ = _mesh_pos()
        return [(1 - x, y), (x, 1 - y), (1 - x, 1 - y)]

    def _g_copy(self, ins, outs, sems, u, k, block, to, own=False):
        a, row0, rows = self.units[u]
        piece = (lambda r: r) if row0 is None else (lambda r: r.at[pl.ds(row0, rows)])
        dst = piece(outs[a].at[_lin(*block)])
        return _remote(piece(ins[a]) if own else dst, dst, sems[0].at[u, k], sems[1].at[u, k], to)

    def _g_first(self, ins, outs, sems):
        x, y, c = _mesh_pos()
        cps = []
        for u in range(len(self.units)):
            cps.append(self._g_copy(ins, outs, sems, u, 0, (x, y, c), (x, y, 1 - c), own=True))
            cps += [self._g_copy(ins, outs, sems, u, 1 + j, (x, y, c), (*chip, c), own=True)
                    for j, chip in enumerate(self._chips())]
        return cps

    def _g_passed(self, ins, outs, sems):
        x, y, c = _mesh_pos()
        return [self._g_copy(ins, outs, sems, u, 4 + j, (*chip, c), (x, y, 1 - c))
                for u in range(len(self.units)) for j, chip in enumerate(self._chips())]

    @staticmethod
    def _peer(k):
        x, y, c = _mesh_pos()
        return (1 - x if k & 4 else x, 1 - y if k & 2 else y, 1 - c if k & 1 else c)

    def _x_sends(self, ins, outs, sems):
        me = _lin(*_mesh_pos())
        return [_remote(self._src(ins, a, _lin(*self._peer(k))), outs[a].at[me], sems[0].at[a, k - 1],
                        sems[1].at[a, k - 1], self._peer(k)) for k in range(1, N_DEV) for a in range(self.n)]

    def start(self, ins, outs, sems):
        for cp in self._local(ins, outs, sems):
            cp.start()
        for cp in (self._g_first if self.kind == "gather" else self._x_sends)(ins, outs, sems):
            cp.start()

    def mid(self, ins, outs, sems):
        if self.kind != "gather":
            return
        x, y, c = _mesh_pos()
        passed = self._g_passed(ins, outs, sems)
        for u in range(len(self.units)):
            for j, chip in enumerate(self._chips()):
                self._g_copy(ins, outs, sems, u, 1 + j, (*chip, c), (x, y, c)).wait_recv()
                passed[3 * u + j].start()

    def finish(self, ins, outs, sems):
        x, y, c = _mesh_pos()
        if self.kind == "gather":
            for u in range(len(self.units)):
                self._g_copy(ins, outs, sems, u, 0, (x, y, 1 - c), (x, y, c)).wait_recv()
                for j, chip in enumerate(self._chips()):
                    self._g_copy(ins, outs, sems, u, 4 + j, (*chip, 1 - c), (x, y, c)).wait_recv()
            sent = self._g_first(ins, outs, sems) + self._g_passed(ins, outs, sems)
        else:
            for k in range(1, N_DEV):
                for a in range(self.n):
                    got = outs[a].at[_lin(*self._peer(k))]
                    _remote(got, got, sems[0].at[a, k - 1], sems[1].at[a, k - 1], self._peer(k)).wait_recv()
            sent = self._x_sends(ins, outs, sems)
        for cp in sent:
            cp.wait_send()
        for cp in self._local(ins, outs, sems):
            cp.wait()


def _comm_call(name, carry):
    n = carry.n

    def body(*refs):
        ins, outs, sems = refs[:n], refs[n:2 * n], refs[2 * n:]
        carry.start(ins, outs, sems)
        carry.mid(ins, outs, sems)
        carry.finish(ins, outs, sems)

    hbm = pl.BlockSpec(memory_space=pl.ANY)
    return pl.pallas_call(body, name=name, in_specs=[hbm] * n, out_specs=[hbm] * n, out_shape=carry.out_shape,
                          scratch_shapes=carry.scratch)(*carry.arrs)


def _call(name, body, *, grid, in_specs, out_specs, out_shape, args, sem, scratch=(), aliases=None, carry=None):
    aliases = aliases or {}
    if carry is None:
        return pl.pallas_call(body, name=name, grid=grid, in_specs=in_specs, out_specs=out_specs, out_shape=out_shape,
                              scratch_shapes=list(scratch), input_output_aliases=aliases, compiler_params=_cp(*sem))(*args)
    ni, no, ns, nc = len(in_specs), len(out_specs), len(scratch), carry.n
    total = 1
    for g in grid:
        total *= g

    def full(*refs):
        ins, cins = refs[:ni], refs[ni:ni + nc]
        outs, couts = refs[ni + nc:ni + nc + no], refs[ni + nc + no:ni + 2 * nc + no]
        scr, sems = refs[ni + 2 * nc + no:ni + 2 * nc + no + ns], refs[ni + 2 * nc + no + ns:]
        step = pl.program_id(0)
        for d in range(1, len(grid)):
            step = step * grid[d] + pl.program_id(d)

        @pl.when(step == 0)
        def _():
            carry.start(cins, couts, sems)

        body(*ins, *outs, *scr)

        @pl.when(step == total // 2)
        def _():
            carry.mid(cins, couts, sems)

        @pl.when(step == total - 1)
        def _():
            carry.finish(cins, couts, sems)

    hbm = pl.BlockSpec(memory_space=pl.ANY)
    res = pl.pallas_call(
        full, name=name, grid=grid, in_specs=list(in_specs) + [hbm] * nc, out_specs=list(out_specs) + [hbm] * nc,
        out_shape=list(out_shape) + carry.out_shape, scratch_shapes=list(scratch) + carry.scratch,
        input_output_aliases=aliases, compiler_params=_cp(*["arbitrary"] * len(grid)))(*args, *carry.arrs)
    carry.result = list(res[no:])
    return list(res[:no])


def _matmul(name, a, b, *, dims, grid, a_spec, b_spec, out_sds, out_spec, acc_shape, add=None, add_spec=None, carry=None):
    nk = grid[2]
    has_add = add is not None

    def body(*refs):
        a_ref, b_ref = refs[0], refs[1]
        o_ref = refs[2 + has_add]
        def finish(r):
            if has_add:
                r = r + refs[2][...]
            o_ref[...] = r.astype(o_ref.dtype)

        if nk == 1:
            finish(_dot(a_ref[...].astype(BF16), b_ref[...].astype(BF16), dims))
        else:
            acc_ref = refs[-1]
            k = pl.program_id(2)

            @pl.when(k == 0)
            def _():
                acc_ref[...] = jnp.zeros_like(acc_ref)

            acc_ref[...] += _dot(a_ref[...].astype(BF16), b_ref[...].astype(BF16), dims)

            @pl.when(k == nk - 1)
            def _():
                finish(acc_ref[...])

    in_specs = [a_spec, b_spec] + ([add_spec] if has_add else [])
    args = (a, b) + ((add,) if has_add else ())
    return _call(name, body, grid=grid, in_specs=in_specs, out_specs=[out_spec], out_shape=[out_sds], args=args,
                 sem=("parallel", "parallel", "arbitrary"), scratch=[pltpu.VMEM(acc_shape, F32)] if nk > 1 else [],
                 carry=carry)[0]


def _blk(per):
    return per if per <= 1024 else per // 2


def _pick_tiles(M, N, K, a_dtype, b_dtype, out_dtype, has_add, tn=None, tk=None):
    isz = lambda dt: jnp.dtype(dt).itemsize
    tms = sorted({_tile(M, t) for t in (2048, 1024, 512)}, reverse=True)
    tks = [tk] if tk else sorted({_tile(K, t) for t in (2048, 1024, 512)}, reverse=True)
    tns = [tn] if tn else sorted({_tile(N, t) for t in (1024, 512)}, reverse=True)
    for m in tms:
        for k in tks:
            for n in tns:
                blocks = m * k * isz(a_dtype) + k * n * isz(b_dtype) + m * n * (isz(out_dtype) + (4 if has_add else 0))
                if 2 * blocks + (m * n * 4 if K > k else 0) <= MATMUL_VMEM:
                    return m, n, k
    raise ValueError("no matmul tiling fits")


def _mm_nn(name, a, b, out_dtype, *, add=None, carry=None):
    M, K = a.shape
    if b.ndim == 3:
        per = b.shape[2]
        N = N_DEV * per
        tm, tn, tk = _pick_tiles(M, N, K, a.dtype, b.dtype, out_dtype, add is not None, tn=_blk(per))
        q = per // tn
        b_spec = pl.BlockSpec((None, tk, tn), lambda m, n, k: (n // q, k, n % q))
    else:
        N = b.shape[1]
        tm, tn, tk = _pick_tiles(M, N, K, a.dtype, b.dtype, out_dtype, add is not None)
        b_spec = pl.BlockSpec((tk, tn), lambda m, n, k: (k, n))
    return _matmul(
        name, a, b, dims=NN, grid=(M // tm, N // tn, K // tk),
        a_spec=pl.BlockSpec((tm, tk), lambda m, n, k: (m, k)), b_spec=b_spec,
        out_sds=jax.ShapeDtypeStruct((M, N), out_dtype), out_spec=pl.BlockSpec((tm, tn), lambda m, n, k: (m, n)),
        acc_shape=(tm, tn), add=add, add_spec=pl.BlockSpec((tm, tn), lambda m, n, k: (m, n)), carry=carry)


def _mm_nt(name, a, b, out_dtype, *, add=None, carry=None):
    M, K = a.shape
    if b.ndim == 3:
        N = b.shape[1]
        tm, tn, tk = _pick_tiles(M, N, K, a.dtype, b.dtype, out_dtype, add is not None, tk=b.shape[2])
        if tn < 1024 < tm:
            tm, tn = 1024, N
        b_spec = pl.BlockSpec((None, tn, tk), lambda m, n, k: (k, n, 0))
    else:
        N = b.shape[0]
        tm, tn, tk = _pick_tiles(M, N, K, a.dtype, b.dtype, out_dtype, add is not None)
        b_spec = pl.BlockSpec((tn, tk), lambda m, n, k: (n, k))
    return _matmul(
        name, a, b, dims=NT, grid=(M // tm, N // tn, K // tk),
        a_spec=pl.BlockSpec((tm, tk), lambda m, n, k: (m, k)), b_spec=b_spec,
        out_sds=jax.ShapeDtypeStruct((M, N), out_dtype), out_spec=pl.BlockSpec((tm, tn), lambda m, n, k: (m, n)),
        acc_shape=(tm, tn), add=add, add_spec=pl.BlockSpec((tm, tn), lambda m, n, k: (m, n)), carry=carry)


def _mm_tn(name, a, b, out_dtype, *, per=None, rows=None, carry=None):
    K, M = a.shape
    N = b.shape[1]
    row0 = 0
    if rows is not None:
        row0, M = rows
    tm, tn, tk = _pick_tiles(M, N, K, a.dtype, b.dtype, out_dtype, False, tn=_blk(per) if per else None)
    m0 = row0 // tm
    if per is not None:
        q = per // tn
        out_sds = jax.ShapeDtypeStruct((N_DEV, M, per), out_dtype)
        out_spec = pl.BlockSpec((None, tm, tn), lambda m, n, k: (n // q, m, n % q))
    else:
        out_sds = jax.ShapeDtypeStruct((M, N), out_dtype)
        out_spec = pl.BlockSpec((tm, tn), lambda m, n, k: (m, n))
    return _matmul(
        name, a, b, dims=TN, grid=(M // tm, N // tn, K // tk),
        a_spec=pl.BlockSpec((tk, tm), lambda m, n, k: (k, m + m0)), b_spec=pl.BlockSpec((tk, tn), lambda m, n, k: (k, n)),
        out_sds=out_sds, out_spec=out_spec, acc_shape=(tm, tn), carry=carry)


def _rms_fwd(name, h, g):
    T, D = h.shape
    tb = _tile(T, 512)

    def body(h_ref, g_ref, o_ref):
        x = h_ref[...]
        r = lax.rsqrt(jnp.mean(x * x, axis=-1, keepdims=True) + EPS)
        o_ref[...] = (x * r * g_ref[...]).astype(o_ref.dtype)

    return pl.pallas_call(
        body, name=name, grid=(T // tb,),
        in_specs=[pl.BlockSpec((tb, D), lambda i: (i, 0)), pl.BlockSpec((1, D), lambda i: (0, 0))],
        out_specs=pl.BlockSpec((tb, D), lambda i: (i, 0)), out_shape=jax.ShapeDtypeStruct((T, D), BF16),
        compiler_params=_cp("parallel"))(h, g)


def _rms_bwd(name, dhn, h, g, dres):
    T, D = h.shape
    tb = _tile(T, 256)

    def body(dhn_ref, h_ref, g_ref, dres_ref, dh_ref, dg_ref):
        i = pl.program_id(0)
        x = h_ref[...]
        r = lax.rsqrt(jnp.mean(x * x, axis=-1, keepdims=True) + EPS)
        y = x * r
        d = dhn_ref[...]
        dy = d * g_ref[...]
        dh_ref[...] = dres_ref[...] + r * (dy - y * jnp.mean(dy * y, axis=-1, keepdims=True))
        part = jnp.sum(d * y, axis=0, keepdims=True)

        @pl.when(i == 0)
        def _():
            dg_ref[...] = part

        @pl.when(i > 0)
        def _():
            dg_ref[...] += part

    row = pl.BlockSpec((tb, D), lambda i: (i, 0))
    vec = pl.BlockSpec((1, D), lambda i: (0, 0))
    return pl.pallas_call(
        body, name=name, grid=(T // tb,), in_specs=[row, row, vec, row], out_specs=[row, vec],
        out_shape=[jax.ShapeDtypeStruct((T, D), F32), jax.ShapeDtypeStruct((1, D), F32)],
        compiler_params=_cp("arbitrary"))(dhn, h, g, dres)


def _loss_head(h, tgt, g):
    T, D = h.shape
    tb = _tile(T, 256)

    def body(h_ref, t_ref, g_ref, loss_ref, dh_ref, dg_ref):
        i = pl.program_id(0)
        x = h_ref[...]
        gg = g_ref[...]
        r = lax.rsqrt(jnp.mean(x * x, axis=-1, keepdims=True) + EPS)
        y0 = x * r
        err = y0 * gg - t_ref[...]
        tot = 0.5 * jnp.sum(jnp.mean(err * err, axis=-1, keepdims=True), axis=0, keepdims=True)
        dy = err * (1.0 / D)
        dyg = dy * gg
        dh_ref[...] = r * (dyg - y0 * jnp.mean(dyg * y0, axis=-1, keepdims=True))
        part = jnp.sum(dy * y0, axis=0, keepdims=True)
        tot = jnp.broadcast_to(tot, loss_ref.shape)

        @pl.when(i == 0)
        def _():
            dg_ref[...] = part
            loss_ref[...] = tot

        @pl.when(i > 0)
        def _():
            dg_ref[...] += part
            loss_ref[...] += tot

    row = pl.BlockSpec((tb, D), lambda i: (i, 0))
    vec = pl.BlockSpec((1, D), lambda i: (0, 0))
    return pl.pallas_call(
        body, name="loss_head", grid=(T // tb,), in_specs=[row, row, vec],
        out_specs=[pl.BlockSpec((SUBLANE, LANE), lambda i: (0, 0)), row, vec],
        out_shape=[jax.ShapeDtypeStruct((SUBLANE, LANE), F32), jax.ShapeDtypeStruct((T, D), F32),
                   jax.ShapeDtypeStruct((1, D), F32)],
        compiler_params=_cp("arbitrary"))(h, tgt, g)


def _rel_onehot(pos_axis, shape):
    pos = lax.broadcasted_iota(I32, shape, pos_axis)
    r = lax.broadcasted_iota(I32, shape, 1 - pos_axis)
    d = jnp.where(pos < KW, N_PAST * CHUNK - pos, N_PAST * CHUNK + ROLL_W - pos)
    return (jnp.clip(d, -MAX_REL, MAX_REL) + MAX_REL == r).astype(F32)


def _roll_rows(x, left):
    row = lax.broadcasted_iota(I32, x.shape, 0)
    for b in range(QB.bit_length() - 1):
        shift = (ROLL_W - (1 << b)) if left else (1 << b)
        x = jnp.where(((row >> b) & 1) == 1, pltpu.roll(x, shift, 1), x)
    return x


def _bias_tile(rel_bias):
    H = rel_bias.shape[0]
    rb = jnp.pad(rel_bias, ((0, 0), (0, REL_PAD - rel_bias.shape[1]))).reshape(H, 1, REL_PAD)

    def body(rb_ref, o_ref):
        row = jnp.broadcast_to(rb_ref[...], (SUBLANE, REL_PAD))
        base = jnp.dot(row, _rel_onehot(1, (REL_PAD, ROLL_W)), precision=lax.Precision.HIGHEST,
                       preferred_element_type=F32)[0:1]
        tile = _roll_rows(jnp.broadcast_to(base, (QB, ROLL_W)), left=False)[:, :KW]
        qc = lax.broadcasted_iota(I32, (QB, KW), 0) // CHUNK
        kc = lax.broadcasted_iota(I32, (QB, KW), 1) // CHUNK - N_PAST
        o_ref[...] = jnp.where((kc >= qc - N_PAST) & (kc <= qc), tile, NEG)

    return pl.pallas_call(
        body, name="bias_tile", grid=(H,),
        in_specs=[pl.BlockSpec((None, 1, REL_PAD), lambda h: (h, 0, 0))],
        out_specs=pl.BlockSpec((None, QB, KW), lambda h: (h, 0, 0)),
        out_shape=jax.ShapeDtypeStruct((H, QB, KW), F32), compiler_params=_cp("parallel"))(rb)


def _bias_tile_grad(dtile, n_rel):
    H = dtile.shape[0]

    def body(dt_ref, o_ref):
        x = jnp.concatenate([dt_ref[...], jnp.zeros((QB, ROLL_W - KW), F32)], axis=1)
        cs = jnp.sum(_roll_rows(x, left=True), axis=0, keepdims=True)
        o_ref[...] = jnp.dot(jnp.broadcast_to(cs, (SUBLANE, ROLL_W)), _rel_onehot(0, (ROLL_W, REL_PAD)),
                             precision=lax.Precision.HIGHEST, preferred_element_type=F32)[0:1]

    out = pl.pallas_call(
        body, name="bias_tile_grad", grid=(H,),
        in_specs=[pl.BlockSpec((None, QB, KW), lambda h: (h, 0, 0))],
        out_specs=pl.BlockSpec((None, 1, REL_PAD), lambda h: (h, 0, 0)),
        out_shape=jax.ShapeDtypeStruct((H, 1, REL_PAD), F32), compiler_params=_cp("parallel"))(dtile)
    return out.reshape(H, REL_PAD)[:, :n_rel]


def _band_specs(G):
    spec = lambda f: pl.BlockSpec((QB, BAND_W), f)
    q = spec(lambda h, i: (i, h))
    ks = [spec(lambda h, i, r=r: (jnp.maximum(i - 2 + r, 0), G + h)) for r in range(3)]
    vs = [spec(lambda h, i, r=r: (jnp.maximum(i - 2 + r, 0), 2 * G + h)) for r in range(3)]
    gate = spec(lambda h, i: (i, 5 * G + h))
    bias = pl.BlockSpec((HB, QB, KW), lambda h, i: (h, 0, 0))
    return [q] + ks + vs + [gate, bias]


def _band_probs(i, hh, q_ref, k_refs, v_refs, bias_ref):
    cols = slice(hh * HEAD_A, (hh + 1) * HEAD_A)
    q = q_ref[:, cols].astype(BF16)
    k = jnp.concatenate([r[:, cols] for r in k_refs], axis=0).astype(BF16)
    v = jnp.concatenate([r[:, cols] for r in v_refs], axis=0).astype(BF16)
    s = _dot(q, k, NT) * (HEAD_A ** -0.5) + bias_ref[hh]
    kpos = (i - 2) * QB + lax.broadcasted_iota(I32, (1, KW), 1)
    s = jnp.where(kpos >= 0, s, NEG)
    e = jnp.exp(s - jnp.max(s, axis=-1, keepdims=True))
    p = e * (1.0 / jnp.sum(e, axis=-1, keepdims=True))
    return p, q, k, v


def _band_attn_fwd(proj, bias_tile, carry=None):
    T = proj.shape[0]
    H = bias_tile.shape[0]

    def body(q_ref, k0, k1, k2, v0, v1, v2, gate_ref, bias_ref, ya_ref, y_ref):
        for hh in range(HB):
            cols = slice(hh * HEAD_A, (hh + 1) * HEAD_A)
            p, _, _, v = _band_probs(pl.program_id(1), hh, q_ref, (k0, k1, k2), (v0, v1, v2), bias_ref)
            o = _dot(p.astype(BF16), v, NN)
            g = gate_ref[:, cols]
            ya_ref[:, cols] = o
            y_ref[:, cols] = (o * (g * _sigmoid(g))).astype(y_ref.dtype)

    out = pl.BlockSpec((QB, BAND_W), lambda h, i: (i, h))
    return _call(
        "band_attn_fwd", body, grid=(H // HB, T // QB), in_specs=_band_specs(H // HB), out_specs=[out, out],
        out_shape=[jax.ShapeDtypeStruct((T, H * HEAD_A), F32), jax.ShapeDtypeStruct((T, 2 * H * HEAD_A), BF16)],
        args=[proj] * 8 + [bias_tile], sem=("parallel", "parallel"), carry=carry)


def _band_attn_bwd(proj, bias_tile, dy, ya, carry=None):
    T = proj.shape[0]
    H = bias_tile.shape[0]
    n_i = T // QB

    def body(q_ref, k0, k1, k2, v0, v1, v2, gate_ref, bias_ref, dy_ref, ya_ref,
             dq_ref, dk_ref, dv_ref, dgate_ref, dbias_ref, dk_acc, dv_acc):
        i = pl.program_id(1)

        @pl.when(i == 0)
        def _():
            dk_acc[...] = jnp.zeros_like(dk_acc)
            dv_acc[...] = jnp.zeros_like(dv_acc)
            dbias_ref[...] = jnp.zeros_like(dbias_ref)

        for hh in range(HB):
            cols = slice(hh * HEAD_A, (hh + 1) * HEAD_A)
            p, q, k, v = _band_probs(i, hh, q_ref, (k0, k1, k2), (v0, v1, v2), bias_ref)
            g = gate_ref[:, cols]
            sg = _sigmoid(g)
            dyv = dy_ref[:, cols]
            dgate_ref[:, cols] = (dyv * ya_ref[:, cols] * _dsilu(g, sg)).astype(dgate_ref.dtype)
            do = (dyv * (g * sg)).astype(BF16)
            dp = _dot(do, v, NT)
            ds = p * (dp - jnp.sum(dp * p, axis=-1, keepdims=True))
            dbias_ref[hh] += ds
            dsb = (ds * (HEAD_A ** -0.5)).astype(BF16)
            dq_ref[:, cols] = _dot(dsb, k, NN).astype(dq_ref.dtype)
            dkc = _dot(dsb, q, TN)
            dvc = _dot(p.astype(BF16), do, TN)
            for r in range(3):
                blk = i - 2 + r

                @pl.when(blk >= 0)
                def _(r=r, blk=blk, cols=cols, dkc=dkc, dvc=dvc):
                    rows = pl.ds(pl.multiple_of(blk * QB, QB), QB)
                    dk_acc[rows, cols] += dkc[r * QB:(r + 1) * QB]
                    dv_acc[rows, cols] += dvc[r * QB:(r + 1) * QB]

        @pl.when(i == n_i - 1)
        def _():
            dk_ref[...] = dk_acc[...].astype(dk_ref.dtype)
            dv_ref[...] = dv_acc[...].astype(dv_ref.dtype)

    blk = pl.BlockSpec((QB, BAND_W), lambda h, i: (i, h))
    col = pl.BlockSpec((T, BAND_W), lambda h, i: (0, h))
    sds = jax.ShapeDtypeStruct((T, H * HEAD_A), BF16)
    return _call(
        "band_attn_bwd", body, grid=(H // HB, n_i), in_specs=_band_specs(H // HB) + [blk, blk],
        out_specs=[blk, col, col, blk, pl.BlockSpec((HB, QB, KW), lambda h, i: (h, 0, 0))],
        out_shape=[sds, sds, sds, sds, jax.ShapeDtypeStruct((H, QB, KW), F32)],
        args=[proj] * 8 + [bias_tile, dy, ya], sem=("parallel", "arbitrary"),
        scratch=[pltpu.VMEM((T, BAND_W), F32), pltpu.VMEM((T, BAND_W), F32)], carry=carry)


def _conv_in_specs(tb, C):
    per = tb // HALO
    nb = C // CONV_LC
    prev = lambda i: jnp.maximum(i * per - 1, 0)
    return [pl.BlockSpec((tb, CONV_LC), lambda c, i: (i, 3 * nb + c)), pl.BlockSpec((tb, CONV_LC), lambda c, i: (i, 4 * nb + c)),
            pl.BlockSpec((HALO, CONV_LC), lambda c, i: (prev(i), 3 * nb + c)),
            pl.BlockSpec((HALO, CONV_LC), lambda c, i: (prev(i), 4 * nb + c))]


def _conv_tb(T):
    return _tile(T, 1024) if T > 1024 else T // 2


def _glu_with_halo(i, ga_ref, gb_ref, ha_ref, hb_ref, scr):
    tb = ga_ref.shape[0]
    halo = ha_ref[...] * _sigmoid(hb_ref[...])
    scr[0:HALO, :] = jnp.where(i > 0, halo, 0.0)
    scr[HALO:HALO + tb, :] = ga_ref[...] * _sigmoid(gb_ref[...])
    scr[HALO + tb:, :] = jnp.zeros((SUBLANE, scr.shape[1]), F32)


def _chunk_taps(src, w_ref, row0, tap_of, uscr):
    acc = None
    for r in range(SUBLANE):
        u = None
        for a in range(HALO // SUBLANE + 1):
            j = tap_of(SUBLANE * a + r)
            if 0 <= j < CONV_K:
                rows = pl.ds(pl.multiple_of(row0 + SUBLANE * a, SUBLANE), CONV_RC + SUBLANE)
                t = w_ref[j:j + 1, :] * src[rows, :]
                u = t if u is None else u + t
        if u is None:
            continue
        if r == 0:
            piece = u[0:CONV_RC]
        else:
            uscr[...] = u
            piece = uscr[pl.ds(r, CONV_RC), :]
        acc = piece if acc is None else acc + piece
    return acc


def _layernorm_stats(z):
    mu = jnp.mean(z, axis=-1, keepdims=True)
    zc = z - mu
    rstd = lax.rsqrt(jnp.mean(zc * zc, axis=-1, keepdims=True) + EPS)
    return zc * rstd, rstd


def _conv_z(proj, conv_w, conv_b):
    T = proj.shape[0]
    C = conv_w.shape[1]
    tb = _conv_tb(T)

    def body(ga_ref, gb_ref, ha_ref, hb_ref, w_ref, cb_ref, z_ref, scr, uscr):
        _glu_with_halo(pl.program_id(1), ga_ref, gb_ref, ha_ref, hb_ref, scr)
        cb = cb_ref[...]

        def chunk(rc, carry):
            row0 = pl.multiple_of(rc * CONV_RC, CONV_RC)
            z = _chunk_taps(scr, w_ref, row0, lambda o: o - (HALO - (CONV_K - 1)), uscr)
            z_ref[pl.ds(row0, CONV_RC), :] = z + cb
            return carry

        lax.fori_loop(0, tb // CONV_RC, chunk, 0)

    return pl.pallas_call(
        body, name="conv_z", grid=(C // CONV_LC, T // tb),
        in_specs=_conv_in_specs(tb, C) + [pl.BlockSpec((HALO, CONV_LC), lambda c, i: (0, c)),
                                          pl.BlockSpec((1, CONV_LC), lambda c, i: (0, c))],
        out_specs=pl.BlockSpec((tb, CONV_LC), lambda c, i: (i, c)), out_shape=jax.ShapeDtypeStruct((T, C), F32),
        scratch_shapes=[pltpu.VMEM((HALO + tb + SUBLANE, CONV_LC), F32), pltpu.VMEM((CONV_RC + SUBLANE, CONV_LC), F32)],
        compiler_params=_cp("parallel", "parallel"))(proj, proj, proj, proj, conv_w, conv_b)


def _conv_ln_fwd(proj, z, y, ln_g, ln_b):
    T, C = z.shape
    tb = _tile(T, 256)

    def body(z_ref, gate_ref, g_ref, b_ref, y_in, y_ref):
        xhat, _ = _layernorm_stats(z_ref[...])
        ln = xhat * g_ref[...] + b_ref[...]
        gate = gate_ref[...]
        y_ref[...] = (ln * _sigmoid(ln) * (gate * _sigmoid(gate))).astype(y_ref.dtype)

    vec = pl.BlockSpec((1, C), lambda i: (0, 0))
    return pl.pallas_call(
        body, name="conv_ln_fwd", grid=(T // tb,),
        in_specs=[pl.BlockSpec((tb, C), lambda i: (i, 0)), pl.BlockSpec((tb, C), lambda i: (i, 6)), vec, vec,
                  pl.BlockSpec(memory_space=pl.ANY)],
        out_specs=pl.BlockSpec((tb, C), lambda i: (i, 1)), out_shape=jax.ShapeDtypeStruct(y.shape, y.dtype),
        input_output_aliases={4: 0}, compiler_params=_cp("parallel"))(z, proj, ln_g, ln_b, y)


def _conv_bwd_ln(proj, z, dy, ln_g, ln_b):
    T, C = z.shape
    tb = _tile(T, 256)

    def body(z_ref, gate_ref, dy_ref, g_ref, b_ref, dz_ref, dgate_ref, dg_ref, db_ref, dcb_ref):
        i = pl.program_id(0)
        xhat, rstd = _layernorm_stats(z_ref[...])
        ln = xhat * g_ref[...] + b_ref[...]
        sl = _sigmoid(ln)
        gate = gate_ref[...]
        sg = _sigmoid(gate)
        dyv = dy_ref[...]
        dgate_ref[...] = (dyv * (ln * sl) * _dsilu(gate, sg)).astype(dgate_ref.dtype)
        dln = dyv * (gate * sg) * _dsilu(ln, sl)
        dxh = dln * g_ref[...]
        dz = rstd * (dxh - jnp.mean(dxh, axis=-1, keepdims=True) - xhat * jnp.mean(dxh * xhat, axis=-1, keepdims=True))
        dz_ref[...] = dz
        parts = (jnp.sum(dln * xhat, axis=0, keepdims=True), jnp.sum(dln, axis=0, keepdims=True),
                 jnp.sum(dz, axis=0, keepdims=True))

        @pl.when(i == 0)
        def _():
            for ref, part in zip((dg_ref, db_ref, dcb_ref), parts):
                ref[...] = part

        @pl.when(i > 0)
        def _():
            for ref, part in zip((dg_ref, db_ref, dcb_ref), parts):
                ref[...] += part

    vec = pl.BlockSpec((1, C), lambda i: (0, 0))
    row = pl.BlockSpec((tb, C), lambda i: (i, 0))
    vsd = jax.ShapeDtypeStruct((1, C), F32)
    return pl.pallas_call(
        body, name="conv_bwd_ln", grid=(T // tb,),
        in_specs=[row, pl.BlockSpec((tb, C), lambda i: (i, 6)), pl.BlockSpec((tb, C), lambda i: (i, 1)), vec, vec],
        out_specs=[row, row, vec, vec, vec],
        out_shape=[jax.ShapeDtypeStruct((T, C), F32), jax.ShapeDtypeStruct((T, C), BF16), vsd, vsd, vsd],
        compiler_params=_cp("arbitrary"))(z, proj, dy, ln_g, ln_b)


def _conv_bwd_taps(proj, dz, conv_w):
    T = proj.shape[0]
    C = conv_w.shape[1]
    tb = _conv_tb(T)
    per = tb // HALO
    n_i = T // tb
    first = HALO - (CONV_K - 1)

    def body(ga_ref, gb_ref, ha_ref, hb_ref, dz_ref, dzn_ref, w_ref, da_ref, db_ref, dw_ref, scr, dscr, uscr, zscr, dwacc):
        i = pl.program_id(1)
        _glu_with_halo(i, ga_ref, gb_ref, ha_ref, hb_ref, scr)
        dscr[0:tb, :] = dz_ref[...]
        dscr[tb:tb + HALO, :] = jnp.where(i < n_i - 1, dzn_ref[...], 0.0)
        dscr[tb + HALO:, :] = jnp.zeros((SUBLANE, CONV_LC), F32)
        zscr[0:SUBLANE, :] = jnp.zeros((SUBLANE, CONV_LC), F32)
        zscr[SUBLANE + CONV_RC:, :] = jnp.zeros((SUBLANE, CONV_LC), F32)

        @pl.when(i == 0)
        def _():
            dwacc[...] = jnp.zeros_like(dwacc)

        def chunk(rc, carry):
            row0 = pl.multiple_of(rc * CONV_RC, CONV_RC)
            rows = pl.ds(row0, CONV_RC)
            dglu = _chunk_taps(dscr, w_ref, row0, lambda o: CONV_K - 1 - o, uscr)
            ga = ga_ref[rows, :]
            sb = _sigmoid(gb_ref[rows, :])
            da_ref[rows, :] = (dglu * sb).astype(da_ref.dtype)
            db_ref[rows, :] = (dglu * ga * sb * (1.0 - sb)).astype(db_ref.dtype)
            zscr[SUBLANE:SUBLANE + CONV_RC, :] = dz_ref[rows, :]
            for r in range(SUBLANE):
                dzs = zscr[pl.ds(SUBLANE - r, CONV_RC + SUBLANE), :]
                for a in range(HALO // SUBLANE + 1):
                    j = SUBLANE * a + r - first
                    if 0 <= j < CONV_K:
                        src = pl.ds(pl.multiple_of(row0 + SUBLANE * a, SUBLANE), CONV_RC + SUBLANE)
                        p = dzs * scr[src, :]
                        f = p[0:SUBLANE]
                        for s in range(1, CONV_RC // SUBLANE + 1):
                            f = f + p[s * SUBLANE:(s + 1) * SUBLANE]
                        dwacc[j * SUBLANE:(j + 1) * SUBLANE, :] += f
            return carry

        lax.fori_loop(0, tb // CONV_RC, chunk, 0)

        @pl.when(i == n_i - 1)
        def _():
            dw_ref[...] = jnp.zeros_like(dw_ref)
            for j in range(CONV_K):
                dw_ref[j:j + 1, :] = jnp.sum(dwacc[j * SUBLANE:(j + 1) * SUBLANE, :], axis=0, keepdims=True)

    blk = pl.BlockSpec((tb, CONV_LC), lambda c, i: (i, c))
    wspec = pl.BlockSpec((HALO, CONV_LC), lambda c, i: (0, c))
    nxt = pl.BlockSpec((HALO, CONV_LC), lambda c, i: (jnp.minimum((i + 1) * per, T // HALO - 1), c))
    return pl.pallas_call(
        body, name="conv_bwd_taps", grid=(C // CONV_LC, n_i),
        in_specs=_conv_in_specs(tb, C) + [blk, nxt, wspec], out_specs=[blk, blk, wspec],
        out_shape=[jax.ShapeDtypeStruct((T, C), BF16), jax.ShapeDtypeStruct((T, C), BF16),
                   jax.ShapeDtypeStruct((HALO, C), F32)],
        scratch_shapes=[pltpu.VMEM((HALO + tb + SUBLANE, CONV_LC), F32), pltpu.VMEM((tb + HALO + SUBLANE, CONV_LC), F32),
                        pltpu.VMEM((CONV_RC + SUBLANE, CONV_LC), F32), pltpu.VMEM((CONV_RC + 2 * SUBLANE, CONV_LC), F32),
                        pltpu.VMEM((HALO * SUBLANE, CONV_LC), F32)],
        compiler_params=_cp("parallel", "arbitrary"))(proj, proj, proj, proj, dz, dz, conv_w)


def _sgu_mask():
    r = lax.broadcasted_iota(I32, (GMLP_CHUNK, GMLP_CHUNK), 0) // CHUNK
    c = lax.broadcasted_iota(I32, (GMLP_CHUNK, GMLP_CHUNK), 1) // CHUNK
    return r >= c


def _sgu_fwd(proj, ln_g, ln_b, w_s, b_s_t):
    T = proj.shape[0]
    W = ln_g.shape[1]
    G = w_s.shape[0]
    cg = W // G
    tb = GMLP_CHUNK

    def body(u_ref, v_ref, gate_ref, g_ref, b_ref, ws_ref, bs_ref, y_ref):
        xhat, _ = _layernorm_stats(v_ref[...])
        vln = (xhat * g_ref[...] + b_ref[...]).astype(BF16)
        mask = _sgu_mask()
        for gi in range(G):
            cols = slice(gi * cg, (gi + 1) * cg)
            ws = jnp.where(mask, ws_ref[gi], 0.0).astype(BF16)
            sg = _dot(ws, vln[:, cols], NN) + bs_ref[:, gi:gi + 1]
            gate = gate_ref[:, cols]
            y_ref[:, cols] = (u_ref[:, cols] * sg * (gate * _sigmoid(gate))).astype(y_ref.dtype)

    vec = pl.BlockSpec((1, W), lambda i: (0, 0))
    return pl.pallas_call(
        body, name="sgu_fwd", grid=(T // tb,),
        in_specs=[pl.BlockSpec((tb, W), lambda i: (i, 0)), pl.BlockSpec((tb, W), lambda i: (i, 1)),
                  pl.BlockSpec((tb, W), lambda i: (i, 2)), vec, vec,
                  pl.BlockSpec((G, GMLP_CHUNK, GMLP_CHUNK), lambda i: (0, 0, 0)),
                  pl.BlockSpec((GMLP_CHUNK, G), lambda i: (0, 0))],
        out_specs=pl.BlockSpec((tb, W), lambda i: (i, 0)), out_shape=jax.ShapeDtypeStruct((T, W), BF16),
        compiler_params=_cp("parallel"))(proj, proj, proj, ln_g, ln_b, w_s, b_s_t)


def _sgu_bwd(proj, dy, ln_g, ln_b, w_s, b_s_t):
    T = proj.shape[0]
    W = ln_g.shape[1]
    G = w_s.shape[0]
    cg = W // G
    tb = GMLP_CHUNK

    def body(u_ref, v_ref, gate_ref, dy_ref, g_ref, b_ref, ws_ref, bs_ref,
             dp_ref, dws_ref, dbs_ref, dg_ref, db_ref, dvln_scr):
        i = pl.program_id(0)

        @pl.when(i == 0)
        def _():
            dws_ref[...] = jnp.zeros_like(dws_ref)
            dbs_ref[...] = jnp.zeros_like(dbs_ref)
            dg_ref[...] = jnp.zeros_like(dg_ref)
            db_ref[...] = jnp.zeros_like(db_ref)

        xhat, rstd = _layernorm_stats(v_ref[...])
        vln = (xhat * g_ref[...] + b_ref[...]).astype(BF16)
        mask = _sgu_mask()
        for gi in range(G):
            cols = slice(gi * cg, (gi + 1) * cg)
            ws = jnp.where(mask, ws_ref[gi], 0.0).astype(BF16)
            vg = vln[:, cols]
            sg = _dot(ws, vg, NN) + bs_ref[:, gi:gi + 1]
            gate = gate_ref[:, cols]
            s = _sigmoid(gate)
            u = u_ref[:, cols]
            dyv = dy_ref[:, cols]
            dyu = dyv * u
            dp_ref[:, cols] = (dyv * sg * (gate * s)).astype(dp_ref.dtype)
            dp_ref[:, 2 * W + gi * cg:2 * W + (gi + 1) * cg] = (dyu * sg * _dsilu(gate, s)).astype(dp_ref.dtype)
            dsg = dyu * (gate * s)
            dsgb = dsg.astype(BF16)
            dvln_scr[:, cols] = _dot(ws, dsgb, TN)
            dws_ref[gi] += _dot(dsgb, vg, NT)
            dbs_ref[:, gi:gi + 1] += jnp.sum(dsg, axis=-1, keepdims=True)
        dvln = dvln_scr[...]
        dg_ref[...] += jnp.sum(dvln * xhat, axis=0, keepdims=True)
        db_ref[...] += jnp.sum(dvln, axis=0, keepdims=True)
        dxh = dvln * g_ref[...]
        dv = rstd * (dxh - jnp.mean(dxh, axis=-1, keepdims=True) - xhat * jnp.mean(dxh * xhat, axis=-1, keepdims=True))
        dp_ref[:, W:2 * W] = dv.astype(dp_ref.dtype)

    vec = pl.BlockSpec((1, W), lambda i: (0, 0))
    wsp = pl.BlockSpec((G, GMLP_CHUNK, GMLP_CHUNK), lambda i: (0, 0, 0))
    bsp = pl.BlockSpec((GMLP_CHUNK, G), lambda i: (0, 0))
    return pl.pallas_call(
        body, name="sgu_bwd", grid=(T // tb,),
        in_specs=[pl.BlockSpec((tb, W), lambda i: (i, 0)), pl.BlockSpec((tb, W), lambda i: (i, 1)),
                  pl.BlockSpec((tb, W), lambda i: (i, 2)), pl.BlockSpec((tb, W), lambda i: (i, 0)), vec, vec, wsp, bsp],
        out_specs=[pl.BlockSpec((tb, 3 * W), lambda i: (i, 0)), wsp, bsp, vec, vec],
        out_shape=[jax.ShapeDtypeStruct((T, 3 * W), BF16), jax.ShapeDtypeStruct((G, GMLP_CHUNK, GMLP_CHUNK), F32),
                   jax.ShapeDtypeStruct((GMLP_CHUNK, G), F32), jax.ShapeDtypeStruct((1, W), F32),
                   jax.ShapeDtypeStruct((1, W), F32)],
        scratch_shapes=[pltpu.VMEM((tb, W), F32)],
        compiler_params=_cp("arbitrary"))(proj, proj, proj, dy, ln_g, ln_b, w_s, b_s_t)


def _xattn_probs(q, k, hd):
    s = _dot(q, k, NT) * (hd ** -0.5)
    e = jnp.exp(s - jnp.max(s, axis=-1, keepdims=True))
    return e * (1.0 / jnp.sum(e, axis=-1, keepdims=True))


def _xattn_fwd(name, q, k, v):
    T, D = q.shape
    M = k.shape[0]
    hd = D // N_HEADS_X
    tb = _tile(T, 512)

    def body(q_ref, k_ref, v_ref, o_ref):
        for h in range(N_HEADS_X):
            cols = slice(h * hd, (h + 1) * hd)
            p = _xattn_probs(q_ref[:, cols], k_ref[:, cols], hd)
            o_ref[:, cols] = _dot(p.astype(BF16), v_ref[:, cols], NN).astype(o_ref.dtype)

    row = pl.BlockSpec((tb, D), lambda i: (i, 0))
    kv = pl.BlockSpec((M, D), lambda i: (0, 0))
    return pl.pallas_call(
        body, name=name, grid=(T // tb,), in_specs=[row, kv, kv], out_specs=row,
        out_shape=jax.ShapeDtypeStruct((T, D), BF16), compiler_params=_cp("parallel"))(q, k, v)


def _xattn_bwd(name, q, k, v, do):
    T, D = q.shape
    M = k.shape[0]
    hd = D // N_HEADS_X
    tb = _tile(T, 512)

    def body(q_ref, k_ref, v_ref, do_ref, dq_ref, dk_ref, dv_ref):
        @pl.when(pl.program_id(0) == 0)
        def _():
            dk_ref[...] = jnp.zeros_like(dk_ref)
            dv_ref[...] = jnp.zeros_like(dv_ref)

        for h in range(N_HEADS_X):
            cols = slice(h * hd, (h + 1) * hd)
            qh, kh, doh = q_ref[:, cols], k_ref[:, cols], do_ref[:, cols]
            p = _xattn_probs(qh, kh, hd)
            dp = _dot(doh, v_ref[:, cols], NT)
            ds = p * (dp - jnp.sum(dp * p, axis=-1, keepdims=True))
            dsb = (ds * (hd ** -0.5)).astype(BF16)
            dq_ref[:, cols] = _dot(dsb, kh, NN).astype(dq_ref.dtype)
            dk_ref[:, cols] += _dot(dsb, qh, TN)
            dv_ref[:, cols] += _dot(p.astype(BF16), doh, TN)

    row = pl.BlockSpec((tb, D), lambda i: (i, 0))
    kv = pl.BlockSpec((M, D), lambda i: (0, 0))
    return pl.pallas_call(
        body, name=name, grid=(T // tb,), in_specs=[row, kv, kv, row], out_specs=[row, kv, kv],
        out_shape=[jax.ShapeDtypeStruct((T, D), BF16), jax.ShapeDtypeStruct((M, D), F32),
                   jax.ShapeDtypeStruct((M, D), F32)],
        compiler_params=_cp("arbitrary"))(q, k, v, do)


def _adamw(name, contrib, w, m, v, carry=None):
    parts = list(contrib) if isinstance(contrib, (list, tuple)) else [contrib]
    n_parts = len(parts)
    R, C = w.shape
    tr = min(min(p.shape[1] for p in parts), 128)
    while any(p.shape[1] % tr for p in parts):
        tr -= SUBLANE
    tiles = [p.shape[1] // tr for p in parts]
    first = [sum(tiles[:p]) for p in range(n_parts + 1)]

    def body(*refs):
        c_refs = refs[:n_parts]
        w_ref, m_ref, v_ref, g_ref, d_ref, nm_ref, nv_ref = refs[n_parts:]

        def update(c_ref):
            g = c_ref[0].astype(F32)
            for s in range(1, N_DEV):
                g = g + c_ref[s].astype(F32)
            nm = ADAM_B1 * m_ref[...] + (1.0 - ADAM_B1) * g
            nv = ADAM_B2 * v_ref[...] + (1.0 - ADAM_B2) * (g * g)
            m_hat = nm / (1.0 - ADAM_B1 ** ADAM_STEP)
            v_hat = nv / (1.0 - ADAM_B2 ** ADAM_STEP)
            g_ref[...] = g
            d_ref[...] = -ADAM_LR * (m_hat / (jnp.sqrt(v_hat) + ADAM_EPS) + ADAM_WD * w_ref[...])
            nm_ref[...] = nm
            nv_ref[...] = nv

        if n_parts == 1:
            update(c_refs[0])
        else:
            i = pl.program_id(0)
            for p in range(n_parts):
                pl.when((i >= first[p]) & (i < first[p + 1]))(lambda p=p: update(c_refs[p]))

    row = pl.BlockSpec((tr, C), lambda i: (i, 0))
    sds = jax.ShapeDtypeStruct((R, C), F32)
    c_specs = [pl.BlockSpec((N_DEV, tr, C), lambda i, p=p: (0, jnp.clip(i - first[p], 0, tiles[p] - 1), 0))
               for p in range(n_parts)]
    return _call(name, body, grid=(R // tr,), in_specs=c_specs + [row, row, row], out_specs=[row] * 4,
                 out_shape=[sds] * 4, args=(*parts, w, m, v), sem=("parallel",), carry=carry)


def _pack(arrs):
    unit = SUBLANE * LANE
    flat = [jnp.pad(a.reshape(-1), (0, -a.size % unit)) for a in arrs]
    return jnp.concatenate(flat).reshape(-1, LANE)


def _unpack(buf, shapes):
    unit = SUBLANE * LANE
    flat = buf.reshape(-1)
    out, off = [], 0
    for s in shapes:
        size = 1
        for d in s:
            size *= d
        out.append(flat[off:off + size].reshape(s))
        off += size + (-size % unit)
    return out


def _cross_attention_fwd(l, h, mem, g_x, g_mem, wq, wk, wv, wo):
    hx = _rms_fwd(f"rms_x{l}", h, g_x)
    memn = _rms_fwd(f"rms_mem{l}", mem, g_mem)
    q = _mm_nn(f"xq{l}", hx, wq, BF16)
    k = _mm_nn(f"xk{l}", memn, wk, BF16)
    v = _mm_nn(f"xv{l}", memn, wv, BF16)
    o = _xattn_fwd(f"xattn_fwd{l}", q, k, v)
    h_out = _mm_nn(f"xo{l}", o, wo, F32, add=h)
    return h_out, (hx, memn, q, k, v, o)


def _cross_attention_bwd(l, dh, h, mem, g_x, g_mem, wq, wk, wv, wo, saved):
    hx, memn, q, k, v, o = saved
    do = _mm_nt(f"xo_dx{l}", dh, wo, BF16)
    dwo = _mm_tn(f"xo_dw{l}", o, dh, BF16)
    dq, dk, dv = _xattn_bwd(f"xattn_bwd{l}", q, k, v, do)
    dwq = _mm_tn(f"xq_dw{l}", hx, dq, BF16)
    dwk = _mm_tn(f"xk_dw{l}", memn, dk, BF16)
    dwv = _mm_tn(f"xv_dw{l}", memn, dv, BF16)
    dmemn = _mm_nt(f"xk_dx{l}", dk, wk, F32)
    dmemn = _mm_nt(f"xv_dx{l}", dv, wv, F32, add=dmemn)
    _, dg_mem = _rms_bwd(f"rms_mem_bwd{l}", dmemn, mem, g_mem, jnp.zeros_like(mem))
    dhx = _mm_nt(f"xq_dx{l}", dq, wq, F32)
    dh_in, dg_x = _rms_bwd(f"rms_x_bwd{l}", dhx, h, g_x, dh)
    return dh_in, (dg_x, dg_mem, dwq, dwk, dwv, dwo)


def kernel(x, mem, norm_mix_g, norm_x_g, norm_mem_g, final_norm_g, w_in_ab, rel_bias, conv_w, conv_b, conv_ln_g, conv_ln_b, w_out_ab, w_in_c, sgu_ln_g, sgu_ln_b, w_s, b_s, w_out_c, w_xq, w_xk, w_xv, w_xo, loss_target, m_norm_mix_g, m_norm_x_g, m_norm_mem_g, m_final_norm_g, m_w_in_ab, m_rel_bias, m_conv_w, m_conv_b, m_conv_ln_g, m_conv_ln_b, m_w_out_ab, m_w_in_c, m_sgu_ln_g, m_sgu_ln_b, m_w_s, m_b_s, m_w_out_c, m_w_xq, m_w_xk, m_w_xv, m_w_xo, v_norm_mix_g, v_norm_x_g, v_norm_mem_g, v_final_norm_g, v_w_in_ab, v_rel_bias, v_conv_w, v_conv_b, v_conv_ln_g, v_conv_ln_b, v_w_out_ab, v_w_in_c, v_sgu_ln_g, v_sgu_ln_b, v_w_s, v_b_s, v_w_out_c, v_w_xq, v_w_xk, v_w_xv, v_w_xo):
    names = ["norm_mix_g", "norm_x_g", "norm_mem_g", "final_norm_g", "w_in_ab", "rel_bias", "conv_w", "conv_b",
             "conv_ln_g", "conv_ln_b", "w_out_ab", "w_in_c", "sgu_ln_g", "sgu_ln_b", "w_s", "b_s", "w_out_c",
             "w_xq", "w_xk", "w_xv", "w_xo"]
    W = dict(zip(names, (norm_mix_g, norm_x_g, norm_mem_g, final_norm_g, w_in_ab, rel_bias, conv_w, conv_b, conv_ln_g,
                         conv_ln_b, w_out_ab, w_in_c, sgu_ln_g, sgu_ln_b, w_s, b_s, w_out_c, w_xq, w_xk, w_xv, w_xo)))
    M1 = dict(zip(names, (m_norm_mix_g, m_norm_x_g, m_norm_mem_g, m_final_norm_g, m_w_in_ab, m_rel_bias, m_conv_w, m_conv_b,
                          m_conv_ln_g, m_conv_ln_b, m_w_out_ab, m_w_in_c, m_sgu_ln_g, m_sgu_ln_b, m_w_s, m_b_s, m_w_out_c,
                          m_w_xq, m_w_xk, m_w_xv, m_w_xo)))
    M2 = dict(zip(names, (v_norm_mix_g, v_norm_x_g, v_norm_mem_g, v_final_norm_g, v_w_in_ab, v_rel_bias, v_conv_w, v_conv_b,
                          v_conv_ln_g, v_conv_ln_b, v_w_out_ab, v_w_in_c, v_sgu_ln_g, v_sgu_ln_b, v_w_s, v_b_s, v_w_out_c,
                          v_w_xq, v_w_xk, v_w_xv, v_w_xo)))

    h0, memv, tgt = x[0], mem[0], loss_target[0]
    T, D = h0.shape
    n_rel = rel_bias.shape[2]
    xnames = ["w_xq", "w_xk", "w_xv", "w_xo"]
    bf = lambda a: a.astype(BF16)
    blocks = lambda g: g.reshape(N_DEV, -1, D)

    small = _pack([conv_w[0], sgu_ln_g[0], sgu_ln_b[0]])
    win_ab, small_g = _comm_call("gather_in_ab", _Carry("gather", [bf(w_in_ab[0]), small]))
    per_cw, per_ln = conv_w.shape[2], sgu_ln_g.shape[1]
    cw_s, lg_s, lb_s = zip(*[_unpack(small_g[d], [(CONV_K, per_cw), (1, per_ln), (1, per_ln)]) for d in range(N_DEV)])
    conv_w_full = jnp.pad(jnp.concatenate(cw_s, axis=1), ((0, HALO - CONV_K), (0, 0)))
    sgu_g_full = jnp.concatenate(lg_s, axis=1)
    sgu_b_full = jnp.concatenate(lb_s, axis=1)
    b_s_t = b_s[0].T

    hn0 = _rms_fwd("rms_mix0", h0, norm_mix_g[0:1])
    ag1 = _Carry("gather", [bf(w_out_ab[0])] + [bf(W[n][0]) for n in xnames])
    proj_ab = _mm_nn("in_ab", hn0, win_ab, F32, carry=ag1)
    wout_ab = ag1.result[0].reshape(-1, D)
    wx0 = [g.reshape(D, D) for g in ag1.result[1:]]
    btile = _bias_tile(rel_bias[0])
    ag2 = _Carry("gather", [bf(w_in_c[0])])
    ya, y_ab = _band_attn_fwd(proj_ab, btile, carry=ag2)
    win_c = ag2.result[0]
    z_conv = _conv_z(proj_ab, conv_w_full, conv_b)
    y_ab = _conv_ln_fwd(proj_ab, z_conv, y_ab, conv_ln_g, conv_ln_b)
    h1 = _mm_nn("out_ab", y_ab, wout_ab, F32, add=h0)
    h2, xs0 = _cross_attention_fwd(0, h1, memv, norm_x_g[0:1], norm_mem_g[0:1], *wx0)
    hn1 = _rms_fwd("rms_mix1", h2, norm_mix_g[1:2])
    ag3 = _Carry("gather", [bf(w_out_c[0])] + [bf(W[n][1]) for n in xnames])
    proj_c = _mm_nn("in_c", hn1, win_c, F32, carry=ag3)
    wout_c = ag3.result[0].reshape(-1, D)
    wx1 = [g.reshape(D, D) for g in ag3.result[1:]]
    y_c = _sgu_fwd(proj_c, sgu_g_full, sgu_b_full, w_s[0], b_s_t)
    h3 = _mm_nn("out_c", y_c, wout_c, F32, add=h2)
    h4, xs1 = _cross_attention_fwd(1, h3, memv, norm_x_g[1:2], norm_mem_g[1:2], *wx1)
    loss_acc, dh4, dg_final = _loss_head(h4, tgt, final_norm_g.reshape(1, D))

    sent = {}

    def ride(**grads):
        carry = _Carry("exchange", list(grads.values()))
        sent.update({name: (carry, i) for i, name in enumerate(grads)})
        return carry

    dh3, (dg_x1, dg_mem1, dwq1, dwk1, dwv1, dwo1) = _cross_attention_bwd(
        1, dh4, h3, memv, norm_x_g[1:2], norm_mem_g[1:2], *wx1, xs1)
    dy_c = _mm_nt("out_c_dx", dh3, wout_c, F32, carry=ride(w_xo1=blocks(dwo1)))
    dwout_c = _mm_tn("out_c_dw", y_c, dh3, BF16, carry=ride(w_xq1=blocks(dwq1)))
    dproj_c, dws, dbs_t, dsgu_g, dsgu_b = _sgu_bwd(proj_c, dy_c, sgu_g_full, sgu_b_full, w_s[0], b_s_t)
    dwin_c = _mm_tn("in_c_dw", hn1, dproj_c, BF16, per=win_c.shape[2],
                    carry=ride(w_xk1=blocks(dwk1), w_out_c=blocks(dwout_c)))
    dhn1 = _mm_nt("in_c_dx", dproj_c, win_c, F32, carry=ride(w_xv1=blocks(dwv1)))
    dh2, dg_mix1 = _rms_bwd("rms_mix1_bwd", dhn1, h2, norm_mix_g[1:2], dh3)
    dh1, (dg_x0, dg_mem0, dwq0, dwk0, dwv0, dwo0) = _cross_attention_bwd(
        0, dh2, h1, memv, norm_x_g[0:1], norm_mem_g[0:1], *wx0, xs0)
    dy_ab = _mm_nt("out_ab_dx", dh1, wout_ab, F32, carry=ride(w_xo0=blocks(dwo0)))
    dwout_ab = _mm_tn("out_ab_dw", y_ab, dh1, BF16, carry=ride(w_xq0=blocks(dwq0)))
    dq, dk, dv, dgate_a, dbtile = _band_attn_bwd(proj_ab, btile, dy_ab, ya, carry=ride(w_in_c=dwin_c))
    drel = _bias_tile_grad(dbtile, n_rel)
    dz, dgate_b, dcln_g, dcln_b, dconv_b = _conv_bwd_ln(proj_ab, z_conv, dy_ab, conv_ln_g, conv_ln_b)
    dglu_a, dglu_b, dconv_w = _conv_bwd_taps(proj_ab, dz, conv_w_full)
    dproj_ab = jnp.concatenate([dq, dk, dv, dglu_a, dglu_b, dgate_a, dgate_b], axis=1)
    dwin_ab_lo = _mm_tn("in_ab_dw_lo", hn0, dproj_ab, BF16, per=win_ab.shape[2], rows=(0, D // 2),
                        carry=ride(w_xk0=blocks(dwk0), w_out_ab=blocks(dwout_ab)))
    dwin_ab_hi = _mm_tn("in_ab_dw_hi", hn0, dproj_ab, BF16, per=win_ab.shape[2], rows=(D // 2, D // 2),
                        carry=ride(w_in_ab_lo=dwin_ab_lo))
    dhn0 = _mm_nt("in_ab_dx", dproj_ab, win_ab, F32, carry=ride(w_in_ab_hi=dwin_ab_hi, w_xv0=blocks(dwv0)))
    dx, dg_mix0 = _rms_bwd("rms_mix0_bwd", dhn0, h0, norm_mix_g[0:1], dh1)
    received = lambda name: sent[name][0].result[sent[name][1]]

    sm = [_pack([dconv_w[:CONV_K, d * per_cw:(d + 1) * per_cw], dsgu_g[:, d * per_ln:(d + 1) * per_ln],
                 dsgu_b[:, d * per_ln:(d + 1) * per_ln]]) for d in range(N_DEV)]
    mask = (jnp.arange(GMLP_CHUNK)[:, None] // CHUNK >= jnp.arange(GMLP_CHUNK)[None, :] // CHUNK).astype(F32)
    rep_names = ["norm_mix_g", "norm_x_g", "norm_mem_g", "final_norm_g", "rel_bias", "conv_b", "conv_ln_g", "conv_ln_b",
                 "w_s", "b_s"]
    rep_grads = {
        "norm_mix_g": jnp.concatenate([dg_mix0, dg_mix1], axis=0),
        "norm_x_g": jnp.concatenate([dg_x0, dg_x1], axis=0),
        "norm_mem_g": jnp.concatenate([dg_mem0, dg_mem1], axis=0),
        "final_norm_g": dg_final.reshape(D),
        "rel_bias": drel[None], "conv_b": dconv_b, "conv_ln_g": dcln_g, "conv_ln_b": dcln_b,
        "w_s": (dws * mask[None])[None], "b_s": dbs_t.T[None],
    }
    ex_e = _Carry("exchange", [jnp.stack(sm), _pack([rep_grads[n] for n in rep_names])], bcast=[False, True])

    out = {}
    kinds = ("grad", "delta", "new_m", "new_v")
    for n in ("w_in_ab", "w_out_ab", "w_in_c", "w_out_c"):
        contrib = [received("w_in_ab_lo"), received("w_in_ab_hi")] if n == "w_in_ab" else received(n)
        res = _adamw(f"adamw_{n}", contrib, W[n][0], M1[n][0], M2[n][0], carry=ex_e if n == "w_in_ab" else None)
        for kind, r in zip(kinds, res):
            out[(kind, n)] = r[None]
    recv_small, recv_rep = ex_e.result
    for n in xnames:
        res = [_adamw(f"adamw_{n}{l}", received(f"{n}{l}"), W[n][l], M1[n][l], M2[n][l]) for l in range(2)]
        for kind, r in zip(kinds, zip(*res)):
            out[(kind, n)] = jnp.stack(r)
    sm_names = ["conv_w", "sgu_ln_g", "sgu_ln_b"]
    res = _adamw("adamw_small", recv_small, *[_pack([D_[n][0] for n in sm_names]) for D_ in (W, M1, M2)])
    for kind, r in zip(kinds, res):
        for n, piece in zip(sm_names, _unpack(r, [W[n].shape for n in sm_names])):
            out[(kind, n)] = piece
    res = _adamw("adamw_replicated", recv_rep, *[_pack([D_[n] for n in rep_names]) for D_ in (W, M1, M2)])
    for kind, r in zip(kinds, res):
        for n, piece in zip(rep_names, _unpack(r, [W[n].shape for n in rep_names])):
            out[(kind, n)] = piece

    loss = lax.psum(loss_acc[0, 0], MESH_AXES)
    return (loss, dx[None]) + tuple(out[(kind, n)] for kind in kinds for n in names)
```

```python
import jax
import jax.numpy as jnp
from jax import lax
from jax.experimental import pallas as pl
from jax.experimental.pallas import tpu as pltpu

F32 = jnp.float32
BF16 = jnp.bfloat16
I32 = jnp.int32

N_DEV = 8
CHUNK = 64
N_PAST = 8
MAX_REL = 128
HEAD_A = 128
CONV_K = 31
GMLP_CHUNK = 128
N_HEADS_X = 4
EPS = 1e-6
NEG = -1e30

ADAM_LR, ADAM_B1, ADAM_B2, ADAM_EPS, ADAM_WD, ADAM_STEP = 0.001, 0.9, 0.999, 1e-08, 0.01, 10

LANE = 128
SUBLANE = 8
VMEM_LIMIT = 56 * 1024 * 1024
MATMUL_VMEM = 44 * 1024 * 1024
QB = 4 * CHUNK
KW = QB + N_PAST * CHUNK
ROLL_W = 1024
REL_PAD = 384
HB = 2
BAND_W = HB * HEAD_A
HALO = 32
CONV_LC = LANE
CONV_RC = 64
MESH_AXES = ("x", "y", "c")
GATHER_PIECES = 4

NN = (((1,), (0,)), ((), ()))
NT = (((1,), (1,)), ((), ()))
TN = (((0,), (0,)), ((), ()))


def _cp(*sem):
    return pltpu.CompilerParams(dimension_semantics=sem, vmem_limit_bytes=VMEM_LIMIT)


def _tile(dim, pref):
    if dim <= pref:
        return dim
    t = (pref // LANE) * LANE
    while dim % t:
        t -= LANE
    return t


def _sigmoid(x):
    return 1.0 / (1.0 + jnp.exp(-x))


def _dsilu(x, s):
    return s * (1.0 + x * (1.0 - s))


def _dot(a, b, dims):
    return lax.dot_general(a, b, dims, preferred_element_type=F32)


def _mesh_pos():
    return lax.axis_index("x"), lax.axis_index("y"), lax.axis_index("c")


def _lin(x, y, c):
    return 4 * x + 2 * y + c


def _remote(src, dst, send_sem, recv_sem, to):
    return pltpu.make_async_remote_copy(src_ref=src, dst_ref=dst, send_sem=send_sem, recv_sem=recv_sem,
                                        device_id=to, device_id_type=pl.DeviceIdType.MESH)


class _Carry:
    def __init__(self, kind, arrs, bcast=None):
        n = len(arrs)
        self.kind, self.arrs, self.n = kind, list(arrs), n
        self.bcast = [kind == "gather"] * n if bcast is None else list(bcast)
        self.out_shape = [jax.ShapeDtypeStruct(((N_DEV,) + a.shape) if b else a.shape, a.dtype)
                          for a, b in zip(arrs, self.bcast)]
        self.units = [(a, None, None) for a in range(n)]
        if kind == "gather":
            self.units = []
            for a, arr in enumerate(arrs):
                pieces = GATHER_PIECES if arr.shape[0] >= GATHER_PIECES * 256 else 1
                rows = arr.shape[0] // pieces
                self.units += [(a, p * rows, rows) if pieces > 1 else (a, None, None) for p in range(pieces)]
        nu = len(self.units)
        self.scratch = [pltpu.SemaphoreType.DMA((nu, 7)), pltpu.SemaphoreType.DMA((nu, 7)), pltpu.SemaphoreType.DMA((n,))]
        self.result = None

    def _src(self, ins, a, d):
        return ins[a] if self.bcast[a] else ins[a].at[d]

    def _local(self, ins, outs, sems):
        me = _lin(*_mesh_pos())
        return [pltpu.make_async_copy(self._src(ins, a, me), outs[a].at[me], sems[2].at[a]) for a in range(self.n)]

    @staticmethod
    def _chips():
        x, y, _ = _mesh_pos()
        return [(1 - x, y), (x, 1 - y), (1 - x, 1 - y)]

    def _g_copy(self, ins, outs, sems, u, k, block, to, own=False):
        a, row0, rows = self.units[u]
        piece = (lambda r: r) if row0 is None else (lambda r: r.at[pl.ds(row0, rows)])
        dst = piece(outs[a].at[_lin(*block)])
        return _remote(piece(ins[a]) if own else dst, dst, sems[0].at[u, k], sems[1].at[u, k], to)

    def _g_first(self, ins, outs, sems):
        x, y, c = _mesh_pos()
        cps = []
        for u in range(len(self.units)):
            cps.append(self._g_copy(ins, outs, sems, u, 0, (x, y, c), (x, y, 1 - c), own=True))
            cps += [self._g_copy(ins, outs, sems, u, 1 + j, (x, y, c), (*chip, c), own=True)
                    for j, chip in enumerate(self._chips())]
        return cps

    def _g_passed(self, ins, outs, sems):
        x, y, c = _mesh_pos()
        return [self._g_copy(ins, outs, sems, u, 4 + j, (*chip, c), (x, y, 1 - c))
                for u in range(len(self.units)) for j, chip in enumerate(self._chips())]

    @staticmethod
    def _peer(k):
        x, y, c = _mesh_pos()
        return (1 - x if k & 4 else x, 1 - y if k & 2 else y, 1 - c if k & 1 else c)

    def _x_sends(self, ins, outs, sems):
        me = _lin(*_mesh_pos())
        return [_remote(self._src(ins, a, _lin(*self._peer(k))), outs[a].at[me], sems[0].at[a, k - 1],
                        sems[1].at[a, k - 1], self._peer(k)) for k in range(1, N_DEV) for a in range(self.n)]

    def start(self, ins, outs, sems):
        for cp in self._local(ins, outs, sems):
            cp.start()
        for cp in (self._g_first if self.kind == "gather" else self._x_sends)(ins, outs, sems):
            cp.start()

    def mid(self, ins, outs, sems):
        if self.kind != "gather":
            return
        x, y, c = _mesh_pos()
        passed = self._g_passed(ins, outs, sems)
        for u in range(len(self.units)):
            for j, chip in enumerate(self._chips()):
                self._g_copy(ins, outs, sems, u, 1 + j, (*chip, c), (x, y, c)).wait_recv()
                passed[3 * u + j].start()

    def finish(self, ins, outs, sems):
        x, y, c = _mesh_pos()
        if self.kind == "gather":
            for u in range(len(self.units)):
                self._g_copy(ins, outs, sems, u, 0, (x, y, 1 - c), (x, y, c)).wait_recv()
                for j, chip in enumerate(self._chips()):
                    self._g_copy(ins, outs, sems, u, 4 + j, (*chip, 1 - c), (x, y, c)).wait_recv()
            sent = self._g_first(ins, outs, sems) + self._g_passed(ins, outs, sems)
        else:
            for k in range(1, N_DEV):
                for a in range(self.n):
                    got = outs[a].at[_lin(*self._peer(k))]
                    _remote(got, got, sems[0].at[a, k - 1], sems[1].at[a, k - 1], self._peer(k)).wait_recv()
            sent = self._x_sends(ins, outs, sems)
        for cp in sent:
            cp.wait_send()
        for cp in self._local(ins, outs, sems):
            cp.wait()


def _comm_call(name, carry):
    n = carry.n

    def body(*refs):
        ins, outs, sems = refs[:n], refs[n:2 * n], refs[2 * n:]
        carry.start(ins, outs, sems)
        carry.mid(ins, outs, sems)
        carry.finish(ins, outs, sems)

    hbm = pl.BlockSpec(memory_space=pl.ANY)
    return pl.pallas_call(body, name=name, in_specs=[hbm] * n, out_specs=[hbm] * n, out_shape=carry.out_shape,
                          scratch_shapes=carry.scratch)(*carry.arrs)


def _call(name, body, *, grid, in_specs, out_specs, out_shape, args, sem, scratch=(), aliases=None, carry=None):
    aliases = aliases or {}
    if carry is None:
        return pl.pallas_call(body, name=name, grid=grid, in_specs=in_specs, out_specs=out_specs, out_shape=out_shape,
                              scratch_shapes=list(scratch), input_output_aliases=aliases, compiler_params=_cp(*sem))(*args)
    ni, no, ns, nc = len(in_specs), len(out_specs), len(scratch), carry.n
    total = 1
    for g in grid:
        total *= g

    def full(*refs):
        ins, cins = refs[:ni], refs[ni:ni + nc]
        outs, couts = refs[ni + nc:ni + nc + no], refs[ni + nc + no:ni + 2 * nc + no]
        scr, sems = refs[ni + 2 * nc + no:ni + 2 * nc + no + ns], refs[ni + 2 * nc + no + ns:]
        step = pl.program_id(0)
        for d in range(1, len(grid)):
            step = step * grid[d] + pl.program_id(d)

        @pl.when(step == 0)
        def _():
            carry.start(cins, couts, sems)

        body(*ins, *outs, *scr)

        @pl.when(step == total // 2)
        def _():
            carry.mid(cins, couts, sems)

        @pl.when(step == total - 1)
        def _():
            carry.finish(cins, couts, sems)

    hbm = pl.BlockSpec(memory_space=pl.ANY)
    res = pl.pallas_call(
        full, name=name, grid=grid, in_specs=list(in_specs) + [hbm] * nc, out_specs=list(out_specs) + [hbm] * nc,
        out_shape=list(out_shape) + carry.out_shape, scratch_shapes=list(scratch) + carry.scratch,
        input_output_aliases=aliases, compiler_params=_cp(*["arbitrary"] * len(grid)))(*args, *carry.arrs)
    carry.result = list(res[no:])
    return list(res[:no])


def _matmul(name, a, b, *, dims, grid, a_spec, b_spec, out_sds, out_spec, acc_shape, add=None, add_spec=None, carry=None):
    nk = grid[2]
    has_add = add is not None
    in_place = nk > 1 and not has_add and out_sds.dtype == F32

    def body(*refs):
        a_ref, b_ref = refs[0], refs[1]
        o_ref = refs[2 + has_add]
        def finish(r):
            if has_add:
                r = r + refs[2][...]
            o_ref[...] = r.astype(o_ref.dtype)

        if nk == 1:
            finish(_dot(a_ref[...].astype(BF16), b_ref[...].astype(BF16), dims))
        elif in_place:
            @pl.when(pl.program_id(2) == 0)
            def _():
                o_ref[...] = jnp.zeros_like(o_ref)

            o_ref[...] += _dot(a_ref[...].astype(BF16), b_ref[...].astype(BF16), dims)
        else:
            acc_ref = refs[-1]
            k = pl.program_id(2)

            @pl.when(k == 0)
            def _():
                acc_ref[...] = jnp.zeros_like(acc_ref)

            acc_ref[...] += _dot(a_ref[...].astype(BF16), b_ref[...].astype(BF16), dims)

            @pl.when(k == nk - 1)
            def _():
                finish(acc_ref[...])

    in_specs = [a_spec, b_spec] + ([add_spec] if has_add else [])
    args = (a, b) + ((add,) if has_add else ())
    return _call(name, body, grid=grid, in_specs=in_specs, out_specs=[out_spec], out_shape=[out_sds], args=args,
                 sem=("parallel", "parallel", "arbitrary"), scratch=[pltpu.VMEM(acc_shape, F32)] if nk > 1 and not in_place else [],
                 carry=carry)[0]


def _blk(per):
    return per if per <= 1024 else per // 2


def _pick_tiles(M, N, K, a_dtype, b_dtype, out_dtype, has_add, tn=None, tk=None):
    isz = lambda dt: jnp.dtype(dt).itemsize
    tms = sorted({_tile(M, t) for t in (2048, 1024, 512)}, reverse=True)
    tks = [tk] if tk else sorted({_tile(K, t) for t in (2048, 1024, 512)}, reverse=True)
    tns = [tn] if tn else sorted({_tile(N, t) for t in (1024, 512)}, reverse=True)
    for m in tms:
        for k in tks:
            for n in tns:
                blocks = m * k * isz(a_dtype) + k * n * isz(b_dtype) + m * n * (isz(out_dtype) + (4 if has_add else 0))
                if 2 * blocks + (m * n * 4 if K > k else 0) <= MATMUL_VMEM:
                    return m, n, k
    raise ValueError("no matmul tiling fits")


def _mm_nn(name, a, b, out_dtype, *, add=None, carry=None):
    M, K = a.shape
    if b.ndim == 3:
        per = b.shape[2]
        N = N_DEV * per
        tm, tn, tk = _pick_tiles(M, N, K, a.dtype, b.dtype, out_dtype, add is not None, tn=_blk(per))
        q = per // tn
        b_spec = pl.BlockSpec((None, tk, tn), lambda m, n, k: (n // q, k, n % q))
    else:
        N = b.shape[1]
        tm, tn, tk = _pick_tiles(M, N, K, a.dtype, b.dtype, out_dtype, add is not None)
        b_spec = pl.BlockSpec((tk, tn), lambda m, n, k: (k, n))
    return _matmul(
        name, a, b, dims=NN, grid=(M // tm, N // tn, K // tk),
        a_spec=pl.BlockSpec((tm, tk), lambda m, n, k: (m, k)), b_spec=b_spec,
        out_sds=jax.ShapeDtypeStruct((M, N), out_dtype), out_spec=pl.BlockSpec((tm, tn), lambda m, n, k: (m, n)),
        acc_shape=(tm, tn), add=add, add_spec=pl.BlockSpec((tm, tn), lambda m, n, k: (m, n)), carry=carry)


def _mm_nt(name, a, b, out_dtype, *, add=None, carry=None):
    M, K = a.shape
    if b.ndim == 3:
        N = b.shape[1]
        tm, tn, tk = _pick_tiles(M, N, K, a.dtype, b.dtype, out_dtype, add is not None, tk=b.shape[2])
        if tn < 1024 < tm:
            tm, tn = 1024, N
        b_spec = pl.BlockSpec((None, tn, tk), lambda m, n, k: (k, n, 0))
    else:
        N = b.shape[0]
        tm, tn, tk = _pick_tiles(M, N, K, a.dtype, b.dtype, out_dtype, add is not None)
        b_spec = pl.BlockSpec((tn, tk), lambda m, n, k: (n, k))
    return _matmul(
        name, a, b, dims=NT, grid=(M // tm, N // tn, K // tk),
        a_spec=pl.BlockSpec((tm, tk), lambda m, n, k: (m, k)), b_spec=b_spec,
        out_sds=jax.ShapeDtypeStruct((M, N), out_dtype), out_spec=pl.BlockSpec((tm, tn), lambda m, n, k: (m, n)),
        acc_shape=(tm, tn), add=add, add_spec=pl.BlockSpec((tm, tn), lambda m, n, k: (m, n)), carry=carry)


def _mm_tn(name, a, b, out_dtype, *, per=None, rows=None, carry=None):
    K, M = a.shape
    N = b.shape[1]
    row0 = 0
    if rows is not None:
        row0, M = rows
    tm, tn, tk = _pick_tiles(M, N, K, a.dtype, b.dtype, out_dtype, False, tn=_blk(per) if per else None)
    m0 = row0 // tm
    if per is not None:
        q = per // tn
        out_sds = jax.ShapeDtypeStruct((N_DEV, M, per), out_dtype)
        out_spec = pl.BlockSpec((None, tm, tn), lambda m, n, k: (n // q, m, n % q))
    else:
        out_sds = jax.ShapeDtypeStruct((M, N), out_dtype)
        out_spec = pl.BlockSpec((tm, tn), lambda m, n, k: (m, n))
    return _matmul(
        name, a, b, dims=TN, grid=(M // tm, N // tn, K // tk),
        a_spec=pl.BlockSpec((tk, tm), lambda m, n, k: (k, m + m0)), b_spec=pl.BlockSpec((tk, tn), lambda m, n, k: (k, n)),
        out_sds=out_sds, out_spec=out_spec, acc_shape=(tm, tn), carry=carry)


def _rms_fwd(name, h, g):
    T, D = h.shape
    tb = _tile(T, 512)

    def body(h_ref, g_ref, o_ref):
        x = h_ref[...]
        r = lax.rsqrt(jnp.mean(x * x, axis=-1, keepdims=True) + EPS)
        o_ref[...] = (x * r * g_ref[...]).astype(o_ref.dtype)

    return pl.pallas_call(
        body, name=name, grid=(T // tb,),
        in_specs=[pl.BlockSpec((tb, D), lambda i: (i, 0)), pl.BlockSpec((1, D), lambda i: (0, 0))],
        out_specs=pl.BlockSpec((tb, D), lambda i: (i, 0)), out_shape=jax.ShapeDtypeStruct((T, D), BF16),
        compiler_params=_cp("parallel"))(h, g)


def _rms_bwd(name, dhn, h, g, dres):
    T, D = h.shape
    tb = _tile(T, 256)

    def body(dhn_ref, h_ref, g_ref, dres_ref, dh_ref, dg_ref):
        i = pl.program_id(0)
        x = h_ref[...]
        r = lax.rsqrt(jnp.mean(x * x, axis=-1, keepdims=True) + EPS)
        y = x * r
        d = dhn_ref[...]
        dy = d * g_ref[...]
        dh_ref[...] = dres_ref[...] + r * (dy - y * jnp.mean(dy * y, axis=-1, keepdims=True))
        part = jnp.sum(d * y, axis=0, keepdims=True)

        @pl.when(i == 0)
        def _():
            dg_ref[...] = part

        @pl.when(i > 0)
        def _():
            dg_ref[...] += part

    row = pl.BlockSpec((tb, D), lambda i: (i, 0))
    vec = pl.BlockSpec((1, D), lambda i: (0, 0))
    return pl.pallas_call(
        body, name=name, grid=(T // tb,), in_specs=[row, row, vec, row], out_specs=[row, vec],
        out_shape=[jax.ShapeDtypeStruct((T, D), F32), jax.ShapeDtypeStruct((1, D), F32)],
        compiler_params=_cp("arbitrary"))(dhn, h, g, dres)


def _loss_head(h, tgt, g):
    T, D = h.shape
    tb = _tile(T, 256)

    def body(h_ref, t_ref, g_ref, loss_ref, dh_ref, dg_ref):
        i = pl.program_id(0)
        x = h_ref[...]
        gg = g_ref[...]
        r = lax.rsqrt(jnp.mean(x * x, axis=-1, keepdims=True) + EPS)
        y0 = x * r
        err = y0 * gg - t_ref[...]
        tot = 0.5 * jnp.sum(jnp.mean(err * err, axis=-1, keepdims=True), axis=0, keepdims=True)
        dy = err * (1.0 / D)
        dyg = dy * gg
        dh_ref[...] = r * (dyg - y0 * jnp.mean(dyg * y0, axis=-1, keepdims=True))
        part = jnp.sum(dy * y0, axis=0, keepdims=True)
        tot = jnp.broadcast_to(tot, loss_ref.shape)

        @pl.when(i == 0)
        def _():
            dg_ref[...] = part
            loss_ref[...] = tot

        @pl.when(i > 0)
        def _():
            dg_ref[...] += part
            loss_ref[...] += tot

    row = pl.BlockSpec((tb, D), lambda i: (i, 0))
    vec = pl.BlockSpec((1, D), lambda i: (0, 0))
    return pl.pallas_call(
        body, name="loss_head", grid=(T // tb,), in_specs=[row, row, vec],
        out_specs=[pl.BlockSpec((SUBLANE, LANE), lambda i: (0, 0)), row, vec],
        out_shape=[jax.ShapeDtypeStruct((SUBLANE, LANE), F32), jax.ShapeDtypeStruct((T, D), F32),
                   jax.ShapeDtypeStruct((1, D), F32)],
        compiler_params=_cp("arbitrary"))(h, tgt, g)


def _rel_onehot(pos_axis, shape):
    pos = lax.broadcasted_iota(I32, shape, pos_axis)
    r = lax.broadcasted_iota(I32, shape, 1 - pos_axis)
    d = jnp.where(pos < KW, N_PAST * CHUNK - pos, N_PAST * CHUNK + ROLL_W - pos)
    return (jnp.clip(d, -MAX_REL, MAX_REL) + MAX_REL == r).astype(F32)


def _roll_rows(x, left):
    row = lax.broadcasted_iota(I32, x.shape, 0)
    for b in range(QB.bit_length() - 1):
        shift = (ROLL_W - (1 << b)) if left else (1 << b)
        x = jnp.where(((row >> b) & 1) == 1, pltpu.roll(x, shift, 1), x)
    return x


def _bias_tile(rel_bias):
    H = rel_bias.shape[0]
    rb = jnp.pad(rel_bias, ((0, 0), (0, REL_PAD - rel_bias.shape[1]))).reshape(H, 1, REL_PAD)

    def body(rb_ref, o_ref):
        row = jnp.broadcast_to(rb_ref[...], (SUBLANE, REL_PAD))
        base = jnp.dot(row, _rel_onehot(1, (REL_PAD, ROLL_W)), precision=lax.Precision.HIGHEST,
                       preferred_element_type=F32)[0:1]
        tile = _roll_rows(jnp.broadcast_to(base, (QB, ROLL_W)), left=False)[:, :KW]
        qc = lax.broadcasted_iota(I32, (QB, KW), 0) // CHUNK
        kc = lax.broadcasted_iota(I32, (QB, KW), 1) // CHUNK - N_PAST
        o_ref[...] = jnp.where((kc >= qc - N_PAST) & (kc <= qc), tile, NEG)

    return pl.pallas_call(
        body, name="bias_tile", grid=(H,),
        in_specs=[pl.BlockSpec((None, 1, REL_PAD), lambda h: (h, 0, 0))],
        out_specs=pl.BlockSpec((None, QB, KW), lambda h: (h, 0, 0)),
        out_shape=jax.ShapeDtypeStruct((H, QB, KW), F32), compiler_params=_cp("parallel"))(rb)


def _bias_tile_grad(dtile, n_rel):
    H = dtile.shape[0]

    def body(dt_ref, o_ref):
        x = jnp.concatenate([dt_ref[...], jnp.zeros((QB, ROLL_W - KW), F32)], axis=1)
        cs = jnp.sum(_roll_rows(x, left=True), axis=0, keepdims=True)
        o_ref[...] = jnp.dot(jnp.broadcast_to(cs, (SUBLANE, ROLL_W)), _rel_onehot(0, (ROLL_W, REL_PAD)),
                             precision=lax.Precision.HIGHEST, preferred_element_type=F32)[0:1]

    out = pl.pallas_call(
        body, name="bias_tile_grad", grid=(H,),
        in_specs=[pl.BlockSpec((None, QB, KW), lambda h: (h, 0, 0))],
        out_specs=pl.BlockSpec((None, 1, REL_PAD), lambda h: (h, 0, 0)),
        out_shape=jax.ShapeDtypeStruct((H, 1, REL_PAD), F32), compiler_params=_cp("parallel"))(dtile)
    return out.reshape(H, REL_PAD)[:, :n_rel]


def _band_specs(G):
    spec = lambda f: pl.BlockSpec((QB, BAND_W), f)
    q = spec(lambda h, i: (i, h))
    ks = [spec(lambda h, i, r=r: (jnp.maximum(i - 2 + r, 0), G + h)) for r in range(3)]
    vs = [spec(lambda h, i, r=r: (jnp.maximum(i - 2 + r, 0), 2 * G + h)) for r in range(3)]
    gate = spec(lambda h, i: (i, 5 * G + h))
    bias = pl.BlockSpec((HB, QB, KW), lambda h, i: (h, 0, 0))
    return [q] + ks + vs + [gate, bias]


def _band_probs(i, hh, q_ref, k_refs, v_refs, bias_ref):
    cols = slice(hh * HEAD_A, (hh + 1) * HEAD_A)
    q = q_ref[:, cols].astype(BF16)
    k = jnp.concatenate([r[:, cols] for r in k_refs], axis=0).astype(BF16)
    v = jnp.concatenate([r[:, cols] for r in v_refs], axis=0).astype(BF16)
    s = _dot(q, k, NT) * (HEAD_A ** -0.5) + bias_ref[hh]
    kpos = (i - 2) * QB + lax.broadcasted_iota(I32, (1, KW), 1)
    s = jnp.where(kpos >= 0, s, NEG)
    e = jnp.exp(s - jnp.max(s, axis=-1, keepdims=True))
    p = e * (1.0 / jnp.sum(e, axis=-1, keepdims=True))
    return p, q, k, v


def _band_attn_fwd(proj, bias_tile, carry=None):
    T = proj.shape[0]
    H = bias_tile.shape[0]

    def body(q_ref, k0, k1, k2, v0, v1, v2, gate_ref, bias_ref, ya_ref, y_ref):
        for hh in range(HB):
            cols = slice(hh * HEAD_A, (hh + 1) * HEAD_A)
            p, _, _, v = _band_probs(pl.program_id(1), hh, q_ref, (k0, k1, k2), (v0, v1, v2), bias_ref)
            o = _dot(p.astype(BF16), v, NN)
            g = gate_ref[:, cols]
            ya_ref[:, cols] = o
            y_ref[:, cols] = (o * (g * _sigmoid(g))).astype(y_ref.dtype)

    out = pl.BlockSpec((QB, BAND_W), lambda h, i: (i, h))
    return _call(
        "band_attn_fwd", body, grid=(H // HB, T // QB), in_specs=_band_specs(H // HB), out_specs=[out, out],
        out_shape=[jax.ShapeDtypeStruct((T, H * HEAD_A), F32), jax.ShapeDtypeStruct((T, 2 * H * HEAD_A), BF16)],
        args=[proj] * 8 + [bias_tile], sem=("parallel", "parallel"), carry=carry)


def _band_attn_bwd(proj, bias_tile, dy, ya, carry=None):
    T = proj.shape[0]
    H = bias_tile.shape[0]
    n_i = T // QB

    def body(q_ref, k0, k1, k2, v0, v1, v2, gate_ref, bias_ref, dy_ref, ya_ref,
             dq_ref, dk_ref, dv_ref, dgate_ref, dbias_ref, dk_acc, dv_acc):
        i = pl.program_id(1)

        @pl.when(i == 0)
        def _():
            dk_acc[...] = jnp.zeros_like(dk_acc)
            dv_acc[...] = jnp.zeros_like(dv_acc)
            dbias_ref[...] = jnp.zeros_like(dbias_ref)

        for hh in range(HB):
            cols = slice(hh * HEAD_A, (hh + 1) * HEAD_A)
            p, q, k, v = _band_probs(i, hh, q_ref, (k0, k1, k2), (v0, v1, v2), bias_ref)
            g = gate_ref[:, cols]
            sg = _sigmoid(g)
            dyv = dy_ref[:, cols]
            dgate_ref[:, cols] = (dyv * ya_ref[:, cols] * _dsilu(g, sg)).astype(dgate_ref.dtype)
            do = (dyv * (g * sg)).astype(BF16)
            dp = _dot(do, v, NT)
            ds = p * (dp - jnp.sum(dp * p, axis=-1, keepdims=True))
            dbias_ref[hh] += ds
            dsb = (ds * (HEAD_A ** -0.5)).astype(BF16)
            dq_ref[:, cols] = _dot(dsb, k, NN).astype(dq_ref.dtype)
            dkc = _dot(dsb, q, TN)
            dvc = _dot(p.astype(BF16), do, TN)
            for r in range(3):
                blk = i - 2 + r

                @pl.when(blk >= 0)
                def _(r=r, blk=blk, cols=cols, dkc=dkc, dvc=dvc):
                    rows = pl.ds(pl.multiple_of(blk * QB, QB), QB)
                    dk_acc[rows, cols] += dkc[r * QB:(r + 1) * QB]
                    dv_acc[rows, cols] += dvc[r * QB:(r + 1) * QB]

        @pl.when(i == n_i - 1)
        def _():
            dk_ref[...] = dk_acc[...].astype(dk_ref.dtype)
            dv_ref[...] = dv_acc[...].astype(dv_ref.dtype)

    blk = pl.BlockSpec((QB, BAND_W), lambda h, i: (i, h))
    col = pl.BlockSpec((T, BAND_W), lambda h, i: (0, h))
    sds = jax.ShapeDtypeStruct((T, H * HEAD_A), BF16)
    return _call(
        "band_attn_bwd", body, grid=(H // HB, n_i), in_specs=_band_specs(H // HB) + [blk, blk],
        out_specs=[blk, col, col, blk, pl.BlockSpec((HB, QB, KW), lambda h, i: (h, 0, 0))],
        out_shape=[sds, sds, sds, sds, jax.ShapeDtypeStruct((H, QB, KW), F32)],
        args=[proj] * 8 + [bias_tile, dy, ya], sem=("parallel", "arbitrary"),
        scratch=[pltpu.VMEM((T, BAND_W), F32), pltpu.VMEM((T, BAND_W), F32)], carry=carry)


def _conv_in_specs(tb, C):
    per = tb // HALO
    nb = C // CONV_LC
    prev = lambda i: jnp.maximum(i * per - 1, 0)
    return [pl.BlockSpec((tb, CONV_LC), lambda c, i: (i, 3 * nb + c)), pl.BlockSpec((tb, CONV_LC), lambda c, i: (i, 4 * nb + c)),
            pl.BlockSpec((HALO, CONV_LC), lambda c, i: (prev(i), 3 * nb + c)),
            pl.BlockSpec((HALO, CONV_LC), lambda c, i: (prev(i), 4 * nb + c))]


def _conv_tb(T):
    return _tile(T, 1024) if T > 1024 else T // 2


def _glu_with_halo(i, ga_ref, gb_ref, ha_ref, hb_ref, scr):
    tb = ga_ref.shape[0]
    halo = ha_ref[...] * _sigmoid(hb_ref[...])
    scr[0:HALO, :] = jnp.where(i > 0, halo, 0.0)
    scr[HALO:HALO + tb, :] = ga_ref[...] * _sigmoid(gb_ref[...])
    scr[HALO + tb:, :] = jnp.zeros((SUBLANE, scr.shape[1]), F32)


def _chunk_taps(src, w_ref, row0, tap_of, uscr):
    acc = None
    for r in range(SUBLANE):
        u = None
        for a in range(HALO // SUBLANE + 1):
            j = tap_of(SUBLANE * a + r)
            if 0 <= j < CONV_K:
                rows = pl.ds(pl.multiple_of(row0 + SUBLANE * a, SUBLANE), CONV_RC + SUBLANE)
                t = w_ref[j:j + 1, :] * src[rows, :]
                u = t if u is None else u + t
        if u is None:
            continue
        if r == 0:
            piece = u[0:CONV_RC]
        else:
            uscr[...] = u
            piece = uscr[pl.ds(r, CONV_RC), :]
        acc = piece if acc is None else acc + piece
    return acc


def _layernorm_stats(z):
    mu = jnp.mean(z, axis=-1, keepdims=True)
    zc = z - mu
    rstd = lax.rsqrt(jnp.mean(zc * zc, axis=-1, keepdims=True) + EPS)
    return zc * rstd, rstd


def _conv_z(proj, conv_w, conv_b):
    T = proj.shape[0]
    C = conv_w.shape[1]
    tb = _conv_tb(T)

    def body(ga_ref, gb_ref, ha_ref, hb_ref, w_ref, cb_ref, z_ref, scr, uscr):
        _glu_with_halo(pl.program_id(1), ga_ref, gb_ref, ha_ref, hb_ref, scr)
        cb = cb_ref[...]

        def chunk(rc, carry):
            row0 = pl.multiple_of(rc * CONV_RC, CONV_RC)
            z = _chunk_taps(scr, w_ref, row0, lambda o: o - (HALO - (CONV_K - 1)), uscr)
            z_ref[pl.ds(row0, CONV_RC), :] = z + cb
            return carry

        lax.fori_loop(0, tb // CONV_RC, chunk, 0)

    return pl.pallas_call(
        body, name="conv_z", grid=(C // CONV_LC, T // tb),
        in_specs=_conv_in_specs(tb, C) + [pl.BlockSpec((HALO, CONV_LC), lambda c, i: (0, c)),
                                          pl.BlockSpec((1, CONV_LC), lambda c, i: (0, c))],
        out_specs=pl.BlockSpec((tb, CONV_LC), lambda c, i: (i, c)), out_shape=jax.ShapeDtypeStruct((T, C), F32),
        scratch_shapes=[pltpu.VMEM((HALO + tb + SUBLANE, CONV_LC), F32), pltpu.VMEM((CONV_RC + SUBLANE, CONV_LC), F32)],
        compiler_params=_cp("parallel", "parallel"))(proj, proj, proj, proj, conv_w, conv_b)


def _conv_ln_fwd(proj, z, y, ln_g, ln_b):
    T, C = z.shape
    tb = _tile(T, 256)

    def body(z_ref, gate_ref, g_ref, b_ref, y_in, y_ref):
        xhat, _ = _layernorm_stats(z_ref[...])
        ln = xhat * g_ref[...] + b_ref[...]
        gate = gate_ref[...]
        y_ref[...] = (ln * _sigmoid(ln) * (gate * _sigmoid(gate))).astype(y_ref.dtype)

    vec = pl.BlockSpec((1, C), lambda i: (0, 0))
    return pl.pallas_call(
        body, name="conv_ln_fwd", grid=(T // tb,),
        in_specs=[pl.BlockSpec((tb, C), lambda i: (i, 0)), pl.BlockSpec((tb, C), lambda i: (i, 6)), vec, vec,
                  pl.BlockSpec(memory_space=pl.ANY)],
        out_specs=pl.BlockSpec((tb, C), lambda i: (i, 1)), out_shape=jax.ShapeDtypeStruct(y.shape, y.dtype),
        input_output_aliases={4: 0}, compiler_params=_cp("parallel"))(z, proj, ln_g, ln_b, y)


def _conv_bwd_ln(proj, z, dy, ln_g, ln_b):
    T, C = z.shape
    tb = _tile(T, 256)

    def body(z_ref, gate_ref, dy_ref, g_ref, b_ref, dz_ref, dgate_ref, dg_ref, db_ref, dcb_ref):
        i = pl.program_id(0)
        xhat, rstd = _layernorm_stats(z_ref[...])
        ln = xhat * g_ref[...] + b_ref[...]
        sl = _sigmoid(ln)
        gate = gate_ref[...]
        sg = _sigmoid(gate)
        dyv = dy_ref[...]
        dgate_ref[...] = (dyv * (ln * sl) * _dsilu(gate, sg)).astype(dgate_ref.dtype)
        dln = dyv * (gate * sg) * _dsilu(ln, sl)
        dxh = dln * g_ref[...]
        dz = rstd * (dxh - jnp.mean(dxh, axis=-1, keepdims=True) - xhat * jnp.mean(dxh * xhat, axis=-1, keepdims=True))
        dz_ref[...] = dz
        parts = (jnp.sum(dln * xhat, axis=0, keepdims=True), jnp.sum(dln, axis=0, keepdims=True),
                 jnp.sum(dz, axis=0, keepdims=True))

        @pl.when(i == 0)
        def _():
            for ref, part in zip((dg_ref, db_ref, dcb_ref), parts):
                ref[...] = part

        @pl.when(i > 0)
        def _():
            for ref, part in zip((dg_ref, db_ref, dcb_ref), parts):
                ref[...] += part

    vec = pl.BlockSpec((1, C), lambda i: (0, 0))
    row = pl.BlockSpec((tb, C), lambda i: (i, 0))
    vsd = jax.ShapeDtypeStruct((1, C), F32)
    return pl.pallas_call(
        body, name="conv_bwd_ln", grid=(T // tb,),
        in_specs=[row, pl.BlockSpec((tb, C), lambda i: (i, 6)), pl.BlockSpec((tb, C), lambda i: (i, 1)), vec, vec],
        out_specs=[row, row, vec, vec, vec],
        out_shape=[jax.ShapeDtypeStruct((T, C), F32), jax.ShapeDtypeStruct((T, C), BF16), vsd, vsd, vsd],
        compiler_params=_cp("arbitrary"))(z, proj, dy, ln_g, ln_b)


def _conv_bwd_taps(proj, dz, conv_w):
    T = proj.shape[0]
    C = conv_w.shape[1]
    tb = _conv_tb(T)
    per = tb // HALO
    n_i = T // tb
    first = HALO - (CONV_K - 1)

    def body(ga_ref, gb_ref, ha_ref, hb_ref, dz_ref, dzn_ref, w_ref, da_ref, db_ref, dw_ref, scr, dscr, uscr, zscr, dwacc):
        i = pl.program_id(1)
        _glu_with_halo(i, ga_ref, gb_ref, ha_ref, hb_ref, scr)
        dscr[0:tb, :] = dz_ref[...]
        dscr[tb:tb + HALO, :] = jnp.where(i < n_i - 1, dzn_ref[...], 0.0)
        dscr[tb + HALO:, :] = jnp.zeros((SUBLANE, CONV_LC), F32)
        zscr[0:SUBLANE, :] = jnp.zeros((SUBLANE, CONV_LC), F32)
        zscr[SUBLANE + CONV_RC:, :] = jnp.zeros((SUBLANE, CONV_LC), F32)

        @pl.when(i == 0)
        def _():
            dwacc[...] = jnp.zeros_like(dwacc)

        def chunk(rc, carry):
            row0 = pl.multiple_of(rc * CONV_RC, CONV_RC)
            rows = pl.ds(row0, CONV_RC)
            dglu = _chunk_taps(dscr, w_ref, row0, lambda o: CONV_K - 1 - o, uscr)
            ga = ga_ref[rows, :]
            sb = _sigmoid(gb_ref[rows, :])
            da_ref[rows, :] = (dglu * sb).astype(da_ref.dtype)
            db_ref[rows, :] = (dglu * ga * sb * (1.0 - sb)).astype(db_ref.dtype)
            zscr[SUBLANE:SUBLANE + CONV_RC, :] = dz_ref[rows, :]
            for r in range(SUBLANE):
                dzs = zscr[pl.ds(SUBLANE - r, CONV_RC + SUBLANE), :]
                for a in range(HALO // SUBLANE + 1):
                    j = SUBLANE * a + r - first
                    if 0 <= j < CONV_K:
                        src = pl.ds(pl.multiple_of(row0 + SUBLANE * a, SUBLANE), CONV_RC + SUBLANE)
                        p = dzs * scr[src, :]
                        f = p[0:SUBLANE]
                        for s in range(1, CONV_RC // SUBLANE + 1):
                            f = f + p[s * SUBLANE:(s + 1) * SUBLANE]
                        dwacc[j * SUBLANE:(j + 1) * SUBLANE, :] += f
            return carry

        lax.fori_loop(0, tb // CONV_RC, chunk, 0)

        @pl.when(i == n_i - 1)
        def _():
            dw_ref[...] = jnp.zeros_like(dw_ref)
            for j in range(CONV_K):
                dw_ref[j:j + 1, :] = jnp.sum(dwacc[j * SUBLANE:(j + 1) * SUBLANE, :], axis=0, keepdims=True)

    blk = pl.BlockSpec((tb, CONV_LC), lambda c, i: (i, c))
    wspec = pl.BlockSpec((HALO, CONV_LC), lambda c, i: (0, c))
    nxt = pl.BlockSpec((HALO, CONV_LC), lambda c, i: (jnp.minimum((i + 1) * per, T // HALO - 1), c))
    return pl.pallas_call(
        body, name="conv_bwd_taps", grid=(C // CONV_LC, n_i),
        in_specs=_conv_in_specs(tb, C) + [blk, nxt, wspec], out_specs=[blk, blk, wspec],
        out_shape=[jax.ShapeDtypeStruct((T, C), BF16), jax.ShapeDtypeStruct((T, C), BF16),
                   jax.ShapeDtypeStruct((HALO, C), F32)],
        scratch_shapes=[pltpu.VMEM((HALO + tb + SUBLANE, CONV_LC), F32), pltpu.VMEM((tb + HALO + SUBLANE, CONV_LC), F32),
                        pltpu.VMEM((CONV_RC + SUBLANE, CONV_LC), F32), pltpu.VMEM((CONV_RC + 2 * SUBLANE, CONV_LC), F32),
                        pltpu.VMEM((HALO * SUBLANE, CONV_LC), F32)],
        compiler_params=_cp("parallel", "arbitrary"))(proj, proj, proj, proj, dz, dz, conv_w)


def _sgu_mask():
    r = lax.broadcasted_iota(I32, (GMLP_CHUNK, GMLP_CHUNK), 0) // CHUNK
    c = lax.broadcasted_iota(I32, (GMLP_CHUNK, GMLP_CHUNK), 1) // CHUNK
    return r >= c


def _sgu_fwd(proj, ln_g, ln_b, w_s, b_s_t):
    T = proj.shape[0]
    W = ln_g.shape[1]
    G = w_s.shape[0]
    cg = W // G
    tb = GMLP_CHUNK

    def body(u_ref, v_ref, gate_ref, g_ref, b_ref, ws_ref, bs_ref, y_ref):
        xhat, _ = _layernorm_stats(v_ref[...])
        vln = (xhat * g_ref[...] + b_ref[...]).astype(BF16)
        mask = _sgu_mask()
        for gi in range(G):
            cols = slice(gi * cg, (gi + 1) * cg)
            ws = jnp.where(mask, ws_ref[gi], 0.0).astype(BF16)
            sg = _dot(ws, vln[:, cols], NN) + bs_ref[:, gi:gi + 1]
            gate = gate_ref[:, cols]
            y_ref[:, cols] = (u_ref[:, cols] * sg * (gate * _sigmoid(gate))).astype(y_ref.dtype)

    vec = pl.BlockSpec((1, W), lambda i: (0, 0))
    return pl.pallas_call(
        body, name="sgu_fwd", grid=(T // tb,),
        in_specs=[pl.BlockSpec((tb, W), lambda i: (i, 0)), pl.BlockSpec((tb, W), lambda i: (i, 1)),
                  pl.BlockSpec((tb, W), lambda i: (i, 2)), vec, vec,
                  pl.BlockSpec((G, GMLP_CHUNK, GMLP_CHUNK), lambda i: (0, 0, 0)),
                  pl.BlockSpec((GMLP_CHUNK, G), lambda i: (0, 0))],
        out_specs=pl.BlockSpec((tb, W), lambda i: (i, 0)), out_shape=jax.ShapeDtypeStruct((T, W), BF16),
        compiler_params=_cp("parallel"))(proj, proj, proj, ln_g, ln_b, w_s, b_s_t)


def _sgu_bwd(proj, dy, ln_g, ln_b, w_s, b_s_t):
    T = proj.shape[0]
    W = ln_g.shape[1]
    G = w_s.shape[0]
    cg = W // G
    tb = GMLP_CHUNK

    def body(u_ref, v_ref, gate_ref, dy_ref, g_ref, b_ref, ws_ref, bs_ref,
             dp_ref, dws_ref, dbs_ref, dg_ref, db_ref, dvln_scr):
        i = pl.program_id(0)

        @pl.when(i == 0)
        def _():
            dws_ref[...] = jnp.zeros_like(dws_ref)
            dbs_ref[...] = jnp.zeros_like(dbs_ref)
            dg_ref[...] = jnp.zeros_like(dg_ref)
            db_ref[...] = jnp.zeros_like(db_ref)

        xhat, rstd = _layernorm_stats(v_ref[...])
        vln = (xhat * g_ref[...] + b_ref[...]).astype(BF16)
        mask = _sgu_mask()
        for gi in range(G):
            cols = slice(gi * cg, (gi + 1) * cg)
            ws = jnp.where(mask, ws_ref[gi], 0.0).astype(BF16)
            vg = vln[:, cols]
            sg = _dot(ws, vg, NN) + bs_ref[:, gi:gi + 1]
            gate = gate_ref[:, cols]
            s = _sigmoid(gate)
            u = u_ref[:, cols]
            dyv = dy_ref[:, cols]
            dyu = dyv * u
            dp_ref[:, cols] = (dyv * sg * (gate * s)).astype(dp_ref.dtype)
            dp_ref[:, 2 * W + gi * cg:2 * W + (gi + 1) * cg] = (dyu * sg * _dsilu(gate, s)).astype(dp_ref.dtype)
            dsg = dyu * (gate * s)
            dsgb = dsg.astype(BF16)
            dvln_scr[:, cols] = _dot(ws, dsgb, TN)
            dws_ref[gi] += _dot(dsgb, vg, NT)
            dbs_ref[:, gi:gi + 1] += jnp.sum(dsg, axis=-1, keepdims=True)
        dvln = dvln_scr[...]
        dg_ref[...] += jnp.sum(dvln * xhat, axis=0, keepdims=True)
        db_ref[...] += jnp.sum(dvln, axis=0, keepdims=True)
        dxh = dvln * g_ref[...]
        dv = rstd * (dxh - jnp.mean(dxh, axis=-1, keepdims=True) - xhat * jnp.mean(dxh * xhat, axis=-1, keepdims=True))
        dp_ref[:, W:2 * W] = dv.astype(dp_ref.dtype)

    vec = pl.BlockSpec((1, W), lambda i: (0, 0))
    wsp = pl.BlockSpec((G, GMLP_CHUNK, GMLP_CHUNK), lambda i: (0, 0, 0))
    bsp = pl.BlockSpec((GMLP_CHUNK, G), lambda i: (0, 0))
    return pl.pallas_call(
        body, name="sgu_bwd", grid=(T // tb,),
        in_specs=[pl.BlockSpec((tb, W), lambda i: (i, 0)), pl.BlockSpec((tb, W), lambda i: (i, 1)),
                  pl.BlockSpec((tb, W), lambda i: (i, 2)), pl.BlockSpec((tb, W), lambda i: (i, 0)), vec, vec, wsp, bsp],
        out_specs=[pl.BlockSpec((tb, 3 * W), lambda i: (i, 0)), wsp, bsp, vec, vec],
        out_shape=[jax.ShapeDtypeStruct((T, 3 * W), BF16), jax.ShapeDtypeStruct((G, GMLP_CHUNK, GMLP_CHUNK), F32),
                   jax.ShapeDtypeStruct((GMLP_CHUNK, G), F32), jax.ShapeDtypeStruct((1, W), F32),
                   jax.ShapeDtypeStruct((1, W), F32)],
        scratch_shapes=[pltpu.VMEM((tb, W), F32)],
        compiler_params=_cp("arbitrary"))(proj, proj, proj, dy, ln_g, ln_b, w_s, b_s_t)


def _xattn_probs(q, k, hd):
    s = _dot(q, k, NT) * (hd ** -0.5)
    e = jnp.exp(s - jnp.max(s, axis=-1, keepdims=True))
    return e * (1.0 / jnp.sum(e, axis=-1, keepdims=True))


def _xattn_fwd(name, q, k, v):
    T, D = q.shape
    M = k.shape[0]
    hd = D // N_HEADS_X
    tb = _tile(T, 512)

    def body(q_ref, k_ref, v_ref, o_ref):
        for h in range(N_HEADS_X):
            cols = slice(h * hd, (h + 1) * hd)
            p = _xattn_probs(q_ref[:, cols], k_ref[:, cols], hd)
            o_ref[:, cols] = _dot(p.astype(BF16), v_ref[:, cols], NN).astype(o_ref.dtype)

    row = pl.BlockSpec((tb, D), lambda i: (i, 0))
    kv = pl.BlockSpec((M, D), lambda i: (0, 0))
    return pl.pallas_call(
        body, name=name, grid=(T // tb,), in_specs=[row, kv, kv], out_specs=row,
        out_shape=jax.ShapeDtypeStruct((T, D), BF16), compiler_params=_cp("parallel"))(q, k, v)


def _xattn_bwd(name, q, k, v, do):
    T, D = q.shape
    M = k.shape[0]
    hd = D // N_HEADS_X
    tb = _tile(T, 512)

    def body(q_ref, k_ref, v_ref, do_ref, dq_ref, dk_ref, dv_ref):
        @pl.when(pl.program_id(0) == 0)
        def _():
            dk_ref[...] = jnp.zeros_like(dk_ref)
            dv_ref[...] = jnp.zeros_like(dv_ref)

        for h in range(N_HEADS_X):
            cols = slice(h * hd, (h + 1) * hd)
            qh, kh, doh = q_ref[:, cols], k_ref[:, cols], do_ref[:, cols]
            p = _xattn_probs(qh, kh, hd)
            dp = _dot(doh, v_ref[:, cols], NT)
            ds = p * (dp - jnp.sum(dp * p, axis=-1, keepdims=True))
            dsb = (ds * (hd ** -0.5)).astype(BF16)
            dq_ref[:, cols] = _dot(dsb, kh, NN).astype(dq_ref.dtype)
            dk_ref[:, cols] += _dot(dsb, qh, TN)
            dv_ref[:, cols] += _dot(p.astype(BF16), doh, TN)

    row = pl.BlockSpec((tb, D), lambda i: (i, 0))
    kv = pl.BlockSpec((M, D), lambda i: (0, 0))
    return pl.pallas_call(
        body, name=name, grid=(T // tb,), in_specs=[row, kv, kv, row], out_specs=[row, kv, kv],
        out_shape=[jax.ShapeDtypeStruct((T, D), BF16), jax.ShapeDtypeStruct((M, D), F32),
                   jax.ShapeDtypeStruct((M, D), F32)],
        compiler_params=_cp("arbitrary"))(q, k, v, do)


def _adamw(name, contrib, w, m, v, carry=None):
    parts = list(contrib) if isinstance(contrib, (list, tuple)) else [contrib]
    n_parts = len(parts)
    R, C = w.shape
    tr = min(min(p.shape[1] for p in parts), 128)
    while any(p.shape[1] % tr for p in parts):
        tr -= SUBLANE
    tiles = [p.shape[1] // tr for p in parts]
    first = [sum(tiles[:p]) for p in range(n_parts + 1)]

    def body(*refs):
        c_refs = refs[:n_parts]
        w_ref, m_ref, v_ref, g_ref, d_ref, nm_ref, nv_ref = refs[n_parts:]

        def update(c_ref):
            g = c_ref[0].astype(F32)
            for s in range(1, N_DEV):
                g = g + c_ref[s].astype(F32)
            nm = ADAM_B1 * m_ref[...] + (1.0 - ADAM_B1) * g
            nv = ADAM_B2 * v_ref[...] + (1.0 - ADAM_B2) * (g * g)
            m_hat = nm / (1.0 - ADAM_B1 ** ADAM_STEP)
            v_hat = nv / (1.0 - ADAM_B2 ** ADAM_STEP)
            g_ref[...] = g
            d_ref[...] = -ADAM_LR * (m_hat / (jnp.sqrt(v_hat) + ADAM_EPS) + ADAM_WD * w_ref[...])
            nm_ref[...] = nm
            nv_ref[...] = nv

        if n_parts == 1:
            update(c_refs[0])
        else:
            i = pl.program_id(0)
            for p in range(n_parts):
                pl.when((i >= first[p]) & (i < first[p + 1]))(lambda p=p: update(c_refs[p]))

    row = pl.BlockSpec((tr, C), lambda i: (i, 0))
    sds = jax.ShapeDtypeStruct((R, C), F32)
    c_specs = [pl.BlockSpec((N_DEV, tr, C), lambda i, p=p: (0, jnp.clip(i - first[p], 0, tiles[p] - 1), 0))
               for p in range(n_parts)]
    return _call(name, body, grid=(R // tr,), in_specs=c_specs + [row, row, row], out_specs=[row] * 4,
                 out_shape=[sds] * 4, args=(*parts, w, m, v), sem=("parallel",), carry=carry)


def _pack(arrs):
    unit = SUBLANE * LANE
    flat = [jnp.pad(a.reshape(-1), (0, -a.size % unit)) for a in arrs]
    return jnp.concatenate(flat).reshape(-1, LANE)


def _unpack(buf, shapes):
    unit = SUBLANE * LANE
    flat = buf.reshape(-1)
    out, off = [], 0
    for s in shapes:
        size = 1
        for d in s:
            size *= d
        out.append(flat[off:off + size].reshape(s))
        off += size + (-size % unit)
    return out


def _cross_attention_fwd(l, h, mem, g_x, g_mem, wq, wk, wv, wo):
    hx = _rms_fwd(f"rms_x{l}", h, g_x)
    memn = _rms_fwd(f"rms_mem{l}", mem, g_mem)
    q = _mm_nn(f"xq{l}", hx, wq, BF16)
    k = _mm_nn(f"xk{l}", memn, wk, BF16)
    v = _mm_nn(f"xv{l}", memn, wv, BF16)
    o = _xattn_fwd(f"xattn_fwd{l}", q, k, v)
    h_out = _mm_nn(f"xo{l}", o, wo, F32, add=h)
    return h_out, (hx, memn, q, k, v, o)


def _cross_attention_bwd(l, dh, h, mem, g_x, g_mem, wq, wk, wv, wo, saved):
    hx, memn, q, k, v, o = saved
    do = _mm_nt(f"xo_dx{l}", dh, wo, BF16)
    dwo = _mm_tn(f"xo_dw{l}", o, dh, BF16)
    dq, dk, dv = _xattn_bwd(f"xattn_bwd{l}", q, k, v, do)
    dwq = _mm_tn(f"xq_dw{l}", hx, dq, BF16)
    dwk = _mm_tn(f"xk_dw{l}", memn, dk, BF16)
    dwv = _mm_tn(f"xv_dw{l}", memn, dv, BF16)
    dmemn = _mm_nt(f"xk_dx{l}", dk, wk, F32)
    dmemn = _mm_nt(f"xv_dx{l}", dv, wv, F32, add=dmemn)
    _, dg_mem = _rms_bwd(f"rms_mem_bwd{l}", dmemn, mem, g_mem, jnp.zeros_like(mem))
    dhx = _mm_nt(f"xq_dx{l}", dq, wq, F32)
    dh_in, dg_x = _rms_bwd(f"rms_x_bwd{l}", dhx, h, g_x, dh)
    return dh_in, (dg_x, dg_mem, dwq, dwk, dwv, dwo)


def kernel(x, mem, norm_mix_g, norm_x_g, norm_mem_g, final_norm_g, w_in_ab, rel_bias, conv_w, conv_b, conv_ln_g, conv_ln_b, w_out_ab, w_in_c, sgu_ln_g, sgu_ln_b, w_s, b_s, w_out_c, w_xq, w_xk, w_xv, w_xo, loss_target, m_norm_mix_g, m_norm_x_g, m_norm_mem_g, m_final_norm_g, m_w_in_ab, m_rel_bias, m_conv_w, m_conv_b, m_conv_ln_g, m_conv_ln_b, m_w_out_ab, m_w_in_c, m_sgu_ln_g, m_sgu_ln_b, m_w_s, m_b_s, m_w_out_c, m_w_xq, m_w_xk, m_w_xv, m_w_xo, v_norm_mix_g, v_norm_x_g, v_norm_mem_g, v_final_norm_g, v_w_in_ab, v_rel_bias, v_conv_w, v_conv_b, v_conv_ln_g, v_conv_ln_b, v_w_out_ab, v_w_in_c, v_sgu_ln_g, v_sgu_ln_b, v_w_s, v_b_s, v_w_out_c, v_w_xq, v_w_xk, v_w_xv, v_w_xo):
    names = ["norm_mix_g", "norm_x_g", "norm_mem_g", "final_norm_g", "w_in_ab", "rel_bias", "conv_w", "conv_b",
             "conv_ln_g", "conv_ln_b", "w_out_ab", "w_in_c", "sgu_ln_g", "sgu_ln_b", "w_s", "b_s", "w_out_c",
             "w_xq", "w_xk", "w_xv", "w_xo"]
    W = dict(zip(names, (norm_mix_g, norm_x_g, norm_mem_g, final_norm_g, w_in_ab, rel_bias, conv_w, conv_b, conv_ln_g,
                         conv_ln_b, w_out_ab, w_in_c, sgu_ln_g, sgu_ln_b, w_s, b_s, w_out_c, w_xq, w_xk, w_xv, w_xo)))
    M1 = dict(zip(names, (m_norm_mix_g, m_norm_x_g, m_norm_mem_g, m_final_norm_g, m_w_in_ab, m_rel_bias, m_conv_w, m_conv_b,
                          m_conv_ln_g, m_conv_ln_b, m_w_out_ab, m_w_in_c, m_sgu_ln_g, m_sgu_ln_b, m_w_s, m_b_s, m_w_out_c,
                          m_w_xq, m_w_xk, m_w_xv, m_w_xo)))
    M2 = dict(zip(names, (v_norm_mix_g, v_norm_x_g, v_norm_mem_g, v_final_norm_g, v_w_in_ab, v_rel_bias, v_conv_w, v_conv_b,
                          v_conv_ln_g, v_conv_ln_b, v_w_out_ab, v_w_in_c, v_sgu_ln_g, v_sgu_ln_b, v_w_s, v_b_s, v_w_out_c,
                          v_w_xq, v_w_xk, v_w_xv, v_w_xo)))

    h0, memv, tgt = x[0], mem[0], loss_target[0]
    T, D = h0.shape
    n_rel = rel_bias.shape[2]
    xnames = ["w_xq", "w_xk", "w_xv", "w_xo"]
    bf = lambda a: a.astype(BF16)
    blocks = lambda g: g.reshape(N_DEV, -1, D)

    small = _pack([conv_w[0], sgu_ln_g[0], sgu_ln_b[0]])
    win_ab, small_g = _comm_call("gather_in_ab", _Carry("gather", [bf(w_in_ab[0]), small]))
    per_cw, per_ln = conv_w.shape[2], sgu_ln_g.shape[1]
    cw_s, lg_s, lb_s = zip(*[_unpack(small_g[d], [(CONV_K, per_cw), (1, per_ln), (1, per_ln)]) for d in range(N_DEV)])
    conv_w_full = jnp.pad(jnp.concatenate(cw_s, axis=1), ((0, HALO - CONV_K), (0, 0)))
    sgu_g_full = jnp.concatenate(lg_s, axis=1)
    sgu_b_full = jnp.concatenate(lb_s, axis=1)
    b_s_t = b_s[0].T

    hn0 = _rms_fwd("rms_mix0", h0, norm_mix_g[0:1])
    ag1 = _Carry("gather", [bf(w_out_ab[0])] + [bf(W[n][0]) for n in xnames])
    proj_ab = _mm_nn("in_ab", hn0, win_ab, F32, carry=ag1)
    wout_ab = ag1.result[0].reshape(-1, D)
    wx0 = [g.reshape(D, D) for g in ag1.result[1:]]
    btile = _bias_tile(rel_bias[0])
    ag2 = _Carry("gather", [bf(w_in_c[0])])
    ya, y_ab = _band_attn_fwd(proj_ab, btile, carry=ag2)
    win_c = ag2.result[0]
    z_conv = _conv_z(proj_ab, conv_w_full, conv_b)
    y_ab = _conv_ln_fwd(proj_ab, z_conv, y_ab, conv_ln_g, conv_ln_b)
    h1 = _mm_nn("out_ab", y_ab, wout_ab, F32, add=h0)
    h2, xs0 = _cross_attention_fwd(0, h1, memv, norm_x_g[0:1], norm_mem_g[0:1], *wx0)
    hn1 = _rms_fwd("rms_mix1", h2, norm_mix_g[1:2])
    ag3 = _Carry("gather", [bf(w_out_c[0])] + [bf(W[n][1]) for n in xnames])
    proj_c = _mm_nn("in_c", hn1, win_c, F32, carry=ag3)
    wout_c = ag3.result[0].reshape(-1, D)
    wx1 = [g.reshape(D, D) for g in ag3.result[1:]]
    y_c = _sgu_fwd(proj_c, sgu_g_full, sgu_b_full, w_s[0], b_s_t)
    h3 = _mm_nn("out_c", y_c, wout_c, F32, add=h2)
    h4, xs1 = _cross_attention_fwd(1, h3, memv, norm_x_g[1:2], norm_mem_g[1:2], *wx1)
    loss_acc, dh4, dg_final = _loss_head(h4, tgt, final_norm_g.reshape(1, D))

    sent = {}

    def ride(**grads):
        carry = _Carry("exchange", list(grads.values()))
        sent.update({name: (carry, i) for i, name in enumerate(grads)})
        return carry

    dh3, (dg_x1, dg_mem1, dwq1, dwk1, dwv1, dwo1) = _cross_attention_bwd(
        1, dh4, h3, memv, norm_x_g[1:2], norm_mem_g[1:2], *wx1, xs1)
    dy_c = _mm_nt("out_c_dx", dh3, wout_c, F32, carry=ride(w_xo1=blocks(dwo1)))
    dwout_c = _mm_tn("out_c_dw", y_c, dh3, BF16, carry=ride(w_xq1=blocks(dwq1)))
    dproj_c, dws, dbs_t, dsgu_g, dsgu_b = _sgu_bwd(proj_c, dy_c, sgu_g_full, sgu_b_full, w_s[0], b_s_t)
    dwin_c = _mm_tn("in_c_dw", hn1, dproj_c, BF16, per=win_c.shape[2],
                    carry=ride(w_xk1=blocks(dwk1), w_out_c=blocks(dwout_c)))
    dhn1 = _mm_nt("in_c_dx", dproj_c, win_c, F32, carry=ride(w_xv1=blocks(dwv1)))
    dh2, dg_mix1 = _rms_bwd("rms_mix1_bwd", dhn1, h2, norm_mix_g[1:2], dh3)
    dh1, (dg_x0, dg_mem0, dwq0, dwk0, dwv0, dwo0) = _cross_attention_bwd(
        0, dh2, h1, memv, norm_x_g[0:1], norm_mem_g[0:1], *wx0, xs0)
    dy_ab = _mm_nt("out_ab_dx", dh1, wout_ab, F32, carry=ride(w_xo0=blocks(dwo0)))
    dwout_ab = _mm_tn("out_ab_dw", y_ab, dh1, BF16, carry=ride(w_xq0=blocks(dwq0)))
    dq, dk, dv, dgate_a, dbtile = _band_attn_bwd(proj_ab, btile, dy_ab, ya, carry=ride(w_in_c=dwin_c))
    drel = _bias_tile_grad(dbtile, n_rel)
    dz, dgate_b, dcln_g, dcln_b, dconv_b = _conv_bwd_ln(proj_ab, z_conv, dy_ab, conv_ln_g, conv_ln_b)
    dglu_a, dglu_b, dconv_w = _conv_bwd_taps(proj_ab, dz, conv_w_full)
    dproj_ab = jnp.concatenate([dq, dk, dv, dglu_a, dglu_b, dgate_a, dgate_b], axis=1)
    dwin_ab_lo = _mm_tn("in_ab_dw_lo", hn0, dproj_ab, BF16, per=win_ab.shape[2], rows=(0, D // 2),
                        carry=ride(w_xk0=blocks(dwk0), w_out_ab=blocks(dwout_ab)))
    dwin_ab_hi = _mm_tn("in_ab_dw_hi", hn0, dproj_ab, BF16, per=win_ab.shape[2], rows=(D // 2, D // 2),
                        carry=ride(w_in_ab_lo=dwin_ab_lo))
    dhn0 = _mm_nt("in_ab_dx", dproj_ab, win_ab, F32, carry=ride(w_in_ab_hi=dwin_ab_hi, w_xv0=blocks(dwv0)))
    dx, dg_mix0 = _rms_bwd("rms_mix0_bwd", dhn0, h0, norm_mix_g[0:1], dh1)
    received = lambda name: sent[name][0].result[sent[name][1]]

    sm = [_pack([dconv_w[:CONV_K, d * per_cw:(d + 1) * per_cw], dsgu_g[:, d * per_ln:(d + 1) * per_ln],
                 dsgu_b[:, d * per_ln:(d + 1) * per_ln]]) for d in range(N_DEV)]
    mask = (jnp.arange(GMLP_CHUNK)[:, None] // CHUNK >= jnp.arange(GMLP_CHUNK)[None, :] // CHUNK).astype(F32)
    rep_names = ["norm_mix_g", "norm_x_g", "norm_mem_g", "final_norm_g", "rel_bias", "conv_b", "conv_ln_g", "conv_ln_b",
                 "w_s", "b_s"]
    rep_grads = {
        "norm_mix_g": jnp.concatenate([dg_mix0, dg_mix1], axis=0),
        "norm_x_g": jnp.concatenate([dg_x0, dg_x1], axis=0),
        "norm_mem_g": jnp.concatenate([dg_mem0, dg_mem1], axis=0),
        "final_norm_g": dg_final.reshape(D),
        "rel_bias": drel[None], "conv_b": dconv_b, "conv_ln_g": dcln_g, "conv_ln_b": dcln_b,
        "w_s": (dws * mask[None])[None], "b_s": dbs_t.T[None],
    }
    ex_e = _Carry("exchange", [jnp.stack(sm), _pack([rep_grads[n] for n in rep_names])], bcast=[False, True])

    out = {}
    kinds = ("grad", "delta", "new_m", "new_v")
    for n in ("w_in_ab", "w_out_ab", "w_in_c", "w_out_c"):
        contrib = [received("w_in_ab_lo"), received("w_in_ab_hi")] if n == "w_in_ab" else received(n)
        res = _adamw(f"adamw_{n}", contrib, W[n][0], M1[n][0], M2[n][0], carry=ex_e if n == "w_in_ab" else None)
        for kind, r in zip(kinds, res):
            out[(kind, n)] = r[None]
    recv_small, recv_rep = ex_e.result
    for n in xnames:
        res = [_adamw(f"adamw_{n}{l}", received(f"{n}{l}"), W[n][l], M1[n][l], M2[n][l]) for l in range(2)]
        for kind, r in zip(kinds, zip(*res)):
            out[(kind, n)] = jnp.stack(r)
    sm_names = ["conv_w", "sgu_ln_g", "sgu_ln_b"]
    res = _adamw("adamw_small", recv_small, *[_pack([D_[n][0] for n in sm_names]) for D_ in (W, M1, M2)])
    for kind, r in zip(kinds, res):
        for n, piece in zip(sm_names, _unpack(r, [W[n].shape for n in sm_names])):
            out[(kind, n)] = piece
    res = _adamw("adamw_replicated", recv_rep, *[_pack([D_[n] for n in rep_names]) for D_ in (W, M1, M2)])
    for kind, r in zip(kinds, res):
        for n, piece in zip(rep_names, _unpack(r, [W[n].shape for n in rep_names])):
            out[(kind, n)] = piece

    loss = lax.psum(loss_acc[0, 0], MESH_AXES)
    return (loss, dx[None]) + tuple(out[(kind, n)] for kind in kinds for n in names)
```
